```python
import numpy as np
import jax, jax.numpy as jnp
from jax import lax

D_MODEL = 1024
BATCH = 8
SEQ = 2048
DEPTH = 2

N_A_LAYERS = DEPTH // 2
N_B_LAYERS = DEPTH - N_A_LAYERS
D_FF = 2816
CONV_WIDTH = 3
N_HEADS = 16
HEAD_DIM = D_MODEL // N_HEADS
N_KV_GROUPS = 4
HEADS_PER_GROUP = N_HEADS // N_KV_GROUPS
CMP_BLOCK = 32
CMP_STRIDE = 16
CMP_HIDDEN = 4 * HEAD_DIM
SEL_BLOCK = 64
N_SELECT = 16
WINDOW = 512
N_BRANCH = 3
N_KV_SLOTS = 6
QUERY_CHUNK = 32
EPS = 1e-6
FORCE_SCORE = 1e9
MASK_SCORE = -1e30

kernel_name = "yoco_shortconv_nsa_macaron"


def _rmsnorm(x, g):
    xf = x.astype(jnp.float32)
    y = xf * lax.rsqrt(jnp.mean(xf * xf, axis=-1, keepdims=True) + EPS)
    return (y * g.astype(jnp.float32)).astype(x.dtype)


def _swiglu(x, w_gate_up, w_down):
    a, b = jnp.split(x @ w_gate_up, 2, axis=-1)
    return (jax.nn.silu(a) * b) @ w_down


def _short_conv(x, w_in, conv_w, w_out):
    b_gate, c_gate, u = jnp.split(x @ w_in, 3, axis=-1)
    v = c_gate * u
    conv = lax.conv_general_dilated(
        v, conv_w[:, None, :], window_strides=(1,), padding=[(CONV_WIDTH - 1, 0)],
        dimension_numbers=("NWC", "WIO", "NWC"), feature_group_count=D_MODEL)
    return (b_gate * conv) @ w_out


def _compress(x_raw, pos, w1, b1, w2):
    s = x_raw.shape[2]
    n_cmp = (s - CMP_BLOCK) // CMP_STRIDE + 1
    idx = np.arange(n_cmp)[:, None] * CMP_STRIDE + np.arange(CMP_BLOCK)[None, :]
    blocks = x_raw[:, :, idx] + pos
    flat = blocks.reshape(blocks.shape[:3] + (CMP_BLOCK * HEAD_DIM,))
    return jax.nn.gelu(flat @ w1 + b1) @ w2


def _shared_kv(h, kv_norm, kv_w, cmp_pos, cmp_w1, cmp_b1, cmp_w2, k_norm):
    b, s, _ = h.shape
    kv = (_rmsnorm(h, kv_norm) @ kv_w).reshape(b, s, N_KV_SLOTS, N_KV_GROUPS, HEAD_DIM)
    kv = jnp.transpose(kv, (2, 0, 3, 1, 4))
    k_cmp = _rmsnorm(_compress(kv[0], cmp_pos[0], cmp_w1[0], cmp_b1[0], cmp_w2[0]), k_norm[0])
    v_cmp = _compress(kv[1], cmp_pos[1], cmp_w1[1], cmp_b1[1], cmp_w2[1])
    n_blk = s // SEL_BLOCK
    k_sel = _rmsnorm(kv[2], k_norm[1]).reshape(b, N_KV_GROUPS, n_blk, SEL_BLOCK, HEAD_DIM)
    v_sel = kv[3].reshape(b, N_KV_GROUPS, n_blk, SEL_BLOCK, HEAD_DIM)
    pad = ((0, 0), (0, 0), (WINDOW, 0), (0, 0))
    k_win = jnp.pad(_rmsnorm(kv[4], k_norm[2]), pad)
    v_win = jnp.pad(kv[5], pad)
    return (k_cmp, v_cmp, k_sel, v_sel, k_win, v_win)


def _softmax_f32(logits, mask):
    return jax.nn.softmax(jnp.where(mask, logits, MASK_SCORE), axis=-1)


def _nsa(x, w_qg, q_norm, w_o, k_cmp, v_cmp, k_sel, v_sel, k_win, v_win):
    b, s, _ = x.shape
    n_chunk = s // QUERY_CHUNK
    n_cmp = k_cmp.shape[2]
    n_blk = k_sel.shape[2]
    n_sel = min(N_SELECT, n_blk)
    scale = HEAD_DIM ** -0.5

    proj = x @ w_qg
    q = _rmsnorm(proj[..., :N_HEADS * HEAD_DIM].reshape(b, s, N_HEADS, HEAD_DIM), q_norm)
    gate = jax.nn.sigmoid(proj[..., N_HEADS * HEAD_DIM:].astype(jnp.float32))
    q = q.reshape(b, n_chunk, QUERY_CHUNK, N_KV_GROUPS, HEADS_PER_GROUP, HEAD_DIM)
    q = q.transpose(1, 0, 3, 4, 2, 5)
    gate = gate.reshape(b, n_chunk, QUERY_CHUNK, N_KV_GROUPS, HEADS_PER_GROUP, N_BRANCH)
    gate = gate.transpose(1, 0, 3, 4, 2, 5)

    cmp_start = np.arange(n_cmp) * CMP_STRIDE
    cmp_end = cmp_start + CMP_BLOCK - 1
    blk_start = np.arange(n_blk) * SEL_BLOCK
    overlap = ((cmp_start[:, None] < blk_start[None, :] + SEL_BLOCK)
               & (cmp_start[:, None] + CMP_BLOCK > blk_start[None, :])).astype(np.float32)
    gather_blocks = jax.vmap(jax.vmap(lambda kb, ib: kb[ib]))
    blk_ids = jnp.arange(n_blk)

    def chunk_fn(args):
        c, qc, gc = args
        t = c * QUERY_CHUNK + jnp.arange(QUERY_CHUNK)
        logits = jnp.einsum("bghtd,bgnd->bghtn", qc, k_cmp).astype(jnp.float32) * scale
        valid = cmp_end[None, :] <= t[:, None]
        p_cmp = _softmax_f32(logits, valid) * valid
        o_cmp = jnp.einsum("bghtn,bgnd->bghtd", p_cmp.astype(v_cmp.dtype), v_cmp)
        imp = jnp.einsum("bghtn,nj->bgtj", p_cmp, overlap)
        cur = (t // SEL_BLOCK)[:, None]
        forced = (blk_ids[None] == 0) | (blk_ids[None] == cur) | (blk_ids[None] == cur - 1)
        imp = jnp.where(forced, FORCE_SCORE, imp)
        imp = jnp.where(blk_ids[None] > cur, MASK_SCORE, imp)
        _, idx = lax.top_k(imp, n_sel)
        ks = gather_blocks(k_sel, idx).reshape(b, N_KV_GROUPS, QUERY_CHUNK, n_sel * SEL_BLOCK, HEAD_DIM)
        vs = gather_blocks(v_sel, idx).reshape(b, N_KV_GROUPS, QUERY_CHUNK, n_sel * SEL_BLOCK, HEAD_DIM)
        kpos = (idx[..., None] * SEL_BLOCK + jnp.arange(SEL_BLOCK)).reshape(
            b, N_KV_GROUPS, QUERY_CHUNK, n_sel * SEL_BLOCK)
        smask = (kpos <= t[:, None])[:, :, None]
        logits = jnp.einsum("bghtd,bgtkd->bghtk", qc, ks).astype(jnp.float32) * scale
        p_sel = _softmax_f32(logits, smask)
        o_sel = jnp.einsum("bghtk,bgtkd->bghtd", p_sel.astype(vs.dtype), vs)
        start = c * QUERY_CHUNK
        kw = lax.dynamic_slice_in_dim(k_win, start, WINDOW + QUERY_CHUNK, axis=2)
        vw = lax.dynamic_slice_in_dim(v_win, start, WINDOW + QUERY_CHUNK, axis=2)
        wpos = start - WINDOW + jnp.arange(WINDOW + QUERY_CHUNK)
        wmask = ((wpos[None] <= t[:, None]) & (wpos[None] > t[:, None] - WINDOW)
                 & (wpos[None] >= 0))
        logits = jnp.einsum("bghtd,bgkd->bghtk", qc, kw).astype(jnp.float32) * scale
        p_win = _softmax_f32(logits, wmask)
        o_win = jnp.einsum("bghtk,bgkd->bghtd", p_win.astype(vw.dtype), vw)
        out = gc[..., 0:1] * o_cmp + gc[..., 1:2] * o_sel + gc[..., 2:3] * o_win
        return out.astype(qc.dtype)

    outs = lax.map(chunk_fn, (jnp.arange(n_chunk), q, gate))
    o = outs.transpose(1, 0, 4, 2, 3, 5).reshape(b, s, N_HEADS * HEAD_DIM)
    return o @ w_o


def setup_inputs(seed: int = 0) -> dict:
    key = jax.random.key(seed)
    ks = jax.random.split(key, 20)
    nrm = lambda k, shape, scale: jax.random.normal(k, shape, jnp.float32) * scale
    gain = lambda k, shape: 1.0 + 0.02 * jax.random.normal(k, shape, jnp.float32)
    qg_width = N_HEADS * HEAD_DIM + N_BRANCH * N_HEADS
    return {
        "x": nrm(ks[0], (BATCH, SEQ, D_MODEL), 1.0),
        "ffn_norm": gain(ks[1], (DEPTH, 2, D_MODEL)),
        "ffn_w_gate_up": nrm(ks[2], (DEPTH, 2, D_MODEL, 2 * D_FF), D_MODEL ** -0.5),
        "ffn_w_down": nrm(ks[3], (DEPTH, 2, D_FF, D_MODEL), D_FF ** -0.5),
        "mix_norm": gain(ks[4], (DEPTH, D_MODEL)),
        "conv_w_in": nrm(ks[5], (N_A_LAYERS, D_MODEL, 3 * D_MODEL), D_MODEL ** -0.5),
        "conv_w": nrm(ks[6], (N_A_LAYERS, CONV_WIDTH, D_MODEL), CONV_WIDTH ** -0.5),
        "conv_w_out": nrm(ks[7], (N_A_LAYERS, D_MODEL, D_MODEL), D_MODEL ** -0.5),
        "kv_norm": gain(ks[8], (D_MODEL,)),
        "kv_w": nrm(ks[9], (D_MODEL, N_KV_SLOTS * N_KV_GROUPS * HEAD_DIM), D_MODEL ** -0.5),
        "cmp_pos": nrm(ks[10], (2, CMP_BLOCK, HEAD_DIM), 0.1),
        "cmp_w1": nrm(ks[11], (2, CMP_BLOCK * HEAD_DIM, CMP_HIDDEN), (CMP_BLOCK * HEAD_DIM) ** -0.5),
        "cmp_b1": nrm(ks[12], (2, CMP_HIDDEN), 0.01),
        "cmp_w2": nrm(ks[13], (2, CMP_HIDDEN, HEAD_DIM), CMP_HIDDEN ** -0.5),
        "k_norm": gain(ks[14], (N_BRANCH, HEAD_DIM)),
        "nsa_w_qg": nrm(ks[15], (N_B_LAYERS, D_MODEL, qg_width), D_MODEL ** -0.5),
        "q_norm": gain(ks[16], (N_B_LAYERS, HEAD_DIM)),
        "nsa_w_o": nrm(ks[17], (N_B_LAYERS, N_HEADS * HEAD_DIM, D_MODEL), (N_HEADS * HEAD_DIM) ** -0.5),
    }


def reference(x, ffn_norm, ffn_w_gate_up, ffn_w_down, mix_norm, conv_w_in, conv_w, conv_w_out,
              kv_norm, kv_w, cmp_pos, cmp_w1, cmp_b1, cmp_w2, k_norm, nsa_w_qg, q_norm, nsa_w_o):
    h = x
    shared = None
    for layer in range(DEPTH):
        h = h + 0.5 * _swiglu(_rmsnorm(h, ffn_norm[layer, 0]), ffn_w_gate_up[layer, 0], ffn_w_down[layer, 0])
        hn = _rmsnorm(h, mix_norm[layer])
        if layer < N_A_LAYERS:
            h = h + _short_conv(hn, conv_w_in[layer], conv_w[layer], conv_w_out[layer])
        else:
            i = layer - N_A_LAYERS
            h = h + _nsa(hn, nsa_w_qg[i], q_norm[i], nsa_w_o[i], *shared)
        h = h + 0.5 * _swiglu(_rmsnorm(h, ffn_norm[layer, 1]), ffn_w_gate_up[layer, 1], ffn_w_down[layer, 1])
        if layer == N_A_LAYERS - 1:
            shared = _shared_kv(h, kv_norm, kv_w, cmp_pos, cmp_w1, cmp_b1, cmp_w2, k_norm)
    return h
```

```python
import functools

import numpy as np
import jax
import jax.numpy as jnp
from jax import lax
from jax.experimental import pallas as pl
from jax.experimental.pallas import tpu as pltpu

EPS = 1e-6
N_HEADS = 16
HEAD_DIM = 64
N_KV_GROUPS = 4
HEADS_PER_GROUP = N_HEADS // N_KV_GROUPS
N_KV_SLOTS = 6
N_BRANCH = 3
CMP_BLOCK = 32
CMP_STRIDE = 16
SEL_BLOCK = 64
N_SELECT = 16
WINDOW = 512
FORCE_SCORE = 1e9
MASK_SCORE = -1e30

LANES = 128
VMEM_LIMIT = 56 * 1024 * 1024

BF16 = jnp.bfloat16
F32 = jnp.float32


def _cparams(sem):
    return pltpu.CompilerParams(dimension_semantics=sem, vmem_limit_bytes=VMEM_LIMIT)


def _rms(x, g):
    ms = jnp.mean(x * x, axis=-1, keepdims=True)
    return x * lax.rsqrt(ms + EPS) * g


def _dot(a, b):
    return jnp.dot(a, b, preferred_element_type=F32)


def _dot_nt(a, b):
    return lax.dot_general(a, b, (((1,), (1,)), ((), ())), preferred_element_type=F32)


def _ffn_kernel(x_ref, g_ref, wa_ref, wb_ref, wd_ref, o_ref, xn_ref, acc_ref, *, nf):
    f = pl.program_id(1)

    @pl.when(f == 0)
    def _():
        xn_ref[...] = _rms(x_ref[...], g_ref[...]).astype(BF16)
        acc_ref[...] = jnp.zeros_like(acc_ref)

    xn = xn_ref[...]
    a = _dot(xn, wa_ref[...])
    b = _dot(xn, wb_ref[...])
    h = (a * jax.nn.sigmoid(a) * b).astype(BF16)
    acc_ref[...] += _dot(h, wd_ref[...])

    @pl.when(f == nf - 1)
    def _():
        o_ref[...] = x_ref[...] + 0.5 * acc_ref[...]


def _ffn(h, g, w_gu, w_d, *, tm=512, tf=1408):
    t, d = h.shape
    dff = w_d.shape[0]
    nf = dff // tf
    return pl.pallas_call(
        functools.partial(_ffn_kernel, nf=nf),
        grid=(t // tm, nf),
        in_specs=[
            pl.BlockSpec((tm, d), lambda i, f: (i, 0)),
            pl.BlockSpec((1, d), lambda i, f: (0, 0)),
            pl.BlockSpec((d, tf), lambda i, f: (0, f)),
            pl.BlockSpec((d, tf), lambda i, f: (0, f + nf)),
            pl.BlockSpec((tf, d), lambda i, f: (f, 0)),
        ],
        out_specs=pl.BlockSpec((tm, d), lambda i, f: (i, 0)),
        out_shape=jax.ShapeDtypeStruct((t, d), F32),
        scratch_shapes=[pltpu.VMEM((tm, d), BF16), pltpu.VMEM((tm, d), F32)],
        compiler_params=_cparams(("parallel", "arbitrary")),
        name="ffn",
    )(h, g.reshape(1, d), w_gu, w_gu, w_d)


CARRY = 8


def _conv_kernel(x_ref, g_ref, win_ref, cw_ref, wout_ref, o_ref, vbuf_ref, *, tm, d):
    j = pl.program_id(1)

    @pl.when(j == 0)
    def _():
        vbuf_ref[0:CARRY, :] = jnp.zeros((CARRY, d), F32)

    x = x_ref[0]
    xn = _rms(x, g_ref[...]).astype(BF16)
    proj = _dot(xn, win_ref[...])
    bg = proj[:, 0:d]
    v = proj[:, d:2 * d] * proj[:, 2 * d:3 * d]
    vbuf_ref[CARRY:CARRY + tm, :] = v
    cw = cw_ref[...]
    conv = (cw[0:1, :] * vbuf_ref[CARRY - 2:CARRY - 2 + tm, :]
            + cw[1:2, :] * vbuf_ref[CARRY - 1:CARRY - 1 + tm, :]
            + cw[2:3, :] * v)
    y = (bg * conv).astype(BF16)
    o_ref[0] = x + _dot(y, wout_ref[...])
    vbuf_ref[0:CARRY, :] = vbuf_ref[tm:tm + CARRY, :]


def _conv_mixer(h, g, w_in, conv_w, w_out, *, tm=512):
    b, s, d = h.shape
    return pl.pallas_call(
        functools.partial(_conv_kernel, tm=tm, d=d),
        grid=(b, s // tm),
        in_specs=[
            pl.BlockSpec((1, tm, d), lambda i, j: (i, j, 0)),
            pl.BlockSpec((1, d), lambda i, j: (0, 0)),
            pl.BlockSpec((d, 3 * d), lambda i, j: (0, 0)),
            pl.BlockSpec((conv_w.shape[0], d), lambda i, j: (0, 0)),
            pl.BlockSpec((d, d), lambda i, j: (0, 0)),
        ],
        out_specs=pl.BlockSpec((1, tm, d), lambda i, j: (i, j, 0)),
        out_shape=jax.ShapeDtypeStruct((b, s, d), F32),
        scratch_shapes=[pltpu.VMEM((CARRY + tm, d), F32)],
        compiler_params=_cparams(("arbitrary", "arbitrary")),
        name="conv_mixer",
    )(h, g.reshape(1, d), w_in, conv_w, w_out)


def _kv_kernel(x_ref, g_ref, w_ref, kn_ref, o_ref):
    xn = _rms(x_ref[...], g_ref[...]).astype(BF16)
    y = _dot(xn, w_ref[...])
    o_ref[...] = y
    gw = N_KV_GROUPS * HEAD_DIM
    for slot, row in ((2, 1), (4, 2)):
        for grp in range(N_KV_GROUPS):
            off = slot * gw + grp * HEAD_DIM
            o_ref[:, off:off + HEAD_DIM] = _rms(y[:, off:off + HEAD_DIM], kn_ref[row:row + 1, :])


def _kv_proj(h, g, w, k_norm, *, tm=512):
    t, d = h.shape
    n = w.shape[1]
    return pl.pallas_call(
        _kv_kernel,
        grid=(t // tm,),
        in_specs=[
            pl.BlockSpec((tm, d), lambda i: (i, 0)),
            pl.BlockSpec((1, d), lambda i: (0, 0)),
            pl.BlockSpec((d, n), lambda i: (0, 0)),
            pl.BlockSpec(k_norm.shape, lambda i: (0, 0)),
        ],
        out_specs=pl.BlockSpec((tm, n), lambda i: (i, 0)),
        out_shape=jax.ShapeDtypeStruct((t, n), F32),
        compiler_params=_cparams(("parallel",)),
        name="kv_proj",
    )(h, g.reshape(1, d), w, k_norm)


def _cmp_kernel(x_ref, pos_ref, w1_ref, b1_ref, w2_ref, kn_ref, o_ref, *, normalize):
    x = (x_ref[...] + pos_ref[...]).astype(BF16)
    hid = jax.nn.gelu(_dot(x, w1_ref[...]) + b1_ref[...])
    y = _dot(hid.astype(BF16), w2_ref[...])
    if normalize:
        y = _rms(y, kn_ref[...])
    o_ref[...] = y.astype(o_ref.dtype)


def _compress(xb, pos, w1, b1, w2, kn, *, normalize, tm=512):
    m, kdim = xb.shape
    hid = w1.shape[1]
    dk = w2.shape[1]
    return pl.pallas_call(
        functools.partial(_cmp_kernel, normalize=normalize),
        grid=(m // tm,),
        in_specs=[
            pl.BlockSpec((tm, kdim), lambda i: (i, 0)),
            pl.BlockSpec((1, kdim), lambda i: (0, 0)),
            pl.BlockSpec((kdim, hid), lambda i: (0, 0)),
            pl.BlockSpec((1, hid), lambda i: (0, 0)),
            pl.BlockSpec((hid, dk), lambda i: (0, 0)),
            pl.BlockSpec((1, dk), lambda i: (0, 0)),
        ],
        out_specs=pl.BlockSpec((tm, dk), lambda i: (i, 0)),
        out_shape=jax.ShapeDtypeStruct((m, dk), BF16),
        compiler_params=_cparams(("parallel",)),
        name="compress",
    )(xb, pos.reshape(1, kdim), w1, b1.reshape(1, hid), w2, kn.reshape(1, dk))


def _qg_kernel(x_ref, g_ref, w_ref, qn_ref, q_ref, gate_ref, *, dq):
    xn = _rms(x_ref[...], g_ref[...]).astype(BF16)
    y = _dot(xn, w_ref[...])
    scale = HEAD_DIM ** -0.5
    qn = qn_ref[...] * scale
    for hd in range(N_HEADS):
        off = hd * HEAD_DIM
        q_ref[:, off:off + HEAD_DIM] = _rms(y[:, off:off + HEAD_DIM], qn).astype(q_ref.dtype)
    gate_ref[...] = jax.nn.sigmoid(y[:, dq:])


def _qg_proj(h, g, w, q_norm, *, tm=512):
    t, d = h.shape
    n = w.shape[1]
    dq = N_HEADS * HEAD_DIM
    return pl.pallas_call(
        functools.partial(_qg_kernel, dq=dq),
        grid=(t // tm,),
        in_specs=[
            pl.BlockSpec((tm, d), lambda i: (i, 0)),
            pl.BlockSpec((1, d), lambda i: (0, 0)),
            pl.BlockSpec((d, n), lambda i: (0, 0)),
            pl.BlockSpec((1, HEAD_DIM), lambda i: (0, 0)),
        ],
        out_specs=[pl.BlockSpec((tm, dq), lambda i: (i, 0)),
                   pl.BlockSpec((tm, n - dq), lambda i: (i, 0))],
        out_shape=[jax.ShapeDtypeStruct((t, dq), BF16),
                   jax.ShapeDtypeStruct((t, n - dq), F32)],
        compiler_params=_cparams(("parallel",)),
        name="qg_proj",
    )(h, g.reshape(1, d), w, q_norm.reshape(1, HEAD_DIM))


def _split3(x):
    hi = x.astype(BF16)
    r = x - hi.astype(F32)
    mid = r.astype(BF16)
    lo = (r - mid.astype(F32)).astype(BF16)
    return hi, mid, lo


def _nsa_kernel(q_ref, gate_ref, kc_ref, vc_ref, ks_ref, vs_ref, kw_ref, vw_ref, ovt_ref, exp_ref,
                o_ref, mask_ref, m_ref, l_ref, acc_ref, *, tq, n_cmp, n_blk):
    hpg = HEADS_PER_GROUP
    rows = hpg * tq
    i = pl.program_id(2)
    t0 = i * tq

    q = q_ref[0]
    q4 = jnp.concatenate([q[:, h * HEAD_DIM:(h + 1) * HEAD_DIM] for h in range(hpg)], axis=0)

    ncp = kc_ref.shape[2]
    s = _dot_nt(q4, kc_ref[0, 0]).reshape(hpg, tq, ncp)
    tpos = t0 + lax.broadcasted_iota(jnp.int32, (tq, ncp), 0)
    nidx = lax.broadcasted_iota(jnp.int32, (tq, ncp), 1)
    valid = ((nidx * CMP_STRIDE + (CMP_BLOCK - 1) <= tpos) & (nidx < n_cmp))[None]
    s = jnp.where(valid, s, MASK_SCORE)
    e = jnp.exp(s - jnp.max(s, axis=-1, keepdims=True))
    p = jnp.where(valid, e / jnp.sum(e, axis=-1, keepdims=True), 0.0)
    o_cmp = _dot(p.reshape(rows, ncp).astype(BF16), vc_ref[0, 0])

    psum = p[0]
    for h in range(1, hpg):
        psum = psum + p[h]
    ovt = ovt_ref[...]
    imp = None
    for part in _split3(psum):
        term = _dot_nt(ovt, part)
        imp = term if imp is None else imp + term
    jblk = lax.broadcasted_iota(jnp.int32, (n_blk, tq), 0)
    cur = (t0 + lax.broadcasted_iota(jnp.int32, (n_blk, tq), 1)) // SEL_BLOCK
    forced = (jblk == 0) | (jblk == cur) | (jblk == cur - 1)
    imp = jnp.where(forced, FORCE_SCORE, imp)
    imp = jnp.where(jblk > cur, MASK_SCORE, imp)
    rank = jnp.zeros((n_blk, tq), F32)
    for ii in range(n_blk):
        row = imp[ii:ii + 1, :]
        rank = rank + jnp.where(row > imp, 1.0, 0.0)
        rank = rank + jnp.where((row == imp) & (jblk > ii), 1.0, 0.0)
    sel_t = jnp.where(rank < float(min(N_SELECT, n_blk)), 1.0, 0.0)
    sel_t = jnp.concatenate([sel_t, jnp.zeros((LANES - n_blk, tq), F32)], axis=0)
    sel = sel_t.T.astype(BF16)
    mask_ref[...] = _dot(sel, exp_ref[...])

    tt = lax.broadcasted_iota(jnp.int32, (tq, tq), 0)
    kk = lax.broadcasted_iota(jnp.int32, (tq, tq), 1)

    def flash_init():
        m_ref[...] = jnp.full(m_ref.shape, MASK_SCORE, F32)
        l_ref[...] = jnp.zeros(l_ref.shape, F32)
        acc_ref[...] = jnp.zeros(acc_ref.shape, F32)

    def flash_step(k, v, keep):
        sc = _dot_nt(q4, k).reshape(hpg, tq, tq)
        sc = jnp.where(keep[None], sc, MASK_SCORE).reshape(rows, tq)
        m_prev = m_ref[...]
        m_new = jnp.maximum(m_prev, jnp.max(sc, axis=-1, keepdims=True))
        alpha = jnp.exp(m_prev - m_new)
        pe = jnp.exp(sc - m_new)
        l_ref[...] = alpha * l_ref[...] + jnp.sum(pe, axis=-1, keepdims=True)
        acc_ref[...] = alpha * acc_ref[...] + _dot(pe.astype(BF16), v)
        m_ref[...] = m_new

    def flash_out():
        return acc_ref[...] / l_ref[...]

    flash_init()

    def sel_body(c, carry):
        off = pl.multiple_of(c * tq, tq)
        keep = mask_ref[:, pl.ds(off, tq)] > 0.5
        flash_step(ks_ref[0, 0, pl.ds(off, tq), :], vs_ref[0, 0, pl.ds(off, tq), :], keep)
        return carry

    lax.fori_loop(0, i, sel_body, 0)
    keep = (mask_ref[:, pl.ds(pl.multiple_of(t0, tq), tq)] > 0.5) & (kk <= tt)
    flash_step(ks_ref[0, 0, pl.ds(pl.multiple_of(t0, tq), tq), :],
               vs_ref[0, 0, pl.ds(pl.multiple_of(t0, tq), tq), :], keep)
    o_sel = flash_out()

    flash_init()
    nw = WINDOW // tq

    def win_body(c, carry):
        off = pl.multiple_of(c * tq, tq)
        keep = (kk > tt) | (c > i - nw)
        flash_step(kw_ref[0, 0, pl.ds(off, tq), :], vw_ref[0, 0, pl.ds(off, tq), :], keep)
        return carry

    lax.fori_loop(jnp.maximum(i - nw, 0), i, win_body, 0)
    flash_step(kw_ref[0, 0, pl.ds(pl.multiple_of(t0, tq), tq), :],
               vw_ref[0, 0, pl.ds(pl.multiple_of(t0, tq), tq), :], kk <= tt)
    o_win = flash_out()

    gate = gate_ref[0]
    outs = []
    for h in range(hpg):
        r0 = h * tq
        c0 = h * N_BRANCH
        outs.append(gate[:, c0:c0 + 1] * o_cmp[r0:r0 + tq]
                    + gate[:, c0 + 1:c0 + 2] * o_sel[r0:r0 + tq]
                    + gate[:, c0 + 2:c0 + 3] * o_win[r0:r0 + tq])
    o_ref[0] = jnp.concatenate(outs, axis=-1).astype(o_ref.dtype)


def _nsa(q, gate, kc, vc, ks, vs, kw, vw, *, tq=256):
    b, s, dq = q.shape
    g = N_KV_GROUPS
    gd = HEADS_PER_GROUP * HEAD_DIM
    n_blk = s // SEL_BLOCK
    n_cmp = (s - CMP_BLOCK) // CMP_STRIDE + 1
    ncp = kc.shape[2]
    cmp_start = np.arange(ncp) * CMP_STRIDE
    blk_start = np.arange(n_blk) * SEL_BLOCK
    ovt = ((cmp_start[None, :] < blk_start[:, None] + SEL_BLOCK)
           & (cmp_start[None, :] + CMP_BLOCK > blk_start[:, None])
           & (np.arange(ncp)[None, :] < n_cmp)).astype(np.float32)
    expand = (np.arange(LANES)[:, None] == (np.arange(s)[None, :] // SEL_BLOCK)).astype(np.float32)
    rows = HEADS_PER_GROUP * tq
    kv_spec = lambda n: pl.BlockSpec((1, 1, n, HEAD_DIM), lambda bi, gi, i: (bi, gi, 0, 0))
    return pl.pallas_call(
        functools.partial(_nsa_kernel, tq=tq, n_cmp=n_cmp, n_blk=n_blk),
        grid=(b, g, s // tq),
        in_specs=[
            pl.BlockSpec((1, tq, gd), lambda bi, gi, i: (bi, i, gi)),
            pl.BlockSpec((1, tq, LANES), lambda bi, gi, i: (bi, i, gi)),
            kv_spec(ncp), kv_spec(ncp), kv_spec(s), kv_spec(s), kv_spec(s), kv_spec(s),
            pl.BlockSpec((n_blk, ncp), lambda bi, gi, i: (0, 0)),
            pl.BlockSpec((LANES, s), lambda bi, gi, i: (0, 0)),
        ],
        out_specs=pl.BlockSpec((1, tq, gd), lambda bi, gi, i: (bi, i, gi)),
        out_shape=jax.ShapeDtypeStruct((b, s, dq), BF16),
        scratch_shapes=[
            pltpu.VMEM((tq, s), F32),
            pltpu.VMEM((rows, 1), F32),
            pltpu.VMEM((rows, 1), F32),
            pltpu.VMEM((rows, HEAD_DIM), F32),
        ],
        compiler_params=_cparams(("parallel", "parallel", "arbitrary")),
        name="nsa",
    )(q, gate, kc, vc, ks, vs, kw, vw, jnp.asarray(ovt, BF16), jnp.asarray(expand, BF16))


def _oproj_kernel(o_ref, w_ref, h_ref, out_ref):
    out_ref[...] = h_ref[...] + _dot(o_ref[...], w_ref[...])


def _out_proj(o, w, h, *, tm=512):
    t, d = h.shape
    k = o.shape[1]
    return pl.pallas_call(
        _oproj_kernel,
        grid=(t // tm,),
        in_specs=[
            pl.BlockSpec((tm, k), lambda i: (i, 0)),
            pl.BlockSpec((k, d), lambda i: (0, 0)),
            pl.BlockSpec((tm, d), lambda i: (i, 0)),
        ],
        out_specs=pl.BlockSpec((tm, d), lambda i: (i, 0)),
        out_shape=jax.ShapeDtypeStruct((t, d), F32),
        compiler_params=_cparams(("parallel",)),
        name="out_proj",
    )(o, w, h)


def _cmp_blocks(x):
    b, g, s, dk = x.shape
    width = CMP_BLOCK * dk
    n_even = s // CMP_BLOCK
    even = x.reshape(b, g, n_even, width)
    odd = x[:, :, CMP_STRIDE:s - CMP_STRIDE].reshape(b, g, n_even - 1, width)
    odd = jnp.concatenate([odd, jnp.zeros((b, g, 1, width), x.dtype)], axis=2)
    blocks = jnp.stack([even, odd], axis=3).reshape(b, g, 2 * n_even, width)
    return blocks.reshape(b * g * 2 * n_even, width)


def _gate_columns(w_qg):
    dq = N_HEADS * HEAD_DIM
    per = HEADS_PER_GROUP * N_BRANCH
    wg = w_qg[:, dq:].reshape(-1, N_KV_GROUPS, per)
    wg = jnp.pad(wg, ((0, 0), (0, 0), (0, LANES - per)))
    return jnp.concatenate([w_qg[:, :dq], wg.reshape(-1, N_KV_GROUPS * LANES)], axis=1)


def kernel(x, ffn_norm, ffn_w_gate_up, ffn_w_down, mix_norm, conv_w_in, conv_w, conv_w_out, kv_norm, kv_w,
           cmp_pos, cmp_w1, cmp_b1, cmp_w2, k_norm, nsa_w_qg, q_norm, nsa_w_o):
    b, s, d = x.shape
    t = b * s
    depth = ffn_norm.shape[0]
    n_a = conv_w_in.shape[0]
    g = N_KV_GROUPS

    def ffn(h2, layer, j):
        return _ffn(h2, ffn_norm[layer, j], ffn_w_gate_up[layer, j].astype(BF16),
                    ffn_w_down[layer, j].astype(BF16))

    h = x.reshape(t, d)
    shared = None
    for layer in range(depth):
        h = ffn(h, layer, 0)
        if layer < n_a:
            h = _conv_mixer(h.reshape(b, s, d), mix_norm[layer], conv_w_in[layer].astype(BF16),
                            conv_w[layer], conv_w_out[layer].astype(BF16)).reshape(t, d)
        else:
            li = layer - n_a
            q, gate = _qg_proj(h, mix_norm[layer], _gate_columns(nsa_w_qg[li]).astype(BF16), q_norm[li])
            o = _nsa(q.reshape(b, s, -1), gate.reshape(b, s, -1), *shared)
            h = _out_proj(o.reshape(t, -1), nsa_w_o[li].astype(BF16), h)
        h = ffn(h, layer, 1)
        if layer == n_a - 1:
            kv = _kv_proj(h, kv_norm, kv_w.astype(BF16), k_norm)
            kv = kv.reshape(b, s, N_KV_SLOTS, g, HEAD_DIM).transpose(2, 0, 3, 1, 4)
            cmp = []
            for c in range(2):
                y = _compress(_cmp_blocks(kv[c]), cmp_pos[c], cmp_w1[c].astype(BF16), cmp_b1[c],
                              cmp_w2[c].astype(BF16), k_norm[0], normalize=(c == 0))
                cmp.append(y.reshape(b, g, -1, HEAD_DIM))
            shared = (cmp[0], cmp[1]) + tuple(kv[c].astype(BF16) for c in range(2, 6))
    return h.reshape(b, s, d)
```

```python
import functools

import numpy as np
import jax
import jax.numpy as jnp
from jax import lax
from jax.experimental import pallas as pl
from jax.experimental.pallas import tpu as pltpu

EPS = 1e-6
N_HEADS = 16
HEAD_DIM = 64
N_KV_GROUPS = 4
HEADS_PER_GROUP = N_HEADS // N_KV_GROUPS
N_KV_SLOTS = 6
N_BRANCH = 3
CMP_BLOCK = 32
CMP_STRIDE = 16
SEL_BLOCK = 64
N_SELECT = 16
WINDOW = 512
FORCE_SCORE = 1e9
MASK_SCORE = -1e30
LOG2E = 1.4426950408889634

LANES = 128
CHUNK = 256
VMEM_LIMIT = 56 * 1024 * 1024

BF16 = jnp.bfloat16
F32 = jnp.float32


def _cparams(sem):
    return pltpu.CompilerParams(dimension_semantics=sem, vmem_limit_bytes=VMEM_LIMIT)


def _rms(x, g):
    ms = jnp.mean(x * x, axis=-1, keepdims=True)
    return x * lax.rsqrt(ms + EPS) * g


def _dot(a, b):
    return jnp.dot(a, b, preferred_element_type=F32)


def _dot_nt(a, b):
    return lax.dot_general(a, b, (((1,), (1,)), ((), ())), preferred_element_type=F32)


def _ffn_kernel(x_ref, g_ref, wa_ref, wb_ref, wd_ref, o_ref, xn_ref, acc_ref, *, nf):
    f = pl.program_id(1)

    @pl.when(f == 0)
    def _():
        xn_ref[...] = _rms(x_ref[...], g_ref[...]).astype(BF16)
        acc_ref[...] = jnp.zeros_like(acc_ref)

    xn = xn_ref[...]
    a = _dot(xn, wa_ref[...])
    b = _dot(xn, wb_ref[...])
    h = (a * jax.nn.sigmoid(a) * b).astype(BF16)
    acc_ref[...] += _dot(h, wd_ref[...])

    @pl.when(f == nf - 1)
    def _():
        o_ref[...] = x_ref[...] + 0.5 * acc_ref[...]


def _ffn(h, g, w_gu, w_d, *, tm=512, tf=1408):
    t, d = h.shape
    dff = w_d.shape[0]
    nf = dff // tf
    return pl.pallas_call(
        functools.partial(_ffn_kernel, nf=nf),
        grid=(t // tm, nf),
        in_specs=[
            pl.BlockSpec((tm, d), lambda i, f: (i, 0)),
            pl.BlockSpec((1, d), lambda i, f: (0, 0)),
            pl.BlockSpec((d, tf), lambda i, f: (0, f)),
            pl.BlockSpec((d, tf), lambda i, f: (0, f + nf)),
            pl.BlockSpec((tf, d), lambda i, f: (f, 0)),
        ],
        out_specs=pl.BlockSpec((tm, d), lambda i, f: (i, 0)),
        out_shape=jax.ShapeDtypeStruct((t, d), F32),
        scratch_shapes=[pltpu.VMEM((tm, d), BF16), pltpu.VMEM((tm, d), F32)],
        compiler_params=_cparams(("parallel", "arbitrary")),
        name="ffn",
    )(h, g.reshape(1, d), w_gu, w_gu, w_d)


CARRY = 8


def _conv_kernel(x_ref, g_ref, win_ref, cw_ref, wout_ref, o_ref, vbuf_ref, *, tm, d):
    j = pl.program_id(1)

    @pl.when(j == 0)
    def _():
        vbuf_ref[0:CARRY, :] = jnp.zeros((CARRY, d), F32)

    x = x_ref[0]
    xn = _rms(x, g_ref[...]).astype(BF16)
    proj = _dot(xn, win_ref[...])
    bg = proj[:, 0:d]
    v = proj[:, d:2 * d] * proj[:, 2 * d:3 * d]
    vbuf_ref[CARRY:CARRY + tm, :] = v
    cw = cw_ref[...]
    conv = (cw[0:1, :] * vbuf_ref[CARRY - 2:CARRY - 2 + tm, :]
            + cw[1:2, :] * vbuf_ref[CARRY - 1:CARRY - 1 + tm, :]
            + cw[2:3, :] * v)
    y = (bg * conv).astype(BF16)
    o_ref[0] = x + _dot(y, wout_ref[...])
    vbuf_ref[0:CARRY, :] = vbuf_ref[tm:tm + CARRY, :]


def _conv_mixer(h, g, w_in, conv_w, w_out, *, tm=512):
    b, s, d = h.shape
    return pl.pallas_call(
        functools.partial(_conv_kernel, tm=tm, d=d),
        grid=(b, s // tm),
        in_specs=[
            pl.BlockSpec((1, tm, d), lambda i, j: (i, j, 0)),
            pl.BlockSpec((1, d), lambda i, j: (0, 0)),
            pl.BlockSpec((d, 3 * d), lambda i, j: (0, 0)),
            pl.BlockSpec((conv_w.shape[0], d), lambda i, j: (0, 0)),
            pl.BlockSpec((d, d), lambda i, j: (0, 0)),
        ],
        out_specs=pl.BlockSpec((1, tm, d), lambda i, j: (i, j, 0)),
        out_shape=jax.ShapeDtypeStruct((b, s, d), F32),
        scratch_shapes=[pltpu.VMEM((CARRY + tm, d), F32)],
        compiler_params=_cparams(("arbitrary", "arbitrary")),
        name="conv_mixer",
    )(h, g.reshape(1, d), w_in, conv_w, w_out)


def _kv_kernel(x_ref, g_ref, w_ref, kn_ref, raw_ref, ks_ref, vst_ref, kw_ref, vwt_ref, *, tm):
    j = pl.program_id(1)
    xn = _rms(x_ref[0], g_ref[...]).astype(BF16)
    y = _dot(xn, w_ref[...])
    gw = N_KV_GROUPS * HEAD_DIM
    raw_ref[0] = y[:, 0:2 * gw]
    pos = j * tm + lax.broadcasted_iota(jnp.int32, (tm, HEAD_DIM), 0)
    lane = lax.broadcasted_iota(jnp.int32, (tm, HEAD_DIM), 1)
    onehot = jnp.where(lane == pos // SEL_BLOCK, 1.0, 0.0).astype(BF16)
    for grp in range(N_KV_GROUPS):
        off = 2 * gw + grp * HEAD_DIM
        ks_ref[0, grp, :, 0:HEAD_DIM] = _rms(y[:, off:off + HEAD_DIM], kn_ref[1:2, :]).astype(BF16)
        ks_ref[0, grp, :, HEAD_DIM:2 * HEAD_DIM] = onehot
        off = 4 * gw + grp * HEAD_DIM
        kw_ref[0, grp] = _rms(y[:, off:off + HEAD_DIM], kn_ref[2:3, :]).astype(BF16)
    for slot, ref in ((3, vst_ref), (5, vwt_ref)):
        vt = y[:, slot * gw:(slot + 1) * gw].T
        for grp in range(N_KV_GROUPS):
            for cc in range(tm // CHUNK):
                ref[0, grp, cc] = vt[grp * HEAD_DIM:(grp + 1) * HEAD_DIM,
                                     cc * CHUNK:(cc + 1) * CHUNK].astype(BF16)


def _kv_proj(h, g, w, k_norm, *, tm=512):
    b, s, d = h.shape
    n = w.shape[1]
    ng = N_KV_GROUPS
    gw = ng * HEAD_DIM
    cpt = tm // CHUNK
    assert s // SEL_BLOCK <= HEAD_DIM
    return pl.pallas_call(
        functools.partial(_kv_kernel, tm=tm),
        grid=(b, s // tm),
        in_specs=[
            pl.BlockSpec((1, tm, d), lambda i, j: (i, j, 0)),
            pl.BlockSpec((1, d), lambda i, j: (0, 0)),
            pl.BlockSpec((d, n), lambda i, j: (0, 0)),
            pl.BlockSpec(k_norm.shape, lambda i, j: (0, 0)),
        ],
        out_specs=[
            pl.BlockSpec((1, tm, 2 * gw), lambda i, j: (i, j, 0)),
            pl.BlockSpec((1, ng, tm, 2 * HEAD_DIM), lambda i, j: (i, 0, j, 0)),
            pl.BlockSpec((1, ng, cpt, HEAD_DIM, CHUNK), lambda i, j: (i, 0, j, 0, 0)),
            pl.BlockSpec((1, ng, tm, HEAD_DIM), lambda i, j: (i, 0, j, 0)),
            pl.BlockSpec((1, ng, cpt, HEAD_DIM, CHUNK), lambda i, j: (i, 0, j, 0, 0)),
        ],
        out_shape=[
            jax.ShapeDtypeStruct((b, s, 2 * gw), F32),
            jax.ShapeDtypeStruct((b, ng, s, 2 * HEAD_DIM), BF16),
            jax.ShapeDtypeStruct((b, ng, s // CHUNK, HEAD_DIM, CHUNK), BF16),
            jax.ShapeDtypeStruct((b, ng, s, HEAD_DIM), BF16),
            jax.ShapeDtypeStruct((b, ng, s // CHUNK, HEAD_DIM, CHUNK), BF16),
        ],
        compiler_params=_cparams(("parallel", "parallel")),
        name="kv_proj",
    )(h, g.reshape(1, d), w, k_norm)


def _cmp_kernel(x_ref, pos_ref, w1_ref, b1_ref, w2_ref, kn_ref, o_ref, *, normalize):
    x = (x_ref[...] + pos_ref[...]).astype(BF16)
    hid = jax.nn.gelu(_dot(x, w1_ref[...]) + b1_ref[...])
    y = _dot(hid.astype(BF16), w2_ref[...])
    if normalize:
        y = _rms(y, kn_ref[...])
    o_ref[...] = y.astype(o_ref.dtype)


def _compress(xb, pos, w1, b1, w2, kn, *, normalize, tm=512):
    m, kdim = xb.shape
    hid = w1.shape[1]
    dk = w2.shape[1]
    return pl.pallas_call(
        functools.partial(_cmp_kernel, normalize=normalize),
        grid=(m // tm,),
        in_specs=[
            pl.BlockSpec((tm, kdim), lambda i: (i, 0)),
            pl.BlockSpec((1, kdim), lambda i: (0, 0)),
            pl.BlockSpec((kdim, hid), lambda i: (0, 0)),
            pl.BlockSpec((1, hid), lambda i: (0, 0)),
            pl.BlockSpec((hid, dk), lambda i: (0, 0)),
            pl.BlockSpec((1, dk), lambda i: (0, 0)),
        ],
        out_specs=pl.BlockSpec((tm, dk), lambda i: (i, 0)),
        out_shape=jax.ShapeDtypeStruct((m, dk), BF16),
        compiler_params=_cparams(("parallel",)),
        name="compress",
    )(xb, pos.reshape(1, kdim), w1, b1.reshape(1, hid), w2, kn.reshape(1, dk))


def _qg_kernel(x_ref, g_ref, w_ref, qn_ref, q_ref, gate_ref, *, dq):
    xn = _rms(x_ref[...], g_ref[...]).astype(BF16)
    y = _dot(xn, w_ref[...])
    qn = qn_ref[...] * (HEAD_DIM ** -0.5 * LOG2E)
    for hd in range(N_HEADS):
        off = hd * HEAD_DIM
        q_ref[:, off:off + HEAD_DIM] = _rms(y[:, off:off + HEAD_DIM], qn).astype(q_ref.dtype)
    gate_ref[...] = jax.nn.sigmoid(y[:, dq:])


def _qg_proj(h, g, w, q_norm, *, tm=512):
    t, d = h.shape
    n = w.shape[1]
    dq = N_HEADS * HEAD_DIM
    return pl.pallas_call(
        functools.partial(_qg_kernel, dq=dq),
        grid=(t // tm,),
        in_specs=[
            pl.BlockSpec((tm, d), lambda i: (i, 0)),
            pl.BlockSpec((1, d), lambda i: (0, 0)),
            pl.BlockSpec((d, n), lambda i: (0, 0)),
            pl.BlockSpec((1, HEAD_DIM), lambda i: (0, 0)),
        ],
        out_specs=[pl.BlockSpec((tm, dq), lambda i: (i, 0)),
                   pl.BlockSpec((tm, n - dq), lambda i: (i, 0))],
        out_shape=[jax.ShapeDtypeStruct((t, dq), BF16),
                   jax.ShapeDtypeStruct((t, n - dq), F32)],
        compiler_params=_cparams(("parallel",)),
        name="qg_proj",
    )(h, g.reshape(1, d), w, q_norm.reshape(1, HEAD_DIM))


def _split3(x):
    hi = x.astype(BF16)
    r = x - hi.astype(F32)
    mid = r.astype(BF16)
    lo = (r - mid.astype(F32)).astype(BF16)
    return hi, mid, lo


def _nsa_kernel(q_ref, gate_ref, kc_ref, vct_ref, ks_ref, vst_ref, kw_ref, vwt_ref, ovt_ref, eye_ref,
                o_ref, qa_ref, m_ref, l_ref, acc_ref, out_ref, *, n_cmp, n_blk):
    hpg = HEADS_PER_GROUP
    tq = CHUNK
    dk = HEAD_DIM
    i = pl.program_id(2)
    t0 = i * tq

    q = q_ref[0]
    eye = eye_ref[...]
    for h in range(hpg):
        qa_ref[h, 0:dk, :] = _dot_nt(eye, q[:, h * dk:(h + 1) * dk]).astype(BF16)
    gate_t = gate_ref[0].T

    ncp = kc_ref.shape[2]
    kc = kc_ref[0, 0]
    vct = vct_ref[0, 0]
    nidx = lax.broadcasted_iota(jnp.int32, (ncp, tq), 0)
    tpos = t0 + lax.broadcasted_iota(jnp.int32, (ncp, tq), 1)
    valid = (nidx * CMP_STRIDE + (CMP_BLOCK - 1) <= tpos) & (nidx < n_cmp)
    psum = None
    for h in range(hpg):
        s = jnp.where(valid, _dot(kc, qa_ref[h, 0:dk, :]), MASK_SCORE)
        e = jnp.exp2(s - jnp.max(s, axis=0, keepdims=True))
        p = jnp.where(valid, e * (1.0 / jnp.sum(e, axis=0, keepdims=True)), 0.0)
        c0 = h * N_BRANCH
        out_ref[h * dk:(h + 1) * dk, :] = gate_t[c0:c0 + 1, :] * _dot(vct, p.astype(BF16))
        psum = p if psum is None else psum + p

    ovt = ovt_ref[...]
    imp = None
    for part in _split3(psum):
        term = _dot(ovt, part)
        imp = term if imp is None else imp + term
    jblk = lax.broadcasted_iota(jnp.int32, (n_blk, tq), 0)
    cur = (t0 + lax.broadcasted_iota(jnp.int32, (n_blk, tq), 1)) // SEL_BLOCK
    forced = (jblk == 0) | (jblk == cur) | (jblk == cur - 1)
    imp = jnp.where(forced, FORCE_SCORE, imp)
    imp = jnp.where(jblk > cur, MASK_SCORE, imp)
    rank = jnp.zeros((n_blk, tq), F32)
    for ii in range(n_blk):
        row = imp[ii:ii + 1, :]
        rank = rank + jnp.where(row > imp, 1.0, 0.0)
        rank = rank + jnp.where((row == imp) & (jblk > ii), 1.0, 0.0)
    bias = jnp.where(rank < float(min(N_SELECT, n_blk)), 0.0, MASK_SCORE)
    bias = jnp.concatenate([bias, jnp.zeros((dk - n_blk, tq), F32)], axis=0).astype(BF16)
    for h in range(hpg):
        qa_ref[h, dk:2 * dk, :] = bias

    kk = lax.broadcasted_iota(jnp.int32, (tq, tq), 0)
    tt = lax.broadcasted_iota(jnp.int32, (tq, tq), 1)

    def flash_init():
        m_ref[...] = jnp.full(m_ref.shape, MASK_SCORE, F32)
        l_ref[...] = jnp.zeros(l_ref.shape, F32)
        acc_ref[...] = jnp.zeros(acc_ref.shape, F32)

    def flash_step(k, vt, qrows, mode):
        for h in range(hpg):
            s = _dot(k, qa_ref[h, 0:qrows, :])
            if mode == "causal":
                s = jnp.where(kk <= tt, s, MASK_SCORE)
            elif mode == "anti":
                s = jnp.where(kk > tt, s, MASK_SCORE)
            m_prev = m_ref[h]
            m_new = jnp.maximum(m_prev, jnp.max(s, axis=0, keepdims=True))
            alpha = jnp.exp2(m_prev - m_new)
            p = jnp.exp2(s - m_new)
            l_ref[h] = alpha * l_ref[h] + jnp.sum(p, axis=0, keepdims=True)
            acc_ref[h] = alpha * acc_ref[h] + _dot(vt, p.astype(BF16))
            m_ref[h] = m_new

    def flash_out(branch):
        for h in range(hpg):
            c0 = h * N_BRANCH + branch
            o = acc_ref[h] * (1.0 / l_ref[h])
            out_ref[h * dk:(h + 1) * dk, :] += gate_t[c0:c0 + 1, :] * o

    def key_rows(ref, c):
        return ref[0, 0, pl.ds(pl.multiple_of(c * tq, tq), tq), :]

    flash_init()

    def sel_body(c, carry):
        flash_step(key_rows(ks_ref, c), vst_ref[0, 0, c], 2 * dk, "none")
        return carry

    lax.fori_loop(0, i, sel_body, 0)
    flash_step(key_rows(ks_ref, i), vst_ref[0, 0, i], 2 * dk, "causal")
    flash_out(1)

    flash_init()
    nw = WINDOW // tq

    @pl.when(i >= nw)
    def _():
        flash_step(key_rows(kw_ref, i - nw), vwt_ref[0, 0, i - nw], dk, "anti")

    for back in range(nw - 1, 0, -1):
        @pl.when(i >= back)
        def _():
            flash_step(key_rows(kw_ref, i - back), vwt_ref[0, 0, i - back], dk, "none")

    flash_step(key_rows(kw_ref, i), vwt_ref[0, 0, i], dk, "causal")
    flash_out(2)

    o_ref[0] = out_ref[...].T.astype(o_ref.dtype)


def _nsa(q, gate, kc, vct, ks, vst, kw, vwt):
    b, s, dq = q.shape
    g = N_KV_GROUPS
    hpg = HEADS_PER_GROUP
    dk = HEAD_DIM
    gd = hpg * dk
    tq = CHUNK
    n_blk = s // SEL_BLOCK
    n_cmp = (s - CMP_BLOCK) // CMP_STRIDE + 1
    ncp = kc.shape[2]
    assert WINDOW % tq == 0 and n_blk <= dk
    cmp_start = np.arange(ncp) * CMP_STRIDE
    blk_start = np.arange(n_blk) * SEL_BLOCK
    ovt = ((cmp_start[None, :] < blk_start[:, None] + SEL_BLOCK)
           & (cmp_start[None, :] + CMP_BLOCK > blk_start[:, None])
           & (np.arange(ncp)[None, :] < n_cmp)).astype(np.float32)
    idx = lambda bi, gi, i: (bi, gi, 0, 0)
    idx5 = lambda bi, gi, i: (bi, gi, 0, 0, 0)
    return pl.pallas_call(
        functools.partial(_nsa_kernel, n_cmp=n_cmp, n_blk=n_blk),
        grid=(b, g, s // tq),
        in_specs=[
            pl.BlockSpec((1, tq, gd), lambda bi, gi, i: (bi, i, gi)),
            pl.BlockSpec((1, tq, LANES), lambda bi, gi, i: (bi, i, gi)),
            pl.BlockSpec((1, 1, ncp, dk), idx),
            pl.BlockSpec((1, 1, dk, ncp), idx),
            pl.BlockSpec((1, 1, s, 2 * dk), idx),
            pl.BlockSpec((1, 1, s // tq, dk, tq), idx5),
            pl.BlockSpec((1, 1, s, dk), idx),
            pl.BlockSpec((1, 1, s // tq, dk, tq), idx5),
            pl.BlockSpec((n_blk, ncp), lambda bi, gi, i: (0, 0)),
            pl.BlockSpec((dk, dk), lambda bi, gi, i: (0, 0)),
        ],
        out_specs=pl.BlockSpec((1, tq, gd), lambda bi, gi, i: (bi, i, gi)),
        out_shape=jax.ShapeDtypeStruct((b, s, dq), BF16),
        scratch_shapes=[
            pltpu.VMEM((hpg, 2 * dk, tq), BF16),
            pltpu.VMEM((hpg, 1, tq), F32),
            pltpu.VMEM((hpg, 1, tq), F32),
            pltpu.VMEM((hpg, dk, tq), F32),
            pltpu.VMEM((gd, tq), F32),
        ],
        compiler_params=_cparams(("parallel", "parallel", "arbitrary")),
        name="nsa",
    )(q, gate, kc, vct, ks, vst, kw, vwt, jnp.asarray(ovt, BF16), jnp.eye(dk, dtype=BF16))


def _oproj_kernel(o_ref, w_ref, h_ref, out_ref):
    out_ref[...] = h_ref[...] + _dot(o_ref[...], w_ref[...])


def _out_proj(o, w, h, *, tm=512):
    t, d = h.shape
    k = o.shape[1]
    return pl.pallas_call(
        _oproj_kernel,
        grid=(t // tm,),
        in_specs=[
            pl.BlockSpec((tm, k), lambda i: (i, 0)),
            pl.BlockSpec((k, d), lambda i: (0, 0)),
            pl.BlockSpec((tm, d), lambda i: (i, 0)),
        ],
        out_specs=pl.BlockSpec((tm, d), lambda i: (i, 0)),
        out_shape=jax.ShapeDtypeStruct((t, d), F32),
        compiler_params=_cparams(("parallel",)),
        name="out_proj",
    )(o, w, h)


def _cmp_blocks(x):
    b, g, s, dk = x.shape
    width = CMP_BLOCK * dk
    n_even = s // CMP_BLOCK
    even = x.reshape(b, g, n_even, width)
    odd = x[:, :, CMP_STRIDE:s - CMP_STRIDE].reshape(b, g, n_even - 1, width)
    odd = jnp.concatenate([odd, jnp.zeros((b, g, 1, width), x.dtype)], axis=2)
    blocks = jnp.stack([even, odd], axis=3).reshape(b, g, 2 * n_even, width)
    return blocks.reshape(b * g * 2 * n_even, width)


def _gate_columns(w_qg):
    dq = N_HEADS * HEAD_DIM
    per = HEADS_PER_GROUP * N_BRANCH
    wg = w_qg[:, dq:].reshape(-1, N_KV_GROUPS, per)
    wg = jnp.pad(wg, ((0, 0), (0, 0), (0, LANES - per)))
    return jnp.concatenate([w_qg[:, :dq], wg.reshape(-1, N_KV_GROUPS * LANES)], axis=1)


def kernel(x, ffn_norm, ffn_w_gate_up, ffn_w_down, mix_norm, conv_w_in, conv_w, conv_w_out, kv_norm, kv_w,
           cmp_pos, cmp_w1, cmp_b1, cmp_w2, k_norm, nsa_w_qg, q_norm, nsa_w_o):
    b, s, d = x.shape
    t = b * s
    depth = ffn_norm.shape[0]
    n_a = conv_w_in.shape[0]
    g = N_KV_GROUPS

    def ffn(h2, layer, j):
        return _ffn(h2, ffn_norm[layer, j], ffn_w_gate_up[layer, j].astype(BF16),
                    ffn_w_down[layer, j].astype(BF16))

    h = x.reshape(t, d)
    shared = None
    for layer in range(depth):
        h = ffn(h, layer, 0)
        if layer < n_a:
            h = _conv_mixer(h.reshape(b, s, d), mix_norm[layer], conv_w_in[layer].astype(BF16),
                            conv_w[layer], conv_w_out[layer].astype(BF16)).reshape(t, d)
        else:
            li = layer - n_a
            q, gate = _qg_proj(h, mix_norm[layer], _gate_columns(nsa_w_qg[li]).astype(BF16), q_norm[li])
            o = _nsa(q.reshape(b, s, -1), gate.reshape(b, s, -1), *shared)
            h = _out_proj(o.reshape(t, -1), nsa_w_o[li].astype(BF16), h)
        h = ffn(h, layer, 1)
        if layer == n_a - 1:
            raw, ks, vst, kw, vwt = _kv_proj(h.reshape(b, s, d), kv_norm, kv_w.astype(BF16), k_norm)
            raw = raw.reshape(b, s, 2, g, HEAD_DIM).transpose(2, 0, 3, 1, 4)
            cmp = []
            for c in range(2):
                y = _compress(_cmp_blocks(raw[c]), cmp_pos[c], cmp_w1[c].astype(BF16), cmp_b1[c],
                              cmp_w2[c].astype(BF16), k_norm[0], normalize=(c == 0))
                cmp.append(y.reshape(b, g, -1, HEAD_DIM))
            shared = (cmp[0], cmp[1].transpose(0, 1, 3, 2), ks, vst, kw, vwt)
    return h.reshape(b, s, d)
```

```python
import functools

import numpy as np
import jax
import jax.numpy as jnp
from jax import lax
from jax.experimental import pallas as pl
from jax.experimental.pallas import tpu as pltpu

EPS = 1e-6
N_HEADS = 16
HEAD_DIM = 64
N_KV_GROUPS = 4
HEADS_PER_GROUP = N_HEADS // N_KV_GROUPS
N_KV_SLOTS = 6
N_BRANCH = 3
CMP_BLOCK = 32
CMP_STRIDE = 16
SEL_BLOCK = 64
N_SELECT = 16
WINDOW = 512
FORCE_SCORE = 1e9
MASK_SCORE = -1e30
LOG2E = 1.4426950408889634

LANES = 128
CHUNK = 256
VMEM_LIMIT = 56 * 1024 * 1024

BF16 = jnp.bfloat16
F32 = jnp.float32


def _cparams(sem):
    return pltpu.CompilerParams(dimension_semantics=sem, vmem_limit_bytes=VMEM_LIMIT)


def _rms(x, g):
    ms = jnp.mean(x * x, axis=-1, keepdims=True)
    return x * lax.rsqrt(ms + EPS) * g


def _dot(a, b):
    return jnp.dot(a, b, preferred_element_type=F32)


def _dot_nt(a, b):
    return lax.dot_general(a, b, (((1,), (1,)), ((), ())), preferred_element_type=F32)


def _ffn_kernel(x_ref, g_ref, wa_ref, wb_ref, wd_ref, o_ref, xn_ref, acc_ref, *, nf):
    f = pl.program_id(1)

    @pl.when(f == 0)
    def _():
        xn_ref[...] = _rms(x_ref[...], g_ref[...]).astype(BF16)
        acc_ref[...] = jnp.zeros_like(acc_ref)

    xn = xn_ref[...]
    a = _dot(xn, wa_ref[...])
    b = _dot(xn, wb_ref[...])
    h = (a * jax.nn.sigmoid(a) * b).astype(BF16)
    acc_ref[...] += _dot(h, wd_ref[...])

    @pl.when(f == nf - 1)
    def _():
        o_ref[...] = x_ref[...] + 0.5 * acc_ref[...]


def _ffn(h, g, w_gu, w_d, *, tm=512, tf=1408):
    t, d = h.shape
    dff = w_d.shape[0]
    nf = dff // tf
    return pl.pallas_call(
        functools.partial(_ffn_kernel, nf=nf),
        grid=(t // tm, nf),
        in_specs=[
            pl.BlockSpec((tm, d), lambda i, f: (i, 0)),
            pl.BlockSpec((1, d), lambda i, f: (0, 0)),
            pl.BlockSpec((d, tf), lambda i, f: (0, f)),
            pl.BlockSpec((d, tf), lambda i, f: (0, f + nf)),
            pl.BlockSpec((tf, d), lambda i, f: (f, 0)),
        ],
        out_specs=pl.BlockSpec((tm, d), lambda i, f: (i, 0)),
        out_shape=jax.ShapeDtypeStruct((t, d), F32),
        scratch_shapes=[pltpu.VMEM((tm, d), BF16), pltpu.VMEM((tm, d), F32)],
        compiler_params=_cparams(("parallel", "arbitrary")),
        name="ffn",
    )(h, g.reshape(1, d), w_gu, w_gu, w_d)


CARRY = 8


def _conv_kernel(x_ref, g_ref, win_ref, cw_ref, wout_ref, o_ref, vbuf_ref, *, tm, d):
    j = pl.program_id(1)

    @pl.when(j == 0)
    def _():
        vbuf_ref[0:CARRY, :] = jnp.zeros((CARRY, d), F32)

    x = x_ref[0]
    xn = _rms(x, g_ref[...]).astype(BF16)
    proj = _dot(xn, win_ref[...])
    bg = proj[:, 0:d]
    v = proj[:, d:2 * d] * proj[:, 2 * d:3 * d]
    vbuf_ref[CARRY:CARRY + tm, :] = v
    cw = cw_ref[...]
    conv = (cw[0:1, :] * vbuf_ref[CARRY - 2:CARRY - 2 + tm, :]
            + cw[1:2, :] * vbuf_ref[CARRY - 1:CARRY - 1 + tm, :]
            + cw[2:3, :] * v)
    y = (bg * conv).astype(BF16)
    o_ref[0] = x + _dot(y, wout_ref[...])
    vbuf_ref[0:CARRY, :] = vbuf_ref[tm:tm + CARRY, :]


def _conv_mixer(h, g, w_in, conv_w, w_out, *, tm=512):
    b, s, d = h.shape
    return pl.pallas_call(
        functools.partial(_conv_kernel, tm=tm, d=d),
        grid=(b, s // tm),
        in_specs=[
            pl.BlockSpec((1, tm, d), lambda i, j: (i, j, 0)),
            pl.BlockSpec((1, d), lambda i, j: (0, 0)),
            pl.BlockSpec((d, 3 * d), lambda i, j: (0, 0)),
            pl.BlockSpec((conv_w.shape[0], d), lambda i, j: (0, 0)),
            pl.BlockSpec((d, d), lambda i, j: (0, 0)),
        ],
        out_specs=pl.BlockSpec((1, tm, d), lambda i, j: (i, j, 0)),
        out_shape=jax.ShapeDtypeStruct((b, s, d), F32),
        scratch_shapes=[pltpu.VMEM((CARRY + tm, d), F32)],
        compiler_params=_cparams(("arbitrary", "arbitrary")),
        name="conv_mixer",
    )(h, g.reshape(1, d), w_in, conv_w, w_out)


def _kv_kernel(x_ref, g_ref, w_ref, kn_ref, raw_ref, ks_ref, vst_ref, kw_ref, vwt_ref, *, tm):
    j = pl.program_id(1)
    xn = _rms(x_ref[0], g_ref[...]).astype(BF16)
    y = _dot(xn, w_ref[...])
    gw = N_KV_GROUPS * HEAD_DIM
    raw_ref[0] = y[:, 0:2 * gw]
    pos = j * tm + lax.broadcasted_iota(jnp.int32, (tm, HEAD_DIM), 0)
    lane = lax.broadcasted_iota(jnp.int32, (tm, HEAD_DIM), 1)
    onehot = jnp.where(lane == pos // SEL_BLOCK, 1.0, 0.0).astype(BF16)
    for grp in range(N_KV_GROUPS):
        off = 2 * gw + grp * HEAD_DIM
        ks_ref[0, grp, :, 0:HEAD_DIM] = _rms(y[:, off:off + HEAD_DIM], kn_ref[1:2, :]).astype(BF16)
        ks_ref[0, grp, :, HEAD_DIM:2 * HEAD_DIM] = onehot
        off = 4 * gw + grp * HEAD_DIM
        kw_ref[0, grp] = _rms(y[:, off:off + HEAD_DIM], kn_ref[2:3, :]).astype(BF16)
    for slot, ref in ((3, vst_ref), (5, vwt_ref)):
        vt = y[:, slot * gw:(slot + 1) * gw].T
        for grp in range(N_KV_GROUPS):
            for cc in range(tm // CHUNK):
                ref[0, grp, cc] = vt[grp * HEAD_DIM:(grp + 1) * HEAD_DIM,
                                     cc * CHUNK:(cc + 1) * CHUNK].astype(BF16)


def _kv_proj(h, g, w, k_norm, *, tm=512):
    b, s, d = h.shape
    n = w.shape[1]
    ng = N_KV_GROUPS
    gw = ng * HEAD_DIM
    cpt = tm // CHUNK
    assert s // SEL_BLOCK <= HEAD_DIM
    return pl.pallas_call(
        functools.partial(_kv_kernel, tm=tm),
        grid=(b, s // tm),
        in_specs=[
            pl.BlockSpec((1, tm, d), lambda i, j: (i, j, 0)),
            pl.BlockSpec((1, d), lambda i, j: (0, 0)),
            pl.BlockSpec((d, n), lambda i, j: (0, 0)),
            pl.BlockSpec(k_norm.shape, lambda i, j: (0, 0)),
        ],
        out_specs=[
            pl.BlockSpec((1, tm, 2 * gw), lambda i, j: (i, j, 0)),
            pl.BlockSpec((1, ng, tm, 2 * HEAD_DIM), lambda i, j: (i, 0, j, 0)),
            pl.BlockSpec((1, ng, cpt, HEAD_DIM, CHUNK), lambda i, j: (i, 0, j, 0, 0)),
            pl.BlockSpec((1, ng, tm, HEAD_DIM), lambda i, j: (i, 0, j, 0)),
            pl.BlockSpec((1, ng, cpt, HEAD_DIM, CHUNK), lambda i, j: (i, 0, j, 0, 0)),
        ],
        out_shape=[
            jax.ShapeDtypeStruct((b, s, 2 * gw), F32),
            jax.ShapeDtypeStruct((b, ng, s, 2 * HEAD_DIM), BF16),
            jax.ShapeDtypeStruct((b, ng, s // CHUNK, HEAD_DIM, CHUNK), BF16),
            jax.ShapeDtypeStruct((b, ng, s, HEAD_DIM), BF16),
            jax.ShapeDtypeStruct((b, ng, s // CHUNK, HEAD_DIM, CHUNK), BF16),
        ],
        compiler_params=_cparams(("parallel", "parallel")),
        name="kv_proj",
    )(h, g.reshape(1, d), w, k_norm)


def _cmp_kernel(x_ref, pos_ref, w1_ref, b1_ref, w2_ref, kn_ref, o_ref, *, normalize):
    x = (x_ref[...] + pos_ref[...]).astype(BF16)
    hid = jax.nn.gelu(_dot(x, w1_ref[...]) + b1_ref[...])
    y = _dot(hid.astype(BF16), w2_ref[...])
    if normalize:
        y = _rms(y, kn_ref[...])
    o_ref[...] = y.astype(o_ref.dtype)


def _compress(xb, pos, w1, b1, w2, kn, *, normalize, tm=512):
    m, kdim = xb.shape
    hid = w1.shape[1]
    dk = w2.shape[1]
    return pl.pallas_call(
        functools.partial(_cmp_kernel, normalize=normalize),
        grid=(m // tm,),
        in_specs=[
            pl.BlockSpec((tm, kdim), lambda i: (i, 0)),
            pl.BlockSpec((1, kdim), lambda i: (0, 0)),
            pl.BlockSpec((kdim, hid), lambda i: (0, 0)),
            pl.BlockSpec((1, hid), lambda i: (0, 0)),
            pl.BlockSpec((hid, dk), lambda i: (0, 0)),
            pl.BlockSpec((1, dk), lambda i: (0, 0)),
        ],
        out_specs=pl.BlockSpec((tm, dk), lambda i: (i, 0)),
        out_shape=jax.ShapeDtypeStruct((m, dk), BF16),
        compiler_params=_cparams(("parallel",)),
        name="compress",
    )(xb, pos.reshape(1, kdim), w1, b1.reshape(1, hid), w2, kn.reshape(1, dk))


def _qg_kernel(x_ref, g_ref, w_ref, qn_ref, q_ref, gate_ref, *, dq):
    xn = _rms(x_ref[...], g_ref[...]).astype(BF16)
    y = _dot(xn, w_ref[...])
    qn = qn_ref[...] * (HEAD_DIM ** -0.5 * LOG2E)
    for hd in range(N_HEADS):
        off = hd * HEAD_DIM
        q_ref[:, off:off + HEAD_DIM] = _rms(y[:, off:off + HEAD_DIM], qn).astype(q_ref.dtype)
    gate_ref[...] = jax.nn.sigmoid(y[:, dq:])


def _qg_proj(h, g, w, q_norm, *, tm=512):
    t, d = h.shape
    n = w.shape[1]
    dq = N_HEADS * HEAD_DIM
    return pl.pallas_call(
        functools.partial(_qg_kernel, dq=dq),
        grid=(t // tm,),
        in_specs=[
            pl.BlockSpec((tm, d), lambda i: (i, 0)),
            pl.BlockSpec((1, d), lambda i: (0, 0)),
            pl.BlockSpec((d, n), lambda i: (0, 0)),
            pl.BlockSpec((1, HEAD_DIM), lambda i: (0, 0)),
        ],
        out_specs=[pl.BlockSpec((tm, dq), lambda i: (i, 0)),
                   pl.BlockSpec((tm, n - dq), lambda i: (i, 0))],
        out_shape=[jax.ShapeDtypeStruct((t, dq), BF16),
                   jax.ShapeDtypeStruct((t, n - dq), F32)],
        compiler_params=_cparams(("parallel",)),
        name="qg_proj",
    )(h, g.reshape(1, d), w, q_norm.reshape(1, HEAD_DIM))


def _split3(x):
    hi = x.astype(BF16)
    r = x - hi.astype(F32)
    mid = r.astype(BF16)
    lo = (r - mid.astype(F32)).astype(BF16)
    return hi, mid, lo


def _nsa_kernel(q_ref, gate_ref, kc_ref, vct_ref, ks_ref, vst_ref, kw_ref, vwt_ref, ovt_ref, eye_ref,
                o_ref, qa_ref, m_ref, l_ref, acc_ref, out_ref, s_ref, cmax_ref, *, n_cmp, n_blk):
    hpg = HEADS_PER_GROUP
    tq = CHUNK
    dk = HEAD_DIM
    wide = hpg * tq
    i = pl.program_id(2)
    t0 = i * tq

    def heads(fn):
        return jnp.concatenate([fn(h) for h in range(hpg)], axis=1)

    q = q_ref[0]
    eye = eye_ref[...]
    qa_ref[0:dk, :] = heads(lambda h: _dot_nt(eye, q[:, h * dk:(h + 1) * dk]).astype(BF16))
    gate_t = gate_ref[0].T
    gates = [heads(lambda h: gate_t[h * N_BRANCH + br:h * N_BRANCH + br + 1, :]) for br in range(N_BRANCH)]

    kk = lax.broadcasted_iota(jnp.int32, (tq, wide), 0)
    tt = lax.broadcasted_iota(jnp.int32, (tq, wide), 1) & (tq - 1)

    m_ref[...] = jnp.full(m_ref.shape, MASK_SCORE, F32)
    l_ref[...] = jnp.zeros(l_ref.shape, F32)
    acc_ref[...] = jnp.zeros(acc_ref.shape, F32)

    def key_rows(ref, c):
        return ref[0, 0, pl.ds(pl.multiple_of(c * tq, tq), tq), :]

    def produce(b, k, qrows, keep):
        s = _dot(k, qa_ref[0:qrows, :])
        if keep is not None:
            s = jnp.where(keep, s, MASK_SCORE)
        s_ref[b] = s
        cmax_ref[b] = jnp.max(s, axis=0, keepdims=True)

    def consume(b, vt, st):
        m_prev = m_ref[st]
        m_new = jnp.maximum(m_prev, cmax_ref[b])
        alpha = jnp.exp2(m_prev - m_new)
        p = jnp.exp2(s_ref[b] - m_new)
        l_ref[st] = alpha * l_ref[st] + jnp.sum(p, axis=0, keepdims=True)
        acc_ref[st] = alpha * acc_ref[st] + _dot(vt, p.astype(BF16))
        m_ref[st] = m_new

    def flash_out(st, branch):
        out_ref[...] += gates[branch] * (acc_ref[st] * (1.0 / l_ref[st]))

    assert WINDOW == 2 * tq
    lo = jnp.maximum(i - 2, 0)
    mid = jnp.maximum(i - 1, 0)
    produce(0, key_rows(kw_ref, lo), dk, kk > tt + jnp.where(i >= 2, 0, tq))

    ncp = kc_ref.shape[2]
    nidx = lax.broadcasted_iota(jnp.int32, (ncp, wide), 0)
    tpos = t0 + (lax.broadcasted_iota(jnp.int32, (ncp, wide), 1) & (tq - 1))
    valid = (nidx * CMP_STRIDE + (CMP_BLOCK - 1) <= tpos) & (nidx < n_cmp)
    s = jnp.where(valid, _dot(kc_ref[0, 0], qa_ref[0:dk, :]), MASK_SCORE)
    e = jnp.exp2(s - jnp.max(s, axis=0, keepdims=True))
    p = jnp.where(valid, e * (1.0 / jnp.sum(e, axis=0, keepdims=True)), 0.0)
    out_ref[...] = gates[0] * _dot(vct_ref[0, 0], p.astype(BF16))
    psum = p[:, 0:tq]
    for h in range(1, hpg):
        psum = psum + p[:, h * tq:(h + 1) * tq]

    produce(1, key_rows(kw_ref, mid), dk, kk > tt + jnp.where(i >= 1, -tq, tq))
    consume(0, vwt_ref[0, 0, lo], 1)

    ovt = ovt_ref[...]
    imp = None
    for part in _split3(psum):
        term = _dot(ovt, part)
        imp = term if imp is None else imp + term
    jblk = lax.broadcasted_iota(jnp.int32, (n_blk, tq), 0)
    cur = (t0 + lax.broadcasted_iota(jnp.int32, (n_blk, tq), 1)) // SEL_BLOCK
    forced = (jblk == 0) | (jblk == cur) | (jblk == cur - 1)
    imp = jnp.where(forced, FORCE_SCORE, imp)
    imp = jnp.where(jblk > cur, MASK_SCORE, imp)
    rank = jnp.zeros((n_blk, tq), F32)
    for ii in range(n_blk):
        row = imp[ii:ii + 1, :]
        rank = rank + jnp.where(row > imp, 1.0, 0.0)
        rank = rank + jnp.where((row == imp) & (jblk > ii), 1.0, 0.0)
    chosen = (rank < float(min(N_SELECT, n_blk))) & (jblk <= cur)
    bias = jnp.where(chosen, 0.0, MASK_SCORE)
    bias = jnp.concatenate([bias, jnp.zeros((dk - n_blk, tq), F32)], axis=0).astype(BF16)
    qa_ref[dk:2 * dk, :] = heads(lambda h: bias)

    produce(0, key_rows(kw_ref, i), dk, kk <= tt)
    consume(1, vwt_ref[0, 0, mid], 1)

    npair = i // 2
    held = jnp.where(i >= 2, 2 * npair - 1, i)
    last = jnp.where(i % 2 == 0, i + 1, i - 1)
    produce(1, key_rows(ks_ref, i), 2 * dk, kk <= tt)
    consume(0, vwt_ref[0, 0, i], 1)
    flash_out(1, 2)

    def pair_body(p, carry):
        c = 2 * p
        produce(0, key_rows(ks_ref, c), 2 * dk, None)
        consume(1, vst_ref[0, 0, jnp.where(p == 0, i, c - 1)], 0)
        produce(1, key_rows(ks_ref, c + 1), 2 * dk, None)
        consume(0, vst_ref[0, 0, c], 0)
        return carry

    lax.fori_loop(0, npair, pair_body, 0)
    produce(0, key_rows(ks_ref, last), 2 * dk, None)
    consume(1, vst_ref[0, 0, held], 0)
    consume(0, vst_ref[0, 0, last], 0)
    flash_out(0, 1)

    out = out_ref[...]
    out = jnp.concatenate([out[:, h * tq:(h + 1) * tq] for h in range(hpg)], axis=0)
    o_ref[0] = out.T.astype(o_ref.dtype)


def _nsa(q, gate, kc, vct, ks, vst, kw, vwt):
    b, s, dq = q.shape
    g = N_KV_GROUPS
    hpg = HEADS_PER_GROUP
    dk = HEAD_DIM
    gd = hpg * dk
    tq = CHUNK
    n_blk = s // SEL_BLOCK
    n_cmp = (s - CMP_BLOCK) // CMP_STRIDE + 1
    ncp = kc.shape[2]
    assert WINDOW == 2 * tq and n_blk <= dk and (s // tq) % 2 == 0
    cmp_start = np.arange(ncp) * CMP_STRIDE
    blk_start = np.arange(n_blk) * SEL_BLOCK
    ovt = ((cmp_start[None, :] < blk_start[:, None] + SEL_BLOCK)
           & (cmp_start[None, :] + CMP_BLOCK > blk_start[:, None])
           & (np.arange(ncp)[None, :] < n_cmp)).astype(np.float32)
    idx = lambda bi, gi, i: (bi, gi, 0, 0)
    idx5 = lambda bi, gi, i: (bi, gi, 0, 0, 0)
    return pl.pallas_call(
        functools.partial(_nsa_kernel, n_cmp=n_cmp, n_blk=n_blk),
        grid=(b, g, s // tq),
        in_specs=[
            pl.BlockSpec((1, tq, gd), lambda bi, gi, i: (bi, i, gi)),
            pl.BlockSpec((1, tq, LANES), lambda bi, gi, i: (bi, i, gi)),
            pl.BlockSpec((1, 1, ncp, dk), idx),
            pl.BlockSpec((1, 1, dk, ncp), idx),
            pl.BlockSpec((1, 1, s, 2 * dk), idx),
            pl.BlockSpec((1, 1, s // tq, dk, tq), idx5),
            pl.BlockSpec((1, 1, s, dk), idx),
            pl.BlockSpec((1, 1, s // tq, dk, tq), idx5),
            pl.BlockSpec((n_blk, ncp), lambda bi, gi, i: (0, 0)),
            pl.BlockSpec((dk, dk), lambda bi, gi, i: (0, 0)),
        ],
        out_specs=pl.BlockSpec((1, tq, gd), lambda bi, gi, i: (bi, i, gi)),
        out_shape=jax.ShapeDtypeStruct((b, s, dq), BF16),
        scratch_shapes=[
            pltpu.VMEM((2 * dk, hpg * tq), BF16),
            pltpu.VMEM((2, 1, hpg * tq), F32),
            pltpu.VMEM((2, 1, hpg * tq), F32),
            pltpu.VMEM((2, dk, hpg * tq), F32),
            pltpu.VMEM((dk, hpg * tq), F32),
            pltpu.VMEM((2, tq, hpg * tq), F32),
            pltpu.VMEM((2, 1, hpg * tq), F32),
        ],
        compiler_params=_cparams(("parallel", "parallel", "arbitrary")),
        name="nsa",
    )(q, gate, kc, vct, ks, vst, kw, vwt, jnp.asarray(ovt, BF16), jnp.eye(dk, dtype=BF16))


def _oproj_kernel(o_ref, w_ref, h_ref, out_ref):
    out_ref[...] = h_ref[...] + _dot(o_ref[...], w_ref[...])


def _out_proj(o, w, h, *, tm=512):
    t, d = h.shape
    k = o.shape[1]
    return pl.pallas_call(
        _oproj_kernel,
        grid=(t // tm,),
        in_specs=[
            pl.BlockSpec((tm, k), lambda i: (i, 0)),
            pl.BlockSpec((k, d), lambda i: (0, 0)),
            pl.BlockSpec((tm, d), lambda i: (i, 0)),
        ],
        out_specs=pl.BlockSpec((tm, d), lambda i: (i, 0)),
        out_shape=jax.ShapeDtypeStruct((t, d), F32),
        compiler_params=_cparams(("parallel",)),
        name="out_proj",
    )(o, w, h)


def _cmp_blocks(x):
    b, g, s, dk = x.shape
    width = CMP_BLOCK * dk
    n_even = s // CMP_BLOCK
    even = x.reshape(b, g, n_even, width)
    odd = x[:, :, CMP_STRIDE:s - CMP_STRIDE].reshape(b, g, n_even - 1, width)
    odd = jnp.concatenate([odd, jnp.zeros((b, g, 1, width), x.dtype)], axis=2)
    blocks = jnp.stack([even, odd], axis=3).reshape(b, g, 2 * n_even, width)
    return blocks.reshape(b * g * 2 * n_even, width)


def _gate_columns(w_qg):
    dq = N_HEADS * HEAD_DIM
    per = HEADS_PER_GROUP * N_BRANCH
    wg = w_qg[:, dq:].reshape(-1, N_KV_GROUPS, per)
    wg = jnp.pad(wg, ((0, 0), (0, 0), (0, LANES - per)))
    return jnp.concatenate([w_qg[:, :dq], wg.reshape(-1, N_KV_GROUPS * LANES)], axis=1)


def kernel(x, ffn_norm, ffn_w_gate_up, ffn_w_down, mix_norm, conv_w_in, conv_w, conv_w_out, kv_norm, kv_w,
           cmp_pos, cmp_w1, cmp_b1, cmp_w2, k_norm, nsa_w_qg, q_norm, nsa_w_o):
    b, s, d = x.shape
    t = b * s
    depth = ffn_norm.shape[0]
    n_a = conv_w_in.shape[0]
    g = N_KV_GROUPS

    def ffn(h2, layer, j):
        return _ffn(h2, ffn_norm[layer, j], ffn_w_gate_up[layer, j].astype(BF16),
                    ffn_w_down[layer, j].astype(BF16))

    h = x.reshape(t, d)
    shared = None
    for layer in range(depth):
        h = ffn(h, layer, 0)
        if layer < n_a:
            h = _conv_mixer(h.reshape(b, s, d), mix_norm[layer], conv_w_in[layer].astype(BF16),
                            conv_w[layer], conv_w_out[layer].astype(BF16)).reshape(t, d)
        else:
            li = layer - n_a
            q, gate = _qg_proj(h, mix_norm[layer], _gate_columns(nsa_w_qg[li]).astype(BF16), q_norm[li])
            o = _nsa(q.reshape(b, s, -1), gate.reshape(b, s, -1), *shared)
            h = _out_proj(o.reshape(t, -1), nsa_w_o[li].astype(BF16), h)
        h = ffn(h, layer, 1)
        if layer == n_a - 1:
            raw, ks, vst, kw, vwt = _kv_proj(h.reshape(b, s, d), kv_norm, kv_w.astype(BF16), k_norm)
            raw = raw.reshape(b, s, 2, g, HEAD_DIM).transpose(2, 0, 3, 1, 4)
            cmp = []
            for c in range(2):
                y = _compress(_cmp_blocks(raw[c]), cmp_pos[c], cmp_w1[c].astype(BF16), cmp_b1[c],
                              cmp_w2[c].astype(BF16), k_norm[0], normalize=(c == 0))
                cmp.append(y.reshape(b, g, -1, HEAD_DIM))
            shared = (cmp[0], cmp[1].transpose(0, 1, 3, 2), ks, vst, kw, vwt)
    return h.reshape(b, s, d)
```

```python
import functools

import numpy as np
import jax
import jax.numpy as jnp
from jax import lax
from jax.experimental import pallas as pl
from jax.experimental.pallas import tpu as pltpu

EPS = 1e-6
N_HEADS = 16
HEAD_DIM = 64
N_KV_GROUPS = 4
HEADS_PER_GROUP = N_HEADS // N_KV_GROUPS
N_KV_SLOTS = 6
N_BRANCH = 3
CMP_BLOCK = 32
CMP_STRIDE = 16
SEL_BLOCK = 64
N_SELECT = 16
WINDOW = 512
FORCE_SCORE = 1e9
MASK_SCORE = -1e30
LOG2E = 1.4426950408889634

LANES = 128
CHUNK = 256
VMEM_LIMIT = 56 * 1024 * 1024

BF16 = jnp.bfloat16
F32 = jnp.float32


def _cparams(sem):
    return pltpu.CompilerParams(dimension_semantics=sem, vmem_limit_bytes=VMEM_LIMIT)


def _rms(x, g):
    ms = jnp.mean(x * x, axis=-1, keepdims=True)
    return x * lax.rsqrt(ms + EPS) * g


def _dot(a, b):
    return jnp.dot(a, b, preferred_element_type=F32)


def _split2(x):
    hi = x.astype(BF16)
    return hi, (x - hi.astype(F32)).astype(BF16)


def _seg_rms(y, seg, seg_t, gain):
    ss = None
    for part in _split2(y * y):
        term = _dot(part, seg)
        ss = term if ss is None else ss + term
    r = lax.rsqrt(ss * (1.0 / HEAD_DIM) + EPS)
    scale = None
    for part in _split2(r):
        term = _dot(part, seg_t)
        scale = term if scale is None else scale + term
    return y * scale * gain


def _seg_matrices(width):
    ind = (np.arange(width)[:, None] // HEAD_DIM == np.arange(LANES)[None, :]).astype(np.float32)
    return jnp.asarray(ind, BF16), jnp.asarray(ind.T, BF16)


def _dot_nt(a, b):
    return lax.dot_general(a, b, (((1,), (1,)), ((), ())), preferred_element_type=F32)


def _ffn_kernel(x_ref, g_ref, wgu_ref, wd_ref, o_ref, *, dff):
    x = x_ref[...]
    xn = _rms(x, g_ref[...]).astype(BF16)
    ab = _dot(xn, wgu_ref[...])
    a = ab[:, 0:dff]
    h = (a * jax.nn.sigmoid(a) * ab[:, dff:2 * dff]).astype(BF16)
    o_ref[...] = x + 0.5 * _dot(h, wd_ref[...])


def _resident(shape):
    return pl.BlockSpec(shape, lambda *_: (0,) * len(shape), pipeline_mode=pl.Buffered(1))


def _ffn(h, g, w_gu, w_d, *, tm=512):
    t, d = h.shape
    dff = w_d.shape[0]
    return pl.pallas_call(
        functools.partial(_ffn_kernel, dff=dff),
        grid=(t // tm,),
        in_specs=[
            pl.BlockSpec((tm, d), lambda i: (i, 0)),
            _resident((1, d)),
            _resident((d, 2 * dff)),
            _resident((dff, d)),
        ],
        out_specs=pl.BlockSpec((tm, d), lambda i: (i, 0)),
        out_shape=jax.ShapeDtypeStruct((t, d), F32),
        compiler_params=_cparams(("parallel",)),
        name="ffn",
    )(h, g.reshape(1, d), w_gu, w_d)


CARRY = 8


def _conv_kernel(x_ref, g_ref, win_ref, cw_ref, wout_ref, o_ref, vbuf_ref, *, tm, d):
    j = pl.program_id(1)

    @pl.when(j == 0)
    def _():
        vbuf_ref[0:CARRY, :] = jnp.zeros((CARRY, d), F32)

    x = x_ref[0]
    xn = _rms(x, g_ref[...]).astype(BF16)
    proj = _dot(xn, win_ref[...])
    bg = proj[:, 0:d]
    v = proj[:, d:2 * d] * proj[:, 2 * d:3 * d]
    vbuf_ref[CARRY:CARRY + tm, :] = v
    cw = cw_ref[...]
    conv = (cw[0:1, :] * vbuf_ref[CARRY - 2:CARRY - 2 + tm, :]
            + cw[1:2, :] * vbuf_ref[CARRY - 1:CARRY - 1 + tm, :]
            + cw[2:3, :] * v)
    y = (bg * conv).astype(BF16)
    o_ref[0] = x + _dot(y, wout_ref[...])
    vbuf_ref[0:CARRY, :] = vbuf_ref[tm:tm + CARRY, :]


def _conv_mixer(h, g, w_in, conv_w, w_out, *, tm=512):
    b, s, d = h.shape
    return pl.pallas_call(
        functools.partial(_conv_kernel, tm=tm, d=d),
        grid=(b, s // tm),
        in_specs=[
            pl.BlockSpec((1, tm, d), lambda i, j: (i, j, 0)),
            pl.BlockSpec((1, d), lambda i, j: (0, 0)),
            pl.BlockSpec((d, 3 * d), lambda i, j: (0, 0)),
            pl.BlockSpec((conv_w.shape[0], d), lambda i, j: (0, 0)),
            pl.BlockSpec((d, d), lambda i, j: (0, 0)),
        ],
        out_specs=pl.BlockSpec((1, tm, d), lambda i, j: (i, j, 0)),
        out_shape=jax.ShapeDtypeStruct((b, s, d), F32),
        scratch_shapes=[pltpu.VMEM((CARRY + tm, d), F32)],
        compiler_params=_cparams(("arbitrary", "arbitrary")),
        name="conv_mixer",
    )(h, g.reshape(1, d), w_in, conv_w, w_out)


def _kv_kernel(x_ref, g_ref, w_ref, kn_ref, seg_ref, segt_ref, raw_ref, ks_ref, vst_ref, kw_ref, vwt_ref,
               *, tm):
    j = pl.program_id(1)
    xn = _rms(x_ref[0], g_ref[...]).astype(BF16)
    y = _dot(xn, w_ref[...])
    gw = N_KV_GROUPS * HEAD_DIM
    raw_ref[0] = y[:, 0:2 * gw]
    pos = j * tm + lax.broadcasted_iota(jnp.int32, (tm, HEAD_DIM), 0)
    lane = lax.broadcasted_iota(jnp.int32, (tm, HEAD_DIM), 1)
    onehot = jnp.where(lane == pos // SEL_BLOCK, 1.0, 0.0).astype(BF16)
    k_sel = _seg_rms(y[:, 2 * gw:3 * gw], seg_ref[...], segt_ref[...], kn_ref[0:1, :]).astype(BF16)
    k_win = _seg_rms(y[:, 4 * gw:5 * gw], seg_ref[...], segt_ref[...], kn_ref[1:2, :]).astype(BF16)
    for grp in range(N_KV_GROUPS):
        ks_ref[0, grp, :, 0:HEAD_DIM] = k_sel[:, grp * HEAD_DIM:(grp + 1) * HEAD_DIM]
        ks_ref[0, grp, :, HEAD_DIM:2 * HEAD_DIM] = onehot
        kw_ref[0, grp] = k_win[:, grp * HEAD_DIM:(grp + 1) * HEAD_DIM]
    for slot, ref in ((3, vst_ref), (5, vwt_ref)):
        vt = y[:, slot * gw:(slot + 1) * gw].T
        for grp in range(N_KV_GROUPS):
            for cc in range(tm // CHUNK):
                ref[0, grp, cc] = vt[grp * HEAD_DIM:(grp + 1) * HEAD_DIM,
                                     cc * CHUNK:(cc + 1) * CHUNK].astype(BF16)


def _kv_proj(h, g, w, k_norm, *, tm=512):
    b, s, d = h.shape
    n = w.shape[1]
    ng = N_KV_GROUPS
    gw = ng * HEAD_DIM
    cpt = tm // CHUNK
    assert s // SEL_BLOCK <= HEAD_DIM
    seg, seg_t = _seg_matrices(gw)
    gains = jnp.tile(k_norm[1:3], (1, ng))
    return pl.pallas_call(
        functools.partial(_kv_kernel, tm=tm),
        grid=(b, s // tm),
        in_specs=[
            pl.BlockSpec((1, tm, d), lambda i, j: (i, j, 0)),
            _resident((1, d)),
            _resident((d, n)),
            _resident(gains.shape),
            _resident(seg.shape),
            _resident(seg_t.shape),
        ],
        out_specs=[
            pl.BlockSpec((1, tm, 2 * gw), lambda i, j: (i, j, 0)),
            pl.BlockSpec((1, ng, tm, 2 * HEAD_DIM), lambda i, j: (i, 0, j, 0)),
            pl.BlockSpec((1, ng, cpt, HEAD_DIM, CHUNK), lambda i, j: (i, 0, j, 0, 0)),
            pl.BlockSpec((1, ng, tm, HEAD_DIM), lambda i, j: (i, 0, j, 0)),
            pl.BlockSpec((1, ng, cpt, HEAD_DIM, CHUNK), lambda i, j: (i, 0, j, 0, 0)),
        ],
        out_shape=[
            jax.ShapeDtypeStruct((b, s, 2 * gw), F32),
            jax.ShapeDtypeStruct((b, ng, s, 2 * HEAD_DIM), BF16),
            jax.ShapeDtypeStruct((b, ng, s // CHUNK, HEAD_DIM, CHUNK), BF16),
            jax.ShapeDtypeStruct((b, ng, s, HEAD_DIM), BF16),
            jax.ShapeDtypeStruct((b, ng, s // CHUNK, HEAD_DIM, CHUNK), BF16),
        ],
        compiler_params=_cparams(("parallel", "parallel")),
        name="kv_proj",
    )(h, g.reshape(1, d), w, gains, seg, seg_t)


def _cmp_kernel(x_ref, pos_ref, w1_ref, b1_ref, w2_ref, kn_ref, o_ref, *, normalize):
    x = (x_ref[...] + pos_ref[...]).astype(BF16)
    hid = jax.nn.gelu(_dot(x, w1_ref[...]) + b1_ref[...])
    y = _dot(hid.astype(BF16), w2_ref[...])
    if normalize:
        y = _rms(y, kn_ref[...])
    o_ref[...] = y.astype(o_ref.dtype)


def _compress(xb, pos, w1, b1, w2, kn, *, normalize, tm=512):
    m, kdim = xb.shape
    hid = w1.shape[1]
    dk = w2.shape[1]
    return pl.pallas_call(
        functools.partial(_cmp_kernel, normalize=normalize),
        grid=(m // tm,),
        in_specs=[
            pl.BlockSpec((tm, kdim), lambda i: (i, 0)),
            pl.BlockSpec((1, kdim), lambda i: (0, 0)),
            pl.BlockSpec((kdim, hid), lambda i: (0, 0)),
            pl.BlockSpec((1, hid), lambda i: (0, 0)),
            pl.BlockSpec((hid, dk), lambda i: (0, 0)),
            pl.BlockSpec((1, dk), lambda i: (0, 0)),
        ],
        out_specs=pl.BlockSpec((tm, dk), lambda i: (i, 0)),
        out_shape=jax.ShapeDtypeStruct((m, dk), BF16),
        compiler_params=_cparams(("parallel",)),
        name="compress",
    )(xb, pos.reshape(1, kdim), w1, b1.reshape(1, hid), w2, kn.reshape(1, dk))


def _qg_kernel(x_ref, g_ref, w_ref, qn_ref, seg_ref, segt_ref, q_ref, gate_ref, *, dq):
    xn = _rms(x_ref[...], g_ref[...]).astype(BF16)
    y = _dot(xn, w_ref[...])
    qn = qn_ref[...] * (HEAD_DIM ** -0.5 * LOG2E)
    q_ref[...] = _seg_rms(y[:, 0:dq], seg_ref[...], segt_ref[...], qn).astype(q_ref.dtype)
    gate_ref[...] = jax.nn.sigmoid(y[:, dq:])


def _qg_proj(h, g, w, q_norm, *, tm=512):
    t, d = h.shape
    n = w.shape[1]
    dq = N_HEADS * HEAD_DIM
    seg, seg_t = _seg_matrices(dq)
    return pl.pallas_call(
        functools.partial(_qg_kernel, dq=dq),
        grid=(t // tm,),
        in_specs=[
            pl.BlockSpec((tm, d), lambda i: (i, 0)),
            _resident((1, d)),
            _resident((d, n)),
            _resident((1, dq)),
            _resident(seg.shape),
            _resident(seg_t.shape),
        ],
        out_specs=[pl.BlockSpec((tm, dq), lambda i: (i, 0)),
                   pl.BlockSpec((tm, n - dq), lambda i: (i, 0))],
        out_shape=[jax.ShapeDtypeStruct((t, dq), BF16),
                   jax.ShapeDtypeStruct((t, n - dq), F32)],
        compiler_params=_cparams(("parallel",)),
        name="qg_proj",
    )(h, g.reshape(1, d), w, jnp.tile(q_norm, N_HEADS).reshape(1, dq), seg, seg_t)


def _split3(x):
    hi = x.astype(BF16)
    r = x - hi.astype(F32)
    mid = r.astype(BF16)
    lo = (r - mid.astype(F32)).astype(BF16)
    return hi, mid, lo


def _nsa_kernel(q_ref, gate_ref, kc_ref, vct_ref, ks_ref, vst_ref, kw_ref, vwt_ref, ovt_ref, eye_ref,
                o_ref, qa_ref, m_ref, l_ref, acc_ref, out_ref, s_ref, cmax_ref, *, n_cmp, n_blk):
    hpg = HEADS_PER_GROUP
    tq = CHUNK
    dk = HEAD_DIM
    wide = hpg * tq
    i = pl.program_id(2)
    t0 = i * tq

    def heads(fn):
        return jnp.concatenate([fn(h) for h in range(hpg)], axis=1)

    q = q_ref[0]
    eye = eye_ref[...]
    qa_ref[0:dk, :] = heads(lambda h: _dot_nt(eye, q[:, h * dk:(h + 1) * dk]).astype(BF16))
    gate_t = gate_ref[0].T
    gates = [heads(lambda h: gate_t[h * N_BRANCH + br:h * N_BRANCH + br + 1, :]) for br in range(N_BRANCH)]

    kk = lax.broadcasted_iota(jnp.int32, (tq, wide), 0)
    tt = lax.broadcasted_iota(jnp.int32, (tq, wide), 1) & (tq - 1)

    m_ref[...] = jnp.full(m_ref.shape, MASK_SCORE, F32)
    l_ref[...] = jnp.zeros(l_ref.shape, F32)
    acc_ref[...] = jnp.zeros(acc_ref.shape, F32)

    def key_rows(ref, c):
        return ref[0, 0, pl.ds(pl.multiple_of(c * tq, tq), tq), :]

    def produce(b, k, qrows, keep):
        s = _dot(k, qa_ref[0:qrows, :])
        if keep is not None:
            s = jnp.where(keep, s, MASK_SCORE)
        s_ref[b] = s
        cmax_ref[b] = jnp.max(s, axis=0, keepdims=True)

    def consume(b, vt, st):
        m_prev = m_ref[st]
        m_new = jnp.maximum(m_prev, cmax_ref[b])
        alpha = jnp.exp2(m_prev - m_new)
        p = jnp.exp2(s_ref[b] - m_new)
        l_ref[st] = alpha * l_ref[st] + jnp.sum(p, axis=0, keepdims=True)
        acc_ref[st] = alpha * acc_ref[st] + _dot(vt, p.astype(BF16))
        m_ref[st] = m_new

    def flash_out(st, branch):
        out_ref[...] += gates[branch] * (acc_ref[st] * (1.0 / l_ref[st]))

    assert WINDOW == 2 * tq
    lo = jnp.maximum(i - 2, 0)
    mid = jnp.maximum(i - 1, 0)
    produce(0, key_rows(kw_ref, lo), dk, kk > tt + jnp.where(i >= 2, 0, tq))

    ncp = kc_ref.shape[2]
    nidx = lax.broadcasted_iota(jnp.int32, (ncp, wide), 0)
    tpos = t0 + (lax.broadcasted_iota(jnp.int32, (ncp, wide), 1) & (tq - 1))
    valid = (nidx * CMP_STRIDE + (CMP_BLOCK - 1) <= tpos) & (nidx < n_cmp)
    s = jnp.where(valid, _dot(kc_ref[0, 0], qa_ref[0:dk, :]), MASK_SCORE)
    e = jnp.exp2(s - jnp.max(s, axis=0, keepdims=True))
    p = jnp.where(valid, e * (1.0 / jnp.sum(e, axis=0, keepdims=True)), 0.0)
    out_ref[...] = gates[0] * _dot(vct_ref[0, 0], p.astype(BF16))
    psum = p[:, 0:tq]
    for h in range(1, hpg):
        psum = psum + p[:, h * tq:(h + 1) * tq]

    produce(1, key_rows(kw_ref, mid), dk, kk > tt + jnp.where(i >= 1, -tq, tq))
    consume(0, vwt_ref[0, 0, lo], 1)

    ovt = ovt_ref[...]
    imp = None
    for part in _split3(psum):
        term = _dot(ovt, part)
        imp = term if imp is None else imp + term
    jblk = lax.broadcasted_iota(jnp.int32, (n_blk, tq), 0)
    cur = (t0 + lax.broadcasted_iota(jnp.int32, (n_blk, tq), 1)) // SEL_BLOCK
    forced = (jblk == 0) | (jblk == cur) | (jblk == cur - 1)
    imp = jnp.where(forced, FORCE_SCORE, imp)
    imp = jnp.where(jblk > cur, MASK_SCORE, imp)
    rank = jnp.zeros((n_blk, tq), F32)
    for ii in range(n_blk):
        row = imp[ii:ii + 1, :]
        rank = rank + jnp.where(row > imp, 1.0, 0.0)
        rank = rank + jnp.where((row == imp) & (jblk > ii), 1.0, 0.0)
    chosen = (rank < float(min(N_SELECT, n_blk))) & (jblk <= cur)
    bias = jnp.where(chosen, 0.0, MASK_SCORE)
    bias = jnp.concatenate([bias, jnp.zeros((dk - n_blk, tq), F32)], axis=0).astype(BF16)
    qa_ref[dk:2 * dk, :] = heads(lambda h: bias)

    produce(0, key_rows(kw_ref, i), dk, kk <= tt)
    consume(1, vwt_ref[0, 0, mid], 1)

    npair = i // 2
    held = jnp.where(i >= 2, 2 * npair - 1, i)
    last = jnp.where(i % 2 == 0, i + 1, i - 1)
    produce(1, key_rows(ks_ref, i), 2 * dk, kk <= tt)
    consume(0, vwt_ref[0, 0, i], 1)
    flash_out(1, 2)

    def pair_body(p, carry):
        c = 2 * p
        produce(0, key_rows(ks_ref, c), 2 * dk, None)
        consume(1, vst_ref[0, 0, jnp.where(p == 0, i, c - 1)], 0)
        produce(1, key_rows(ks_ref, c + 1), 2 * dk, None)
        consume(0, vst_ref[0, 0, c], 0)
        return carry

    lax.fori_loop(0, npair, pair_body, 0)
    produce(0, key_rows(ks_ref, last), 2 * dk, None)
    consume(1, vst_ref[0, 0, held], 0)
    consume(0, vst_ref[0, 0, last], 0)
    flash_out(0, 1)

    out = out_ref[...]
    out = jnp.concatenate([out[:, h * tq:(h + 1) * tq] for h in range(hpg)], axis=0)
    o_ref[0] = out.T.astype(o_ref.dtype)


def _nsa(q, gate, kc, vct, ks, vst, kw, vwt):
    b, s, dq = q.shape
    g = N_KV_GROUPS
    hpg = HEADS_PER_GROUP
    dk = HEAD_DIM
    gd = hpg * dk
    tq = CHUNK
    n_blk = s // SEL_BLOCK
    n_cmp = (s - CMP_BLOCK) // CMP_STRIDE + 1
    ncp = kc.shape[2]
    assert WINDOW == 2 * tq and n_blk <= dk and (s // tq) % 2 == 0
    cmp_start = np.arange(ncp) * CMP_STRIDE
    blk_start = np.arange(n_blk) * SEL_BLOCK
    ovt = ((cmp_start[None, :] < blk_start[:, None] + SEL_BLOCK)
           & (cmp_start[None, :] + CMP_BLOCK > blk_start[:, None])
           & (np.arange(ncp)[None, :] < n_cmp)).astype(np.float32)
    idx = lambda bi, gi, i: (bi, gi, 0, 0)
    idx5 = lambda bi, gi, i: (bi, gi, 0, 0, 0)
    return pl.pallas_call(
        functools.partial(_nsa_kernel, n_cmp=n_cmp, n_blk=n_blk),
        grid=(b, g, s // tq),
        in_specs=[
            pl.BlockSpec((1, tq, gd), lambda bi, gi, i: (bi, i, gi)),
            pl.BlockSpec((1, tq, LANES), lambda bi, gi, i: (bi, i, gi)),
            pl.BlockSpec((1, 1, ncp, dk), idx),
            pl.BlockSpec((1, 1, dk, ncp), idx),
            pl.BlockSpec((1, 1, s, 2 * dk), idx),
            pl.BlockSpec((1, 1, s // tq, dk, tq), idx5),
            pl.BlockSpec((1, 1, s, dk), idx),
            pl.BlockSpec((1, 1, s // tq, dk, tq), idx5),
            pl.BlockSpec((n_blk, ncp), lambda bi, gi, i: (0, 0)),
            pl.BlockSpec((dk, dk), lambda bi, gi, i: (0, 0)),
        ],
        out_specs=pl.BlockSpec((1, tq, gd), lambda bi, gi, i: (bi, i, gi)),
        out_shape=jax.ShapeDtypeStruct((b, s, dq), BF16),
        scratch_shapes=[
            pltpu.VMEM((2 * dk, hpg * tq), BF16),
            pltpu.VMEM((2, 1, hpg * tq), F32),
            pltpu.VMEM((2, 1, hpg * tq), F32),
            pltpu.VMEM((2, dk, hpg * tq), F32),
            pltpu.VMEM((dk, hpg * tq), F32),
            pltpu.VMEM((2, tq, hpg * tq), F32),
            pltpu.VMEM((2, 1, hpg * tq), F32),
        ],
        compiler_params=_cparams(("parallel", "parallel", "arbitrary")),
        name="nsa",
    )(q, gate, kc, vct, ks, vst, kw, vwt, jnp.asarray(ovt, BF16), jnp.eye(dk, dtype=BF16))


def _oproj_kernel(o_ref, w_ref, h_ref, out_ref):
    out_ref[...] = h_ref[...] + _dot(o_ref[...], w_ref[...])


def _out_proj(o, w, h, *, tm=512):
    t, d = h.shape
    k = o.shape[1]
    return pl.pallas_call(
        _oproj_kernel,
        grid=(t // tm,),
        in_specs=[
            pl.BlockSpec((tm, k), lambda i: (i, 0)),
            pl.BlockSpec((k, d), lambda i: (0, 0)),
            pl.BlockSpec((tm, d), lambda i: (i, 0)),
        ],
        out_specs=pl.BlockSpec((tm, d), lambda i: (i, 0)),
        out_shape=jax.ShapeDtypeStruct((t, d), F32),
        compiler_params=_cparams(("parallel",)),
        name="out_proj",
    )(o, w, h)


def _cmp_blocks(x):
    b, g, s, dk = x.shape
    width = CMP_BLOCK * dk
    n_even = s // CMP_BLOCK
    even = x.reshape(b, g, n_even, width)
    odd = x[:, :, CMP_STRIDE:s - CMP_STRIDE].reshape(b, g, n_even - 1, width)
    odd = jnp.concatenate([odd, jnp.zeros((b, g, 1, width), x.dtype)], axis=2)
    blocks = jnp.stack([even, odd], axis=3).reshape(b, g, 2 * n_even, width)
    return blocks.reshape(b * g * 2 * n_even, width)


def _gate_columns(w_qg):
    dq = N_HEADS * HEAD_DIM
    per = HEADS_PER_GROUP * N_BRANCH
    wg = w_qg[:, dq:].reshape(-1, N_KV_GROUPS, per)
    wg = jnp.pad(wg, ((0, 0), (0, 0), (0, LANES - per)))
    return jnp.concatenate([w_qg[:, :dq], wg.reshape(-1, N_KV_GROUPS * LANES)], axis=1)


def kernel(x, ffn_norm, ffn_w_gate_up, ffn_w_down, mix_norm, conv_w_in, conv_w, conv_w_out, kv_norm, kv_w,
           cmp_pos, cmp_w1, cmp_b1, cmp_w2, k_norm, nsa_w_qg, q_norm, nsa_w_o):
    b, s, d = x.shape
    t = b * s
    depth = ffn_norm.shape[0]
    n_a = conv_w_in.shape[0]
    g = N_KV_GROUPS

    def ffn(h2, layer, j):
        return _ffn(h2, ffn_norm[layer, j], ffn_w_gate_up[layer, j].astype(BF16),
                    ffn_w_down[layer, j].astype(BF16))

    h = x.reshape(t, d)
    shared = None
    for layer in range(depth):
        h = ffn(h, layer, 0)
        if layer < n_a:
            h = _conv_mixer(h.reshape(b, s, d), mix_norm[layer], conv_w_in[layer].astype(BF16),
                            conv_w[layer], conv_w_out[layer].astype(BF16)).reshape(t, d)
        else:
            li = layer - n_a
            q, gate = _qg_proj(h, mix_norm[layer], _gate_columns(nsa_w_qg[li]).astype(BF16), q_norm[li])
            o = _nsa(q.reshape(b, s, -1), gate.reshape(b, s, -1), *shared)
            h = _out_proj(o.reshape(t, -1), nsa_w_o[li].astype(BF16), h)
        h = ffn(h, layer, 1)
        if layer == n_a - 1:
            raw, ks, vst, kw, vwt = _kv_proj(h.reshape(b, s, d), kv_norm, kv_w.astype(BF16), k_norm)
            raw = raw.reshape(b, s, 2, g, HEAD_DIM).transpose(2, 0, 3, 1, 4)
            cmp = []
            for c in range(2):
                y = _compress(_cmp_blocks(raw[c]), cmp_pos[c], cmp_w1[c].astype(BF16), cmp_b1[c],
                              cmp_w2[c].astype(BF16), k_norm[0], normalize=(c == 0))
                cmp.append(y.reshape(b, g, -1, HEAD_DIM))
            shared = (cmp[0], cmp[1].transpose(0, 1, 3, 2), ks, vst, kw, vwt)
    return h.reshape(b, s, d)
```

```python
import functools

import numpy as np
import jax
import jax.numpy as jnp
from jax import lax
from jax.experimental import pallas as pl
from jax.experimental.pallas import tpu as pltpu

EPS = 1e-6
N_HEADS = 16
HEAD_DIM = 64
N_KV_GROUPS = 4
HEADS_PER_GROUP = N_HEADS // N_KV_GROUPS
N_KV_SLOTS = 6
N_BRANCH = 3
CMP_BLOCK = 32
CMP_STRIDE = 16
SEL_BLOCK = 64
N_SELECT = 16
WINDOW = 512
FORCE_SCORE = 1e9
MASK_SCORE = -1e30
LOG2E = 1.4426950408889634

LANES = 128
CHUNK = 256
VMEM_LIMIT = 56 * 1024 * 1024

BF16 = jnp.bfloat16
F32 = jnp.float32


def _cparams(sem):
    return pltpu.CompilerParams(dimension_semantics=sem, vmem_limit_bytes=VMEM_LIMIT)


def _rms(x, g):
    ms = jnp.mean(x * x, axis=-1, keepdims=True)
    return x * lax.rsqrt(ms + EPS) * g


def _dot(a, b):
    return jnp.dot(a, b, preferred_element_type=F32)


def _split2(x):
    hi = x.astype(BF16)
    return hi, (x - hi.astype(F32)).astype(BF16)


def _seg_rms(y, seg, seg_t, gain):
    ss = None
    for part in _split2(y * y):
        term = _dot(part, seg)
        ss = term if ss is None else ss + term
    r = lax.rsqrt(ss * (1.0 / HEAD_DIM) + EPS)
    scale = None
    for part in _split2(r):
        term = _dot(part, seg_t)
        scale = term if scale is None else scale + term
    return y * scale * gain


def _seg_matrices(width):
    ind = (np.arange(width)[:, None] // HEAD_DIM == np.arange(LANES)[None, :]).astype(np.float32)
    return jnp.asarray(ind, BF16), jnp.asarray(ind.T, BF16)


def _dot_nt(a, b):
    return lax.dot_general(a, b, (((1,), (1,)), ((), ())), preferred_element_type=F32)


def _ffn_kernel(x_ref, g_ref, wgu_ref, wd_ref, o_ref, *, dff):
    x = x_ref[...]
    xn = _rms(x, g_ref[...]).astype(BF16)
    ab = _dot(xn, wgu_ref[...])
    a = ab[:, 0:dff]
    h = (a * jax.nn.sigmoid(a) * ab[:, dff:2 * dff]).astype(BF16)
    o_ref[...] = x + 0.5 * _dot(h, wd_ref[...])


def _resident(shape, lead=()):
    index = tuple(lead) + (0,) * len(shape)
    return pl.BlockSpec((None,) * len(lead) + tuple(shape), lambda *_: index, pipeline_mode=pl.Buffered(1))


def _ffn(h, g_all, w_gu_all, w_d_all, lead, *, tm=512):
    t, d = h.shape
    dff = w_d_all.shape[-2]
    return pl.pallas_call(
        functools.partial(_ffn_kernel, dff=dff),
        grid=(t // tm,),
        in_specs=[
            pl.BlockSpec((tm, d), lambda i: (i, 0)),
            _resident((1, d), lead),
            _resident((d, 2 * dff), lead),
            _resident((dff, d), lead),
        ],
        out_specs=pl.BlockSpec((tm, d), lambda i: (i, 0)),
        out_shape=jax.ShapeDtypeStruct((t, d), F32),
        compiler_params=_cparams(("parallel",)),
        name="ffn",
    )(h, g_all, w_gu_all, w_d_all)


CARRY = 8


def _conv_kernel(x_ref, g_ref, win_ref, cw_ref, wout_ref, o_ref, vbuf_ref, *, tm, d):
    j = pl.program_id(1)

    @pl.when(j == 0)
    def _():
        vbuf_ref[0:CARRY, :] = jnp.zeros((CARRY, d), F32)

    x = x_ref[0]
    xn = _rms(x, g_ref[...]).astype(BF16)
    proj = _dot(xn, win_ref[...])
    bg = proj[:, 0:d]
    v = proj[:, d:2 * d] * proj[:, 2 * d:3 * d]
    vbuf_ref[CARRY:CARRY + tm, :] = v
    cw = cw_ref[...]
    conv = (cw[0:1, :] * vbuf_ref[CARRY - 2:CARRY - 2 + tm, :]
            + cw[1:2, :] * vbuf_ref[CARRY - 1:CARRY - 1 + tm, :]
            + cw[2:3, :] * v)
    y = (bg * conv).astype(BF16)
    o_ref[0] = x + _dot(y, wout_ref[...])
    vbuf_ref[0:CARRY, :] = vbuf_ref[tm:tm + CARRY, :]


def _conv_mixer(h, g, w_in, conv_w, w_out, *, tm=512):
    b, s, d = h.shape
    return pl.pallas_call(
        functools.partial(_conv_kernel, tm=tm, d=d),
        grid=(b, s // tm),
        in_specs=[
            pl.BlockSpec((1, tm, d), lambda i, j: (i, j, 0)),
            pl.BlockSpec((1, d), lambda i, j: (0, 0)),
            pl.BlockSpec((d, 3 * d), lambda i, j: (0, 0)),
            pl.BlockSpec((conv_w.shape[0], d), lambda i, j: (0, 0)),
            pl.BlockSpec((d, d), lambda i, j: (0, 0)),
        ],
        out_specs=pl.BlockSpec((1, tm, d), lambda i, j: (i, j, 0)),
        out_shape=jax.ShapeDtypeStruct((b, s, d), F32),
        scratch_shapes=[pltpu.VMEM((CARRY + tm, d), F32)],
        compiler_params=_cparams(("arbitrary", "arbitrary")),
        name="conv_mixer",
    )(h, g.reshape(1, d), w_in, conv_w, w_out)


def _kv_kernel(x_ref, g_ref, w_ref, kn_ref, seg_ref, segt_ref, raw_ref, ks_ref, vst_ref, kw_ref, vwt_ref,
               *, tm):
    j = pl.program_id(1)
    xn = _rms(x_ref[0], g_ref[...]).astype(BF16)
    y = _dot(xn, w_ref[...])
    gw = N_KV_GROUPS * HEAD_DIM
    raw_ref[0] = y[:, 0:2 * gw]
    pos = j * tm + lax.broadcasted_iota(jnp.int32, (tm, HEAD_DIM), 0)
    lane = lax.broadcasted_iota(jnp.int32, (tm, HEAD_DIM), 1)
    onehot = jnp.where(lane == pos // SEL_BLOCK, 1.0, 0.0).astype(BF16)
    k_sel = _seg_rms(y[:, 2 * gw:3 * gw], seg_ref[...], segt_ref[...], kn_ref[0:1, :]).astype(BF16)
    k_win = _seg_rms(y[:, 4 * gw:5 * gw], seg_ref[...], segt_ref[...], kn_ref[1:2, :]).astype(BF16)
    for grp in range(N_KV_GROUPS):
        ks_ref[0, grp, :, 0:HEAD_DIM] = k_sel[:, grp * HEAD_DIM:(grp + 1) * HEAD_DIM]
        ks_ref[0, grp, :, HEAD_DIM:2 * HEAD_DIM] = onehot
        kw_ref[0, grp] = k_win[:, grp * HEAD_DIM:(grp + 1) * HEAD_DIM]
    for slot, ref in ((3, vst_ref), (5, vwt_ref)):
        vt = y[:, slot * gw:(slot + 1) * gw].T
        for grp in range(N_KV_GROUPS):
            for cc in range(tm // CHUNK):
                ref[0, grp, cc] = vt[grp * HEAD_DIM:(grp + 1) * HEAD_DIM,
                                     cc * CHUNK:(cc + 1) * CHUNK].astype(BF16)


def _kv_proj(h, g, w, k_norm, *, tm=512):
    b, s, d = h.shape
    n = w.shape[1]
    ng = N_KV_GROUPS
    gw = ng * HEAD_DIM
    cpt = tm // CHUNK
    assert s // SEL_BLOCK <= HEAD_DIM
    seg, seg_t = _seg_matrices(gw)
    gains = jnp.tile(k_norm[1:3], (1, ng))
    return pl.pallas_call(
        functools.partial(_kv_kernel, tm=tm),
        grid=(b, s // tm),
        in_specs=[
            pl.BlockSpec((1, tm, d), lambda i, j: (i, j, 0)),
            _resident((1, d)),
            _resident((d, n)),
            _resident(gains.shape),
            _resident(seg.shape),
            _resident(seg_t.shape),
        ],
        out_specs=[
            pl.BlockSpec((1, tm, 2 * gw), lambda i, j: (i, j, 0)),
            pl.BlockSpec((1, ng, tm, 2 * HEAD_DIM), lambda i, j: (i, 0, j, 0)),
            pl.BlockSpec((1, ng, cpt, HEAD_DIM, CHUNK), lambda i, j: (i, 0, j, 0, 0)),
            pl.BlockSpec((1, ng, tm, HEAD_DIM), lambda i, j: (i, 0, j, 0)),
            pl.BlockSpec((1, ng, cpt, HEAD_DIM, CHUNK), lambda i, j: (i, 0, j, 0, 0)),
        ],
        out_shape=[
            jax.ShapeDtypeStruct((b, s, 2 * gw), F32),
            jax.ShapeDtypeStruct((b, ng, s, 2 * HEAD_DIM), BF16),
            jax.ShapeDtypeStruct((b, ng, s // CHUNK, HEAD_DIM, CHUNK), BF16),
            jax.ShapeDtypeStruct((b, ng, s, HEAD_DIM), BF16),
            jax.ShapeDtypeStruct((b, ng, s // CHUNK, HEAD_DIM, CHUNK), BF16),
        ],
        compiler_params=_cparams(("parallel", "parallel")),
        name="kv_proj",
    )(h, g.reshape(1, d), w, gains, seg, seg_t)


def _cmp_kernel(x_ref, pos_ref, w1_ref, b1_ref, w2_ref, kn_ref, o_ref):
    c = pl.program_id(0)
    nb, ne, width = x_ref.shape
    rows = nb * ne
    half = width // 2
    x = x_ref[...].reshape(rows, width)
    pos = pos_ref[...]
    w1 = w1_ref[...]
    h_even = _dot((x + pos).astype(BF16), w1)
    a = _dot((x[:, half:] + pos[:, :half]).astype(BF16), w1[0:half])
    b = _dot((x[:, :half] + pos[:, half:]).astype(BF16), w1[half:width])
    h_odd = a + pltpu.roll(b, rows - 1, 0)

    def tail(hid):
        y = _dot(jax.nn.gelu(hid + b1_ref[...]).astype(BF16), w2_ref[...])
        y = jnp.where(c == 0, _rms(y, kn_ref[...]), y)
        return y.astype(o_ref.dtype).reshape(nb, ne, y.shape[-1])

    o_ref[:, 0:ne, :] = tail(h_even)
    o_ref[:, ne:2 * ne, :] = tail(h_odd)


def _compress(xe, pos, w1, b1, w2, kn, *, nb=8):
    _, nseq, ne, width = xe.shape
    nb = min(nb, nseq)
    hid = w1.shape[-1]
    dk = w2.shape[-1]
    return pl.pallas_call(
        _cmp_kernel,
        grid=(2, nseq // nb),
        in_specs=[
            pl.BlockSpec((None, nb, ne, width), lambda c, i: (c, i, 0, 0)),
            pl.BlockSpec((None, 1, width), lambda c, i: (c, 0, 0)),
            pl.BlockSpec((None, width, hid), lambda c, i: (c, 0, 0)),
            pl.BlockSpec((None, 1, hid), lambda c, i: (c, 0, 0)),
            pl.BlockSpec((None, hid, dk), lambda c, i: (c, 0, 0)),
            pl.BlockSpec((1, dk), lambda c, i: (0, 0)),
        ],
        out_specs=pl.BlockSpec((None, nb, 2 * ne, dk), lambda c, i: (c, i, 0, 0)),
        out_shape=jax.ShapeDtypeStruct((2, nseq, 2 * ne, dk), BF16),
        compiler_params=_cparams(("parallel", "parallel")),
        name="compress",
    )(xe, pos.reshape(2, 1, width), w1, b1.reshape(2, 1, hid), w2, kn.reshape(1, dk))


def _qg_kernel(x_ref, g_ref, w_ref, qn_ref, seg_ref, segt_ref, q_ref, gate_ref, *, dq):
    xn = _rms(x_ref[...], g_ref[...]).astype(BF16)
    y = _dot(xn, w_ref[...])
    qn = qn_ref[...] * (HEAD_DIM ** -0.5 * LOG2E)
    q_ref[...] = _seg_rms(y[:, 0:dq], seg_ref[...], segt_ref[...], qn).astype(q_ref.dtype)
    gate_ref[...] = jax.nn.sigmoid(y[:, dq:])


def _qg_proj(h, g, w, q_norm, *, tm=512):
    t, d = h.shape
    n = w.shape[1]
    dq = N_HEADS * HEAD_DIM
    seg, seg_t = _seg_matrices(dq)
    return pl.pallas_call(
        functools.partial(_qg_kernel, dq=dq),
        grid=(t // tm,),
        in_specs=[
            pl.BlockSpec((tm, d), lambda i: (i, 0)),
            _resident((1, d)),
            _resident((d, n)),
            _resident((1, dq)),
            _resident(seg.shape),
            _resident(seg_t.shape),
        ],
        out_specs=[pl.BlockSpec((tm, dq), lambda i: (i, 0)),
                   pl.BlockSpec((tm, n - dq), lambda i: (i, 0))],
        out_shape=[jax.ShapeDtypeStruct((t, dq), BF16),
                   jax.ShapeDtypeStruct((t, n - dq), F32)],
        compiler_params=_cparams(("parallel",)),
        name="qg_proj",
    )(h, g.reshape(1, d), w, jnp.tile(q_norm, N_HEADS).reshape(1, dq), seg, seg_t)


def _split3(x):
    hi = x.astype(BF16)
    r = x - hi.astype(F32)
    mid = r.astype(BF16)
    lo = (r - mid.astype(F32)).astype(BF16)
    return hi, mid, lo


def _cmp_block_of_row(row, nrows):
    return jnp.where(row < nrows // 2, 2 * row, 2 * row - (nrows - 1))


def _nsa_kernel(q_ref, gate_ref, kc_ref, vct_ref, ks_ref, vst_ref, kw_ref, vwt_ref, ovt_ref, eye_ref,
                o_ref, qa_ref, m_ref, l_ref, acc_ref, out_ref, s_ref, cmax_ref, *, n_cmp, n_blk):
    hpg = HEADS_PER_GROUP
    tq = CHUNK
    dk = HEAD_DIM
    wide = hpg * tq
    i = pl.program_id(2)
    t0 = i * tq

    def heads(fn):
        return jnp.concatenate([fn(h) for h in range(hpg)], axis=1)

    q = q_ref[0]
    eye = eye_ref[...]
    qa_ref[0:dk, :] = heads(lambda h: _dot_nt(eye, q[:, h * dk:(h + 1) * dk]).astype(BF16))
    gate_t = gate_ref[0].T
    gates = [heads(lambda h: gate_t[h * N_BRANCH + br:h * N_BRANCH + br + 1, :]) for br in range(N_BRANCH)]

    kk = lax.broadcasted_iota(jnp.int32, (tq, wide), 0)
    tt = lax.broadcasted_iota(jnp.int32, (tq, wide), 1) & (tq - 1)

    m_ref[...] = jnp.full(m_ref.shape, MASK_SCORE, F32)
    l_ref[...] = jnp.zeros(l_ref.shape, F32)
    acc_ref[...] = jnp.zeros(acc_ref.shape, F32)

    def key_rows(ref, c):
        return ref[0, 0, pl.ds(pl.multiple_of(c * tq, tq), tq), :]

    def produce(b, k, qrows, keep):
        s = _dot(k, qa_ref[0:qrows, :])
        if keep is not None:
            s = jnp.where(keep, s, MASK_SCORE)
        s_ref[b] = s
        cmax_ref[b] = jnp.max(s, axis=0, keepdims=True)

    def consume(b, vt, st):
        m_prev = m_ref[st]
        m_new = jnp.maximum(m_prev, cmax_ref[b])
        alpha = jnp.exp2(m_prev - m_new)
        p = jnp.exp2(s_ref[b] - m_new)
        l_ref[st] = alpha * l_ref[st] + jnp.sum(p, axis=0, keepdims=True)
        acc_ref[st] = alpha * acc_ref[st] + _dot(vt, p.astype(BF16))
        m_ref[st] = m_new

    def flash_out(st, branch):
        out_ref[...] += gates[branch] * (acc_ref[st] * (1.0 / l_ref[st]))

    assert WINDOW == 2 * tq
    lo = jnp.maximum(i - 2, 0)
    mid = jnp.maximum(i - 1, 0)
    produce(0, key_rows(kw_ref, lo), dk, kk > tt + jnp.where(i >= 2, 0, tq))

    ncp = kc_ref.shape[2]
    nidx = _cmp_block_of_row(lax.broadcasted_iota(jnp.int32, (ncp, wide), 0), ncp)
    tpos = t0 + (lax.broadcasted_iota(jnp.int32, (ncp, wide), 1) & (tq - 1))
    valid = (nidx * CMP_STRIDE + (CMP_BLOCK - 1) <= tpos) & (nidx < n_cmp)
    s = jnp.where(valid, _dot(kc_ref[0, 0], qa_ref[0:dk, :]), MASK_SCORE)
    e = jnp.exp2(s - jnp.max(s, axis=0, keepdims=True))
    p = jnp.where(valid, e * (1.0 / jnp.sum(e, axis=0, keepdims=True)), 0.0)
    out_ref[...] = gates[0] * _dot(vct_ref[0, 0], p.astype(BF16))
    psum = p[:, 0:tq]
    for h in range(1, hpg):
        psum = psum + p[:, h * tq:(h + 1) * tq]

    produce(1, key_rows(kw_ref, mid), dk, kk > tt + jnp.where(i >= 1, -tq, tq))
    consume(0, vwt_ref[0, 0, lo], 1)

    ovt = ovt_ref[...]
    imp = None
    for part in _split3(psum):
        term = _dot(ovt, part)
        imp = term if imp is None else imp + term
    jblk = lax.broadcasted_iota(jnp.int32, (n_blk, tq), 0)
    cur = (t0 + lax.broadcasted_iota(jnp.int32, (n_blk, tq), 1)) // SEL_BLOCK
    forced = (jblk == 0) | (jblk == cur) | (jblk == cur - 1)
    imp = jnp.where(forced, FORCE_SCORE, imp)
    imp = jnp.where(jblk > cur, MASK_SCORE, imp)
    rank = jnp.zeros((n_blk, tq), F32)
    for ii in range(n_blk):
        row = imp[ii:ii + 1, :]
        rank = rank + jnp.where(row > imp, 1.0, 0.0)
        rank = rank + jnp.where((row == imp) & (jblk > ii), 1.0, 0.0)
    chosen = (rank < float(min(N_SELECT, n_blk))) & (jblk <= cur)
    bias = jnp.where(chosen, 0.0, MASK_SCORE)
    bias = jnp.concatenate([bias, jnp.zeros((dk - n_blk, tq), F32)], axis=0).astype(BF16)
    qa_ref[dk:2 * dk, :] = heads(lambda h: bias)

    produce(0, key_rows(kw_ref, i), dk, kk <= tt)
    consume(1, vwt_ref[0, 0, mid], 1)

    npair = i // 2
    held = jnp.where(i >= 2, 2 * npair - 1, i)
    last = jnp.where(i % 2 == 0, i + 1, i - 1)
    produce(1, key_rows(ks_ref, i), 2 * dk, kk <= tt)
    consume(0, vwt_ref[0, 0, i], 1)
    flash_out(1, 2)

    def pair_body(p, carry):
        c = 2 * p
        produce(0, key_rows(ks_ref, c), 2 * dk, None)
        consume(1, vst_ref[0, 0, jnp.where(p == 0, i, c - 1)], 0)
        produce(1, key_rows(ks_ref, c + 1), 2 * dk, None)
        consume(0, vst_ref[0, 0, c], 0)
        return carry

    lax.fori_loop(0, npair, pair_body, 0)
    produce(0, key_rows(ks_ref, last), 2 * dk, None)
    consume(1, vst_ref[0, 0, held], 0)
    consume(0, vst_ref[0, 0, last], 0)
    flash_out(0, 1)

    out = out_ref[...]
    out = jnp.concatenate([out[:, h * tq:(h + 1) * tq] for h in range(hpg)], axis=0)
    o_ref[0] = out.T.astype(o_ref.dtype)


def _nsa(q, gate, kc, vct, ks, vst, kw, vwt):
    b, s, dq = q.shape
    g = N_KV_GROUPS
    hpg = HEADS_PER_GROUP
    dk = HEAD_DIM
    gd = hpg * dk
    tq = CHUNK
    n_blk = s // SEL_BLOCK
    n_cmp = (s - CMP_BLOCK) // CMP_STRIDE + 1
    ncp = kc.shape[2]
    assert WINDOW == 2 * tq and n_blk <= dk and (s // tq) % 2 == 0
    rows_np = np.arange(ncp)
    cmp_idx = np.where(rows_np < ncp // 2, 2 * rows_np, 2 * rows_np - (ncp - 1))
    cmp_start = cmp_idx * CMP_STRIDE
    blk_start = np.arange(n_blk) * SEL_BLOCK
    ovt = ((cmp_start[None, :] < blk_start[:, None] + SEL_BLOCK)
           & (cmp_start[None, :] + CMP_BLOCK > blk_start[:, None])
           & (cmp_idx[None, :] < n_cmp)).astype(np.float32)
    idx = lambda bi, gi, i: (bi, gi, 0, 0)
    idx5 = lambda bi, gi, i: (bi, gi, 0, 0, 0)
    return pl.pallas_call(
        functools.partial(_nsa_kernel, n_cmp=n_cmp, n_blk=n_blk),
        grid=(b, g, s // tq),
        in_specs=[
            pl.BlockSpec((1, tq, gd), lambda bi, gi, i: (bi, i, gi)),
            pl.BlockSpec((1, tq, LANES), lambda bi, gi, i: (bi, i, gi)),
            pl.BlockSpec((1, 1, ncp, dk), idx),
            pl.BlockSpec((1, 1, dk, ncp), idx),
            pl.BlockSpec((1, 1, s, 2 * dk), idx),
            pl.BlockSpec((1, 1, s // tq, dk, tq), idx5),
            pl.BlockSpec((1, 1, s, dk), idx),
            pl.BlockSpec((1, 1, s // tq, dk, tq), idx5),
            pl.BlockSpec((n_blk, ncp), lambda bi, gi, i: (0, 0)),
            pl.BlockSpec((dk, dk), lambda bi, gi, i: (0, 0)),
        ],
        out_specs=pl.BlockSpec((1, tq, gd), lambda bi, gi, i: (bi, i, gi)),
        out_shape=jax.ShapeDtypeStruct((b, s, dq), BF16),
        scratch_shapes=[
            pltpu.VMEM((2 * dk, hpg * tq), BF16),
            pltpu.VMEM((2, 1, hpg * tq), F32),
            pltpu.VMEM((2, 1, hpg * tq), F32),
            pltpu.VMEM((2, dk, hpg * tq), F32),
            pltpu.VMEM((dk, hpg * tq), F32),
            pltpu.VMEM((2, tq, hpg * tq), F32),
            pltpu.VMEM((2, 1, hpg * tq), F32),
        ],
        compiler_params=_cparams(("parallel", "parallel", "arbitrary")),
        name="nsa",
    )(q, gate, kc, vct, ks, vst, kw, vwt, jnp.asarray(ovt, BF16), jnp.eye(dk, dtype=BF16))


def _oproj_kernel(o_ref, w_ref, h_ref, out_ref):
    out_ref[...] = h_ref[...] + _dot(o_ref[...], w_ref[...])


def _out_proj(o, w, h, *, tm=512):
    t, d = h.shape
    k = o.shape[1]
    return pl.pallas_call(
        _oproj_kernel,
        grid=(t // tm,),
        in_specs=[
            pl.BlockSpec((tm, k), lambda i: (i, 0)),
            pl.BlockSpec((k, d), lambda i: (0, 0)),
            pl.BlockSpec((tm, d), lambda i: (i, 0)),
        ],
        out_specs=pl.BlockSpec((tm, d), lambda i: (i, 0)),
        out_shape=jax.ShapeDtypeStruct((t, d), F32),
        compiler_params=_cparams(("parallel",)),
        name="out_proj",
    )(o, w, h)


def _even_blocks(raw):
    b, s, _ = raw.shape
    ne = s // CMP_BLOCK
    x = raw.reshape(b, ne, CMP_BLOCK, 2, N_KV_GROUPS, HEAD_DIM).transpose(3, 0, 4, 1, 2, 5)
    return x.reshape(2, b * N_KV_GROUPS, ne, CMP_BLOCK * HEAD_DIM)


def _gate_columns(w_qg):
    dq = N_HEADS * HEAD_DIM
    per = HEADS_PER_GROUP * N_BRANCH
    wg = w_qg[:, dq:].reshape(-1, N_KV_GROUPS, per)
    wg = jnp.pad(wg, ((0, 0), (0, 0), (0, LANES - per)))
    return jnp.concatenate([w_qg[:, :dq], wg.reshape(-1, N_KV_GROUPS * LANES)], axis=1)


def kernel(x, ffn_norm, ffn_w_gate_up, ffn_w_down, mix_norm, conv_w_in, conv_w, conv_w_out, kv_norm, kv_w,
           cmp_pos, cmp_w1, cmp_b1, cmp_w2, k_norm, nsa_w_qg, q_norm, nsa_w_o):
    b, s, d = x.shape
    t = b * s
    depth = ffn_norm.shape[0]
    n_a = conv_w_in.shape[0]
    g = N_KV_GROUPS

    ffn_g = ffn_norm.reshape(depth, 2, 1, d)
    ffn_wgu = ffn_w_gate_up.astype(BF16)
    ffn_wd = ffn_w_down.astype(BF16)

    def ffn(h2, layer, j):
        return _ffn(h2, ffn_g, ffn_wgu, ffn_wd, (layer, j))

    h = x.reshape(t, d)
    shared = None
    for layer in range(depth):
        h = ffn(h, layer, 0)
        if layer < n_a:
            h = _conv_mixer(h.reshape(b, s, d), mix_norm[layer], conv_w_in[layer].astype(BF16),
                            conv_w[layer], conv_w_out[layer].astype(BF16)).reshape(t, d)
        else:
            li = layer - n_a
            q, gate = _qg_proj(h, mix_norm[layer], _gate_columns(nsa_w_qg[li]).astype(BF16), q_norm[li])
            o = _nsa(q.reshape(b, s, -1), gate.reshape(b, s, -1), *shared)
            h = _out_proj(o.reshape(t, -1), nsa_w_o[li].astype(BF16), h)
        h = ffn(h, layer, 1)
        if layer == n_a - 1:
            raw, ks, vst, kw, vwt = _kv_proj(h.reshape(b, s, d), kv_norm, kv_w.astype(BF16), k_norm)
            cmp = _compress(_even_blocks(raw), cmp_pos.reshape(2, -1), cmp_w1.astype(BF16), cmp_b1,
                            cmp_w2.astype(BF16), k_norm[0])
            cmp = cmp.reshape(2, b, g, -1, HEAD_DIM)
            shared = (cmp[0], cmp[1].transpose(0, 1, 3, 2), ks, vst, kw, vwt)
    return h.reshape(b, s, d)
```

```python
import functools

import numpy as np
import jax
import jax.numpy as jnp
from jax import lax
from jax.experimental import pallas as pl
from jax.experimental.pallas import tpu as pltpu

EPS = 1e-6
N_HEADS = 16
HEAD_DIM = 64
N_KV_GROUPS = 4
HEADS_PER_GROUP = N_HEADS // N_KV_GROUPS
N_KV_SLOTS = 6
N_BRANCH = 3
CMP_BLOCK = 32
CMP_STRIDE = 16
SEL_BLOCK = 64
N_SELECT = 16
WINDOW = 512
FORCE_SCORE = 1e9
MASK_SCORE = -1e30
LOG2E = 1.4426950408889634

LANES = 128
CHUNK = 256
V_ROWS = HEAD_DIM + 16
VMEM_LIMIT = 56 * 1024 * 1024

BF16 = jnp.bfloat16
F32 = jnp.float32


def _cparams(sem):
    return pltpu.CompilerParams(dimension_semantics=sem, vmem_limit_bytes=VMEM_LIMIT)


def _rms(x, g):
    ms = jnp.mean(x * x, axis=-1, keepdims=True)
    return x * lax.rsqrt(ms + EPS) * g


def _dot(a, b):
    return jnp.dot(a, b, preferred_element_type=F32)


def _split2(x):
    hi = x.astype(BF16)
    return hi, (x - hi.astype(F32)).astype(BF16)


def _seg_rms(y, seg, seg_t, gain):
    ss = None
    for part in _split2(y * y):
        term = _dot(part, seg)
        ss = term if ss is None else ss + term
    r = lax.rsqrt(ss * (1.0 / HEAD_DIM) + EPS)
    scale = None
    for part in _split2(r):
        term = _dot(part, seg_t)
        scale = term if scale is None else scale + term
    return y * scale * gain


def _seg_matrices(width):
    ind = (np.arange(width)[:, None] // HEAD_DIM == np.arange(LANES)[None, :]).astype(np.float32)
    return jnp.asarray(ind, BF16), jnp.asarray(ind.T, BF16)


def _dot_nt(a, b):
    return lax.dot_general(a, b, (((1,), (1,)), ((), ())), preferred_element_type=F32)


def _ffn_kernel(x_ref, g_ref, wgu_ref, wd_ref, o_ref, *, dff):
    x = x_ref[...]
    xn = _rms(x, g_ref[...]).astype(BF16)
    ab = _dot(xn, wgu_ref[...])
    a = ab[:, 0:dff]
    h = (a * jax.nn.sigmoid(a) * ab[:, dff:2 * dff]).astype(BF16)
    o_ref[...] = x + 0.5 * _dot(h, wd_ref[...])


def _resident(shape, lead=()):
    index = tuple(lead) + (0,) * len(shape)
    return pl.BlockSpec((None,) * len(lead) + tuple(shape), lambda *_: index, pipeline_mode=pl.Buffered(1))


def _ffn(h, g_all, w_gu_all, w_d_all, lead, *, tm=512):
    t, d = h.shape
    dff = w_d_all.shape[-2]
    return pl.pallas_call(
        functools.partial(_ffn_kernel, dff=dff),
        grid=(t // tm,),
        in_specs=[
            pl.BlockSpec((tm, d), lambda i: (i, 0)),
            _resident((1, d), lead),
            _resident((d, 2 * dff), lead),
            _resident((dff, d), lead),
        ],
        out_specs=pl.BlockSpec((tm, d), lambda i: (i, 0)),
        out_shape=jax.ShapeDtypeStruct((t, d), F32),
        compiler_params=_cparams(("parallel",)),
        name="ffn",
    )(h, g_all, w_gu_all, w_d_all)


CARRY = 8


def _conv_kernel(x_ref, g_ref, win_ref, cw_ref, wout_ref, o_ref, vbuf_ref, *, tm, d):
    j = pl.program_id(1)

    @pl.when(j == 0)
    def _():
        vbuf_ref[0:CARRY, :] = jnp.zeros((CARRY, d), F32)

    x = x_ref[0]
    xn = _rms(x, g_ref[...]).astype(BF16)
    proj = _dot(xn, win_ref[...])
    bg = proj[:, 0:d]
    v = proj[:, d:2 * d] * proj[:, 2 * d:3 * d]
    vbuf_ref[CARRY:CARRY + tm, :] = v
    cw = cw_ref[...]
    conv = (cw[0:1, :] * vbuf_ref[CARRY - 2:CARRY - 2 + tm, :]
            + cw[1:2, :] * vbuf_ref[CARRY - 1:CARRY - 1 + tm, :]
            + cw[2:3, :] * v)
    y = (bg * conv).astype(BF16)
    o_ref[0] = x + _dot(y, wout_ref[...])
    vbuf_ref[0:CARRY, :] = vbuf_ref[tm:tm + CARRY, :]


def _conv_mixer(h, g, w_in, conv_w, w_out, *, tm=512):
    b, s, d = h.shape
    return pl.pallas_call(
        functools.partial(_conv_kernel, tm=tm, d=d),
        grid=(b, s // tm),
        in_specs=[
            pl.BlockSpec((1, tm, d), lambda i, j: (i, j, 0)),
            pl.BlockSpec((1, d), lambda i, j: (0, 0)),
            pl.BlockSpec((d, 3 * d), lambda i, j: (0, 0)),
            pl.BlockSpec((conv_w.shape[0], d), lambda i, j: (0, 0)),
            pl.BlockSpec((d, d), lambda i, j: (0, 0)),
        ],
        out_specs=pl.BlockSpec((1, tm, d), lambda i, j: (i, j, 0)),
        out_shape=jax.ShapeDtypeStruct((b, s, d), F32),
        scratch_shapes=[pltpu.VMEM((CARRY + tm, d), F32)],
        compiler_params=_cparams(("arbitrary", "arbitrary")),
        name="conv_mixer",
    )(h, g.reshape(1, d), w_in, conv_w, w_out)


def _kv_kernel(x_ref, g_ref, w_ref, kn_ref, seg_ref, segt_ref, raw_ref, ks_ref, vst_ref, kw_ref, vwt_ref,
               *, tm):
    j = pl.program_id(1)
    xn = _rms(x_ref[0], g_ref[...]).astype(BF16)
    y = _dot(xn, w_ref[...])
    gw = N_KV_GROUPS * HEAD_DIM
    raw_ref[0] = y[:, 0:2 * gw]
    pos = j * tm + lax.broadcasted_iota(jnp.int32, (tm, HEAD_DIM), 0)
    lane = lax.broadcasted_iota(jnp.int32, (tm, HEAD_DIM), 1)
    onehot = jnp.where(lane == pos // SEL_BLOCK, 1.0, 0.0).astype(BF16)
    k_sel = _seg_rms(y[:, 2 * gw:3 * gw], seg_ref[...], segt_ref[...], kn_ref[0:1, :]).astype(BF16)
    k_win = _seg_rms(y[:, 4 * gw:5 * gw], seg_ref[...], segt_ref[...], kn_ref[1:2, :]).astype(BF16)
    for grp in range(N_KV_GROUPS):
        ks_ref[0, grp, :, 0:HEAD_DIM] = k_sel[:, grp * HEAD_DIM:(grp + 1) * HEAD_DIM]
        ks_ref[0, grp, :, HEAD_DIM:2 * HEAD_DIM] = onehot
        kw_ref[0, grp] = k_win[:, grp * HEAD_DIM:(grp + 1) * HEAD_DIM]
    extra = jnp.where(lax.broadcasted_iota(jnp.int32, (V_ROWS - HEAD_DIM, CHUNK), 0) == 0, 1.0, 0.0).astype(BF16)
    for slot, ref in ((3, vst_ref), (5, vwt_ref)):
        vt = y[:, slot * gw:(slot + 1) * gw].T
        for grp in range(N_KV_GROUPS):
            for cc in range(tm // CHUNK):
                ref[0, grp, cc, 0:HEAD_DIM, :] = vt[grp * HEAD_DIM:(grp + 1) * HEAD_DIM,
                                                    cc * CHUNK:(cc + 1) * CHUNK].astype(BF16)
                ref[0, grp, cc, HEAD_DIM:V_ROWS, :] = extra


def _kv_proj(h, g, w, k_norm, *, tm=512):
    b, s, d = h.shape
    n = w.shape[1]
    ng = N_KV_GROUPS
    gw = ng * HEAD_DIM
    cpt = tm // CHUNK
    assert s // SEL_BLOCK <= HEAD_DIM
    seg, seg_t = _seg_matrices(gw)
    gains = jnp.tile(k_norm[1:3], (1, ng))
    return pl.pallas_call(
        functools.partial(_kv_kernel, tm=tm),
        grid=(b, s // tm),
        in_specs=[
            pl.BlockSpec((1, tm, d), lambda i, j: (i, j, 0)),
            _resident((1, d)),
            _resident((d, n)),
            _resident(gains.shape),
            _resident(seg.shape),
            _resident(seg_t.shape),
        ],
        out_specs=[
            pl.BlockSpec((1, tm, 2 * gw), lambda i, j: (i, j, 0)),
            pl.BlockSpec((1, ng, tm, 2 * HEAD_DIM), lambda i, j: (i, 0, j, 0)),
            pl.BlockSpec((1, ng, cpt, V_ROWS, CHUNK), lambda i, j: (i, 0, j, 0, 0)),
            pl.BlockSpec((1, ng, tm, HEAD_DIM), lambda i, j: (i, 0, j, 0)),
            pl.BlockSpec((1, ng, cpt, V_ROWS, CHUNK), lambda i, j: (i, 0, j, 0, 0)),
        ],
        out_shape=[
            jax.ShapeDtypeStruct((b, s, 2 * gw), F32),
            jax.ShapeDtypeStruct((b, ng, s, 2 * HEAD_DIM), BF16),
            jax.ShapeDtypeStruct((b, ng, s // CHUNK, V_ROWS, CHUNK), BF16),
            jax.ShapeDtypeStruct((b, ng, s, HEAD_DIM), BF16),
            jax.ShapeDtypeStruct((b, ng, s // CHUNK, V_ROWS, CHUNK), BF16),
        ],
        compiler_params=_cparams(("parallel", "parallel")),
        name="kv_proj",
    )(h, g.reshape(1, d), w, gains, seg, seg_t)


def _cmp_kernel(x_ref, pos_ref, w1_ref, b1_ref, w2_ref, kn_ref, o_ref):
    c = pl.program_id(0)
    nb, ne, width = x_ref.shape
    rows = nb * ne
    half = width // 2
    x = x_ref[...].reshape(rows, width)
    pos = pos_ref[...]
    w1 = w1_ref[...]
    h_even = _dot((x + pos).astype(BF16), w1)
    a = _dot((x[:, half:] + pos[:, :half]).astype(BF16), w1[0:half])
    b = _dot((x[:, :half] + pos[:, half:]).astype(BF16), w1[half:width])
    h_odd = a + pltpu.roll(b, rows - 1, 0)

    def tail(hid):
        y = _dot(jax.nn.gelu(hid + b1_ref[...]).astype(BF16), w2_ref[...])
        y = jnp.where(c == 0, _rms(y, kn_ref[...]), y)
        return y.astype(o_ref.dtype).reshape(nb, ne, y.shape[-1])

    o_ref[:, 0:ne, :] = tail(h_even)
    o_ref[:, ne:2 * ne, :] = tail(h_odd)


def _compress(xe, pos, w1, b1, w2, kn, *, nb=8):
    _, nseq, ne, width = xe.shape
    nb = min(nb, nseq)
    hid = w1.shape[-1]
    dk = w2.shape[-1]
    return pl.pallas_call(
        _cmp_kernel,
        grid=(2, nseq // nb),
        in_specs=[
            pl.BlockSpec((None, nb, ne, width), lambda c, i: (c, i, 0, 0)),
            pl.BlockSpec((None, 1, width), lambda c, i: (c, 0, 0)),
            pl.BlockSpec((None, width, hid), lambda c, i: (c, 0, 0)),
            pl.BlockSpec((None, 1, hid), lambda c, i: (c, 0, 0)),
            pl.BlockSpec((None, hid, dk), lambda c, i: (c, 0, 0)),
            pl.BlockSpec((1, dk), lambda c, i: (0, 0)),
        ],
        out_specs=pl.BlockSpec((None, nb, 2 * ne, dk), lambda c, i: (c, i, 0, 0)),
        out_shape=jax.ShapeDtypeStruct((2, nseq, 2 * ne, dk), BF16),
        compiler_params=_cparams(("parallel", "parallel")),
        name="compress",
    )(xe, pos.reshape(2, 1, width), w1, b1.reshape(2, 1, hid), w2, kn.reshape(1, dk))


def _qg_kernel(x_ref, g_ref, w_ref, qn_ref, seg_ref, segt_ref, q_ref, gate_ref, *, dq):
    xn = _rms(x_ref[...], g_ref[...]).astype(BF16)
    y = _dot(xn, w_ref[...])
    qn = qn_ref[...] * (HEAD_DIM ** -0.5 * LOG2E)
    q_ref[...] = _seg_rms(y[:, 0:dq], seg_ref[...], segt_ref[...], qn).astype(q_ref.dtype)
    gate_ref[...] = jax.nn.sigmoid(y[:, dq:])


def _qg_proj(h, g, w, q_norm, *, tm=512):
    t, d = h.shape
    n = w.shape[1]
    dq = N_HEADS * HEAD_DIM
    seg, seg_t = _seg_matrices(dq)
    return pl.pallas_call(
        functools.partial(_qg_kernel, dq=dq),
        grid=(t // tm,),
        in_specs=[
            pl.BlockSpec((tm, d), lambda i: (i, 0)),
            _resident((1, d)),
            _resident((d, n)),
            _resident((1, dq)),
            _resident(seg.shape),
            _resident(seg_t.shape),
        ],
        out_specs=[pl.BlockSpec((tm, dq), lambda i: (i, 0)),
                   pl.BlockSpec((tm, n - dq), lambda i: (i, 0))],
        out_shape=[jax.ShapeDtypeStruct((t, dq), BF16),
                   jax.ShapeDtypeStruct((t, n - dq), F32)],
        compiler_params=_cparams(("parallel",)),
        name="qg_proj",
    )(h, g.reshape(1, d), w, jnp.tile(q_norm, N_HEADS).reshape(1, dq), seg, seg_t)


def _split3(x):
    hi = x.astype(BF16)
    r = x - hi.astype(F32)
    mid = r.astype(BF16)
    lo = (r - mid.astype(F32)).astype(BF16)
    return hi, mid, lo


def _cmp_block_of_row(row, nrows):
    return jnp.where(row < nrows // 2, 2 * row, 2 * row - (nrows - 1))


def _nsa_kernel(q_ref, gate_ref, kc_ref, vct_ref, ks_ref, vst_ref, kw_ref, vwt_ref, ovt_ref, eye_ref,
                o_ref, qa_ref, m_ref, acc_ref, out_ref, s_ref, cmax_ref, *, n_cmp, n_blk):
    hpg = HEADS_PER_GROUP
    tq = CHUNK
    dk = HEAD_DIM
    wide = hpg * tq
    i = pl.program_id(2)
    t0 = i * tq

    def heads(fn):
        return jnp.concatenate([fn(h) for h in range(hpg)], axis=1)

    q = q_ref[0]
    eye = eye_ref[...]
    qa_ref[0:dk, :] = heads(lambda h: _dot_nt(eye, q[:, h * dk:(h + 1) * dk]).astype(BF16))
    gate_t = gate_ref[0].T
    gates = [heads(lambda h: gate_t[h * N_BRANCH + br:h * N_BRANCH + br + 1, :]) for br in range(N_BRANCH)]

    kk = lax.broadcasted_iota(jnp.int32, (tq, wide), 0)
    tt = lax.broadcasted_iota(jnp.int32, (tq, wide), 1) & (tq - 1)

    m_ref[...] = jnp.full(m_ref.shape, MASK_SCORE, F32)
    acc_ref[...] = jnp.zeros(acc_ref.shape, F32)

    def key_rows(ref, c):
        return ref[0, 0, pl.ds(pl.multiple_of(c * tq, tq), tq), :]

    def produce(b, k, qrows, keep):
        s = _dot(k, qa_ref[0:qrows, :])
        if keep is not None:
            s = jnp.where(keep, s, MASK_SCORE)
        s_ref[b] = s
        cmax_ref[b] = jnp.max(s, axis=0, keepdims=True)

    def consume(b, vt, st):
        m_prev = m_ref[st]
        m_new = jnp.maximum(m_prev, cmax_ref[b])
        alpha = jnp.exp2(m_prev - m_new)
        p = jnp.exp2(s_ref[b] - m_new)
        acc_ref[st] = alpha * acc_ref[st] + _dot(vt, p.astype(BF16))
        m_ref[st] = m_new

    def flash_out(st, branch):
        out_ref[...] += gates[branch] * (acc_ref[st, 0:dk, :] * (1.0 / acc_ref[st, dk:dk + 1, :]))

    assert WINDOW == 2 * tq
    lo = jnp.maximum(i - 2, 0)
    mid = jnp.maximum(i - 1, 0)
    produce(0, key_rows(kw_ref, lo), dk, kk > tt + jnp.where(i >= 2, 0, tq))

    ncp = kc_ref.shape[2]
    nidx = _cmp_block_of_row(lax.broadcasted_iota(jnp.int32, (ncp, wide), 0), ncp)
    tpos = t0 + (lax.broadcasted_iota(jnp.int32, (ncp, wide), 1) & (tq - 1))
    valid = (nidx * CMP_STRIDE + (CMP_BLOCK - 1) <= tpos) & (nidx < n_cmp)
    s = jnp.where(valid, _dot(kc_ref[0, 0], qa_ref[0:dk, :]), MASK_SCORE)
    e = jnp.exp2(s - jnp.max(s, axis=0, keepdims=True))
    p = jnp.where(valid, e * (1.0 / jnp.sum(e, axis=0, keepdims=True)), 0.0)
    out_ref[...] = gates[0] * _dot(vct_ref[0, 0], p.astype(BF16))
    psum = p[:, 0:tq]
    for h in range(1, hpg):
        psum = psum + p[:, h * tq:(h + 1) * tq]

    produce(1, key_rows(kw_ref, mid), dk, kk > tt + jnp.where(i >= 1, -tq, tq))
    consume(0, vwt_ref[0, 0, lo], 1)

    ovt = ovt_ref[...]
    imp = None
    for part in _split3(psum):
        term = _dot(ovt, part)
        imp = term if imp is None else imp + term
    jblk = lax.broadcasted_iota(jnp.int32, (n_blk, tq), 0)
    cur = (t0 + lax.broadcasted_iota(jnp.int32, (n_blk, tq), 1)) // SEL_BLOCK
    forced = (jblk == 0) | (jblk == cur) | (jblk == cur - 1)
    imp = jnp.where(forced, FORCE_SCORE, imp)
    imp = jnp.where(jblk > cur, MASK_SCORE, imp)
    sub = 8
    parts = [imp[r:r + sub] for r in range(0, n_blk, sub)]
    ranks = [jnp.zeros((sub, tq), F32) for _ in parts]
    jsub = lax.broadcasted_iota(jnp.int32, (sub, tq), 0)
    for ii in range(n_blk):
        row = imp[ii:ii + 1, :]
        for gi, x in enumerate(parts):
            if ii < gi * sub:
                beats = row >= x
            elif ii >= (gi + 1) * sub:
                beats = row > x
            else:
                beats = (row > x) | ((row == x) & (jsub > ii - gi * sub))
            ranks[gi] = ranks[gi] + jnp.where(beats, 1.0, 0.0)
    rank = jnp.concatenate(ranks, axis=0)
    chosen = (rank < float(min(N_SELECT, n_blk))) & (jblk <= cur)
    bias = jnp.where(chosen, 0.0, MASK_SCORE)
    bias = jnp.concatenate([bias, jnp.zeros((dk - n_blk, tq), F32)], axis=0).astype(BF16)
    qa_ref[dk:2 * dk, :] = heads(lambda h: bias)

    produce(0, key_rows(kw_ref, i), dk, kk <= tt)
    consume(1, vwt_ref[0, 0, mid], 1)

    npair = i // 2
    held = jnp.where(i >= 2, 2 * npair - 1, i)
    last = jnp.where(i % 2 == 0, i + 1, i - 1)
    produce(1, key_rows(ks_ref, i), 2 * dk, kk <= tt)
    consume(0, vwt_ref[0, 0, i], 1)
    flash_out(1, 2)

    def pair_body(p, carry):
        c = 2 * p
        produce(0, key_rows(ks_ref, c), 2 * dk, None)
        consume(1, vst_ref[0, 0, jnp.where(p == 0, i, c - 1)], 0)
        produce(1, key_rows(ks_ref, c + 1), 2 * dk, None)
        consume(0, vst_ref[0, 0, c], 0)
        return carry

    lax.fori_loop(0, npair, pair_body, 0)
    produce(0, key_rows(ks_ref, last), 2 * dk, None)
    consume(1, vst_ref[0, 0, held], 0)
    consume(0, vst_ref[0, 0, last], 0)
    flash_out(0, 1)

    out = out_ref[...]
    out = jnp.concatenate([out[:, h * tq:(h + 1) * tq] for h in range(hpg)], axis=0)
    o_ref[0] = out.T.astype(o_ref.dtype)


def _nsa(q, gate, kc, vct, ks, vst, kw, vwt):
    b, s, dq = q.shape
    g = N_KV_GROUPS
    hpg = HEADS_PER_GROUP
    dk = HEAD_DIM
    gd = hpg * dk
    tq = CHUNK
    n_blk = s // SEL_BLOCK
    n_cmp = (s - CMP_BLOCK) // CMP_STRIDE + 1
    ncp = kc.shape[2]
    assert WINDOW == 2 * tq and n_blk <= dk and (s // tq) % 2 == 0
    rows_np = np.arange(ncp)
    cmp_idx = np.where(rows_np < ncp // 2, 2 * rows_np, 2 * rows_np - (ncp - 1))
    cmp_start = cmp_idx * CMP_STRIDE
    blk_start = np.arange(n_blk) * SEL_BLOCK
    ovt = ((cmp_start[None, :] < blk_start[:, None] + SEL_BLOCK)
           & (cmp_start[None, :] + CMP_BLOCK > blk_start[:, None])
           & (cmp_idx[None, :] < n_cmp)).astype(np.float32)
    idx = lambda bi, gi, i: (bi, gi, 0, 0)
    idx5 = lambda bi, gi, i: (bi, gi, 0, 0, 0)
    return pl.pallas_call(
        functools.partial(_nsa_kernel, n_cmp=n_cmp, n_blk=n_blk),
        grid=(b, g, s // tq),
        in_specs=[
            pl.BlockSpec((1, tq, gd), lambda bi, gi, i: (bi, i, gi)),
            pl.BlockSpec((1, tq, LANES), lambda bi, gi, i: (bi, i, gi)),
            pl.BlockSpec((1, 1, ncp, dk), idx),
            pl.BlockSpec((1, 1, dk, ncp), idx),
            pl.BlockSpec((1, 1, s, 2 * dk), idx),
            pl.BlockSpec((1, 1, s // tq, V_ROWS, tq), idx5),
            pl.BlockSpec((1, 1, s, dk), idx),
            pl.BlockSpec((1, 1, s // tq, V_ROWS, tq), idx5),
            pl.BlockSpec((n_blk, ncp), lambda bi, gi, i: (0, 0)),
            pl.BlockSpec((dk, dk), lambda bi, gi, i: (0, 0)),
        ],
        out_specs=pl.BlockSpec((1, tq, gd), lambda bi, gi, i: (bi, i, gi)),
        out_shape=jax.ShapeDtypeStruct((b, s, dq), BF16),
        scratch_shapes=[
            pltpu.VMEM((2 * dk, hpg * tq), BF16),
            pltpu.VMEM((2, 1, hpg * tq), F32),
            pltpu.VMEM((2, V_ROWS, hpg * tq), F32),
            pltpu.VMEM((dk, hpg * tq), F32),
            pltpu.VMEM((2, tq, hpg * tq), F32),
            pltpu.VMEM((2, 1, hpg * tq), F32),
        ],
        compiler_params=_cparams(("parallel", "parallel", "arbitrary")),
        name="nsa",
    )(q, gate, kc, vct, ks, vst, kw, vwt, jnp.asarray(ovt, BF16), jnp.eye(dk, dtype=BF16))


def _oproj_kernel(o_ref, w_ref, h_ref, out_ref):
    out_ref[...] = h_ref[...] + _dot(o_ref[...], w_ref[...])


def _out_proj(o, w, h, *, tm=512):
    t, d = h.shape
    k = o.shape[1]
    return pl.pallas_call(
        _oproj_kernel,
        grid=(t // tm,),
        in_specs=[
            pl.BlockSpec((tm, k), lambda i: (i, 0)),
            pl.BlockSpec((k, d), lambda i: (0, 0)),
            pl.BlockSpec((tm, d), lambda i: (i, 0)),
        ],
        out_specs=pl.BlockSpec((tm, d), lambda i: (i, 0)),
        out_shape=jax.ShapeDtypeStruct((t, d), F32),
        compiler_params=_cparams(("parallel",)),
        name="out_proj",
    )(o, w, h)


def _even_blocks(raw):
    b, s, _ = raw.shape
    ne = s // CMP_BLOCK
    x = raw.reshape(b, ne, CMP_BLOCK, 2, N_KV_GROUPS, HEAD_DIM).transpose(3, 0, 4, 1, 2, 5)
    return x.reshape(2, b * N_KV_GROUPS, ne, CMP_BLOCK * HEAD_DIM)


def _gate_columns(w_qg):
    dq = N_HEADS * HEAD_DIM
    per = HEADS_PER_GROUP * N_BRANCH
    wg = w_qg[:, dq:].reshape(-1, N_KV_GROUPS, per)
    wg = jnp.pad(wg, ((0, 0), (0, 0), (0, LANES - per)))
    return jnp.concatenate([w_qg[:, :dq], wg.reshape(-1, N_KV_GROUPS * LANES)], axis=1)


def kernel(x, ffn_norm, ffn_w_gate_up, ffn_w_down, mix_norm, conv_w_in, conv_w, conv_w_out, kv_norm, kv_w,
           cmp_pos, cmp_w1, cmp_b1, cmp_w2, k_norm, nsa_w_qg, q_norm, nsa_w_o):
    b, s, d = x.shape
    t = b * s
    depth = ffn_norm.shape[0]
    n_a = conv_w_in.shape[0]
    g = N_KV_GROUPS

    ffn_g = ffn_norm.reshape(depth, 2, 1, d)
    ffn_wgu = ffn_w_gate_up.astype(BF16)
    ffn_wd = ffn_w_down.astype(BF16)

    def ffn(h2, layer, j):
        return _ffn(h2, ffn_g, ffn_wgu, ffn_wd, (layer, j))

    h = x.reshape(t, d)
    shared = None
    for layer in range(depth):
        h = ffn(h, layer, 0)
        if layer < n_a:
            h = _conv_mixer(h.reshape(b, s, d), mix_norm[layer], conv_w_in[layer].astype(BF16),
                            conv_w[layer], conv_w_out[layer].astype(BF16)).reshape(t, d)
        else:
            li = layer - n_a
            q, gate = _qg_proj(h, mix_norm[layer], _gate_columns(nsa_w_qg[li]).astype(BF16), q_norm[li])
            o = _nsa(q.reshape(b, s, -1), gate.reshape(b, s, -1), *shared)
            h = _out_proj(o.reshape(t, -1), nsa_w_o[li].astype(BF16), h)
        h = ffn(h, layer, 1)
        if layer == n_a - 1:
            raw, ks, vst, kw, vwt = _kv_proj(h.reshape(b, s, d), kv_norm, kv_w.astype(BF16), k_norm)
            cmp = _compress(_even_blocks(raw), cmp_pos.reshape(2, -1), cmp_w1.astype(BF16), cmp_b1,
                            cmp_w2.astype(BF16), k_norm[0])
            cmp = cmp.reshape(2, b, g, -1, HEAD_DIM)
            shared = (cmp[0], cmp[1].transpose(0, 1, 3, 2), ks, vst, kw, vwt)
    return h.reshape(b, s, d)
```

```python
import functools

import numpy as np
import jax
import jax.numpy as jnp
from jax import lax
from jax.experimental import pallas as pl
from jax.experimental.pallas import tpu as pltpu

EPS = 1e-6
N_HEADS = 16
HEAD_DIM = 64
N_KV_GROUPS = 4
HEADS_PER_GROUP = N_HEADS // N_KV_GROUPS
N_KV_SLOTS = 6
N_BRANCH = 3
CMP_BLOCK = 32
CMP_STRIDE = 16
SEL_BLOCK = 64
N_SELECT = 16
WINDOW = 512
FORCE_SCORE = 1e9
MASK_SCORE = -1e30
LOG2E = 1.4426950408889634

LANES = 128
CHUNK = 256
V_ROWS = HEAD_DIM + 16
VMEM_LIMIT = 56 * 1024 * 1024

BF16 = jnp.bfloat16
F32 = jnp.float32


def _cparams(sem):
    return pltpu.CompilerParams(dimension_semantics=sem, vmem_limit_bytes=VMEM_LIMIT)


def _rms(x, g):
    ms = jnp.mean(x * x, axis=-1, keepdims=True)
    return x * lax.rsqrt(ms + EPS) * g


def _dot(a, b):
    return jnp.dot(a, b, preferred_element_type=F32)


def _split2(x):
    hi = x.astype(BF16)
    return hi, (x - hi.astype(F32)).astype(BF16)


def _seg_rms(y, seg, seg_t, gain):
    ss = None
    for part in _split2(y * y):
        term = _dot(part, seg)
        ss = term if ss is None else ss + term
    r = lax.rsqrt(ss * (1.0 / HEAD_DIM) + EPS)
    scale = None
    for part in _split2(r):
        term = _dot(part, seg_t)
        scale = term if scale is None else scale + term
    return y * scale * gain


def _seg_matrices(width):
    ind = (np.arange(width)[:, None] // HEAD_DIM == np.arange(LANES)[None, :]).astype(np.float32)
    return jnp.asarray(ind, BF16), jnp.asarray(ind.T, BF16)


def _dot_nt(a, b):
    return lax.dot_general(a, b, (((1,), (1,)), ((), ())), preferred_element_type=F32)


def _ffn_kernel(x_ref, g_ref, wgu_ref, wd_ref, *rest, dff):
    x = x_ref[...]
    if len(rest) == 3:
        a_ref, w_ref, o_ref = rest
        x = x + _dot(a_ref[...], w_ref[...])
    else:
        (o_ref,) = rest
    xn = _rms(x, g_ref[...]).astype(BF16)
    ab = _dot(xn, wgu_ref[...])
    a = ab[:, 0:dff]
    h = (a * jax.nn.sigmoid(a) * ab[:, dff:2 * dff]).astype(BF16)
    o_ref[...] = x + 0.5 * _dot(h, wd_ref[...])


def _resident(shape, lead=()):
    index = tuple(lead) + (0,) * len(shape)
    return pl.BlockSpec((None,) * len(lead) + tuple(shape), lambda *_: index, pipeline_mode=pl.Buffered(1))


def _ffn(h, g_all, w_gu_all, w_d_all, lead, proj=None, *, tm=512):
    t, d = h.shape
    dff = w_d_all.shape[-2]
    in_specs = [
        pl.BlockSpec((tm, d), lambda i: (i, 0)),
        _resident((1, d), lead),
        _resident((d, 2 * dff), lead),
        _resident((dff, d), lead),
    ]
    args = [h, g_all, w_gu_all, w_d_all]
    if proj is not None:
        a, w = proj
        in_specs += [pl.BlockSpec((tm, a.shape[1]), lambda i: (i, 0)), _resident(w.shape)]
        args += [a, w]
    return pl.pallas_call(
        functools.partial(_ffn_kernel, dff=dff),
        grid=(t // tm,),
        in_specs=in_specs,
        out_specs=pl.BlockSpec((tm, d), lambda i: (i, 0)),
        out_shape=jax.ShapeDtypeStruct((t, d), F32),
        compiler_params=_cparams(("parallel",)),
        name="ffn",
    )(*args)


CARRY = 8


def _conv_kernel(x_ref, g_ref, win_ref, cw_ref, wout_ref, o_ref, vbuf_ref, *, tm, d):
    j = pl.program_id(1)

    @pl.when(j == 0)
    def _():
        vbuf_ref[0:CARRY, :] = jnp.zeros((CARRY, d), F32)

    x = x_ref[0]
    xn = _rms(x, g_ref[...]).astype(BF16)
    proj = _dot(xn, win_ref[...])
    bg = proj[:, 0:d]
    v = proj[:, d:2 * d] * proj[:, 2 * d:3 * d]
    vbuf_ref[CARRY:CARRY + tm, :] = v
    cw = cw_ref[...]
    conv = (cw[0:1, :] * vbuf_ref[CARRY - 2:CARRY - 2 + tm, :]
            + cw[1:2, :] * vbuf_ref[CARRY - 1:CARRY - 1 + tm, :]
            + cw[2:3, :] * v)
    y = (bg * conv).astype(BF16)
    o_ref[0] = x + _dot(y, wout_ref[...])
    vbuf_ref[0:CARRY, :] = vbuf_ref[tm:tm + CARRY, :]


def _conv_mixer(h, g, w_in, conv_w, w_out, *, tm=512):
    b, s, d = h.shape
    return pl.pallas_call(
        functools.partial(_conv_kernel, tm=tm, d=d),
        grid=(b, s // tm),
        in_specs=[
            pl.BlockSpec((1, tm, d), lambda i, j: (i, j, 0)),
            pl.BlockSpec((1, d), lambda i, j: (0, 0)),
            pl.BlockSpec((d, 3 * d), lambda i, j: (0, 0)),
            pl.BlockSpec((conv_w.shape[0], d), lambda i, j: (0, 0)),
            pl.BlockSpec((d, d), lambda i, j: (0, 0)),
        ],
        out_specs=pl.BlockSpec((1, tm, d), lambda i, j: (i, j, 0)),
        out_shape=jax.ShapeDtypeStruct((b, s, d), F32),
        scratch_shapes=[pltpu.VMEM((CARRY + tm, d), F32)],
        compiler_params=_cparams(("arbitrary", "arbitrary")),
        name="conv_mixer",
    )(h, g.reshape(1, d), w_in, conv_w, w_out)


def _kv_kernel(x_ref, g_ref, w_ref, kn_ref, seg_ref, segt_ref, blk_ref, ks_ref, vst_ref, kw_ref, vwt_ref,
               raw_ref, *, tm):
    j = pl.program_id(1)
    xn = _rms(x_ref[0], g_ref[...]).astype(BF16)
    y = _dot(xn, w_ref[...])
    gw = N_KV_GROUPS * HEAD_DIM
    per = LANES // HEAD_DIM
    for slab in range(raw_ref.shape[0]):
        raw_ref[slab] = y[:, slab * LANES:(slab + 1) * LANES]
    for l in range(CMP_BLOCK):
        for slab in range(raw_ref.shape[0]):
            rows = raw_ref[slab, pl.ds(l, tm // CMP_BLOCK, stride=CMP_BLOCK), :]
            for k in range(per):
                sg = slab * per + k
                blk_ref[sg // N_KV_GROUPS, 0, sg % N_KV_GROUPS, :, l * HEAD_DIM:(l + 1) * HEAD_DIM] = (
                    rows[:, k * HEAD_DIM:(k + 1) * HEAD_DIM])
    pos = j * tm + lax.broadcasted_iota(jnp.int32, (tm, HEAD_DIM), 0)
    lane = lax.broadcasted_iota(jnp.int32, (tm, HEAD_DIM), 1)
    onehot = jnp.where(lane == pos // SEL_BLOCK, 1.0, 0.0).astype(BF16)
    k_sel = _seg_rms(y[:, 2 * gw:3 * gw], seg_ref[...], segt_ref[...], kn_ref[0:1, :]).astype(BF16)
    k_win = _seg_rms(y[:, 4 * gw:5 * gw], seg_ref[...], segt_ref[...], kn_ref[1:2, :]).astype(BF16)
    for grp in range(N_KV_GROUPS):
        ks_ref[0, grp, :, 0:HEAD_DIM] = k_sel[:, grp * HEAD_DIM:(grp + 1) * HEAD_DIM]
        ks_ref[0, grp, :, HEAD_DIM:2 * HEAD_DIM] = onehot
        kw_ref[0, grp] = k_win[:, grp * HEAD_DIM:(grp + 1) * HEAD_DIM]
    extra = jnp.where(lax.broadcasted_iota(jnp.int32, (V_ROWS - HEAD_DIM, CHUNK), 0) == 0, 1.0, 0.0).astype(BF16)
    for slot, ref in ((3, vst_ref), (5, vwt_ref)):
        vt = y[:, slot * gw:(slot + 1) * gw].T
        for grp in range(N_KV_GROUPS):
            for cc in range(tm // CHUNK):
                ref[0, grp, cc, 0:HEAD_DIM, :] = vt[grp * HEAD_DIM:(grp + 1) * HEAD_DIM,
                                                    cc * CHUNK:(cc + 1) * CHUNK].astype(BF16)
                ref[0, grp, cc, HEAD_DIM:V_ROWS, :] = extra


def _kv_proj(h, g, w, k_norm, *, tm=512):
    b, s, d = h.shape
    n = w.shape[1]
    ng = N_KV_GROUPS
    gw = ng * HEAD_DIM
    cpt = tm // CHUNK
    assert s // SEL_BLOCK <= HEAD_DIM
    seg, seg_t = _seg_matrices(gw)
    gains = jnp.tile(k_norm[1:3], (1, ng))
    return pl.pallas_call(
        functools.partial(_kv_kernel, tm=tm),
        grid=(b, s // tm),
        in_specs=[
            pl.BlockSpec((1, tm, d), lambda i, j: (i, j, 0)),
            _resident((1, d)),
            _resident((d, n)),
            _resident(gains.shape),
            _resident(seg.shape),
            _resident(seg_t.shape),
        ],
        out_specs=[
            pl.BlockSpec((2, 1, ng, tm // CMP_BLOCK, CMP_BLOCK * HEAD_DIM), lambda i, j: (0, i, 0, j, 0)),
            pl.BlockSpec((1, ng, tm, 2 * HEAD_DIM), lambda i, j: (i, 0, j, 0)),
            pl.BlockSpec((1, ng, cpt, V_ROWS, CHUNK), lambda i, j: (i, 0, j, 0, 0)),
            pl.BlockSpec((1, ng, tm, HEAD_DIM), lambda i, j: (i, 0, j, 0)),
            pl.BlockSpec((1, ng, cpt, V_ROWS, CHUNK), lambda i, j: (i, 0, j, 0, 0)),
        ],
        out_shape=[
            jax.ShapeDtypeStruct((2, b, ng, s // CMP_BLOCK, CMP_BLOCK * HEAD_DIM), F32),
            jax.ShapeDtypeStruct((b, ng, s, 2 * HEAD_DIM), BF16),
            jax.ShapeDtypeStruct((b, ng, s // CHUNK, V_ROWS, CHUNK), BF16),
            jax.ShapeDtypeStruct((b, ng, s, HEAD_DIM), BF16),
            jax.ShapeDtypeStruct((b, ng, s // CHUNK, V_ROWS, CHUNK), BF16),
        ],
        scratch_shapes=[pltpu.VMEM((2 * gw // LANES, tm, LANES), F32)],
        compiler_params=_cparams(("parallel", "parallel")),
        name="kv_proj",
    )(h, g.reshape(1, d), w, gains, seg, seg_t)


def _cmp_kernel(x_ref, pos_ref, w1_ref, b1_ref, w2_ref, kn_ref, o_ref):
    c = pl.program_id(0)
    nb, ne, width = x_ref.shape
    rows = nb * ne
    half = width // 2
    x = x_ref[...].reshape(rows, width)
    pos = pos_ref[...]
    w1 = w1_ref[...]
    h_even = _dot((x + pos).astype(BF16), w1)
    a = _dot((x[:, half:] + pos[:, :half]).astype(BF16), w1[0:half])
    b = _dot((x[:, :half] + pos[:, half:]).astype(BF16), w1[half:width])
    h_odd = a + pltpu.roll(b, rows - 1, 0)

    def tail(hid):
        y = _dot(jax.nn.gelu(hid + b1_ref[...]).astype(BF16), w2_ref[...])
        y = jnp.where(c == 0, _rms(y, kn_ref[...]), y)
        return y.astype(o_ref.dtype).reshape(nb, ne, y.shape[-1])

    o_ref[:, 0:ne, :] = tail(h_even)
    o_ref[:, ne:2 * ne, :] = tail(h_odd)


def _compress(xe, pos, w1, b1, w2, kn, *, nb=8):
    _, nseq, ne, width = xe.shape
    nb = min(nb, nseq)
    hid = w1.shape[-1]
    dk = w2.shape[-1]
    return pl.pallas_call(
        _cmp_kernel,
        grid=(2, nseq // nb),
        in_specs=[
            pl.BlockSpec((None, nb, ne, width), lambda c, i: (c, i, 0, 0)),
            pl.BlockSpec((None, 1, width), lambda c, i: (c, 0, 0)),
            pl.BlockSpec((None, width, hid), lambda c, i: (c, 0, 0)),
            pl.BlockSpec((None, 1, hid), lambda c, i: (c, 0, 0)),
            pl.BlockSpec((None, hid, dk), lambda c, i: (c, 0, 0)),
            pl.BlockSpec((1, dk), lambda c, i: (0, 0)),
        ],
        out_specs=pl.BlockSpec((None, nb, 2 * ne, dk), lambda c, i: (c, i, 0, 0)),
        out_shape=jax.ShapeDtypeStruct((2, nseq, 2 * ne, dk), BF16),
        compiler_params=_cparams(("parallel", "parallel")),
        name="compress",
    )(xe, pos.reshape(2, 1, width), w1, b1.reshape(2, 1, hid), w2, kn.reshape(1, dk))


def _qg_kernel(x_ref, g_ref, w_ref, qn_ref, seg_ref, segt_ref, q_ref, gate_ref, *, dq):
    xn = _rms(x_ref[...], g_ref[...]).astype(BF16)
    y = _dot(xn, w_ref[...])
    qn = qn_ref[...] * (HEAD_DIM ** -0.5 * LOG2E)
    q_ref[...] = _seg_rms(y[:, 0:dq], seg_ref[...], segt_ref[...], qn).astype(q_ref.dtype)
    gate_ref[...] = jax.nn.sigmoid(y[:, dq:])


def _qg_proj(h, g, w, q_norm, *, tm=512):
    t, d = h.shape
    n = w.shape[1]
    dq = N_HEADS * HEAD_DIM
    seg, seg_t = _seg_matrices(dq)
    return pl.pallas_call(
        functools.partial(_qg_kernel, dq=dq),
        grid=(t // tm,),
        in_specs=[
            pl.BlockSpec((tm, d), lambda i: (i, 0)),
            _resident((1, d)),
            _resident((d, n)),
            _resident((1, dq)),
            _resident(seg.shape),
            _resident(seg_t.shape),
        ],
        out_specs=[pl.BlockSpec((tm, dq), lambda i: (i, 0)),
                   pl.BlockSpec((tm, n - dq), lambda i: (i, 0))],
        out_shape=[jax.ShapeDtypeStruct((t, dq), BF16),
                   jax.ShapeDtypeStruct((t, n - dq), F32)],
        compiler_params=_cparams(("parallel",)),
        name="qg_proj",
    )(h, g.reshape(1, d), w, jnp.tile(q_norm, N_HEADS).reshape(1, dq), seg, seg_t)


def _split3(x):
    hi = x.astype(BF16)
    r = x - hi.astype(F32)
    mid = r.astype(BF16)
    lo = (r - mid.astype(F32)).astype(BF16)
    return hi, mid, lo


def _cmp_block_of_row(row, nrows):
    return jnp.where(row < nrows // 2, 2 * row, 2 * row - (nrows - 1))


def _nsa_kernel(q_ref, gate_ref, kc_ref, vct_ref, ks_ref, vst_ref, kw_ref, vwt_ref, ovt_ref, eye_ref,
                o_ref, qa_ref, m_ref, acc_ref, out_ref, s_ref, cmax_ref, *, n_cmp, n_blk):
    hpg = HEADS_PER_GROUP
    tq = CHUNK
    dk = HEAD_DIM
    wide = hpg * tq
    i = pl.program_id(2)
    t0 = i * tq

    def heads(fn):
        return jnp.concatenate([fn(h) for h in range(hpg)], axis=1)

    q = q_ref[0]
    eye = eye_ref[...]
    qa_ref[0:dk, :] = heads(lambda h: _dot_nt(eye, q[:, h * dk:(h + 1) * dk]).astype(BF16))
    gate_t = gate_ref[0].T
    gates = [heads(lambda h: gate_t[h * N_BRANCH + br:h * N_BRANCH + br + 1, :]) for br in range(N_BRANCH)]

    kk = lax.broadcasted_iota(jnp.int32, (tq, wide), 0)
    tt = lax.broadcasted_iota(jnp.int32, (tq, wide), 1) & (tq - 1)

    m_ref[...] = jnp.full(m_ref.shape, MASK_SCORE, F32)
    acc_ref[...] = jnp.zeros(acc_ref.shape, F32)

    def key_rows(ref, c):
        return ref[0, 0, pl.ds(pl.multiple_of(c * tq, tq), tq), :]

    def produce(b, k, qrows, keep):
        s = _dot(k, qa_ref[0:qrows, :])
        if keep is not None:
            s = jnp.where(keep, s, MASK_SCORE)
        s_ref[b] = s
        cmax_ref[b] = jnp.max(s, axis=0, keepdims=True)

    def consume(b, vt, st):
        m_prev = m_ref[st]
        m_new = jnp.maximum(m_prev, cmax_ref[b])
        alpha = jnp.exp2(m_prev - m_new)
        p = jnp.exp2(s_ref[b] - m_new)
        acc_ref[st] = alpha * acc_ref[st] + _dot(vt, p.astype(BF16))
        m_ref[st] = m_new

    def flash_out(st, branch):
        out_ref[...] += gates[branch] * (acc_ref[st, 0:dk, :] * (1.0 / acc_ref[st, dk:dk + 1, :]))

    assert WINDOW == 2 * tq
    lo = jnp.maximum(i - 2, 0)
    mid = jnp.maximum(i - 1, 0)
    produce(0, key_rows(kw_ref, lo), dk, kk > tt + jnp.where(i >= 2, 0, tq))

    ncp = kc_ref.shape[2]
    nidx = _cmp_block_of_row(lax.broadcasted_iota(jnp.int32, (ncp, wide), 0), ncp)
    tpos = t0 + (lax.broadcasted_iota(jnp.int32, (ncp, wide), 1) & (tq - 1))
    valid = (nidx * CMP_STRIDE + (CMP_BLOCK - 1) <= tpos) & (nidx < n_cmp)
    s = jnp.where(valid, _dot(kc_ref[0, 0], qa_ref[0:dk, :]), MASK_SCORE)
    e = jnp.exp2(s - jnp.max(s, axis=0, keepdims=True))
    p = jnp.where(valid, e * (1.0 / jnp.sum(e, axis=0, keepdims=True)), 0.0)
    out_ref[...] = gates[0] * _dot(vct_ref[0, 0], p.astype(BF16))
    psum = p[:, 0:tq]
    for h in range(1, hpg):
        psum = psum + p[:, h * tq:(h + 1) * tq]

    produce(1, key_rows(kw_ref, mid), dk, kk > tt + jnp.where(i >= 1, -tq, tq))
    consume(0, vwt_ref[0, 0, lo], 1)

    ovt = ovt_ref[...]
    imp = None
    for part in _split3(psum):
        term = _dot(ovt, part)
        imp = term if imp is None else imp + term
    jblk = lax.broadcasted_iota(jnp.int32, (n_blk, tq), 0)
    cur = (t0 + lax.broadcasted_iota(jnp.int32, (n_blk, tq), 1)) // SEL_BLOCK
    forced = (jblk == 0) | (jblk == cur) | (jblk == cur - 1)
    imp = jnp.where(forced, FORCE_SCORE, imp)
    imp = jnp.where(jblk > cur, MASK_SCORE, imp)
    sub = 8
    parts = [imp[r:r + sub] for r in range(0, n_blk, sub)]
    ranks = [jnp.zeros((sub, tq), F32) for _ in parts]
    jsub = lax.broadcasted_iota(jnp.int32, (sub, tq), 0)
    for ii in range(n_blk):
        row = imp[ii:ii + 1, :]
        for gi, x in enumerate(parts):
            if ii < gi * sub:
                beats = row >= x
            elif ii >= (gi + 1) * sub:
                beats = row > x
            else:
                beats = (row > x) | ((row == x) & (jsub > ii - gi * sub))
            ranks[gi] = ranks[gi] + jnp.where(beats, 1.0, 0.0)
    rank = jnp.concatenate(ranks, axis=0)
    chosen = (rank < float(min(N_SELECT, n_blk))) & (jblk <= cur)
    bias = jnp.where(chosen, 0.0, MASK_SCORE)
    bias = jnp.concatenate([bias, jnp.zeros((dk - n_blk, tq), F32)], axis=0).astype(BF16)
    qa_ref[dk:2 * dk, :] = heads(lambda h: bias)

    produce(0, key_rows(kw_ref, i), dk, kk <= tt)
    consume(1, vwt_ref[0, 0, mid], 1)

    npair = i // 2
    held = jnp.where(i >= 2, 2 * npair - 1, i)
    last = jnp.where(i % 2 == 0, i + 1, i - 1)
    produce(1, key_rows(ks_ref, i), 2 * dk, kk <= tt)
    consume(0, vwt_ref[0, 0, i], 1)
    flash_out(1, 2)

    def pair_body(p, carry):
        c = 2 * p
        produce(0, key_rows(ks_ref, c), 2 * dk, None)
        consume(1, vst_ref[0, 0, jnp.where(p == 0, i, c - 1)], 0)
        produce(1, key_rows(ks_ref, c + 1), 2 * dk, None)
        consume(0, vst_ref[0, 0, c], 0)
        return carry

    lax.fori_loop(0, npair, pair_body, 0)
    produce(0, key_rows(ks_ref, last), 2 * dk, None)
    consume(1, vst_ref[0, 0, held], 0)
    consume(0, vst_ref[0, 0, last], 0)
    flash_out(0, 1)

    out = out_ref[...]
    out = jnp.concatenate([out[:, h * tq:(h + 1) * tq] for h in range(hpg)], axis=0)
    o_ref[0] = out.T.astype(o_ref.dtype)


def _nsa(q, gate, kc, vct, ks, vst, kw, vwt):
    b, s, dq = q.shape
    g = N_KV_GROUPS
    hpg = HEADS_PER_GROUP
    dk = HEAD_DIM
    gd = hpg * dk
    tq = CHUNK
    n_blk = s // SEL_BLOCK
    n_cmp = (s - CMP_BLOCK) // CMP_STRIDE + 1
    ncp = kc.shape[2]
    assert WINDOW == 2 * tq and n_blk <= dk and (s // tq) % 2 == 0
    rows_np = np.arange(ncp)
    cmp_idx = np.where(rows_np < ncp // 2, 2 * rows_np, 2 * rows_np - (ncp - 1))
    cmp_start = cmp_idx * CMP_STRIDE
    blk_start = np.arange(n_blk) * SEL_BLOCK
    ovt = ((cmp_start[None, :] < blk_start[:, None] + SEL_BLOCK)
           & (cmp_start[None, :] + CMP_BLOCK > blk_start[:, None])
           & (cmp_idx[None, :] < n_cmp)).astype(np.float32)
    idx = lambda bi, gi, i: (bi, gi, 0, 0)
    idx5 = lambda bi, gi, i: (bi, gi, 0, 0, 0)
    return pl.pallas_call(
        functools.partial(_nsa_kernel, n_cmp=n_cmp, n_blk=n_blk),
        grid=(b, g, s // tq),
        in_specs=[
            pl.BlockSpec((1, tq, gd), lambda bi, gi, i: (bi, i, gi)),
            pl.BlockSpec((1, tq, LANES), lambda bi, gi, i: (bi, i, gi)),
            pl.BlockSpec((1, 1, ncp, dk), idx),
            pl.BlockSpec((1, 1, dk, ncp), idx),
            pl.BlockSpec((1, 1, s, 2 * dk), idx),
            pl.BlockSpec((1, 1, s // tq, V_ROWS, tq), idx5),
            pl.BlockSpec((1, 1, s, dk), idx),
            pl.BlockSpec((1, 1, s // tq, V_ROWS, tq), idx5),
            pl.BlockSpec((n_blk, ncp), lambda bi, gi, i: (0, 0)),
            pl.BlockSpec((dk, dk), lambda bi, gi, i: (0, 0)),
        ],
        out_specs=pl.BlockSpec((1, tq, gd), lambda bi, gi, i: (bi, i, gi)),
        out_shape=jax.ShapeDtypeStruct((b, s, dq), BF16),
        scratch_shapes=[
            pltpu.VMEM((2 * dk, hpg * tq), BF16),
            pltpu.VMEM((2, 1, hpg * tq), F32),
            pltpu.VMEM((2, V_ROWS, hpg * tq), F32),
            pltpu.VMEM((dk, hpg * tq), F32),
            pltpu.VMEM((2, tq, hpg * tq), F32),
            pltpu.VMEM((2, 1, hpg * tq), F32),
        ],
        compiler_params=_cparams(("parallel", "parallel", "arbitrary")),
        name="nsa",
    )(q, gate, kc, vct, ks, vst, kw, vwt, jnp.asarray(ovt, BF16), jnp.eye(dk, dtype=BF16))


def _gate_columns(w_qg):
    dq = N_HEADS * HEAD_DIM
    per = HEADS_PER_GROUP * N_BRANCH
    wg = w_qg[:, dq:].reshape(-1, N_KV_GROUPS, per)
    wg = jnp.pad(wg, ((0, 0), (0, 0), (0, LANES - per)))
    return jnp.concatenate([w_qg[:, :dq], wg.reshape(-1, N_KV_GROUPS * LANES)], axis=1)


def kernel(x, ffn_norm, ffn_w_gate_up, ffn_w_down, mix_norm, conv_w_in, conv_w, conv_w_out, kv_norm, kv_w,
           cmp_pos, cmp_w1, cmp_b1, cmp_w2, k_norm, nsa_w_qg, q_norm, nsa_w_o):
    b, s, d = x.shape
    t = b * s
    depth = ffn_norm.shape[0]
    n_a = conv_w_in.shape[0]
    g = N_KV_GROUPS

    ffn_g = ffn_norm.reshape(depth, 2, 1, d)
    ffn_wgu = ffn_w_gate_up.astype(BF16)
    ffn_wd = ffn_w_down.astype(BF16)

    def ffn(h2, layer, j, proj=None):
        return _ffn(h2, ffn_g, ffn_wgu, ffn_wd, (layer, j), proj)

    h = x.reshape(t, d)
    shared = None
    for layer in range(depth):
        h = ffn(h, layer, 0)
        proj = None
        if layer < n_a:
            h = _conv_mixer(h.reshape(b, s, d), mix_norm[layer], conv_w_in[layer].astype(BF16),
                            conv_w[layer], conv_w_out[layer].astype(BF16)).reshape(t, d)
        else:
            li = layer - n_a
            q, gate = _qg_proj(h, mix_norm[layer], _gate_columns(nsa_w_qg[li]).astype(BF16), q_norm[li])
            o = _nsa(q.reshape(b, s, -1), gate.reshape(b, s, -1), *shared)
            proj = (o.reshape(t, -1), nsa_w_o[li].astype(BF16))
        h = ffn(h, layer, 1, proj)
        if layer == n_a - 1:
            blk, ks, vst, kw, vwt = _kv_proj(h.reshape(b, s, d), kv_norm, kv_w.astype(BF16), k_norm)
            blk = blk.reshape(2, b * g, s // CMP_BLOCK, CMP_BLOCK * HEAD_DIM)
            cmp = _compress(blk, cmp_pos.reshape(2, -1), cmp_w1.astype(BF16), cmp_b1,
                            cmp_w2.astype(BF16), k_norm[0])
            cmp = cmp.reshape(2, b, g, -1, HEAD_DIM)
            shared = (cmp[0], cmp[1].transpose(0, 1, 3, 2), ks, vst, kw, vwt)
    return h.reshape(b, s, d)
```

```python
import functools

import numpy as np
import jax
import jax.numpy as jnp
from jax import lax
from jax.experimental import pallas as pl
from jax.experimental.pallas import tpu as pltpu

EPS = 1e-6
N_HEADS = 16
HEAD_DIM = 64
N_KV_GROUPS = 4
HEADS_PER_GROUP = N_HEADS // N_KV_GROUPS
N_KV_SLOTS = 6
N_BRANCH = 3
CMP_BLOCK = 32
CMP_STRIDE = 16
SEL_BLOCK = 64
N_SELECT = 16
WINDOW = 512
FORCE_SCORE = 1e9
MASK_SCORE = -1e30
LOG2E = 1.4426950408889634

LANES = 128
CHUNK = 256
V_ROWS = HEAD_DIM + 16
VMEM_LIMIT = 56 * 1024 * 1024

BF16 = jnp.bfloat16
F32 = jnp.float32


def _cparams(sem):
    return pltpu.CompilerParams(dimension_semantics=sem, vmem_limit_bytes=VMEM_LIMIT)


def _rms(x, g):
    ms = jnp.mean(x * x, axis=-1, keepdims=True)
    return x * lax.rsqrt(ms + EPS) * g


def _dot(a, b):
    return jnp.dot(a, b, preferred_element_type=F32)


def _split2(x):
    hi = x.astype(BF16)
    return hi, (x - hi.astype(F32)).astype(BF16)


def _seg_rms(y, seg, seg_t, gain):
    ss = None
    for part in _split2(y * y):
        term = _dot(part, seg)
        ss = term if ss is None else ss + term
    r = lax.rsqrt(ss * (1.0 / HEAD_DIM) + EPS)
    scale = None
    for part in _split2(r):
        term = _dot(part, seg_t)
        scale = term if scale is None else scale + term
    return y * scale * gain


def _seg_matrices(width):
    ind = (np.arange(width)[:, None] // HEAD_DIM == np.arange(LANES)[None, :]).astype(np.float32)
    return jnp.asarray(ind, BF16), jnp.asarray(ind.T, BF16)


def _dot_nt(a, b):
    return lax.dot_general(a, b, (((1,), (1,)), ((), ())), preferred_element_type=F32)


def _ffn_kernel(x_ref, g_ref, wgu_ref, wd_ref, *rest, dff):
    x = x_ref[...]
    if len(rest) == 3:
        a_ref, w_ref, o_ref = rest
        x = x + _dot(a_ref[...], w_ref[...])
    else:
        (o_ref,) = rest
    xn = _rms(x, g_ref[...]).astype(BF16)
    ab = _dot(xn, wgu_ref[...])
    a = ab[:, 0:dff]
    h = (a * jax.nn.sigmoid(a) * ab[:, dff:2 * dff]).astype(BF16)
    o_ref[...] = x + 0.5 * _dot(h, wd_ref[...])


def _resident(shape, lead=()):
    index = tuple(lead) + (0,) * len(shape)
    return pl.BlockSpec((None,) * len(lead) + tuple(shape), lambda *_: index, pipeline_mode=pl.Buffered(1))


def _ffn(h, g_all, w_gu_all, w_d_all, lead, proj=None, *, tm=512):
    t, d = h.shape
    dff = w_d_all.shape[-2]
    in_specs = [
        pl.BlockSpec((tm, d), lambda i: (i, 0)),
        _resident((1, d), lead),
        _resident((d, 2 * dff), lead),
        _resident((dff, d), lead),
    ]
    args = [h, g_all, w_gu_all, w_d_all]
    if proj is not None:
        a, w = proj
        in_specs += [pl.BlockSpec((tm, a.shape[1]), lambda i: (i, 0)), _resident(w.shape)]
        args += [a, w]
    return pl.pallas_call(
        functools.partial(_ffn_kernel, dff=dff),
        grid=(t // tm,),
        in_specs=in_specs,
        out_specs=pl.BlockSpec((tm, d), lambda i: (i, 0)),
        out_shape=jax.ShapeDtypeStruct((t, d), F32),
        compiler_params=_cparams(("parallel",)),
        name="ffn",
    )(*args)


CARRY = 8


def _conv_kernel(x_ref, g_ref, win_ref, cw_ref, wout_ref, o_ref, vbuf_ref, *, tm, d):
    j = pl.program_id(1)

    @pl.when(j == 0)
    def _():
        vbuf_ref[0:CARRY, :] = jnp.zeros((CARRY, d), F32)

    x = x_ref[0]
    xn = _rms(x, g_ref[...]).astype(BF16)
    proj = _dot(xn, win_ref[...])
    bg = proj[:, 0:d]
    v = proj[:, d:2 * d] * proj[:, 2 * d:3 * d]
    vbuf_ref[CARRY:CARRY + tm, :] = v
    cw = cw_ref[...]
    conv = (cw[0:1, :] * vbuf_ref[CARRY - 2:CARRY - 2 + tm, :]
            + cw[1:2, :] * vbuf_ref[CARRY - 1:CARRY - 1 + tm, :]
            + cw[2:3, :] * v)
    y = (bg * conv).astype(BF16)
    o_ref[0] = x + _dot(y, wout_ref[...])
    vbuf_ref[0:CARRY, :] = vbuf_ref[tm:tm + CARRY, :]


def _conv_mixer(h, g, w_in, conv_w, w_out, *, tm=512):
    b, s, d = h.shape
    return pl.pallas_call(
        functools.partial(_conv_kernel, tm=tm, d=d),
        grid=(b, s // tm),
        in_specs=[
            pl.BlockSpec((1, tm, d), lambda i, j: (i, j, 0)),
            pl.BlockSpec((1, d), lambda i, j: (0, 0)),
            pl.BlockSpec((d, 3 * d), lambda i, j: (0, 0)),
            pl.BlockSpec((conv_w.shape[0], d), lambda i, j: (0, 0)),
            pl.BlockSpec((d, d), lambda i, j: (0, 0)),
        ],
        out_specs=pl.BlockSpec((1, tm, d), lambda i, j: (i, j, 0)),
        out_shape=jax.ShapeDtypeStruct((b, s, d), F32),
        scratch_shapes=[pltpu.VMEM((CARRY + tm, d), F32)],
        compiler_params=_cparams(("arbitrary", "arbitrary")),
        name="conv_mixer",
    )(h, g.reshape(1, d), w_in, conv_w, w_out)


def _kv_kernel(x_ref, g_ref, w_ref, kn_ref, seg_ref, segt_ref, blk_ref, ks_ref, vst_ref, kw_ref, vwt_ref,
               raw_ref, *, tm):
    j = pl.program_id(1)
    xn = _rms(x_ref[0], g_ref[...]).astype(BF16)
    y = _dot(xn, w_ref[...])
    gw = N_KV_GROUPS * HEAD_DIM
    per = LANES // HEAD_DIM
    for slab in range(raw_ref.shape[0]):
        raw_ref[slab] = y[:, slab * LANES:(slab + 1) * LANES]
    for l in range(CMP_BLOCK):
        for slab in range(raw_ref.shape[0]):
            rows = raw_ref[slab, pl.ds(l, tm // CMP_BLOCK, stride=CMP_BLOCK), :]
            for k in range(per):
                sg = slab * per + k
                blk_ref[sg // N_KV_GROUPS, 0, sg % N_KV_GROUPS, :, l * HEAD_DIM:(l + 1) * HEAD_DIM] = (
                    rows[:, k * HEAD_DIM:(k + 1) * HEAD_DIM])
    pos = j * tm + lax.broadcasted_iota(jnp.int32, (tm, HEAD_DIM), 0)
    lane = lax.broadcasted_iota(jnp.int32, (tm, HEAD_DIM), 1)
    onehot = jnp.where(lane == pos // SEL_BLOCK, 1.0, 0.0).astype(BF16)
    k_sel = _seg_rms(y[:, 2 * gw:3 * gw], seg_ref[...], segt_ref[...], kn_ref[0:1, :]).astype(BF16)
    k_win = _seg_rms(y[:, 4 * gw:5 * gw], seg_ref[...], segt_ref[...], kn_ref[1:2, :]).astype(BF16)
    for grp in range(N_KV_GROUPS):
        ks_ref[0, grp, :, 0:HEAD_DIM] = k_sel[:, grp * HEAD_DIM:(grp + 1) * HEAD_DIM]
        ks_ref[0, grp, :, HEAD_DIM:2 * HEAD_DIM] = onehot
        kw_ref[0, grp] = k_win[:, grp * HEAD_DIM:(grp + 1) * HEAD_DIM]
    extra = jnp.where(lax.broadcasted_iota(jnp.int32, (V_ROWS - HEAD_DIM, CHUNK), 0) == 0, 1.0, 0.0).astype(BF16)
    for slot, ref in ((3, vst_ref), (5, vwt_ref)):
        vt = y[:, slot * gw:(slot + 1) * gw].T
        for grp in range(N_KV_GROUPS):
            for cc in range(tm // CHUNK):
                ref[0, grp, cc, 0:HEAD_DIM, :] = vt[grp * HEAD_DIM:(grp + 1) * HEAD_DIM,
                                                    cc * CHUNK:(cc + 1) * CHUNK].astype(BF16)
                ref[0, grp, cc, HEAD_DIM:V_ROWS, :] = extra


def _kv_proj(h, g, w, k_norm, *, tm=512):
    b, s, d = h.shape
    n = w.shape[1]
    ng = N_KV_GROUPS
    gw = ng * HEAD_DIM
    cpt = tm // CHUNK
    assert s // SEL_BLOCK <= HEAD_DIM
    seg, seg_t = _seg_matrices(gw)
    gains = jnp.tile(k_norm[1:3], (1, ng))
    return pl.pallas_call(
        functools.partial(_kv_kernel, tm=tm),
        grid=(b, s // tm),
        in_specs=[
            pl.BlockSpec((1, tm, d), lambda i, j: (i, j, 0)),
            _resident((1, d)),
            _resident((d, n)),
            _resident(gains.shape),
            _resident(seg.shape),
            _resident(seg_t.shape),
        ],
        out_specs=[
            pl.BlockSpec((2, 1, ng, tm // CMP_BLOCK, CMP_BLOCK * HEAD_DIM), lambda i, j: (0, i, 0, j, 0)),
            pl.BlockSpec((1, ng, tm, 2 * HEAD_DIM), lambda i, j: (i, 0, j, 0)),
            pl.BlockSpec((1, ng, cpt, V_ROWS, CHUNK), lambda i, j: (i, 0, j, 0, 0)),
            pl.BlockSpec((1, ng, tm, HEAD_DIM), lambda i, j: (i, 0, j, 0)),
            pl.BlockSpec((1, ng, cpt, V_ROWS, CHUNK), lambda i, j: (i, 0, j, 0, 0)),
        ],
        out_shape=[
            jax.ShapeDtypeStruct((2, b, ng, s // CMP_BLOCK, CMP_BLOCK * HEAD_DIM), F32),
            jax.ShapeDtypeStruct((b, ng, s, 2 * HEAD_DIM), BF16),
            jax.ShapeDtypeStruct((b, ng, s // CHUNK, V_ROWS, CHUNK), BF16),
            jax.ShapeDtypeStruct((b, ng, s, HEAD_DIM), BF16),
            jax.ShapeDtypeStruct((b, ng, s // CHUNK, V_ROWS, CHUNK), BF16),
        ],
        scratch_shapes=[pltpu.VMEM((2 * gw // LANES, tm, LANES), F32)],
        compiler_params=_cparams(("parallel", "parallel")),
        name="kv_proj",
    )(h, g.reshape(1, d), w, gains, seg, seg_t)


def _cmp_kernel(x_ref, pos_ref, w1_ref, b1_ref, w2_ref, kn_ref, o_ref):
    c = pl.program_id(0)
    nb, ne, width = x_ref.shape
    rows = nb * ne
    half = width // 2
    x = x_ref[...].reshape(rows, width)
    pos = pos_ref[...]
    w1 = w1_ref[...]
    h_even = _dot((x + pos).astype(BF16), w1)
    a = _dot((x[:, half:] + pos[:, :half]).astype(BF16), w1[0:half])
    b = _dot((x[:, :half] + pos[:, half:]).astype(BF16), w1[half:width])
    h_odd = a + pltpu.roll(b, rows - 1, 0)

    def tail(hid):
        y = _dot(jax.nn.gelu(hid + b1_ref[...]).astype(BF16), w2_ref[...])
        y = jnp.where(c == 0, _rms(y, kn_ref[...]), y)
        return y.astype(o_ref.dtype).reshape(nb, ne, y.shape[-1])

    o_ref[:, 0:ne, :] = tail(h_even)
    o_ref[:, ne:2 * ne, :] = tail(h_odd)


def _compress(xe, pos, w1, b1, w2, kn, *, nb=8):
    _, nseq, ne, width = xe.shape
    nb = min(nb, nseq)
    hid = w1.shape[-1]
    dk = w2.shape[-1]
    return pl.pallas_call(
        _cmp_kernel,
        grid=(2, nseq // nb),
        in_specs=[
            pl.BlockSpec((None, nb, ne, width), lambda c, i: (c, i, 0, 0)),
            pl.BlockSpec((None, 1, width), lambda c, i: (c, 0, 0)),
            pl.BlockSpec((None, width, hid), lambda c, i: (c, 0, 0)),
            pl.BlockSpec((None, 1, hid), lambda c, i: (c, 0, 0)),
            pl.BlockSpec((None, hid, dk), lambda c, i: (c, 0, 0)),
            pl.BlockSpec((1, dk), lambda c, i: (0, 0)),
        ],
        out_specs=pl.BlockSpec((None, nb, 2 * ne, dk), lambda c, i: (c, i, 0, 0)),
        out_shape=jax.ShapeDtypeStruct((2, nseq, 2 * ne, dk), BF16),
        compiler_params=_cparams(("parallel", "parallel")),
        name="compress",
    )(xe, pos.reshape(2, 1, width), w1, b1.reshape(2, 1, hid), w2, kn.reshape(1, dk))


def _qg_kernel(x_ref, g_ref, w_ref, qn_ref, seg_ref, segt_ref, q_ref, gate_ref, *, dq):
    xn = _rms(x_ref[...], g_ref[...]).astype(BF16)
    y = _dot(xn, w_ref[...])
    qn = qn_ref[...] * (HEAD_DIM ** -0.5 * LOG2E)
    q_ref[...] = _seg_rms(y[:, 0:dq], seg_ref[...], segt_ref[...], qn).astype(q_ref.dtype)
    gate_ref[...] = jax.nn.sigmoid(y[:, dq:])


def _qg_proj(h, g, w, q_norm, *, tm=512):
    t, d = h.shape
    n = w.shape[1]
    dq = N_HEADS * HEAD_DIM
    seg, seg_t = _seg_matrices(dq)
    return pl.pallas_call(
        functools.partial(_qg_kernel, dq=dq),
        grid=(t // tm,),
        in_specs=[
            pl.BlockSpec((tm, d), lambda i: (i, 0)),
            _resident((1, d)),
            _resident((d, n)),
            _resident((1, dq)),
            _resident(seg.shape),
            _resident(seg_t.shape),
        ],
        out_specs=[pl.BlockSpec((tm, dq), lambda i: (i, 0)),
                   pl.BlockSpec((tm, n - dq), lambda i: (i, 0))],
        out_shape=[jax.ShapeDtypeStruct((t, dq), BF16),
                   jax.ShapeDtypeStruct((t, n - dq), F32)],
        compiler_params=_cparams(("parallel",)),
        name="qg_proj",
    )(h, g.reshape(1, d), w, jnp.tile(q_norm, N_HEADS).reshape(1, dq), seg, seg_t)


def _split3(x):
    hi = x.astype(BF16)
    r = x - hi.astype(F32)
    mid = r.astype(BF16)
    lo = (r - mid.astype(F32)).astype(BF16)
    return hi, mid, lo


def _cmp_block_of_row(row, nrows):
    return jnp.where(row < nrows // 2, 2 * row, 2 * row - (nrows - 1))


def _nsa_kernel(*refs, n_cmp, n_blk):
    def group_step(g, carry):
        _nsa_group(g, *refs, n_cmp=n_cmp, n_blk=n_blk)
        return carry

    lax.fori_loop(0, N_KV_GROUPS, group_step, 0)


def _nsa_group(g, q_ref, gate_ref, kc_ref, vct_ref, ks_ref, vst_ref, kw_ref, vwt_ref, ovt_ref, eye_ref,
               o_ref, qa_ref, m_ref, acc_ref, out_ref, s_ref, cmax_ref, *, n_cmp, n_blk):
    hpg = HEADS_PER_GROUP
    tq = CHUNK
    dk = HEAD_DIM
    wide = hpg * tq
    i = pl.program_id(1)
    gd = hpg * dk
    q_lanes = pl.ds(pl.multiple_of(g * gd, gd), gd)
    t0 = i * tq

    def heads(fn):
        return jnp.concatenate([fn(h) for h in range(hpg)], axis=1)

    q = q_ref[0, :, q_lanes]
    eye = eye_ref[...]
    qa_ref[0:dk, :] = heads(lambda h: _dot_nt(eye, q[:, h * dk:(h + 1) * dk]).astype(BF16))
    gate_t = gate_ref[0, :, pl.ds(pl.multiple_of(g * LANES, LANES), LANES)].T
    gates = [heads(lambda h: gate_t[h * N_BRANCH + br:h * N_BRANCH + br + 1, :]) for br in range(N_BRANCH)]

    kk = lax.broadcasted_iota(jnp.int32, (tq, wide), 0)
    tt = lax.broadcasted_iota(jnp.int32, (tq, wide), 1) & (tq - 1)

    m_ref[...] = jnp.full(m_ref.shape, MASK_SCORE, F32)
    acc_ref[...] = jnp.zeros(acc_ref.shape, F32)

    def key_rows(ref, c):
        return ref[0, g, pl.ds(pl.multiple_of(c * tq, tq), tq), :]

    def produce(b, k, qrows, keep):
        s = _dot(k, qa_ref[0:qrows, :])
        if keep is not None:
            s = jnp.where(keep, s, MASK_SCORE)
        s_ref[b] = s
        cmax_ref[b] = jnp.max(s, axis=0, keepdims=True)

    def consume(b, vt, st):
        m_prev = m_ref[st]
        m_new = jnp.maximum(m_prev, cmax_ref[b])
        alpha = jnp.exp2(m_prev - m_new)
        p = jnp.exp2(s_ref[b] - m_new)
        acc_ref[st] = alpha * acc_ref[st] + _dot(vt, p.astype(BF16))
        m_ref[st] = m_new

    def flash_out(st, branch):
        out_ref[...] += gates[branch] * (acc_ref[st, 0:dk, :] * (1.0 / acc_ref[st, dk:dk + 1, :]))

    assert WINDOW == 2 * tq
    lo = jnp.maximum(i - 2, 0)
    mid = jnp.maximum(i - 1, 0)
    produce(0, key_rows(kw_ref, lo), dk, kk > tt + jnp.where(i >= 2, 0, tq))

    ncp = kc_ref.shape[2]
    nidx = _cmp_block_of_row(lax.broadcasted_iota(jnp.int32, (ncp, wide), 0), ncp)
    tpos = t0 + (lax.broadcasted_iota(jnp.int32, (ncp, wide), 1) & (tq - 1))
    valid = (nidx * CMP_STRIDE + (CMP_BLOCK - 1) <= tpos) & (nidx < n_cmp)
    s = jnp.where(valid, _dot(kc_ref[0, g], qa_ref[0:dk, :]), MASK_SCORE)
    e = jnp.exp2(s - jnp.max(s, axis=0, keepdims=True))
    p = jnp.where(valid, e * (1.0 / jnp.sum(e, axis=0, keepdims=True)), 0.0)
    out_ref[...] = gates[0] * _dot(vct_ref[0, g], p.astype(BF16))
    psum = p[:, 0:tq]
    for h in range(1, hpg):
        psum = psum + p[:, h * tq:(h + 1) * tq]

    produce(1, key_rows(kw_ref, mid), dk, kk > tt + jnp.where(i >= 1, -tq, tq))
    consume(0, vwt_ref[0, g, lo], 1)

    ovt = ovt_ref[...]
    imp = None
    for part in _split3(psum):
        term = _dot(ovt, part)
        imp = term if imp is None else imp + term
    jblk = lax.broadcasted_iota(jnp.int32, (n_blk, tq), 0)
    cur = (t0 + lax.broadcasted_iota(jnp.int32, (n_blk, tq), 1)) // SEL_BLOCK
    forced = (jblk == 0) | (jblk == cur) | (jblk == cur - 1)
    imp = jnp.where(forced, FORCE_SCORE, imp)
    imp = jnp.where(jblk > cur, MASK_SCORE, imp)
    sub = 8
    parts = [imp[r:r + sub] for r in range(0, n_blk, sub)]
    ranks = [jnp.zeros((sub, tq), F32) for _ in parts]
    jsub = lax.broadcasted_iota(jnp.int32, (sub, tq), 0)
    for ii in range(n_blk):
        row = imp[ii:ii + 1, :]
        for gi, x in enumerate(parts):
            if ii < gi * sub:
                beats = row >= x
            elif ii >= (gi + 1) * sub:
                beats = row > x
            else:
                beats = (row > x) | ((row == x) & (jsub > ii - gi * sub))
            ranks[gi] = ranks[gi] + jnp.where(beats, 1.0, 0.0)
    rank = jnp.concatenate(ranks, axis=0)
    chosen = (rank < float(min(N_SELECT, n_blk))) & (jblk <= cur)
    bias = jnp.where(chosen, 0.0, MASK_SCORE)
    bias = jnp.concatenate([bias, jnp.zeros((dk - n_blk, tq), F32)], axis=0).astype(BF16)
    qa_ref[dk:2 * dk, :] = heads(lambda h: bias)

    produce(0, key_rows(kw_ref, i), dk, kk <= tt)
    consume(1, vwt_ref[0, g, mid], 1)

    npair = i // 2
    held = jnp.where(i >= 2, 2 * npair - 1, i)
    last = jnp.where(i % 2 == 0, i + 1, i - 1)
    produce(1, key_rows(ks_ref, i), 2 * dk, kk <= tt)
    consume(0, vwt_ref[0, g, i], 1)
    flash_out(1, 2)

    def pair_body(p, carry):
        c = 2 * p
        produce(0, key_rows(ks_ref, c), 2 * dk, None)
        consume(1, vst_ref[0, g, jnp.where(p == 0, i, c - 1)], 0)
        produce(1, key_rows(ks_ref, c + 1), 2 * dk, None)
        consume(0, vst_ref[0, g, c], 0)
        return carry

    lax.fori_loop(0, npair, pair_body, 0)
    produce(0, key_rows(ks_ref, last), 2 * dk, None)
    consume(1, vst_ref[0, g, held], 0)
    consume(0, vst_ref[0, g, last], 0)
    flash_out(0, 1)

    out = out_ref[...]
    out = jnp.concatenate([out[:, h * tq:(h + 1) * tq] for h in range(hpg)], axis=0)
    o_ref[0, :, q_lanes] = out.T.astype(o_ref.dtype)


def _nsa(q, gate, kc, vct, ks, vst, kw, vwt):
    b, s, dq = q.shape
    g = N_KV_GROUPS
    hpg = HEADS_PER_GROUP
    dk = HEAD_DIM
    gd = hpg * dk
    tq = CHUNK
    n_blk = s // SEL_BLOCK
    n_cmp = (s - CMP_BLOCK) // CMP_STRIDE + 1
    ncp = kc.shape[2]
    assert WINDOW == 2 * tq and n_blk <= dk and (s // tq) % 2 == 0
    rows_np = np.arange(ncp)
    cmp_idx = np.where(rows_np < ncp // 2, 2 * rows_np, 2 * rows_np - (ncp - 1))
    cmp_start = cmp_idx * CMP_STRIDE
    blk_start = np.arange(n_blk) * SEL_BLOCK
    ovt = ((cmp_start[None, :] < blk_start[:, None] + SEL_BLOCK)
           & (cmp_start[None, :] + CMP_BLOCK > blk_start[:, None])
           & (cmp_idx[None, :] < n_cmp)).astype(np.float32)
    idx = lambda bi, i: (bi, 0, 0, 0)
    idx5 = lambda bi, i: (bi, 0, 0, 0, 0)
    return pl.pallas_call(
        functools.partial(_nsa_kernel, n_cmp=n_cmp, n_blk=n_blk),
        grid=(b, s // tq),
        in_specs=[
            pl.BlockSpec((1, tq, dq), lambda bi, i: (bi, i, 0)),
            pl.BlockSpec((1, tq, g * LANES), lambda bi, i: (bi, i, 0)),
            pl.BlockSpec((1, g, ncp, dk), idx),
            pl.BlockSpec((1, g, dk, ncp), idx),
            pl.BlockSpec((1, g, s, 2 * dk), idx),
            pl.BlockSpec((1, g, s // tq, V_ROWS, tq), idx5),
            pl.BlockSpec((1, g, s, dk), idx),
            pl.BlockSpec((1, g, s // tq, V_ROWS, tq), idx5),
            _resident((n_blk, ncp)),
            _resident((dk, dk)),
        ],
        out_specs=pl.BlockSpec((1, tq, dq), lambda bi, i: (bi, i, 0)),
        out_shape=jax.ShapeDtypeStruct((b, s, dq), BF16),
        scratch_shapes=[
            pltpu.VMEM((2 * dk, hpg * tq), BF16),
            pltpu.VMEM((2, 1, hpg * tq), F32),
            pltpu.VMEM((2, V_ROWS, hpg * tq), F32),
            pltpu.VMEM((dk, hpg * tq), F32),
            pltpu.VMEM((2, tq, hpg * tq), F32),
            pltpu.VMEM((2, 1, hpg * tq), F32),
        ],
        compiler_params=_cparams(("parallel", "arbitrary")),
        name="nsa",
    )(q, gate, kc, vct, ks, vst, kw, vwt, jnp.asarray(ovt, BF16), jnp.eye(dk, dtype=BF16))


def _gate_columns(w_qg):
    dq = N_HEADS * HEAD_DIM
    per = HEADS_PER_GROUP * N_BRANCH
    wg = w_qg[:, dq:].reshape(-1, N_KV_GROUPS, per)
    wg = jnp.pad(wg, ((0, 0), (0, 0), (0, LANES - per)))
    return jnp.concatenate([w_qg[:, :dq], wg.reshape(-1, N_KV_GROUPS * LANES)], axis=1)


def kernel(x, ffn_norm, ffn_w_gate_up, ffn_w_down, mix_norm, conv_w_in, conv_w, conv_w_out, kv_norm, kv_w,
           cmp_pos, cmp_w1, cmp_b1, cmp_w2, k_norm, nsa_w_qg, q_norm, nsa_w_o):
    b, s, d = x.shape
    t = b * s
    depth = ffn_norm.shape[0]
    n_a = conv_w_in.shape[0]
    g = N_KV_GROUPS

    ffn_g = ffn_norm.reshape(depth, 2, 1, d)
    ffn_wgu = ffn_w_gate_up.astype(BF16)
    ffn_wd = ffn_w_down.astype(BF16)

    def ffn(h2, layer, j, proj=None):
        return _ffn(h2, ffn_g, ffn_wgu, ffn_wd, (layer, j), proj)

    h = x.reshape(t, d)
    shared = None
    for layer in range(depth):
        h = ffn(h, layer, 0)
        proj = None
        if layer < n_a:
            h = _conv_mixer(h.reshape(b, s, d), mix_norm[layer], conv_w_in[layer].astype(BF16),
                            conv_w[layer], conv_w_out[layer].astype(BF16)).reshape(t, d)
        else:
            li = layer - n_a
            q, gate = _qg_proj(h, mix_norm[layer], _gate_columns(nsa_w_qg[li]).astype(BF16), q_norm[li])
            o = _nsa(q.reshape(b, s, -1), gate.reshape(b, s, -1), *shared)
            proj = (o.reshape(t, -1), nsa_w_o[li].astype(BF16))
        h = ffn(h, layer, 1, proj)
        if layer == n_a - 1:
            blk, ks, vst, kw, vwt = _kv_proj(h.reshape(b, s, d), kv_norm, kv_w.astype(BF16), k_norm)
            blk = blk.reshape(2, b * g, s // CMP_BLOCK, CMP_BLOCK * HEAD_DIM)
            cmp = _compress(blk, cmp_pos.reshape(2, -1), cmp_w1.astype(BF16), cmp_b1,
                            cmp_w2.astype(BF16), k_norm[0])
            cmp = cmp.reshape(2, b, g, -1, HEAD_DIM)
            shared = (cmp[0], cmp[1].transpose(0, 1, 3, 2), ks, vst, kw, vwt)
    return h.reshape(b, s, d)
```

```python
import functools

import numpy as np
import jax
import jax.numpy as jnp
from jax import lax
from jax.experimental import pallas as pl
from jax.experimental.pallas import tpu as pltpu

EPS = 1e-6
N_HEADS = 16
HEAD_DIM = 64
N_KV_GROUPS = 4
HEADS_PER_GROUP = N_HEADS // N_KV_GROUPS
N_KV_SLOTS = 6
N_BRANCH = 3
CMP_BLOCK = 32
CMP_STRIDE = 16
SEL_BLOCK = 64
N_SELECT = 16
WINDOW = 512
FORCE_SCORE = 1e9
MASK_SCORE = -1e30
LOG2E = 1.4426950408889634

LANES = 128
CHUNK = 256
V_ROWS = HEAD_DIM + 16
VMEM_LIMIT = 56 * 1024 * 1024

BF16 = jnp.bfloat16
F32 = jnp.float32


def _cparams(sem):
    return pltpu.CompilerParams(dimension_semantics=sem, vmem_limit_bytes=VMEM_LIMIT)


def _rms(x, g):
    ms = jnp.mean(x * x, axis=-1, keepdims=True)
    return x * lax.rsqrt(ms + EPS) * g


def _dot(a, b):
    return jnp.dot(a, b, preferred_element_type=F32)


def _split2(x):
    hi = x.astype(BF16)
    return hi, (x - hi.astype(F32)).astype(BF16)


def _seg_rms(y, seg, seg_t, gain):
    ss = None
    for part in _split2(y * y):
        term = _dot(part, seg)
        ss = term if ss is None else ss + term
    r = lax.rsqrt(ss * (1.0 / HEAD_DIM) + EPS)
    scale = None
    for part in _split2(r):
        term = _dot(part, seg_t)
        scale = term if scale is None else scale + term
    return y * scale * gain


def _seg_matrices(width):
    ind = (np.arange(width)[:, None] // HEAD_DIM == np.arange(LANES)[None, :]).astype(np.float32)
    return jnp.asarray(ind, BF16), jnp.asarray(ind.T, BF16)


def _dot_nt(a, b):
    return lax.dot_general(a, b, (((1,), (1,)), ((), ())), preferred_element_type=F32)


def _ffn_kernel(x_ref, g_ref, wgu_ref, wd_ref, *rest, dff):
    x = x_ref[...]
    if len(rest) == 3:
        a_ref, w_ref, o_ref = rest
        x = x + _dot(a_ref[...], w_ref[...])
    else:
        (o_ref,) = rest
    xn = _rms(x, g_ref[...]).astype(BF16)
    ab = _dot(xn, wgu_ref[...])
    a = ab[:, 0:dff]
    h = (a * jax.nn.sigmoid(a) * ab[:, dff:2 * dff]).astype(BF16)
    o_ref[...] = x + 0.5 * _dot(h, wd_ref[...])


def _resident(shape, lead=()):
    index = tuple(lead) + (0,) * len(shape)
    return pl.BlockSpec((None,) * len(lead) + tuple(shape), lambda *_: index, pipeline_mode=pl.Buffered(1))


def _ffn(h, g_all, w_gu_all, w_d_all, lead, proj=None, *, tm=512):
    t, d = h.shape
    dff = w_d_all.shape[-2]
    in_specs = [
        pl.BlockSpec((tm, d), lambda i: (i, 0)),
        _resident((1, d), lead),
        _resident((d, 2 * dff), lead),
        _resident((dff, d), lead),
    ]
    args = [h, g_all, w_gu_all, w_d_all]
    if proj is not None:
        a, w = proj
        in_specs += [pl.BlockSpec((tm, a.shape[1]), lambda i: (i, 0)), _resident(w.shape)]
        args += [a, w]
    return pl.pallas_call(
        functools.partial(_ffn_kernel, dff=dff),
        grid=(t // tm,),
        in_specs=in_specs,
        out_specs=pl.BlockSpec((tm, d), lambda i: (i, 0)),
        out_shape=jax.ShapeDtypeStruct((t, d), F32),
        compiler_params=_cparams(("parallel",)),
        name="ffn",
    )(*args)


CARRY = 8


def _conv_kernel(x_ref, g_ref, win_ref, cw_ref, wout_ref, o_ref, vbuf_ref, *, tm, d):
    j = pl.program_id(1)

    @pl.when(j == 0)
    def _():
        vbuf_ref[0:CARRY, :] = jnp.zeros((CARRY, d), F32)

    x = x_ref[0]
    xn = _rms(x, g_ref[...]).astype(BF16)
    proj = _dot(xn, win_ref[...])
    bg = proj[:, 0:d]
    v = proj[:, d:2 * d] * proj[:, 2 * d:3 * d]
    vbuf_ref[CARRY:CARRY + tm, :] = v
    cw = cw_ref[...]
    conv = (cw[0:1, :] * vbuf_ref[CARRY - 2:CARRY - 2 + tm, :]
            + cw[1:2, :] * vbuf_ref[CARRY - 1:CARRY - 1 + tm, :]
            + cw[2:3, :] * v)
    y = (bg * conv).astype(BF16)
    o_ref[0] = x + _dot(y, wout_ref[...])
    vbuf_ref[0:CARRY, :] = vbuf_ref[tm:tm + CARRY, :]


def _conv_mixer(h, g, w_in, conv_w, w_out, *, tm=512):
    b, s, d = h.shape
    return pl.pallas_call(
        functools.partial(_conv_kernel, tm=tm, d=d),
        grid=(b, s // tm),
        in_specs=[
            pl.BlockSpec((1, tm, d), lambda i, j: (i, j, 0)),
            pl.BlockSpec((1, d), lambda i, j: (0, 0)),
            pl.BlockSpec((d, 3 * d), lambda i, j: (0, 0)),
            pl.BlockSpec((conv_w.shape[0], d), lambda i, j: (0, 0)),
            pl.BlockSpec((d, d), lambda i, j: (0, 0)),
        ],
        out_specs=pl.BlockSpec((1, tm, d), lambda i, j: (i, j, 0)),
        out_shape=jax.ShapeDtypeStruct((b, s, d), F32),
        scratch_shapes=[pltpu.VMEM((CARRY + tm, d), F32)],
        compiler_params=_cparams(("arbitrary", "arbitrary")),
        name="conv_mixer",
    )(h, g.reshape(1, d), w_in, conv_w, w_out)


def _kv_kernel(x_ref, g_ref, w_ref, kn_ref, seg_ref, segt_ref, blk_ref, ks_ref, vst_ref, kw_ref, vwt_ref,
               raw_ref, *, tm):
    j = pl.program_id(1)
    xn = _rms(x_ref[0], g_ref[...]).astype(BF16)
    y = _dot(xn, w_ref[...])
    gw = N_KV_GROUPS * HEAD_DIM
    per = LANES // HEAD_DIM
    for slab in range(raw_ref.shape[0]):
        raw_ref[slab] = y[:, slab * LANES:(slab + 1) * LANES]
    for l in range(CMP_BLOCK):
        for slab in range(raw_ref.shape[0]):
            rows = raw_ref[slab, pl.ds(l, tm // CMP_BLOCK, stride=CMP_BLOCK), :]
            for k in range(per):
                sg = slab * per + k
                blk_ref[sg // N_KV_GROUPS, 0, sg % N_KV_GROUPS, :, l * HEAD_DIM:(l + 1) * HEAD_DIM] = (
                    rows[:, k * HEAD_DIM:(k + 1) * HEAD_DIM])
    pos = j * tm + lax.broadcasted_iota(jnp.int32, (tm, HEAD_DIM), 0)
    lane = lax.broadcasted_iota(jnp.int32, (tm, HEAD_DIM), 1)
    onehot = jnp.where(lane == pos // SEL_BLOCK, 1.0, 0.0).astype(BF16)
    k_sel = _seg_rms(y[:, 2 * gw:3 * gw], seg_ref[...], segt_ref[...], kn_ref[0:1, :]).astype(BF16)
    k_win = _seg_rms(y[:, 4 * gw:5 * gw], seg_ref[...], segt_ref[...], kn_ref[1:2, :]).astype(BF16)
    for grp in range(N_KV_GROUPS):
        ks_ref[0, grp, :, 0:HEAD_DIM] = k_sel[:, grp * HEAD_DIM:(grp + 1) * HEAD_DIM]
        ks_ref[0, grp, :, HEAD_DIM:2 * HEAD_DIM] = onehot
        kw_ref[0, grp] = k_win[:, grp * HEAD_DIM:(grp + 1) * HEAD_DIM]
    extra = jnp.where(lax.broadcasted_iota(jnp.int32, (V_ROWS - HEAD_DIM, CHUNK), 0) == 0, 1.0, 0.0).astype(BF16)
    for slot, ref in ((3, vst_ref), (5, vwt_ref)):
        vt = y[:, slot * gw:(slot + 1) * gw].T
        for grp in range(N_KV_GROUPS):
            for cc in range(tm // CHUNK):
                ref[0, grp, cc, 0:HEAD_DIM, :] = vt[grp * HEAD_DIM:(grp + 1) * HEAD_DIM,
                                                    cc * CHUNK:(cc + 1) * CHUNK].astype(BF16)
                ref[0, grp, cc, HEAD_DIM:V_ROWS, :] = extra


def _kv_proj(h, g, w, k_norm, *, tm=512):
    b, s, d = h.shape
    n = w.shape[1]
    ng = N_KV_GROUPS
    gw = ng * HEAD_DIM
    cpt = tm // CHUNK
    assert s // SEL_BLOCK <= HEAD_DIM
    seg, seg_t = _seg_matrices(gw)
    gains = jnp.tile(k_norm[1:3], (1, ng))
    return pl.pallas_call(
        functools.partial(_kv_kernel, tm=tm),
        grid=(b, s // tm),
        in_specs=[
            pl.BlockSpec((1, tm, d), lambda i, j: (i, j, 0)),
            _resident((1, d)),
            _resident((d, n)),
            _resident(gains.shape),
            _resident(seg.shape),
            _resident(seg_t.shape),
        ],
        out_specs=[
            pl.BlockSpec((2, 1, ng, tm // CMP_BLOCK, CMP_BLOCK * HEAD_DIM), lambda i, j: (0, i, 0, j, 0)),
            pl.BlockSpec((1, ng, tm, 2 * HEAD_DIM), lambda i, j: (i, 0, j, 0)),
            pl.BlockSpec((1, ng, cpt, V_ROWS, CHUNK), lambda i, j: (i, 0, j, 0, 0)),
            pl.BlockSpec((1, ng, tm, HEAD_DIM), lambda i, j: (i, 0, j, 0)),
            pl.BlockSpec((1, ng, cpt, V_ROWS, CHUNK), lambda i, j: (i, 0, j, 0, 0)),
        ],
        out_shape=[
            jax.ShapeDtypeStruct((2, b, ng, s // CMP_BLOCK, CMP_BLOCK * HEAD_DIM), F32),
            jax.ShapeDtypeStruct((b, ng, s, 2 * HEAD_DIM), BF16),
            jax.ShapeDtypeStruct((b, ng, s // CHUNK, V_ROWS, CHUNK), BF16),
            jax.ShapeDtypeStruct((b, ng, s, HEAD_DIM), BF16),
            jax.ShapeDtypeStruct((b, ng, s // CHUNK, V_ROWS, CHUNK), BF16),
        ],
        scratch_shapes=[pltpu.VMEM((2 * gw // LANES, tm, LANES), F32)],
        compiler_params=_cparams(("parallel", "parallel")),
        name="kv_proj",
    )(h, g.reshape(1, d), w, gains, seg, seg_t)


def _cmp_kernel(x_ref, pos_ref, w1_ref, b1_ref, w2_ref, kn_ref, o_ref):
    c = pl.program_id(0)
    nb, ne, width = x_ref.shape
    rows = nb * ne
    half = width // 2
    x = x_ref[...].reshape(rows, width)
    pos = pos_ref[...]
    w1 = w1_ref[...]
    h_even = _dot((x + pos).astype(BF16), w1)
    a = _dot((x[:, half:] + pos[:, :half]).astype(BF16), w1[0:half])
    b = _dot((x[:, :half] + pos[:, half:]).astype(BF16), w1[half:width])
    h_odd = a + pltpu.roll(b, rows - 1, 0)

    def tail(hid):
        y = _dot(jax.nn.gelu(hid + b1_ref[...]).astype(BF16), w2_ref[...])
        y = jnp.where(c == 0, _rms(y, kn_ref[...]), y)
        return y.astype(o_ref.dtype).reshape(nb, ne, y.shape[-1])

    o_ref[:, 0:ne, :] = tail(h_even)
    o_ref[:, ne:2 * ne, :] = tail(h_odd)


def _compress(xe, pos, w1, b1, w2, kn, *, nb=8):
    _, nseq, ne, width = xe.shape
    nb = min(nb, nseq)
    hid = w1.shape[-1]
    dk = w2.shape[-1]
    return pl.pallas_call(
        _cmp_kernel,
        grid=(2, nseq // nb),
        in_specs=[
            pl.BlockSpec((None, nb, ne, width), lambda c, i: (c, i, 0, 0)),
            pl.BlockSpec((None, 1, width), lambda c, i: (c, 0, 0)),
            pl.BlockSpec((None, width, hid), lambda c, i: (c, 0, 0)),
            pl.BlockSpec((None, 1, hid), lambda c, i: (c, 0, 0)),
            pl.BlockSpec((None, hid, dk), lambda c, i: (c, 0, 0)),
            pl.BlockSpec((1, dk), lambda c, i: (0, 0)),
        ],
        out_specs=pl.BlockSpec((None, nb, 2 * ne, dk), lambda c, i: (c, i, 0, 0)),
        out_shape=jax.ShapeDtypeStruct((2, nseq, 2 * ne, dk), BF16),
        compiler_params=_cparams(("parallel", "parallel")),
        name="compress",
    )(xe, pos.reshape(2, 1, width), w1, b1.reshape(2, 1, hid), w2, kn.reshape(1, dk))


def _qg_kernel(x_ref, g_ref, w_ref, qn_ref, seg_ref, segt_ref, q_ref, gate_ref, *, dq):
    xn = _rms(x_ref[...], g_ref[...]).astype(BF16)
    y = _dot(xn, w_ref[...])
    qn = qn_ref[...] * (HEAD_DIM ** -0.5 * LOG2E)
    q_ref[...] = _seg_rms(y[:, 0:dq], seg_ref[...], segt_ref[...], qn).astype(q_ref.dtype)
    gate_ref[...] = jax.nn.sigmoid(y[:, dq:])


def _qg_proj(h, g, w, q_norm, *, tm=512):
    t, d = h.shape
    n = w.shape[1]
    dq = N_HEADS * HEAD_DIM
    seg, seg_t = _seg_matrices(dq)
    return pl.pallas_call(
        functools.partial(_qg_kernel, dq=dq),
        grid=(t // tm,),
        in_specs=[
            pl.BlockSpec((tm, d), lambda i: (i, 0)),
            _resident((1, d)),
            _resident((d, n)),
            _resident((1, dq)),
            _resident(seg.shape),
            _resident(seg_t.shape),
        ],
        out_specs=[pl.BlockSpec((tm, dq), lambda i: (i, 0)),
                   pl.BlockSpec((tm, n - dq), lambda i: (i, 0))],
        out_shape=[jax.ShapeDtypeStruct((t, dq), BF16),
                   jax.ShapeDtypeStruct((t, n - dq), F32)],
        compiler_params=_cparams(("parallel",)),
        name="qg_proj",
    )(h, g.reshape(1, d), w, jnp.tile(q_norm, N_HEADS).reshape(1, dq), seg, seg_t)


def _split3(x):
    hi = x.astype(BF16)
    r = x - hi.astype(F32)
    mid = r.astype(BF16)
    lo = (r - mid.astype(F32)).astype(BF16)
    return hi, mid, lo


def _cmp_block_of_row(row, nrows):
    return jnp.where(row < nrows // 2, 2 * row, 2 * row - (nrows - 1))


def _nsa_kernel(*refs, n_cmp, n_blk):
    mask_ref, cmask_ref = refs[-2:]
    tq = CHUNK
    _, ncp, wide = cmask_ref.shape
    i = pl.program_id(1)
    kk = lax.broadcasted_iota(jnp.int32, (tq, wide), 0)
    tt = lax.broadcasted_iota(jnp.int32, (tq, wide), 1) & (tq - 1)
    mask_ref[0] = jnp.where(kk <= tt, 0.0, MASK_SCORE)
    mask_ref[1] = jnp.where(kk > tt + jnp.where(i >= 2, 0, tq), 0.0, MASK_SCORE)
    mask_ref[2] = jnp.where(kk > tt + jnp.where(i >= 1, -tq, tq), 0.0, MASK_SCORE)
    nidx = _cmp_block_of_row(lax.broadcasted_iota(jnp.int32, (ncp, wide), 0), ncp)
    tpos = i * tq + (lax.broadcasted_iota(jnp.int32, (ncp, wide), 1) & (tq - 1))
    valid = (nidx * CMP_STRIDE + (CMP_BLOCK - 1) <= tpos) & (nidx < n_cmp)
    cmask_ref[0] = jnp.where(valid, 0.0, MASK_SCORE)
    cmask_ref[1] = jnp.where(valid, 1.0, 0.0)

    n_in = 11
    scratch = refs[n_in:-2]
    for g in range(N_KV_GROUPS):
        slot = [r.at[g % 2] for r in scratch]
        _nsa_group(g, *refs[:n_in], *slot, mask_ref, cmask_ref, n_blk=n_blk)


def _nsa_group(g, q_ref, gate_ref, kc_ref, vct_ref, ks_ref, vst_ref, kw_ref, vwt_ref, ovt_ref, eye_ref,
               o_ref, qa_ref, m_ref, acc_ref, out_ref, s_ref, cmax_ref, mask_ref, cmask_ref, *, n_blk):
    hpg = HEADS_PER_GROUP
    tq = CHUNK
    dk = HEAD_DIM
    wide = hpg * tq
    i = pl.program_id(1)
    gd = hpg * dk
    q_lanes = pl.ds(g * gd, gd)
    t0 = i * tq

    def heads(fn):
        return jnp.concatenate([fn(h) for h in range(hpg)], axis=1)

    q = q_ref[0, :, q_lanes]
    eye = eye_ref[...]
    qa_ref[0:dk, :] = heads(lambda h: _dot_nt(eye, q[:, h * dk:(h + 1) * dk]).astype(BF16))
    gate_t = gate_ref[0, :, pl.ds(g * LANES, LANES)].T
    gates = [heads(lambda h: gate_t[h * N_BRANCH + br:h * N_BRANCH + br + 1, :]) for br in range(N_BRANCH)]

    causal, win_lo, win_mid = 0, 1, 2

    m_ref[...] = jnp.full(m_ref.shape, MASK_SCORE, F32)
    acc_ref[...] = jnp.zeros(acc_ref.shape, F32)

    def key_rows(ref, c):
        return ref[0, g, pl.ds(pl.multiple_of(c * tq, tq), tq), :]

    def produce(b, k, qrows, mask):
        s = _dot(k, qa_ref[0:qrows, :])
        if mask is not None:
            s = s + mask_ref[mask]
        s_ref[b] = s
        cmax_ref[b] = jnp.max(s, axis=0, keepdims=True)

    def consume(b, vt, st):
        m_prev = m_ref[st]
        m_new = jnp.maximum(m_prev, cmax_ref[b])
        alpha = jnp.exp2(m_prev - m_new)
        p = jnp.exp2(s_ref[b] - m_new)
        acc_ref[st] = alpha * acc_ref[st] + _dot(vt, p.astype(BF16))
        m_ref[st] = m_new

    def flash_out(st, branch):
        out_ref[...] += gates[branch] * (acc_ref[st, 0:dk, :] * (1.0 / acc_ref[st, dk:dk + 1, :]))

    assert WINDOW == 2 * tq
    lo = jnp.maximum(i - 2, 0)
    mid = jnp.maximum(i - 1, 0)
    produce(0, key_rows(kw_ref, lo), dk, win_lo)

    s = _dot(kc_ref[0, g], qa_ref[0:dk, :]) + cmask_ref[0]
    e = jnp.exp2(s - jnp.max(s, axis=0, keepdims=True))
    p = e * (1.0 / jnp.sum(e, axis=0, keepdims=True)) * cmask_ref[1]
    out_ref[...] = gates[0] * _dot(vct_ref[0, g], p.astype(BF16))
    psum = p[:, 0:tq]
    for h in range(1, hpg):
        psum = psum + p[:, h * tq:(h + 1) * tq]

    produce(1, key_rows(kw_ref, mid), dk, win_mid)
    consume(0, vwt_ref[0, g, lo], 1)

    ovt = ovt_ref[...]
    imp = None
    for part in _split3(psum):
        term = _dot(ovt, part)
        imp = term if imp is None else imp + term
    jblk = lax.broadcasted_iota(jnp.int32, (n_blk, tq), 0)
    cur = (t0 + lax.broadcasted_iota(jnp.int32, (n_blk, tq), 1)) // SEL_BLOCK
    forced = (jblk == 0) | (jblk == cur) | (jblk == cur - 1)
    imp = jnp.where(forced, FORCE_SCORE, imp)
    imp = jnp.where(jblk > cur, MASK_SCORE, imp)
    sub = 8
    parts = [imp[r:r + sub] for r in range(0, n_blk, sub)]
    ranks = [jnp.zeros((sub, tq), F32) for _ in parts]
    jsub = lax.broadcasted_iota(jnp.int32, (sub, tq), 0)
    for ii in range(n_blk):
        row = imp[ii:ii + 1, :]
        for gi, x in enumerate(parts):
            if ii < gi * sub:
                beats = row >= x
            elif ii >= (gi + 1) * sub:
                beats = row > x
            else:
                beats = (row > x) | ((row == x) & (jsub > ii - gi * sub))
            ranks[gi] = ranks[gi] + jnp.where(beats, 1.0, 0.0)
    rank = jnp.concatenate(ranks, axis=0)
    chosen = (rank < float(min(N_SELECT, n_blk))) & (jblk <= cur)
    bias = jnp.where(chosen, 0.0, MASK_SCORE)
    bias = jnp.concatenate([bias, jnp.zeros((dk - n_blk, tq), F32)], axis=0).astype(BF16)
    qa_ref[dk:2 * dk, :] = heads(lambda h: bias)

    produce(0, key_rows(kw_ref, i), dk, causal)
    consume(1, vwt_ref[0, g, mid], 1)

    npair = i // 2
    held = jnp.where(i >= 2, 2 * npair - 1, i)
    last = jnp.where(i % 2 == 0, i + 1, i - 1)
    produce(1, key_rows(ks_ref, i), 2 * dk, causal)
    consume(0, vwt_ref[0, g, i], 1)
    flash_out(1, 2)

    def pair_body(p, carry):
        c = 2 * p
        produce(0, key_rows(ks_ref, c), 2 * dk, None)
        consume(1, vst_ref[0, g, jnp.where(p == 0, i, c - 1)], 0)
        produce(1, key_rows(ks_ref, c + 1), 2 * dk, None)
        consume(0, vst_ref[0, g, c], 0)
        return carry

    lax.fori_loop(0, npair, pair_body, 0)
    produce(0, key_rows(ks_ref, last), 2 * dk, None)
    consume(1, vst_ref[0, g, held], 0)
    consume(0, vst_ref[0, g, last], 0)
    flash_out(0, 1)

    out = out_ref[...]
    out = jnp.concatenate([out[:, h * tq:(h + 1) * tq] for h in range(hpg)], axis=0)
    o_ref[0, :, q_lanes] = out.T.astype(o_ref.dtype)


def _nsa(q, gate, kc, vct, ks, vst, kw, vwt):
    b, s, dq = q.shape
    g = N_KV_GROUPS
    hpg = HEADS_PER_GROUP
    dk = HEAD_DIM
    gd = hpg * dk
    tq = CHUNK
    n_blk = s // SEL_BLOCK
    n_cmp = (s - CMP_BLOCK) // CMP_STRIDE + 1
    ncp = kc.shape[2]
    assert WINDOW == 2 * tq and n_blk <= dk and (s // tq) % 2 == 0
    rows_np = np.arange(ncp)
    cmp_idx = np.where(rows_np < ncp // 2, 2 * rows_np, 2 * rows_np - (ncp - 1))
    cmp_start = cmp_idx * CMP_STRIDE
    blk_start = np.arange(n_blk) * SEL_BLOCK
    ovt = ((cmp_start[None, :] < blk_start[:, None] + SEL_BLOCK)
           & (cmp_start[None, :] + CMP_BLOCK > blk_start[:, None])
           & (cmp_idx[None, :] < n_cmp)).astype(np.float32)
    idx = lambda bi, i: (bi, 0, 0, 0)
    idx5 = lambda bi, i: (bi, 0, 0, 0, 0)
    return pl.pallas_call(
        functools.partial(_nsa_kernel, n_cmp=n_cmp, n_blk=n_blk),
        grid=(b, s // tq),
        in_specs=[
            pl.BlockSpec((1, tq, dq), lambda bi, i: (bi, i, 0)),
            pl.BlockSpec((1, tq, g * LANES), lambda bi, i: (bi, i, 0)),
            pl.BlockSpec((1, g, ncp, dk), idx),
            pl.BlockSpec((1, g, dk, ncp), idx),
            pl.BlockSpec((1, g, s, 2 * dk), idx),
            pl.BlockSpec((1, g, s // tq, V_ROWS, tq), idx5),
            pl.BlockSpec((1, g, s, dk), idx),
            pl.BlockSpec((1, g, s // tq, V_ROWS, tq), idx5),
            _resident((n_blk, ncp)),
            _resident((dk, dk)),
        ],
        out_specs=pl.BlockSpec((1, tq, dq), lambda bi, i: (bi, i, 0)),
        out_shape=jax.ShapeDtypeStruct((b, s, dq), BF16),
        scratch_shapes=[
            pltpu.VMEM((2, 2 * dk, hpg * tq), BF16),
            pltpu.VMEM((2, 2, 1, hpg * tq), F32),
            pltpu.VMEM((2, 2, V_ROWS, hpg * tq), F32),
            pltpu.VMEM((2, dk, hpg * tq), F32),
            pltpu.VMEM((2, 2, tq, hpg * tq), F32),
            pltpu.VMEM((2, 2, 1, hpg * tq), F32),
            pltpu.VMEM((3, tq, hpg * tq), F32),
            pltpu.VMEM((2, ncp, hpg * tq), F32),
        ],
        compiler_params=_cparams(("parallel", "arbitrary")),
        name="nsa",
    )(q, gate, kc, vct, ks, vst, kw, vwt, jnp.asarray(ovt, BF16), jnp.eye(dk, dtype=BF16))


def _gate_columns(w_qg):
    dq = N_HEADS * HEAD_DIM
    per = HEADS_PER_GROUP * N_BRANCH
    wg = w_qg[:, dq:].reshape(-1, N_KV_GROUPS, per)
    wg = jnp.pad(wg, ((0, 0), (0, 0), (0, LANES - per)))
    return jnp.concatenate([w_qg[:, :dq], wg.reshape(-1, N_KV_GROUPS * LANES)], axis=1)


def kernel(x, ffn_norm, ffn_w_gate_up, ffn_w_down, mix_norm, conv_w_in, conv_w, conv_w_out, kv_norm, kv_w,
           cmp_pos, cmp_w1, cmp_b1, cmp_w2, k_norm, nsa_w_qg, q_norm, nsa_w_o):
    b, s, d = x.shape
    t = b * s
    depth = ffn_norm.shape[0]
    n_a = conv_w_in.shape[0]
    g = N_KV_GROUPS

    ffn_g = ffn_norm.reshape(depth, 2, 1, d)
    ffn_wgu = ffn_w_gate_up.astype(BF16)
    ffn_wd = ffn_w_down.astype(BF16)

    def ffn(h2, layer, j, proj=None):
        return _ffn(h2, ffn_g, ffn_wgu, ffn_wd, (layer, j), proj)

    h = x.reshape(t, d)
    shared = None
    for layer in range(depth):
        h = ffn(h, layer, 0)
        proj = None
        if layer < n_a:
            h = _conv_mixer(h.reshape(b, s, d), mix_norm[layer], conv_w_in[layer].astype(BF16),
                            conv_w[layer], conv_w_out[layer].astype(BF16)).reshape(t, d)
        else:
            li = layer - n_a
            q, gate = _qg_proj(h, mix_norm[layer], _gate_columns(nsa_w_qg[li]).astype(BF16), q_norm[li])
            o = _nsa(q.reshape(b, s, -1), gate.reshape(b, s, -1), *shared)
            proj = (o.reshape(t, -1), nsa_w_o[li].astype(BF16))
        h = ffn(h, layer, 1, proj)
        if layer == n_a - 1:
            blk, ks, vst, kw, vwt = _kv_proj(h.reshape(b, s, d), kv_norm, kv_w.astype(BF16), k_norm)
            blk = blk.reshape(2, b * g, s // CMP_BLOCK, CMP_BLOCK * HEAD_DIM)
            cmp = _compress(blk, cmp_pos.reshape(2, -1), cmp_w1.astype(BF16), cmp_b1,
                            cmp_w2.astype(BF16), k_norm[0])
            cmp = cmp.reshape(2, b, g, -1, HEAD_DIM)
            shared = (cmp[0], cmp[1].transpose(0, 1, 3, 2), ks, vst, kw, vwt)
    return h.reshape(b, s, d)
```

```python
import functools

import numpy as np
import jax
import jax.numpy as jnp
from jax import lax
from jax.experimental import pallas as pl
from jax.experimental.pallas import tpu as pltpu

EPS = 1e-6
N_HEADS = 16
HEAD_DIM = 64
N_KV_GROUPS = 4
HEADS_PER_GROUP = N_HEADS // N_KV_GROUPS
N_KV_SLOTS = 6
N_BRANCH = 3
CMP_BLOCK = 32
CMP_STRIDE = 16
SEL_BLOCK = 64
N_SELECT = 16
WINDOW = 512
FORCE_SCORE = 1e9
MASK_SCORE = -1e30
LOG2E = 1.4426950408889634

LANES = 128
CHUNK = 256
V_ROWS = HEAD_DIM + 16
VMEM_LIMIT = 56 * 1024 * 1024

BF16 = jnp.bfloat16
F32 = jnp.float32


def _cparams(sem):
    return pltpu.CompilerParams(dimension_semantics=sem, vmem_limit_bytes=VMEM_LIMIT)


def _rms(x, g):
    ms = jnp.mean(x * x, axis=-1, keepdims=True)
    return x * lax.rsqrt(ms + EPS) * g


def _dot(a, b):
    return jnp.dot(a, b, preferred_element_type=F32)


def _split2(x):
    hi = x.astype(BF16)
    return hi, (x - hi.astype(F32)).astype(BF16)


def _seg_rms(y, seg, seg_t, gain):
    ss = None
    for part in _split2(y * y):
        term = _dot(part, seg)
        ss = term if ss is None else ss + term
    r = lax.rsqrt(ss * (1.0 / HEAD_DIM) + EPS)
    scale = None
    for part in _split2(r):
        term = _dot(part, seg_t)
        scale = term if scale is None else scale + term
    return y * scale * gain


def _seg_matrices(width):
    ind = (np.arange(width)[:, None] // HEAD_DIM == np.arange(LANES)[None, :]).astype(np.float32)
    return jnp.asarray(ind, BF16), jnp.asarray(ind.T, BF16)


def _dot_nt(a, b):
    return lax.dot_general(a, b, (((1,), (1,)), ((), ())), preferred_element_type=F32)


def _ffn_kernel(x_ref, g_ref, wgu_ref, wd_ref, *rest, dff):
    x = x_ref[...]
    if len(rest) == 3:
        a_ref, w_ref, o_ref = rest
        x = x + _dot(a_ref[...], w_ref[...])
    else:
        (o_ref,) = rest
    xn = _rms(x, g_ref[...]).astype(BF16)
    ab = _dot(xn, wgu_ref[...])
    a = ab[:, 0:dff]
    h = (a * jax.nn.sigmoid(a) * ab[:, dff:2 * dff]).astype(BF16)
    o_ref[...] = x + 0.5 * _dot(h, wd_ref[...])


def _resident(shape, lead=()):
    index = tuple(lead) + (0,) * len(shape)
    return pl.BlockSpec((None,) * len(lead) + tuple(shape), lambda *_: index, pipeline_mode=pl.Buffered(1))


def _ffn(h, g_all, w_gu_all, w_d_all, lead, proj=None, *, tm=512):
    t, d = h.shape
    dff = w_d_all.shape[-2]
    in_specs = [
        pl.BlockSpec((tm, d), lambda i: (i, 0)),
        _resident((1, d), lead),
        _resident((d, 2 * dff), lead),
        _resident((dff, d), lead),
    ]
    args = [h, g_all, w_gu_all, w_d_all]
    if proj is not None:
        a, w = proj
        in_specs += [pl.BlockSpec((tm, a.shape[1]), lambda i: (i, 0)), _resident(w.shape)]
        args += [a, w]
    return pl.pallas_call(
        functools.partial(_ffn_kernel, dff=dff),
        grid=(t // tm,),
        in_specs=in_specs,
        out_specs=pl.BlockSpec((tm, d), lambda i: (i, 0)),
        out_shape=jax.ShapeDtypeStruct((t, d), F32),
        compiler_params=_cparams(("parallel",)),
        name="ffn",
    )(*args)


CARRY = 8


def _conv_kernel(x_ref, g_ref, win_ref, cw_ref, wout_ref, o_ref, vbuf_ref, *, tm, d):
    j = pl.program_id(1)

    @pl.when(j == 0)
    def _():
        vbuf_ref[0:CARRY, :] = jnp.zeros((CARRY, d), F32)

    x = x_ref[0]
    xn = _rms(x, g_ref[...]).astype(BF16)
    proj = _dot(xn, win_ref[...])
    bg = proj[:, 0:d]
    v = proj[:, d:2 * d] * proj[:, 2 * d:3 * d]
    vbuf_ref[CARRY:CARRY + tm, :] = v
    cw = cw_ref[...]
    conv = (cw[0:1, :] * vbuf_ref[CARRY - 2:CARRY - 2 + tm, :]
            + cw[1:2, :] * vbuf_ref[CARRY - 1:CARRY - 1 + tm, :]
            + cw[2:3, :] * v)
    y = (bg * conv).astype(BF16)
    o_ref[0] = x + _dot(y, wout_ref[...])
    vbuf_ref[0:CARRY, :] = vbuf_ref[tm:tm + CARRY, :]


def _conv_mixer(h, g, w_in, conv_w, w_out, *, tm=512):
    b, s, d = h.shape
    return pl.pallas_call(
        functools.partial(_conv_kernel, tm=tm, d=d),
        grid=(b, s // tm),
        in_specs=[
            pl.BlockSpec((1, tm, d), lambda i, j: (i, j, 0)),
            pl.BlockSpec((1, d), lambda i, j: (0, 0)),
            pl.BlockSpec((d, 3 * d), lambda i, j: (0, 0)),
            pl.BlockSpec((conv_w.shape[0], d), lambda i, j: (0, 0)),
            pl.BlockSpec((d, d), lambda i, j: (0, 0)),
        ],
        out_specs=pl.BlockSpec((1, tm, d), lambda i, j: (i, j, 0)),
        out_shape=jax.ShapeDtypeStruct((b, s, d), F32),
        scratch_shapes=[pltpu.VMEM((CARRY + tm, d), F32)],
        compiler_params=_cparams(("arbitrary", "arbitrary")),
        name="conv_mixer",
    )(h, g.reshape(1, d), w_in, conv_w, w_out)


def _kv_kernel(x_ref, g_ref, w_ref, kn_ref, seg_ref, segt_ref, blk_ref, ks_ref, vst_ref, kw_ref, vwt_ref,
               raw_ref, *, tm):
    j = pl.program_id(1)
    xn = _rms(x_ref[0], g_ref[...]).astype(BF16)
    y = _dot(xn, w_ref[...])
    gw = N_KV_GROUPS * HEAD_DIM
    per = LANES // HEAD_DIM
    for slab in range(raw_ref.shape[0]):
        raw_ref[slab] = y[:, slab * LANES:(slab + 1) * LANES]
    for l in range(CMP_BLOCK):
        for slab in range(raw_ref.shape[0]):
            rows = raw_ref[slab, pl.ds(l, tm // CMP_BLOCK, stride=CMP_BLOCK), :]
            for k in range(per):
                sg = slab * per + k
                blk_ref[sg // N_KV_GROUPS, 0, sg % N_KV_GROUPS, :, l * HEAD_DIM:(l + 1) * HEAD_DIM] = (
                    rows[:, k * HEAD_DIM:(k + 1) * HEAD_DIM])
    pos = j * tm + lax.broadcasted_iota(jnp.int32, (tm, HEAD_DIM), 0)
    lane = lax.broadcasted_iota(jnp.int32, (tm, HEAD_DIM), 1)
    onehot = jnp.where(lane == pos // SEL_BLOCK, 1.0, 0.0).astype(BF16)
    k_sel = _seg_rms(y[:, 2 * gw:3 * gw], seg_ref[...], segt_ref[...], kn_ref[0:1, :]).astype(BF16)
    k_win = _seg_rms(y[:, 4 * gw:5 * gw], seg_ref[...], segt_ref[...], kn_ref[1:2, :]).astype(BF16)
    for grp in range(N_KV_GROUPS):
        ks_ref[0, grp, :, 0:HEAD_DIM] = k_sel[:, grp * HEAD_DIM:(grp + 1) * HEAD_DIM]
        ks_ref[0, grp, :, HEAD_DIM:2 * HEAD_DIM] = onehot
        kw_ref[0, grp] = k_win[:, grp * HEAD_DIM:(grp + 1) * HEAD_DIM]
    extra = jnp.where(lax.broadcasted_iota(jnp.int32, (V_ROWS - HEAD_DIM, CHUNK), 0) == 0, 1.0, 0.0).astype(BF16)
    for slot, ref in ((3, vst_ref), (5, vwt_ref)):
        vt = y[:, slot * gw:(slot + 1) * gw].T
        for grp in range(N_KV_GROUPS):
            for cc in range(tm // CHUNK):
                ref[0, grp, cc, 0:HEAD_DIM, :] = vt[grp * HEAD_DIM:(grp + 1) * HEAD_DIM,
                                                    cc * CHUNK:(cc + 1) * CHUNK].astype(BF16)
                ref[0, grp, cc, HEAD_DIM:V_ROWS, :] = extra


def _kv_proj(h, g, w, k_norm, *, tm=512):
    b, s, d = h.shape
    n = w.shape[1]
    ng = N_KV_GROUPS
    gw = ng * HEAD_DIM
    cpt = tm // CHUNK
    assert s // SEL_BLOCK <= HEAD_DIM
    seg, seg_t = _seg_matrices(gw)
    gains = jnp.tile(k_norm[1:3], (1, ng))
    return pl.pallas_call(
        functools.partial(_kv_kernel, tm=tm),
        grid=(b, s // tm),
        in_specs=[
            pl.BlockSpec((1, tm, d), lambda i, j: (i, j, 0)),
            _resident((1, d)),
            _resident((d, n)),
            _resident(gains.shape),
            _resident(seg.shape),
            _resident(seg_t.shape),
        ],
        out_specs=[
            pl.BlockSpec((2, 1, ng, tm // CMP_BLOCK, CMP_BLOCK * HEAD_DIM), lambda i, j: (0, i, 0, j, 0)),
            pl.BlockSpec((1, ng, tm, 2 * HEAD_DIM), lambda i, j: (i, 0, j, 0)),
            pl.BlockSpec((1, ng, cpt, V_ROWS, CHUNK), lambda i, j: (i, 0, j, 0, 0)),
            pl.BlockSpec((1, ng, tm, HEAD_DIM), lambda i, j: (i, 0, j, 0)),
            pl.BlockSpec((1, ng, cpt, V_ROWS, CHUNK), lambda i, j: (i, 0, j, 0, 0)),
        ],
        out_shape=[
            jax.ShapeDtypeStruct((2, b, ng, s // CMP_BLOCK, CMP_BLOCK * HEAD_DIM), F32),
            jax.ShapeDtypeStruct((b, ng, s, 2 * HEAD_DIM), BF16),
            jax.ShapeDtypeStruct((b, ng, s // CHUNK, V_ROWS, CHUNK), BF16),
            jax.ShapeDtypeStruct((b, ng, s, HEAD_DIM), BF16),
            jax.ShapeDtypeStruct((b, ng, s // CHUNK, V_ROWS, CHUNK), BF16),
        ],
        scratch_shapes=[pltpu.VMEM((2 * gw // LANES, tm, LANES), F32)],
        compiler_params=_cparams(("parallel", "parallel")),
        name="kv_proj",
    )(h, g.reshape(1, d), w, gains, seg, seg_t)


def _cmp_kernel(x_ref, pos_ref, w1_ref, b1_ref, w2_ref, kn_ref, o_ref):
    c = pl.program_id(0)
    nb, ne, width = x_ref.shape
    rows = nb * ne
    half = width // 2
    x = x_ref[...].reshape(rows, width)
    pos = pos_ref[...]
    w1 = w1_ref[...]
    h_even = _dot((x + pos).astype(BF16), w1)
    a = _dot((x[:, half:] + pos[:, :half]).astype(BF16), w1[0:half])
    b = _dot((x[:, :half] + pos[:, half:]).astype(BF16), w1[half:width])
    h_odd = a + pltpu.roll(b, rows - 1, 0)

    def tail(hid):
        y = _dot(jax.nn.gelu(hid + b1_ref[...]).astype(BF16), w2_ref[...])
        y = jnp.where(c == 0, _rms(y, kn_ref[...]), y)
        return y.astype(o_ref.dtype).reshape(nb, ne, y.shape[-1])

    o_ref[:, 0:ne, :] = tail(h_even)
    o_ref[:, ne:2 * ne, :] = tail(h_odd)


def _compress(xe, pos, w1, b1, w2, kn, *, nb=8):
    _, nseq, ne, width = xe.shape
    nb = min(nb, nseq)
    hid = w1.shape[-1]
    dk = w2.shape[-1]
    return pl.pallas_call(
        _cmp_kernel,
        grid=(2, nseq // nb),
        in_specs=[
            pl.BlockSpec((None, nb, ne, width), lambda c, i: (c, i, 0, 0)),
            pl.BlockSpec((None, 1, width), lambda c, i: (c, 0, 0)),
            pl.BlockSpec((None, width, hid), lambda c, i: (c, 0, 0)),
            pl.BlockSpec((None, 1, hid), lambda c, i: (c, 0, 0)),
            pl.BlockSpec((None, hid, dk), lambda c, i: (c, 0, 0)),
            pl.BlockSpec((1, dk), lambda c, i: (0, 0)),
        ],
        out_specs=pl.BlockSpec((None, nb, 2 * ne, dk), lambda c, i: (c, i, 0, 0)),
        out_shape=jax.ShapeDtypeStruct((2, nseq, 2 * ne, dk), BF16),
        compiler_params=_cparams(("parallel", "parallel")),
        name="compress",
    )(xe, pos.reshape(2, 1, width), w1, b1.reshape(2, 1, hid), w2, kn.reshape(1, dk))


def _qg_kernel(x_ref, g_ref, w_ref, q_ref, gate_ref, *, dq):
    xn = _rms(x_ref[...], g_ref[...]).astype(BF16)
    y = _dot(xn, w_ref[...])
    q_ref[...] = y[:, 0:dq]
    gate_ref[...] = jax.nn.sigmoid(y[:, dq:])


def _qg_proj(h, g, w, *, tm=512):
    t, d = h.shape
    n = w.shape[1]
    dq = N_HEADS * HEAD_DIM
    return pl.pallas_call(
        functools.partial(_qg_kernel, dq=dq),
        grid=(t // tm,),
        in_specs=[
            pl.BlockSpec((tm, d), lambda i: (i, 0)),
            _resident((1, d)),
            _resident((d, n)),
        ],
        out_specs=[pl.BlockSpec((tm, dq), lambda i: (i, 0)),
                   pl.BlockSpec((tm, n - dq), lambda i: (i, 0))],
        out_shape=[jax.ShapeDtypeStruct((t, dq), F32),
                   jax.ShapeDtypeStruct((t, n - dq), F32)],
        compiler_params=_cparams(("parallel",)),
        name="qg_proj",
    )(h, g.reshape(1, d), w)


def _split3(x):
    hi = x.astype(BF16)
    r = x - hi.astype(F32)
    mid = r.astype(BF16)
    lo = (r - mid.astype(F32)).astype(BF16)
    return hi, mid, lo


def _cmp_block_of_row(row, nrows):
    return jnp.where(row < nrows // 2, 2 * row, 2 * row - (nrows - 1))


def _nsa_kernel(*refs, n_cmp, n_blk):
    mask_ref, cmask_ref = refs[-2:]
    tq = CHUNK
    _, ncp, wide = cmask_ref.shape
    i = pl.program_id(1)
    kk = lax.broadcasted_iota(jnp.int32, (tq, wide), 0)
    tt = lax.broadcasted_iota(jnp.int32, (tq, wide), 1) & (tq - 1)
    mask_ref[0] = jnp.where(kk <= tt, 0.0, MASK_SCORE)
    mask_ref[1] = jnp.where(kk > tt + jnp.where(i >= 2, 0, tq), 0.0, MASK_SCORE)
    mask_ref[2] = jnp.where(kk > tt + jnp.where(i >= 1, -tq, tq), 0.0, MASK_SCORE)
    nidx = _cmp_block_of_row(lax.broadcasted_iota(jnp.int32, (ncp, wide), 0), ncp)
    tpos = i * tq + (lax.broadcasted_iota(jnp.int32, (ncp, wide), 1) & (tq - 1))
    valid = (nidx * CMP_STRIDE + (CMP_BLOCK - 1) <= tpos) & (nidx < n_cmp)
    cmask_ref[0] = jnp.where(valid, 0.0, MASK_SCORE)
    cmask_ref[1] = jnp.where(valid, 1.0, 0.0)

    n_in = 11
    scratch = refs[n_in:-2]
    for g in range(N_KV_GROUPS):
        slot = [r.at[g % 2] for r in scratch]
        _nsa_group(g, *refs[:n_in], *slot, mask_ref, cmask_ref, n_blk=n_blk)


def _nsa_group(g, q_ref, gate_ref, kc_ref, vct_ref, ks_ref, vst_ref, kw_ref, vwt_ref, ovt_ref, qgain_ref,
               o_ref, qa_ref, m_ref, acc_ref, out_ref, s_ref, cmax_ref, mask_ref, cmask_ref, *, n_blk):
    hpg = HEADS_PER_GROUP
    tq = CHUNK
    dk = HEAD_DIM
    wide = hpg * tq
    i = pl.program_id(1)
    gd = hpg * dk
    q_lanes = pl.ds(g * gd, gd)
    t0 = i * tq

    def heads(fn):
        return jnp.concatenate([fn(h) for h in range(hpg)], axis=1)

    q_t = q_ref[0, :, q_lanes].T
    q_t = heads(lambda h: q_t[h * dk:(h + 1) * dk, :])
    ms = jnp.sum(q_t * q_t, axis=0, keepdims=True) * (1.0 / dk)
    qa_ref[0:dk, :] = (q_t * lax.rsqrt(ms + EPS) * qgain_ref[...]).astype(BF16)
    gate_t = gate_ref[0, :, pl.ds(g * LANES, LANES)].T
    gates = [heads(lambda h: gate_t[h * N_BRANCH + br:h * N_BRANCH + br + 1, :]) for br in range(N_BRANCH)]

    causal, win_lo, win_mid = 0, 1, 2

    m_ref[...] = jnp.full(m_ref.shape, MASK_SCORE, F32)
    acc_ref[...] = jnp.zeros(acc_ref.shape, F32)

    def key_rows(ref, c):
        return ref[0, g, pl.ds(pl.multiple_of(c * tq, tq), tq), :]

    def produce(b, k, qrows, mask):
        s = _dot(k, qa_ref[0:qrows, :])
        if mask is not None:
            s = s + mask_ref[mask]
        s_ref[b] = s
        cmax_ref[b] = jnp.max(s, axis=0, keepdims=True)

    def consume(b, vt, st):
        m_prev = m_ref[st]
        m_new = jnp.maximum(m_prev, cmax_ref[b])
        alpha = jnp.exp2(m_prev - m_new)
        p = jnp.exp2(s_ref[b] - m_new)
        acc_ref[st] = alpha * acc_ref[st] + _dot(vt, p.astype(BF16))
        m_ref[st] = m_new

    def flash_out(st, branch):
        out_ref[...] += gates[branch] * (acc_ref[st, 0:dk, :] * (1.0 / acc_ref[st, dk:dk + 1, :]))

    assert WINDOW == 2 * tq
    lo = jnp.maximum(i - 2, 0)
    mid = jnp.maximum(i - 1, 0)
    produce(0, key_rows(kw_ref, lo), dk, win_lo)

    s = _dot(kc_ref[0, g], qa_ref[0:dk, :]) + cmask_ref[0]
    e = jnp.exp2(s - jnp.max(s, axis=0, keepdims=True))
    p = e * (1.0 / jnp.sum(e, axis=0, keepdims=True)) * cmask_ref[1]
    out_ref[...] = gates[0] * _dot(vct_ref[0, g], p.astype(BF16))
    psum = p[:, 0:tq]
    for h in range(1, hpg):
        psum = psum + p[:, h * tq:(h + 1) * tq]

    produce(1, key_rows(kw_ref, mid), dk, win_mid)
    consume(0, vwt_ref[0, g, lo], 1)

    ovt = ovt_ref[...]
    imp = None
    for part in _split3(psum):
        term = _dot(ovt, part)
        imp = term if imp is None else imp + term
    jblk = lax.broadcasted_iota(jnp.int32, (n_blk, tq), 0)
    cur = (t0 + lax.broadcasted_iota(jnp.int32, (n_blk, tq), 1)) // SEL_BLOCK
    forced = (jblk == 0) | (jblk == cur) | (jblk == cur - 1)
    imp = jnp.where(forced, FORCE_SCORE, imp)
    imp = jnp.where(jblk > cur, MASK_SCORE, imp)
    sub = 8
    parts = [imp[r:r + sub] for r in range(0, n_blk, sub)]
    ranks = [jnp.zeros((sub, tq), F32) for _ in parts]
    jsub = lax.broadcasted_iota(jnp.int32, (sub, tq), 0)
    for ii in range(n_blk):
        row = imp[ii:ii + 1, :]
        for gi, x in enumerate(parts):
            if ii < gi * sub:
                beats = row >= x
            elif ii >= (gi + 1) * sub:
                beats = row > x
            else:
                beats = (row > x) | ((row == x) & (jsub > ii - gi * sub))
            ranks[gi] = ranks[gi] + jnp.where(beats, 1.0, 0.0)
    rank = jnp.concatenate(ranks, axis=0)
    chosen = (rank < float(min(N_SELECT, n_blk))) & (jblk <= cur)
    bias = jnp.where(chosen, 0.0, MASK_SCORE)
    bias = jnp.concatenate([bias, jnp.zeros((dk - n_blk, tq), F32)], axis=0).astype(BF16)
    qa_ref[dk:2 * dk, :] = heads(lambda h: bias)

    produce(0, key_rows(kw_ref, i), dk, causal)
    consume(1, vwt_ref[0, g, mid], 1)

    npair = i // 2
    held = jnp.where(i >= 2, 2 * npair - 1, i)
    last = jnp.where(i % 2 == 0, i + 1, i - 1)
    produce(1, key_rows(ks_ref, i), 2 * dk, causal)
    consume(0, vwt_ref[0, g, i], 1)
    flash_out(1, 2)

    def pair_body(p, carry):
        c = 2 * p
        produce(0, key_rows(ks_ref, c), 2 * dk, None)
        consume(1, vst_ref[0, g, jnp.where(p == 0, i, c - 1)], 0)
        produce(1, key_rows(ks_ref, c + 1), 2 * dk, None)
        consume(0, vst_ref[0, g, c], 0)
        return carry

    lax.fori_loop(0, npair, pair_body, 0)
    produce(0, key_rows(ks_ref, last), 2 * dk, None)
    consume(1, vst_ref[0, g, held], 0)
    consume(0, vst_ref[0, g, last], 0)
    flash_out(0, 1)

    out = out_ref[...]
    out = jnp.concatenate([out[:, h * tq:(h + 1) * tq] for h in range(hpg)], axis=0)
    o_ref[0, :, q_lanes] = out.T.astype(o_ref.dtype)


def _nsa(q, gate, q_norm, kc, vct, ks, vst, kw, vwt):
    b, s, dq = q.shape
    g = N_KV_GROUPS
    hpg = HEADS_PER_GROUP
    dk = HEAD_DIM
    gd = hpg * dk
    tq = CHUNK
    n_blk = s // SEL_BLOCK
    n_cmp = (s - CMP_BLOCK) // CMP_STRIDE + 1
    ncp = kc.shape[2]
    assert WINDOW == 2 * tq and n_blk <= dk and (s // tq) % 2 == 0
    rows_np = np.arange(ncp)
    cmp_idx = np.where(rows_np < ncp // 2, 2 * rows_np, 2 * rows_np - (ncp - 1))
    cmp_start = cmp_idx * CMP_STRIDE
    blk_start = np.arange(n_blk) * SEL_BLOCK
    ovt = ((cmp_start[None, :] < blk_start[:, None] + SEL_BLOCK)
           & (cmp_start[None, :] + CMP_BLOCK > blk_start[:, None])
           & (cmp_idx[None, :] < n_cmp)).astype(np.float32)
    qgain = jnp.broadcast_to((q_norm * (dk ** -0.5 * LOG2E))[:, None], (dk, hpg * tq))
    idx = lambda bi, i: (bi, 0, 0, 0)
    idx5 = lambda bi, i: (bi, 0, 0, 0, 0)
    return pl.pallas_call(
        functools.partial(_nsa_kernel, n_cmp=n_cmp, n_blk=n_blk),
        grid=(b, s // tq),
        in_specs=[
            pl.BlockSpec((1, tq, dq), lambda bi, i: (bi, i, 0)),
            pl.BlockSpec((1, tq, g * LANES), lambda bi, i: (bi, i, 0)),
            pl.BlockSpec((1, g, ncp, dk), idx),
            pl.BlockSpec((1, g, dk, ncp), idx),
            pl.BlockSpec((1, g, s, 2 * dk), idx),
            pl.BlockSpec((1, g, s // tq, V_ROWS, tq), idx5),
            pl.BlockSpec((1, g, s, dk), idx),
            pl.BlockSpec((1, g, s // tq, V_ROWS, tq), idx5),
            _resident((n_blk, ncp)),
            _resident((dk, hpg * tq)),
        ],
        out_specs=pl.BlockSpec((1, tq, dq), lambda bi, i: (bi, i, 0)),
        out_shape=jax.ShapeDtypeStruct((b, s, dq), BF16),
        scratch_shapes=[
            pltpu.VMEM((2, 2 * dk, hpg * tq), BF16),
            pltpu.VMEM((2, 2, 1, hpg * tq), F32),
            pltpu.VMEM((2, 2, V_ROWS, hpg * tq), F32),
            pltpu.VMEM((2, dk, hpg * tq), F32),
            pltpu.VMEM((2, 2, tq, hpg * tq), F32),
            pltpu.VMEM((2, 2, 1, hpg * tq), F32),
            pltpu.VMEM((3, tq, hpg * tq), F32),
            pltpu.VMEM((2, ncp, hpg * tq), F32),
        ],
        compiler_params=_cparams(("parallel", "arbitrary")),
        name="nsa",
    )(q, gate, kc, vct, ks, vst, kw, vwt, jnp.asarray(ovt, BF16), qgain)


def _gate_columns(w_qg):
    dq = N_HEADS * HEAD_DIM
    per = HEADS_PER_GROUP * N_BRANCH
    wg = w_qg[:, dq:].reshape(-1, N_KV_GROUPS, per)
    wg = jnp.pad(wg, ((0, 0), (0, 0), (0, LANES - per)))
    return jnp.concatenate([w_qg[:, :dq], wg.reshape(-1, N_KV_GROUPS * LANES)], axis=1)


def kernel(x, ffn_norm, ffn_w_gate_up, ffn_w_down, mix_norm, conv_w_in, conv_w, conv_w_out, kv_norm, kv_w,
           cmp_pos, cmp_w1, cmp_b1, cmp_w2, k_norm, nsa_w_qg, q_norm, nsa_w_o):
    b, s, d = x.shape
    t = b * s
    depth = ffn_norm.shape[0]
    n_a = conv_w_in.shape[0]
    g = N_KV_GROUPS

    ffn_g = ffn_norm.reshape(depth, 2, 1, d)
    ffn_wgu = ffn_w_gate_up.astype(BF16)
    ffn_wd = ffn_w_down.astype(BF16)

    def ffn(h2, layer, j, proj=None):
        return _ffn(h2, ffn_g, ffn_wgu, ffn_wd, (layer, j), proj)

    h = x.reshape(t, d)
    shared = None
    for layer in range(depth):
        h = ffn(h, layer, 0)
        proj = None
        if layer < n_a:
            h = _conv_mixer(h.reshape(b, s, d), mix_norm[layer], conv_w_in[layer].astype(BF16),
                            conv_w[layer], conv_w_out[layer].astype(BF16)).reshape(t, d)
        else:
            li = layer - n_a
            q, gate = _qg_proj(h, mix_norm[layer], _gate_columns(nsa_w_qg[li]).astype(BF16))
            o = _nsa(q.reshape(b, s, -1), gate.reshape(b, s, -1), q_norm[li], *shared)
            proj = (o.reshape(t, -1), nsa_w_o[li].astype(BF16))
        h = ffn(h, layer, 1, proj)
        if layer == n_a - 1:
            blk, ks, vst, kw, vwt = _kv_proj(h.reshape(b, s, d), kv_norm, kv_w.astype(BF16), k_norm)
            blk = blk.reshape(2, b * g, s // CMP_BLOCK, CMP_BLOCK * HEAD_DIM)
            cmp = _compress(blk, cmp_pos.reshape(2, -1), cmp_w1.astype(BF16), cmp_b1,
                            cmp_w2.astype(BF16), k_norm[0])
            cmp = cmp.reshape(2, b, g, -1, HEAD_DIM)
            shared = (cmp[0], cmp[1].transpose(0, 1, 3, 2), ks, vst, kw, vwt)
    return h.reshape(b, s, d)
```

```python
import functools

import numpy as np
import jax
import jax.numpy as jnp
from jax import lax
from jax.experimental import pallas as pl
from jax.experimental.pallas import tpu as pltpu

EPS = 1e-6
N_HEADS = 16
HEAD_DIM = 64
N_KV_GROUPS = 4
HEADS_PER_GROUP = N_HEADS // N_KV_GROUPS
N_KV_SLOTS = 6
N_BRANCH = 3
CMP_BLOCK = 32
CMP_STRIDE = 16
SEL_BLOCK = 64
N_SELECT = 16
WINDOW = 512
FORCE_SCORE = 1e9
MASK_SCORE = -1e30
LOG2E = 1.4426950408889634

LANES = 128
CHUNK = 256
V_ROWS = HEAD_DIM + 16
VMEM_LIMIT = 56 * 1024 * 1024

BF16 = jnp.bfloat16
F32 = jnp.float32


def _cparams(sem):
    return pltpu.CompilerParams(dimension_semantics=sem, vmem_limit_bytes=VMEM_LIMIT)


def _rms(x, g):
    ms = jnp.mean(x * x, axis=-1, keepdims=True)
    return x * lax.rsqrt(ms + EPS) * g


def _dot(a, b):
    return jnp.dot(a, b, preferred_element_type=F32)


def _split2(x):
    hi = x.astype(BF16)
    return hi, (x - hi.astype(F32)).astype(BF16)


def _seg_rms(y, seg, seg_t, gain):
    ss = None
    for part in _split2(y * y):
        term = _dot(part, seg)
        ss = term if ss is None else ss + term
    r = lax.rsqrt(ss * (1.0 / HEAD_DIM) + EPS)
    scale = None
    for part in _split2(r):
        term = _dot(part, seg_t)
        scale = term if scale is None else scale + term
    return y * scale * gain


def _seg_matrices(width):
    ind = (np.arange(width)[:, None] // HEAD_DIM == np.arange(LANES)[None, :]).astype(np.float32)
    return jnp.asarray(ind, BF16), jnp.asarray(ind.T, BF16)


def _dot_nt(a, b):
    return lax.dot_general(a, b, (((1,), (1,)), ((), ())), preferred_element_type=F32)


def _ffn_kernel(x_ref, g_ref, wgu_ref, wd_ref, *rest, dff):
    x = x_ref[...]
    if len(rest) == 3:
        a_ref, w_ref, o_ref = rest
        x = x + _dot(a_ref[...], w_ref[...])
    else:
        (o_ref,) = rest
    xn = _rms(x, g_ref[...]).astype(BF16)
    ab = _dot(xn, wgu_ref[...])
    a = ab[:, 0:dff]
    h = (a * jax.nn.sigmoid(a) * ab[:, dff:2 * dff]).astype(BF16)
    o_ref[...] = x + 0.5 * _dot(h, wd_ref[...])


def _resident(shape, lead=()):
    index = tuple(lead) + (0,) * len(shape)
    return pl.BlockSpec((None,) * len(lead) + tuple(shape), lambda *_: index, pipeline_mode=pl.Buffered(1))


def _ffn(h, g_all, w_gu_all, w_d_all, lead, proj=None, *, tm=512):
    t, d = h.shape
    dff = w_d_all.shape[-2]
    in_specs = [
        pl.BlockSpec((tm, d), lambda i: (i, 0)),
        _resident((1, d), lead),
        _resident((d, 2 * dff), lead),
        _resident((dff, d), lead),
    ]
    args = [h, g_all, w_gu_all, w_d_all]
    if proj is not None:
        a, w = proj
        in_specs += [pl.BlockSpec((tm, a.shape[1]), lambda i: (i, 0)), _resident(w.shape)]
        args += [a, w]
    return pl.pallas_call(
        functools.partial(_ffn_kernel, dff=dff),
        grid=(t // tm,),
        in_specs=in_specs,
        out_specs=pl.BlockSpec((tm, d), lambda i: (i, 0)),
        out_shape=jax.ShapeDtypeStruct((t, d), F32),
        compiler_params=_cparams(("parallel",)),
        name="ffn",
    )(*args)


CARRY = 8


def _conv_kernel(x_ref, g_ref, win_ref, cw_ref, wout_ref, o_ref, vbuf_ref, *, tm, d):
    j = pl.program_id(1)

    @pl.when(j == 0)
    def _():
        vbuf_ref[0:CARRY, :] = jnp.zeros((CARRY, d), F32)

    x = x_ref[0]
    xn = _rms(x, g_ref[...]).astype(BF16)
    proj = _dot(xn, win_ref[...])
    bg = proj[:, 0:d]
    v = proj[:, d:2 * d] * proj[:, 2 * d:3 * d]
    vbuf_ref[CARRY:CARRY + tm, :] = v
    cw = cw_ref[...]
    conv = (cw[0:1, :] * vbuf_ref[CARRY - 2:CARRY - 2 + tm, :]
            + cw[1:2, :] * vbuf_ref[CARRY - 1:CARRY - 1 + tm, :]
            + cw[2:3, :] * v)
    y = (bg * conv).astype(BF16)
    o_ref[0] = x + _dot(y, wout_ref[...])
    vbuf_ref[0:CARRY, :] = vbuf_ref[tm:tm + CARRY, :]


def _conv_mixer(h, g, w_in, conv_w, w_out, *, tm=512):
    b, s, d = h.shape
    return pl.pallas_call(
        functools.partial(_conv_kernel, tm=tm, d=d),
        grid=(b, s // tm),
        in_specs=[
            pl.BlockSpec((1, tm, d), lambda i, j: (i, j, 0)),
            pl.BlockSpec((1, d), lambda i, j: (0, 0)),
            pl.BlockSpec((d, 3 * d), lambda i, j: (0, 0)),
            pl.BlockSpec((conv_w.shape[0], d), lambda i, j: (0, 0)),
            pl.BlockSpec((d, d), lambda i, j: (0, 0)),
        ],
        out_specs=pl.BlockSpec((1, tm, d), lambda i, j: (i, j, 0)),
        out_shape=jax.ShapeDtypeStruct((b, s, d), F32),
        scratch_shapes=[pltpu.VMEM((CARRY + tm, d), F32)],
        compiler_params=_cparams(("arbitrary", "arbitrary")),
        name="conv_mixer",
    )(h, g.reshape(1, d), w_in, conv_w, w_out)


def _kv_kernel(x_ref, g_ref, w_ref, kn_ref, seg_ref, segt_ref, blk_ref, ks_ref, vst_ref, kw_ref, vwt_ref,
               raw_ref, *, tm):
    j = pl.program_id(1)
    xn = _rms(x_ref[0], g_ref[...]).astype(BF16)
    y = _dot(xn, w_ref[...])
    gw = N_KV_GROUPS * HEAD_DIM
    per = LANES // HEAD_DIM
    for slab in range(raw_ref.shape[0]):
        raw_ref[slab] = y[:, slab * LANES:(slab + 1) * LANES]
    for l in range(CMP_BLOCK):
        for slab in range(raw_ref.shape[0]):
            rows = raw_ref[slab, pl.ds(l, tm // CMP_BLOCK, stride=CMP_BLOCK), :]
            for k in range(per):
                sg = slab * per + k
                blk_ref[sg // N_KV_GROUPS, 0, sg % N_KV_GROUPS, :, l * HEAD_DIM:(l + 1) * HEAD_DIM] = (
                    rows[:, k * HEAD_DIM:(k + 1) * HEAD_DIM])
    pos = j * tm + lax.broadcasted_iota(jnp.int32, (tm, HEAD_DIM), 0)
    lane = lax.broadcasted_iota(jnp.int32, (tm, HEAD_DIM), 1)
    onehot = jnp.where(lane == pos // SEL_BLOCK, 1.0, 0.0).astype(BF16)
    k_sel = _seg_rms(y[:, 2 * gw:3 * gw], seg_ref[...], segt_ref[...], kn_ref[0:1, :]).astype(BF16)
    k_win = _seg_rms(y[:, 4 * gw:5 * gw], seg_ref[...], segt_ref[...], kn_ref[1:2, :]).astype(BF16)
    for grp in range(N_KV_GROUPS):
        ks_ref[0, grp, :, 0:HEAD_DIM] = k_sel[:, grp * HEAD_DIM:(grp + 1) * HEAD_DIM]
        ks_ref[0, grp, :, HEAD_DIM:2 * HEAD_DIM] = onehot
        kw_ref[0, grp] = k_win[:, grp * HEAD_DIM:(grp + 1) * HEAD_DIM]
    extra = jnp.where(lax.broadcasted_iota(jnp.int32, (V_ROWS - HEAD_DIM, CHUNK), 0) == 0, 1.0, 0.0).astype(BF16)
    for slot, ref in ((3, vst_ref), (5, vwt_ref)):
        vt = y[:, slot * gw:(slot + 1) * gw].T
        for grp in range(N_KV_GROUPS):
            for cc in range(tm // CHUNK):
                ref[0, grp, cc, 0:HEAD_DIM, :] = vt[grp * HEAD_DIM:(grp + 1) * HEAD_DIM,
                                                    cc * CHUNK:(cc + 1) * CHUNK].astype(BF16)
                ref[0, grp, cc, HEAD_DIM:V_ROWS, :] = extra


def _kv_proj(h, g, w, k_norm, *, tm=512):
    b, s, d = h.shape
    n = w.shape[1]
    ng = N_KV_GROUPS
    gw = ng * HEAD_DIM
    cpt = tm // CHUNK
    assert s // SEL_BLOCK <= HEAD_DIM
    seg, seg_t = _seg_matrices(gw)
    gains = jnp.tile(k_norm[1:3], (1, ng))
    return pl.pallas_call(
        functools.partial(_kv_kernel, tm=tm),
        grid=(b, s // tm),
        in_specs=[
            pl.BlockSpec((1, tm, d), lambda i, j: (i, j, 0)),
            _resident((1, d)),
            _resident((d, n)),
            _resident(gains.shape),
            _resident(seg.shape),
            _resident(seg_t.shape),
        ],
        out_specs=[
            pl.BlockSpec((2, 1, ng, tm // CMP_BLOCK, CMP_BLOCK * HEAD_DIM), lambda i, j: (0, i, 0, j, 0)),
            pl.BlockSpec((1, ng, tm, 2 * HEAD_DIM), lambda i, j: (i, 0, j, 0)),
            pl.BlockSpec((1, ng, cpt, V_ROWS, CHUNK), lambda i, j: (i, 0, j, 0, 0)),
            pl.BlockSpec((1, ng, tm, HEAD_DIM), lambda i, j: (i, 0, j, 0)),
            pl.BlockSpec((1, ng, cpt, V_ROWS, CHUNK), lambda i, j: (i, 0, j, 0, 0)),
        ],
        out_shape=[
            jax.ShapeDtypeStruct((2, b, ng, s // CMP_BLOCK, CMP_BLOCK * HEAD_DIM), F32),
            jax.ShapeDtypeStruct((b, ng, s, 2 * HEAD_DIM), BF16),
            jax.ShapeDtypeStruct((b, ng, s // CHUNK, V_ROWS, CHUNK), BF16),
            jax.ShapeDtypeStruct((b, ng, s, HEAD_DIM), BF16),
            jax.ShapeDtypeStruct((b, ng, s // CHUNK, V_ROWS, CHUNK), BF16),
        ],
        scratch_shapes=[pltpu.VMEM((2 * gw // LANES, tm, LANES), F32)],
        compiler_params=_cparams(("parallel", "parallel")),
        name="kv_proj",
    )(h, g.reshape(1, d), w, gains, seg, seg_t)


def _cmp_kernel(x_ref, pos_ref, w1_ref, b1_ref, w2_ref, kn_ref, o_ref):
    c = pl.program_id(0)
    nb, ne, width = x_ref.shape
    rows = nb * ne
    half = width // 2
    x = x_ref[...].reshape(rows, width)
    pos = pos_ref[...]
    w1 = w1_ref[...]
    h_even = _dot((x + pos).astype(BF16), w1)
    a = _dot((x[:, half:] + pos[:, :half]).astype(BF16), w1[0:half])
    b = _dot((x[:, :half] + pos[:, half:]).astype(BF16), w1[half:width])
    h_odd = a + pltpu.roll(b, rows - 1, 0)

    def tail(hid):
        y = _dot(jax.nn.gelu(hid + b1_ref[...]).astype(BF16), w2_ref[...])
        y = jnp.where(c == 0, _rms(y, kn_ref[...]), y)
        return y.astype(o_ref.dtype).reshape(nb, ne, y.shape[-1])

    o_ref[:, 0:ne, :] = tail(h_even)
    o_ref[:, ne:2 * ne, :] = tail(h_odd)


def _compress(xe, pos, w1, b1, w2, kn, *, nb=8):
    _, nseq, ne, width = xe.shape
    nb = min(nb, nseq)
    hid = w1.shape[-1]
    dk = w2.shape[-1]
    return pl.pallas_call(
        _cmp_kernel,
        grid=(2, nseq // nb),
        in_specs=[
            pl.BlockSpec((None, nb, ne, width), lambda c, i: (c, i, 0, 0)),
            pl.BlockSpec((None, 1, width), lambda c, i: (c, 0, 0)),
            pl.BlockSpec((None, width, hid), lambda c, i: (c, 0, 0)),
            pl.BlockSpec((None, 1, hid), lambda c, i: (c, 0, 0)),
            pl.BlockSpec((None, hid, dk), lambda c, i: (c, 0, 0)),
            pl.BlockSpec((1, dk), lambda c, i: (0, 0)),
        ],
        out_specs=pl.BlockSpec((None, nb, 2 * ne, dk), lambda c, i: (c, i, 0, 0)),
        out_shape=jax.ShapeDtypeStruct((2, nseq, 2 * ne, dk), BF16),
        compiler_params=_cparams(("parallel", "parallel")),
        name="compress",
    )(xe, pos.reshape(2, 1, width), w1, b1.reshape(2, 1, hid), w2, kn.reshape(1, dk))


def _qg_kernel(x_ref, g_ref, w_ref, q_ref, gate_ref, *, dq):
    xn = _rms(x_ref[...], g_ref[...]).astype(BF16)
    y = _dot(xn, w_ref[...])
    q_ref[...] = y[:, 0:dq]
    gate_ref[...] = jax.nn.sigmoid(y[:, dq:])


def _qg_proj(h, g, w, *, tm=512):
    t, d = h.shape
    n = w.shape[1]
    dq = N_HEADS * HEAD_DIM
    return pl.pallas_call(
        functools.partial(_qg_kernel, dq=dq),
        grid=(t // tm,),
        in_specs=[
            pl.BlockSpec((tm, d), lambda i: (i, 0)),
            _resident((1, d)),
            _resident((d, n)),
        ],
        out_specs=[pl.BlockSpec((tm, dq), lambda i: (i, 0)),
                   pl.BlockSpec((tm, n - dq), lambda i: (i, 0))],
        out_shape=[jax.ShapeDtypeStruct((t, dq), F32),
                   jax.ShapeDtypeStruct((t, n - dq), F32)],
        compiler_params=_cparams(("parallel",)),
        name="qg_proj",
    )(h, g.reshape(1, d), w)


def _split3(x):
    hi = x.astype(BF16)
    r = x - hi.astype(F32)
    mid = r.astype(BF16)
    lo = (r - mid.astype(F32)).astype(BF16)
    return hi, mid, lo


def _cmp_block_of_row(row, nrows):
    return jnp.where(row < nrows // 2, 2 * row, 2 * row - (nrows - 1))


def _nsa_kernel(*refs, n_cmp, n_blk):
    mask_ref, cmask_ref = refs[-2:]
    tq = CHUNK
    _, ncp, wide = cmask_ref.shape
    i = pl.program_id(1)
    kk = lax.broadcasted_iota(jnp.int32, (tq, wide), 0)
    tt = lax.broadcasted_iota(jnp.int32, (tq, wide), 1) & (tq - 1)
    mask_ref[0] = jnp.where(kk <= tt, 0.0, MASK_SCORE)
    mask_ref[1] = jnp.where(kk > tt + jnp.where(i >= 2, 0, tq), 0.0, MASK_SCORE)
    mask_ref[2] = jnp.where(kk > tt + jnp.where(i >= 1, -tq, tq), 0.0, MASK_SCORE)
    nidx = _cmp_block_of_row(lax.broadcasted_iota(jnp.int32, (ncp, wide), 0), ncp)
    tpos = i * tq + (lax.broadcasted_iota(jnp.int32, (ncp, wide), 1) & (tq - 1))
    valid = (nidx * CMP_STRIDE + (CMP_BLOCK - 1) <= tpos) & (nidx < n_cmp)
    cmask_ref[0] = jnp.where(valid, 0.0, MASK_SCORE)
    cmask_ref[1] = jnp.where(valid, 1.0, 0.0)

    n_in = 11
    scratch = refs[n_in:-2]
    groups = [_nsa_group(g, *refs[:n_in], *[r.at[g] for r in scratch], mask_ref, cmask_ref, n_blk=n_blk)
              for g in range(N_KV_GROUPS)]
    done = object()
    while True:
        stage = [next(grp, done) for grp in groups]
        if stage[0] is done:
            break
        if stage[0] is not None:
            def pairs(p, carry):
                for body in stage:
                    body(p)
                return carry

            lax.fori_loop(0, i // 2, pairs, 0)


def _nsa_group(g, q_ref, gate_ref, kc_ref, vct_ref, ks_ref, vst_ref, kw_ref, vwt_ref, ovt_ref, qgain_ref,
               o_ref, qa_ref, m_ref, acc_ref, out_ref, s_ref, cmax_ref, mask_ref, cmask_ref, *, n_blk):
    hpg = HEADS_PER_GROUP
    tq = CHUNK
    dk = HEAD_DIM
    wide = hpg * tq
    i = pl.program_id(1)
    gd = hpg * dk
    q_lanes = pl.ds(g * gd, gd)
    t0 = i * tq

    def heads(fn):
        return jnp.concatenate([fn(h) for h in range(hpg)], axis=1)

    q_t = q_ref[0, :, q_lanes].T
    q_t = heads(lambda h: q_t[h * dk:(h + 1) * dk, :])
    ms = jnp.sum(q_t * q_t, axis=0, keepdims=True) * (1.0 / dk)
    qa_ref[0:dk, :] = (q_t * lax.rsqrt(ms + EPS) * qgain_ref[...]).astype(BF16)
    gate_t = gate_ref[0, :, pl.ds(g * LANES, LANES)].T
    gates = [heads(lambda h: gate_t[h * N_BRANCH + br:h * N_BRANCH + br + 1, :]) for br in range(N_BRANCH)]

    causal, win_lo, win_mid = 0, 1, 2

    m_ref[...] = jnp.full(m_ref.shape, MASK_SCORE, F32)
    acc_ref[...] = jnp.zeros(acc_ref.shape, F32)

    def key_rows(ref, c):
        return ref[0, g, pl.ds(pl.multiple_of(c * tq, tq), tq), :]

    def produce(b, k, qrows, mask):
        s = _dot(k, qa_ref[0:qrows, :])
        if mask is not None:
            s = s + mask_ref[mask]
        s_ref[b] = s
        cmax_ref[b] = jnp.max(s, axis=0, keepdims=True)

    def consume(b, vt, st):
        m_prev = m_ref[st]
        m_new = jnp.maximum(m_prev, cmax_ref[b])
        alpha = jnp.exp2(m_prev - m_new)
        p = jnp.exp2(s_ref[b] - m_new)
        acc_ref[st] = alpha * acc_ref[st] + _dot(vt, p.astype(BF16))
        m_ref[st] = m_new

    def flash_out(st, branch):
        out_ref[...] += gates[branch] * (acc_ref[st, 0:dk, :] * (1.0 / acc_ref[st, dk:dk + 1, :]))

    assert WINDOW == 2 * tq
    lo = jnp.maximum(i - 2, 0)
    mid = jnp.maximum(i - 1, 0)
    yield
    produce(0, key_rows(kw_ref, lo), dk, win_lo)
    yield

    s = _dot(kc_ref[0, g], qa_ref[0:dk, :]) + cmask_ref[0]
    e = jnp.exp2(s - jnp.max(s, axis=0, keepdims=True))
    p = e * (1.0 / jnp.sum(e, axis=0, keepdims=True)) * cmask_ref[1]
    out_ref[...] = gates[0] * _dot(vct_ref[0, g], p.astype(BF16))
    psum = p[:, 0:tq]
    for h in range(1, hpg):
        psum = psum + p[:, h * tq:(h + 1) * tq]
    yield

    produce(1, key_rows(kw_ref, mid), dk, win_mid)
    consume(0, vwt_ref[0, g, lo], 1)
    yield

    ovt = ovt_ref[...]
    imp = None
    for part in _split3(psum):
        term = _dot(ovt, part)
        imp = term if imp is None else imp + term
    jblk = lax.broadcasted_iota(jnp.int32, (n_blk, tq), 0)
    cur = (t0 + lax.broadcasted_iota(jnp.int32, (n_blk, tq), 1)) // SEL_BLOCK
    forced = (jblk == 0) | (jblk == cur) | (jblk == cur - 1)
    imp = jnp.where(forced, FORCE_SCORE, imp)
    imp = jnp.where(jblk > cur, MASK_SCORE, imp)
    sub = 8
    parts = [imp[r:r + sub] for r in range(0, n_blk, sub)]
    ranks = [jnp.zeros((sub, tq), F32) for _ in parts]
    jsub = lax.broadcasted_iota(jnp.int32, (sub, tq), 0)
    for ii in range(n_blk):
        row = imp[ii:ii + 1, :]
        for gi, x in enumerate(parts):
            if ii < gi * sub:
                beats = row >= x
            elif ii >= (gi + 1) * sub:
                beats = row > x
            else:
                beats = (row > x) | ((row == x) & (jsub > ii - gi * sub))
            ranks[gi] = ranks[gi] + jnp.where(beats, 1.0, 0.0)
    rank = jnp.concatenate(ranks, axis=0)
    chosen = (rank < float(min(N_SELECT, n_blk))) & (jblk <= cur)
    bias = jnp.where(chosen, 0.0, MASK_SCORE)
    bias = jnp.concatenate([bias, jnp.zeros((dk - n_blk, tq), F32)], axis=0).astype(BF16)
    qa_ref[dk:2 * dk, :] = heads(lambda h: bias)
    yield

    produce(0, key_rows(kw_ref, i), dk, causal)
    consume(1, vwt_ref[0, g, mid], 1)
    yield

    npair = i // 2
    held = jnp.where(i >= 2, 2 * npair - 1, i)
    last = jnp.where(i % 2 == 0, i + 1, i - 1)
    produce(1, key_rows(ks_ref, i), 2 * dk, causal)
    consume(0, vwt_ref[0, g, i], 1)
    flash_out(1, 2)

    def pair_body(p):
        c = 2 * p
        produce(0, key_rows(ks_ref, c), 2 * dk, None)
        consume(1, vst_ref[0, g, jnp.where(p == 0, i, c - 1)], 0)
        produce(1, key_rows(ks_ref, c + 1), 2 * dk, None)
        consume(0, vst_ref[0, g, c], 0)

    yield pair_body
    produce(0, key_rows(ks_ref, last), 2 * dk, None)
    consume(1, vst_ref[0, g, held], 0)
    yield
    consume(0, vst_ref[0, g, last], 0)
    flash_out(0, 1)

    out = out_ref[...]
    out = jnp.concatenate([out[:, h * tq:(h + 1) * tq] for h in range(hpg)], axis=0)
    o_ref[0, :, q_lanes] = out.T.astype(o_ref.dtype)


def _nsa(q, gate, q_norm, kc, vct, ks, vst, kw, vwt):
    b, s, dq = q.shape
    g = N_KV_GROUPS
    hpg = HEADS_PER_GROUP
    dk = HEAD_DIM
    gd = hpg * dk
    tq = CHUNK
    n_blk = s // SEL_BLOCK
    n_cmp = (s - CMP_BLOCK) // CMP_STRIDE + 1
    ncp = kc.shape[2]
    assert WINDOW == 2 * tq and n_blk <= dk and (s // tq) % 2 == 0
    rows_np = np.arange(ncp)
    cmp_idx = np.where(rows_np < ncp // 2, 2 * rows_np, 2 * rows_np - (ncp - 1))
    cmp_start = cmp_idx * CMP_STRIDE
    blk_start = np.arange(n_blk) * SEL_BLOCK
    ovt = ((cmp_start[None, :] < blk_start[:, None] + SEL_BLOCK)
           & (cmp_start[None, :] + CMP_BLOCK > blk_start[:, None])
           & (cmp_idx[None, :] < n_cmp)).astype(np.float32)
    qgain = jnp.broadcast_to((q_norm * (dk ** -0.5 * LOG2E))[:, None], (dk, hpg * tq))
    idx = lambda bi, i: (bi, 0, 0, 0)
    idx5 = lambda bi, i: (bi, 0, 0, 0, 0)
    return pl.pallas_call(
        functools.partial(_nsa_kernel, n_cmp=n_cmp, n_blk=n_blk),
        grid=(b, s // tq),
        in_specs=[
            pl.BlockSpec((1, tq, dq), lambda bi, i: (bi, i, 0)),
            pl.BlockSpec((1, tq, g * LANES), lambda bi, i: (bi, i, 0)),
            pl.BlockSpec((1, g, ncp, dk), idx),
            pl.BlockSpec((1, g, dk, ncp), idx),
            pl.BlockSpec((1, g, s, 2 * dk), idx),
            pl.BlockSpec((1, g, s // tq, V_ROWS, tq), idx5),
            pl.BlockSpec((1, g, s, dk), idx),
            pl.BlockSpec((1, g, s // tq, V_ROWS, tq), idx5),
            _resident((n_blk, ncp)),
            _resident((dk, hpg * tq)),
        ],
        out_specs=pl.BlockSpec((1, tq, dq), lambda bi, i: (bi, i, 0)),
        out_shape=jax.ShapeDtypeStruct((b, s, dq), BF16),
        scratch_shapes=[
            pltpu.VMEM((g, 2 * dk, hpg * tq), BF16),
            pltpu.VMEM((g, 2, 1, hpg * tq), F32),
            pltpu.VMEM((g, 2, V_ROWS, hpg * tq), F32),
            pltpu.VMEM((g, dk, hpg * tq), F32),
            pltpu.VMEM((g, 2, tq, hpg * tq), F32),
            pltpu.VMEM((g, 2, 1, hpg * tq), F32),
            pltpu.VMEM((3, tq, hpg * tq), F32),
            pltpu.VMEM((2, ncp, hpg * tq), F32),
        ],
        compiler_params=_cparams(("parallel", "arbitrary")),
        name="nsa",
    )(q, gate, kc, vct, ks, vst, kw, vwt, jnp.asarray(ovt, BF16), qgain)


def _gate_columns(w_qg):
    dq = N_HEADS * HEAD_DIM
    per = HEADS_PER_GROUP * N_BRANCH
    wg = w_qg[:, dq:].reshape(-1, N_KV_GROUPS, per)
    wg = jnp.pad(wg, ((0, 0), (0, 0), (0, LANES - per)))
    return jnp.concatenate([w_qg[:, :dq], wg.reshape(-1, N_KV_GROUPS * LANES)], axis=1)


def kernel(x, ffn_norm, ffn_w_gate_up, ffn_w_down, mix_norm, conv_w_in, conv_w, conv_w_out, kv_norm, kv_w,
           cmp_pos, cmp_w1, cmp_b1, cmp_w2, k_norm, nsa_w_qg, q_norm, nsa_w_o):
    b, s, d = x.shape
    t = b * s
    depth = ffn_norm.shape[0]
    n_a = conv_w_in.shape[0]
    g = N_KV_GROUPS

    ffn_g = ffn_norm.reshape(depth, 2, 1, d)
    ffn_wgu = ffn_w_gate_up.astype(BF16)
    ffn_wd = ffn_w_down.astype(BF16)

    def ffn(h2, layer, j, proj=None):
        return _ffn(h2, ffn_g, ffn_wgu, ffn_wd, (layer, j), proj)

    h = x.reshape(t, d)
    shared = None
    for layer in range(depth):
        h = ffn(h, layer, 0)
        proj = None
        if layer < n_a:
            h = _conv_mixer(h.reshape(b, s, d), mix_norm[layer], conv_w_in[layer].astype(BF16),
                            conv_w[layer], conv_w_out[layer].astype(BF16)).reshape(t, d)
        else:
            li = layer - n_a
            q, gate = _qg_proj(h, mix_norm[layer], _gate_columns(nsa_w_qg[li]).astype(BF16))
            o = _nsa(q.reshape(b, s, -1), gate.reshape(b, s, -1), q_norm[li], *shared)
            proj = (o.reshape(t, -1), nsa_w_o[li].astype(BF16))
        h = ffn(h, layer, 1, proj)
        if layer == n_a - 1:
            blk, ks, vst, kw, vwt = _kv_proj(h.reshape(b, s, d), kv_norm, kv_w.astype(BF16), k_norm)
            blk = blk.reshape(2, b * g, s // CMP_BLOCK, CMP_BLOCK * HEAD_DIM)
            cmp = _compress(blk, cmp_pos.reshape(2, -1), cmp_w1.astype(BF16), cmp_b1,
                            cmp_w2.astype(BF16), k_norm[0])
            cmp = cmp.reshape(2, b, g, -1, HEAD_DIM)
            shared = (cmp[0], cmp[1].transpose(0, 1, 3, 2), ks, vst, kw, vwt)
    return h.reshape(b, s, d)
```

```python
import functools

import numpy as np
import jax
import jax.numpy as jnp
from jax import lax
from jax.experimental import pallas as pl
from jax.experimental.pallas import tpu as pltpu

EPS = 1e-6
N_HEADS = 16
HEAD_DIM = 64
N_KV_GROUPS = 4
HEADS_PER_GROUP = N_HEADS // N_KV_GROUPS
N_KV_SLOTS = 6
N_BRANCH = 3
CMP_BLOCK = 32
CMP_STRIDE = 16
SEL_BLOCK = 64
N_SELECT = 16
WINDOW = 512
FORCE_SCORE = 1e9
MASK_SCORE = -1e30
LOG2E = 1.4426950408889634

LANES = 128
CHUNK = 256
V_ROWS = HEAD_DIM + 16
VMEM_LIMIT = 56 * 1024 * 1024

BF16 = jnp.bfloat16
F32 = jnp.float32


def _cparams(sem):
    return pltpu.CompilerParams(dimension_semantics=sem, vmem_limit_bytes=VMEM_LIMIT)


def _rms(x, g):
    ms = jnp.mean(x * x, axis=-1, keepdims=True)
    return x * lax.rsqrt(ms + EPS) * g


def _dot(a, b):
    return jnp.dot(a, b, preferred_element_type=F32)


def _split2(x):
    hi = x.astype(BF16)
    return hi, (x - hi.astype(F32)).astype(BF16)


def _seg_rms(y, seg, seg_t, gain):
    ss = None
    for part in _split2(y * y):
        term = _dot(part, seg)
        ss = term if ss is None else ss + term
    r = lax.rsqrt(ss * (1.0 / HEAD_DIM) + EPS)
    scale = None
    for part in _split2(r):
        term = _dot(part, seg_t)
        scale = term if scale is None else scale + term
    return y * scale * gain


def _seg_matrices(width):
    ind = (np.arange(width)[:, None] // HEAD_DIM == np.arange(LANES)[None, :]).astype(np.float32)
    return jnp.asarray(ind, BF16), jnp.asarray(ind.T, BF16)


def _dot_nt(a, b):
    return lax.dot_general(a, b, (((1,), (1,)), ((), ())), preferred_element_type=F32)


def _ffn_kernel(x_ref, g_ref, wgu_ref, wd_ref, *rest, dff, has_proj, n_tail_in, tail):
    x = x_ref[...]
    if has_proj:
        a_ref, w_ref = rest[:2]
        rest = rest[2:]
        x = x + _dot(a_ref[...], w_ref[...])
    tail_in, o_ref, tail_out = rest[:n_tail_in], rest[n_tail_in], rest[n_tail_in + 1:]
    xn = _rms(x, g_ref[...]).astype(BF16)
    ab = _dot(xn, wgu_ref[...])
    a = ab[:, 0:dff]
    h = (a * jax.nn.sigmoid(a) * ab[:, dff:2 * dff]).astype(BF16)
    out = x + 0.5 * _dot(h, wd_ref[...])
    o_ref[...] = out
    if tail is not None:
        tail(out, *tail_in, *tail_out)


def _resident(shape, lead=()):
    index = tuple(lead) + (0,) * len(shape)
    return pl.BlockSpec((None,) * len(lead) + tuple(shape), lambda *_: index, pipeline_mode=pl.Buffered(1))


FFN_TM = 512


def _ffn(h, g_all, w_gu_all, w_d_all, lead, proj=None, tail=None):
    tm = FFN_TM
    t, d = h.shape
    dff = w_d_all.shape[-2]
    in_specs = [
        pl.BlockSpec((tm, d), lambda i: (i, 0)),
        _resident((1, d), lead),
        _resident((d, 2 * dff), lead),
        _resident((dff, d), lead),
    ]
    args = [h, g_all, w_gu_all, w_d_all]
    if proj is not None:
        a, w = proj
        in_specs += [pl.BlockSpec((tm, a.shape[1]), lambda i: (i, 0)), _resident(w.shape)]
        args += [a, w]
    out_specs = [pl.BlockSpec((tm, d), lambda i: (i, 0))]
    out_shape = [jax.ShapeDtypeStruct((t, d), F32)]
    tail_fn, n_tail_in = None, 0
    if tail is not None:
        tail_fn, tail_args, tail_in_specs, tail_out_specs, tail_out_shapes = tail
        in_specs += tail_in_specs
        args += tail_args
        n_tail_in = len(tail_args)
        out_specs += tail_out_specs
        out_shape += tail_out_shapes
    outs = pl.pallas_call(
        functools.partial(_ffn_kernel, dff=dff, has_proj=proj is not None, n_tail_in=n_tail_in, tail=tail_fn),
        grid=(t // tm,),
        in_specs=in_specs,
        out_specs=out_specs,
        out_shape=out_shape,
        compiler_params=_cparams(("parallel",)),
        name="ffn",
    )(*args)
    return outs[0] if tail is None else outs


CARRY = 8


def _conv_kernel(x_ref, g_ref, win_ref, cw_ref, wout_ref, o_ref, vbuf_ref, *, tm, d):
    j = pl.program_id(1)

    @pl.when(j == 0)
    def _():
        vbuf_ref[0:CARRY, :] = jnp.zeros((CARRY, d), F32)

    x = x_ref[0]
    xn = _rms(x, g_ref[...]).astype(BF16)
    proj = _dot(xn, win_ref[...])
    bg = proj[:, 0:d]
    v = proj[:, d:2 * d] * proj[:, 2 * d:3 * d]
    vbuf_ref[CARRY:CARRY + tm, :] = v
    cw = cw_ref[...]
    conv = (cw[0:1, :] * vbuf_ref[CARRY - 2:CARRY - 2 + tm, :]
            + cw[1:2, :] * vbuf_ref[CARRY - 1:CARRY - 1 + tm, :]
            + cw[2:3, :] * v)
    y = (bg * conv).astype(BF16)
    o_ref[0] = x + _dot(y, wout_ref[...])
    vbuf_ref[0:CARRY, :] = vbuf_ref[tm:tm + CARRY, :]


def _conv_mixer(h, g, w_in, conv_w, w_out, *, tm=512):
    b, s, d = h.shape
    return pl.pallas_call(
        functools.partial(_conv_kernel, tm=tm, d=d),
        grid=(b, s // tm),
        in_specs=[
            pl.BlockSpec((1, tm, d), lambda i, j: (i, j, 0)),
            pl.BlockSpec((1, d), lambda i, j: (0, 0)),
            pl.BlockSpec((d, 3 * d), lambda i, j: (0, 0)),
            pl.BlockSpec((conv_w.shape[0], d), lambda i, j: (0, 0)),
            pl.BlockSpec((d, d), lambda i, j: (0, 0)),
        ],
        out_specs=pl.BlockSpec((1, tm, d), lambda i, j: (i, j, 0)),
        out_shape=jax.ShapeDtypeStruct((b, s, d), F32),
        scratch_shapes=[pltpu.VMEM((CARRY + tm, d), F32)],
        compiler_params=_cparams(("arbitrary", "arbitrary")),
        name="conv_mixer",
    )(h, g.reshape(1, d), w_in, conv_w, w_out)


def _kv_kernel(x_ref, g_ref, w_ref, kn_ref, seg_ref, segt_ref, blk_ref, ks_ref, vst_ref, kw_ref, vwt_ref,
               raw_ref, *, tm):
    j = pl.program_id(1)
    xn = _rms(x_ref[0], g_ref[...]).astype(BF16)
    y = _dot(xn, w_ref[...])
    gw = N_KV_GROUPS * HEAD_DIM
    per = LANES // HEAD_DIM
    for slab in range(raw_ref.shape[0]):
        raw_ref[slab] = y[:, slab * LANES:(slab + 1) * LANES]
    for l in range(CMP_BLOCK):
        for slab in range(raw_ref.shape[0]):
            rows = raw_ref[slab, pl.ds(l, tm // CMP_BLOCK, stride=CMP_BLOCK), :]
            for k in range(per):
                sg = slab * per + k
                blk_ref[sg // N_KV_GROUPS, 0, sg % N_KV_GROUPS, :, l * HEAD_DIM:(l + 1) * HEAD_DIM] = (
                    rows[:, k * HEAD_DIM:(k + 1) * HEAD_DIM])
    pos = j * tm + lax.broadcasted_iota(jnp.int32, (tm, HEAD_DIM), 0)
    lane = lax.broadcasted_iota(jnp.int32, (tm, HEAD_DIM), 1)
    onehot = jnp.where(lane == pos // SEL_BLOCK, 1.0, 0.0).astype(BF16)
    k_sel = _seg_rms(y[:, 2 * gw:3 * gw], seg_ref[...], segt_ref[...], kn_ref[0:1, :]).astype(BF16)
    k_win = _seg_rms(y[:, 4 * gw:5 * gw], seg_ref[...], segt_ref[...], kn_ref[1:2, :]).astype(BF16)
    for grp in range(N_KV_GROUPS):
        ks_ref[0, grp, :, 0:HEAD_DIM] = k_sel[:, grp * HEAD_DIM:(grp + 1) * HEAD_DIM]
        ks_ref[0, grp, :, HEAD_DIM:2 * HEAD_DIM] = onehot
        kw_ref[0, grp] = k_win[:, grp * HEAD_DIM:(grp + 1) * HEAD_DIM]
    extra = jnp.where(lax.broadcasted_iota(jnp.int32, (V_ROWS - HEAD_DIM, CHUNK), 0) == 0, 1.0, 0.0).astype(BF16)
    for slot, ref in ((3, vst_ref), (5, vwt_ref)):
        vt = y[:, slot * gw:(slot + 1) * gw].T
        for grp in range(N_KV_GROUPS):
            for cc in range(tm // CHUNK):
                ref[0, grp, cc, 0:HEAD_DIM, :] = vt[grp * HEAD_DIM:(grp + 1) * HEAD_DIM,
                                                    cc * CHUNK:(cc + 1) * CHUNK].astype(BF16)
                ref[0, grp, cc, HEAD_DIM:V_ROWS, :] = extra


def _kv_proj(h, g, w, k_norm, *, tm=512):
    b, s, d = h.shape
    n = w.shape[1]
    ng = N_KV_GROUPS
    gw = ng * HEAD_DIM
    cpt = tm // CHUNK
    assert s // SEL_BLOCK <= HEAD_DIM
    seg, seg_t = _seg_matrices(gw)
    gains = jnp.tile(k_norm[1:3], (1, ng))
    return pl.pallas_call(
        functools.partial(_kv_kernel, tm=tm),
        grid=(b, s // tm),
        in_specs=[
            pl.BlockSpec((1, tm, d), lambda i, j: (i, j, 0)),
            _resident((1, d)),
            _resident((d, n)),
            _resident(gains.shape),
            _resident(seg.shape),
            _resident(seg_t.shape),
        ],
        out_specs=[
            pl.BlockSpec((2, 1, ng, tm // CMP_BLOCK, CMP_BLOCK * HEAD_DIM), lambda i, j: (0, i, 0, j, 0)),
            pl.BlockSpec((1, ng, tm, 2 * HEAD_DIM), lambda i, j: (i, 0, j, 0)),
            pl.BlockSpec((1, ng, cpt, V_ROWS, CHUNK), lambda i, j: (i, 0, j, 0, 0)),
            pl.BlockSpec((1, ng, tm, HEAD_DIM), lambda i, j: (i, 0, j, 0)),
            pl.BlockSpec((1, ng, cpt, V_ROWS, CHUNK), lambda i, j: (i, 0, j, 0, 0)),
        ],
        out_shape=[
            jax.ShapeDtypeStruct((2, b, ng, s // CMP_BLOCK, CMP_BLOCK * HEAD_DIM), F32),
            jax.ShapeDtypeStruct((b, ng, s, 2 * HEAD_DIM), BF16),
            jax.ShapeDtypeStruct((b, ng, s // CHUNK, V_ROWS, CHUNK), BF16),
            jax.ShapeDtypeStruct((b, ng, s, HEAD_DIM), BF16),
            jax.ShapeDtypeStruct((b, ng, s // CHUNK, V_ROWS, CHUNK), BF16),
        ],
        scratch_shapes=[pltpu.VMEM((2 * gw // LANES, tm, LANES), F32)],
        compiler_params=_cparams(("parallel", "parallel")),
        name="kv_proj",
    )(h, g.reshape(1, d), w, gains, seg, seg_t)


def _cmp_kernel(x_ref, pos_ref, w1_ref, b1_ref, w2_ref, kn_ref, o_ref):
    c = pl.program_id(0)
    nb, ne, width = x_ref.shape
    rows = nb * ne
    half = width // 2
    x = x_ref[...].reshape(rows, width)
    pos = pos_ref[...]
    w1 = w1_ref[...]
    h_even = _dot((x + pos).astype(BF16), w1)
    a = _dot((x[:, half:] + pos[:, :half]).astype(BF16), w1[0:half])
    b = _dot((x[:, :half] + pos[:, half:]).astype(BF16), w1[half:width])
    h_odd = a + pltpu.roll(b, rows - 1, 0)

    def tail(hid):
        y = _dot(jax.nn.gelu(hid + b1_ref[...]).astype(BF16), w2_ref[...])
        y = jnp.where(c == 0, _rms(y, kn_ref[...]), y)
        return y.astype(o_ref.dtype).reshape(nb, ne, y.shape[-1])

    o_ref[:, 0:ne, :] = tail(h_even)
    o_ref[:, ne:2 * ne, :] = tail(h_odd)


def _compress(xe, pos, w1, b1, w2, kn, *, nb=8):
    _, nseq, ne, width = xe.shape
    nb = min(nb, nseq)
    hid = w1.shape[-1]
    dk = w2.shape[-1]
    return pl.pallas_call(
        _cmp_kernel,
        grid=(2, nseq // nb),
        in_specs=[
            pl.BlockSpec((None, nb, ne, width), lambda c, i: (c, i, 0, 0)),
            pl.BlockSpec((None, 1, width), lambda c, i: (c, 0, 0)),
            pl.BlockSpec((None, width, hid), lambda c, i: (c, 0, 0)),
            pl.BlockSpec((None, 1, hid), lambda c, i: (c, 0, 0)),
            pl.BlockSpec((None, hid, dk), lambda c, i: (c, 0, 0)),
            pl.BlockSpec((1, dk), lambda c, i: (0, 0)),
        ],
        out_specs=pl.BlockSpec((None, nb, 2 * ne, dk), lambda c, i: (c, i, 0, 0)),
        out_shape=jax.ShapeDtypeStruct((2, nseq, 2 * ne, dk), BF16),
        compiler_params=_cparams(("parallel", "parallel")),
        name="compress",
    )(xe, pos.reshape(2, 1, width), w1, b1.reshape(2, 1, hid), w2, kn.reshape(1, dk))


def _qg_tail_kernel(x, g_ref, w_ref, q_ref, gate_ref):
    dq = q_ref.shape[-1]
    xn = _rms(x, g_ref[...]).astype(BF16)
    y = _dot(xn, w_ref[...])
    q_ref[...] = y[:, 0:dq]
    gate_ref[...] = jax.nn.sigmoid(y[:, dq:])


def _qg_tail(t, g, w):
    tm = FFN_TM
    d, n = w.shape
    dq = N_HEADS * HEAD_DIM
    return (
        _qg_tail_kernel,
        [g.reshape(1, d), w],
        [_resident((1, d)), _resident((d, n))],
        [pl.BlockSpec((tm, dq), lambda i: (i, 0)), pl.BlockSpec((tm, n - dq), lambda i: (i, 0))],
        [jax.ShapeDtypeStruct((t, dq), F32), jax.ShapeDtypeStruct((t, n - dq), F32)],
    )


def _split3(x):
    hi = x.astype(BF16)
    r = x - hi.astype(F32)
    mid = r.astype(BF16)
    lo = (r - mid.astype(F32)).astype(BF16)
    return hi, mid, lo


def _cmp_block_of_row(row, nrows):
    return jnp.where(row < nrows // 2, 2 * row, 2 * row - (nrows - 1))


def _nsa_kernel(*refs, n_cmp, n_blk):
    mask_ref, cmask_ref = refs[-2:]
    tq = CHUNK
    _, ncp, wide = cmask_ref.shape
    i = pl.program_id(1)
    kk = lax.broadcasted_iota(jnp.int32, (tq, wide), 0)
    tt = lax.broadcasted_iota(jnp.int32, (tq, wide), 1) & (tq - 1)
    mask_ref[0] = jnp.where(kk <= tt, 0.0, MASK_SCORE)
    mask_ref[1] = jnp.where(kk > tt + jnp.where(i >= 2, 0, tq), 0.0, MASK_SCORE)
    nidx = _cmp_block_of_row(lax.broadcasted_iota(jnp.int32, (ncp, wide), 0), ncp)
    tpos = i * tq + (lax.broadcasted_iota(jnp.int32, (ncp, wide), 1) & (tq - 1))
    valid = (nidx * CMP_STRIDE + (CMP_BLOCK - 1) <= tpos) & (nidx < n_cmp)
    cmask_ref[0] = jnp.where(valid, 0.0, MASK_SCORE)
    cmask_ref[1] = jnp.where(valid, 1.0, 0.0)

    n_in = 11
    scratch = refs[n_in:-2]
    groups = [_nsa_group(g, *refs[:n_in], *[r.at[g] for r in scratch], mask_ref, cmask_ref, n_blk=n_blk)
              for g in range(N_KV_GROUPS)]
    done = object()
    while True:
        stage = [next(grp, done) for grp in groups]
        if stage[0] is done:
            break
        if stage[0] is not None:
            def pairs(p, carry):
                for body in stage:
                    body(p)
                return carry

            lax.fori_loop(0, i // 2, pairs, 0)


def _nsa_group(g, q_ref, gate_ref, kc_ref, vct_ref, ks_ref, vst_ref, kw_ref, vwt_ref, ovt_ref, qgain_ref,
               o_ref, qa_ref, m_ref, acc_ref, out_ref, s_ref, cmax_ref, mask_ref, cmask_ref, *, n_blk):
    hpg = HEADS_PER_GROUP
    tq = CHUNK
    dk = HEAD_DIM
    wide = hpg * tq
    i = pl.program_id(1)
    gd = hpg * dk
    q_lanes = pl.ds(g * gd, gd)
    t0 = i * tq

    def heads(fn):
        return jnp.concatenate([fn(h) for h in range(hpg)], axis=1)

    q_t = q_ref[0, :, q_lanes].T
    q_t = heads(lambda h: q_t[h * dk:(h + 1) * dk, :])
    ms = jnp.sum(q_t * q_t, axis=0, keepdims=True) * (1.0 / dk)
    qa_ref[0:dk, :] = (q_t * lax.rsqrt(ms + EPS) * qgain_ref[...]).astype(BF16)
    gate_t = gate_ref[0, :, pl.ds(g * LANES, LANES)].T
    gates = [heads(lambda h: gate_t[h * N_BRANCH + br:h * N_BRANCH + br + 1, :]) for br in range(N_BRANCH)]

    causal, win_lo = 0, 1
    sel_state, win_state, void_state = 0, 1, 2

    m_ref[...] = jnp.full(m_ref.shape, MASK_SCORE, F32)
    acc_ref[...] = jnp.zeros(acc_ref.shape, F32)

    def key_rows(ref, c):
        return ref[0, g, pl.ds(pl.multiple_of(c * tq, tq), tq), :]

    def produce(b, k, qrows, mask):
        s = _dot(k, qa_ref[0:qrows, :])
        if mask is not None:
            s = s + mask_ref[mask]
        s_ref[b] = s
        cmax_ref[b] = jnp.max(s, axis=0, keepdims=True)

    def consume(b, vt, st):
        m_prev = m_ref[st]
        m_new = jnp.maximum(m_prev, cmax_ref[b])
        alpha = jnp.exp2(m_prev - m_new)
        p = jnp.exp2(s_ref[b] - m_new)
        acc_ref[st] = alpha * acc_ref[st] + _dot(vt, p.astype(BF16))
        m_ref[st] = m_new

    def flash_out(st, branch):
        out_ref[...] += gates[branch] * (acc_ref[st, 0:dk, :] * (1.0 / acc_ref[st, dk:dk + 1, :]))

    assert WINDOW == 2 * tq
    lo = jnp.maximum(i - 2, 0)
    mid = jnp.maximum(i - 1, 0)
    mid_state = jnp.where(i >= 1, win_state, void_state)
    yield
    produce(0, key_rows(kw_ref, lo), dk, win_lo)
    yield

    s = _dot(kc_ref[0, g], qa_ref[0:dk, :]) + cmask_ref[0]
    e = jnp.exp2(s - jnp.max(s, axis=0, keepdims=True))
    p = e * (1.0 / jnp.sum(e, axis=0, keepdims=True)) * cmask_ref[1]
    out_ref[...] = gates[0] * _dot(vct_ref[0, g], p.astype(BF16))
    psum = p[:, 0:tq]
    for h in range(1, hpg):
        psum = psum + p[:, h * tq:(h + 1) * tq]
    yield

    produce(1, key_rows(kw_ref, mid), dk, None)
    consume(0, vwt_ref[0, g, lo], win_state)
    yield

    ovt = ovt_ref[...]
    imp = None
    for part in _split3(psum):
        term = _dot(ovt, part)
        imp = term if imp is None else imp + term
    jblk = lax.broadcasted_iota(jnp.int32, (n_blk, tq), 0)
    cur = (t0 + lax.broadcasted_iota(jnp.int32, (n_blk, tq), 1)) // SEL_BLOCK
    forced = (jblk == 0) | (jblk == cur) | (jblk == cur - 1)
    imp = jnp.where(forced, FORCE_SCORE, imp)
    imp = jnp.where(jblk > cur, MASK_SCORE, imp)
    sub = 8
    parts = [imp[r:r + sub] for r in range(0, n_blk, sub)]
    ranks = [jnp.zeros((sub, tq), F32) for _ in parts]
    jsub = lax.broadcasted_iota(jnp.int32, (sub, tq), 0)
    for ii in range(n_blk):
        row = imp[ii:ii + 1, :]
        for gi, x in enumerate(parts):
            if ii < gi * sub:
                beats = row >= x
            elif ii >= (gi + 1) * sub:
                beats = row > x
            else:
                beats = (row > x) | ((row == x) & (jsub > ii - gi * sub))
            ranks[gi] = ranks[gi] + jnp.where(beats, 1.0, 0.0)
    rank = jnp.concatenate(ranks, axis=0)
    chosen = (rank < float(min(N_SELECT, n_blk))) & (jblk <= cur)
    bias = jnp.where(chosen, 0.0, MASK_SCORE)
    bias = jnp.concatenate([bias, jnp.zeros((dk - n_blk, tq), F32)], axis=0).astype(BF16)
    qa_ref[dk:2 * dk, :] = heads(lambda h: bias)
    yield

    produce(0, key_rows(kw_ref, i), dk, causal)
    consume(1, vwt_ref[0, g, mid], mid_state)
    yield

    npair = i // 2
    held = jnp.where(i >= 2, 2 * npair - 1, i)
    last = jnp.where(i % 2 == 0, i + 1, i - 1)
    produce(1, key_rows(ks_ref, i), 2 * dk, causal)
    consume(0, vwt_ref[0, g, i], 1)
    flash_out(1, 2)

    def pair_body(p):
        c = 2 * p
        produce(0, key_rows(ks_ref, c), 2 * dk, None)
        consume(1, vst_ref[0, g, jnp.where(p == 0, i, c - 1)], 0)
        produce(1, key_rows(ks_ref, c + 1), 2 * dk, None)
        consume(0, vst_ref[0, g, c], 0)

    yield pair_body
    produce(0, key_rows(ks_ref, last), 2 * dk, None)
    consume(1, vst_ref[0, g, held], 0)
    yield
    consume(0, vst_ref[0, g, last], 0)
    flash_out(0, 1)

    out = out_ref[...]
    out = jnp.concatenate([out[:, h * tq:(h + 1) * tq] for h in range(hpg)], axis=0)
    o_ref[0, :, q_lanes] = out.T.astype(o_ref.dtype)


def _nsa(q, gate, q_norm, kc, vct, ks, vst, kw, vwt):
    b, s, dq = q.shape
    g = N_KV_GROUPS
    hpg = HEADS_PER_GROUP
    dk = HEAD_DIM
    gd = hpg * dk
    tq = CHUNK
    n_blk = s // SEL_BLOCK
    n_cmp = (s - CMP_BLOCK) // CMP_STRIDE + 1
    ncp = kc.shape[2]
    assert WINDOW == 2 * tq and n_blk <= dk and (s // tq) % 2 == 0
    rows_np = np.arange(ncp)
    cmp_idx = np.where(rows_np < ncp // 2, 2 * rows_np, 2 * rows_np - (ncp - 1))
    cmp_start = cmp_idx * CMP_STRIDE
    blk_start = np.arange(n_blk) * SEL_BLOCK
    ovt = ((cmp_start[None, :] < blk_start[:, None] + SEL_BLOCK)
           & (cmp_start[None, :] + CMP_BLOCK > blk_start[:, None])
           & (cmp_idx[None, :] < n_cmp)).astype(np.float32)
    qgain = jnp.broadcast_to((q_norm * (dk ** -0.5 * LOG2E))[:, None], (dk, hpg * tq))
    idx = lambda bi, i: (bi, 0, 0, 0)
    idx5 = lambda bi, i: (bi, 0, 0, 0, 0)
    return pl.pallas_call(
        functools.partial(_nsa_kernel, n_cmp=n_cmp, n_blk=n_blk),
        grid=(b, s // tq),
        in_specs=[
            pl.BlockSpec((1, tq, dq), lambda bi, i: (bi, i, 0)),
            pl.BlockSpec((1, tq, g * LANES), lambda bi, i: (bi, i, 0)),
            pl.BlockSpec((1, g, ncp, dk), idx),
            pl.BlockSpec((1, g, dk, ncp), idx),
            pl.BlockSpec((1, g, s, 2 * dk), idx),
            pl.BlockSpec((1, g, s // tq, V_ROWS, tq), idx5),
            pl.BlockSpec((1, g, s, dk), idx),
            pl.BlockSpec((1, g, s // tq, V_ROWS, tq), idx5),
            _resident((n_blk, ncp)),
            _resident((dk, hpg * tq)),
        ],
        out_specs=pl.BlockSpec((1, tq, dq), lambda bi, i: (bi, i, 0)),
        out_shape=jax.ShapeDtypeStruct((b, s, dq), BF16),
        scratch_shapes=[
            pltpu.VMEM((g, 2 * dk, hpg * tq), BF16),
            pltpu.VMEM((g, 3, 1, hpg * tq), F32),
            pltpu.VMEM((g, 3, V_ROWS, hpg * tq), F32),
            pltpu.VMEM((g, dk, hpg * tq), F32),
            pltpu.VMEM((g, 2, tq, hpg * tq), F32),
            pltpu.VMEM((g, 2, 1, hpg * tq), F32),
            pltpu.VMEM((2, tq, hpg * tq), F32),
            pltpu.VMEM((2, ncp, hpg * tq), F32),
        ],
        compiler_params=_cparams(("parallel", "arbitrary")),
        name="nsa",
    )(q, gate, kc, vct, ks, vst, kw, vwt, jnp.asarray(ovt, BF16), qgain)


def _gate_columns(w_qg):
    dq = N_HEADS * HEAD_DIM
    per = HEADS_PER_GROUP * N_BRANCH
    wg = w_qg[:, dq:].reshape(-1, N_KV_GROUPS, per)
    wg = jnp.pad(wg, ((0, 0), (0, 0), (0, LANES - per)))
    return jnp.concatenate([w_qg[:, :dq], wg.reshape(-1, N_KV_GROUPS * LANES)], axis=1)


def kernel(x, ffn_norm, ffn_w_gate_up, ffn_w_down, mix_norm, conv_w_in, conv_w, conv_w_out, kv_norm, kv_w,
           cmp_pos, cmp_w1, cmp_b1, cmp_w2, k_norm, nsa_w_qg, q_norm, nsa_w_o):
    b, s, d = x.shape
    t = b * s
    depth = ffn_norm.shape[0]
    n_a = conv_w_in.shape[0]
    g = N_KV_GROUPS

    ffn_g = ffn_norm.reshape(depth, 2, 1, d)
    ffn_wgu = ffn_w_gate_up.astype(BF16)
    ffn_wd = ffn_w_down.astype(BF16)

    def ffn(h2, layer, j, proj=None, tail=None):
        return _ffn(h2, ffn_g, ffn_wgu, ffn_wd, (layer, j), proj, tail)

    h = x.reshape(t, d)
    shared = None
    for layer in range(depth):
        proj = None
        if layer < n_a:
            h = ffn(h, layer, 0)
            h = _conv_mixer(h.reshape(b, s, d), mix_norm[layer], conv_w_in[layer].astype(BF16),
                            conv_w[layer], conv_w_out[layer].astype(BF16)).reshape(t, d)
        else:
            li = layer - n_a
            h, q, gate = ffn(h, layer, 0, tail=_qg_tail(t, mix_norm[layer],
                                                        _gate_columns(nsa_w_qg[li]).astype(BF16)))
            o = _nsa(q.reshape(b, s, -1), gate.reshape(b, s, -1), q_norm[li], *shared)
            proj = (o.reshape(t, -1), nsa_w_o[li].astype(BF16))
        h = ffn(h, layer, 1, proj)
        if layer == n_a - 1:
            blk, ks, vst, kw, vwt = _kv_proj(h.reshape(b, s, d), kv_norm, kv_w.astype(BF16), k_norm)
            blk = blk.reshape(2, b * g, s // CMP_BLOCK, CMP_BLOCK * HEAD_DIM)
            cmp = _compress(blk, cmp_pos.reshape(2, -1), cmp_w1.astype(BF16), cmp_b1,
                            cmp_w2.astype(BF16), k_norm[0])
            cmp = cmp.reshape(2, b, g, -1, HEAD_DIM)
            shared = (cmp[0], cmp[1].transpose(0, 1, 3, 2), ks, vst, kw, vwt)
    return h.reshape(b, s, d)
```

```python
import functools

import numpy as np
import jax
import jax.numpy as jnp
from jax import lax
from jax.experimental import pallas as pl
from jax.experimental.pallas import tpu as pltpu

EPS = 1e-6
N_HEADS = 16
HEAD_DIM = 64
N_KV_GROUPS = 4
HEADS_PER_GROUP = N_HEADS // N_KV_GROUPS
N_KV_SLOTS = 6
N_BRANCH = 3
CMP_BLOCK = 32
CMP_STRIDE = 16
SEL_BLOCK = 64
N_SELECT = 16
WINDOW = 512
FORCE_SCORE = 1e9
MASK_SCORE = -1e30
LOG2E = 1.4426950408889634

LANES = 128
CHUNK = 256
V_ROWS = HEAD_DIM + 16
VMEM_LIMIT = 56 * 1024 * 1024

BF16 = jnp.bfloat16
F32 = jnp.float32


def _cparams(sem):
    return pltpu.CompilerParams(dimension_semantics=sem, vmem_limit_bytes=VMEM_LIMIT)


def _rms(x, g):
    ms = jnp.mean(x * x, axis=-1, keepdims=True)
    return x * lax.rsqrt(ms + EPS) * g


def _dot(a, b):
    return jnp.dot(a, b, preferred_element_type=F32)


def _split2(x):
    hi = x.astype(BF16)
    return hi, (x - hi.astype(F32)).astype(BF16)


def _seg_rms(y, seg, seg_t, gain):
    ss = None
    for part in _split2(y * y):
        term = _dot(part, seg)
        ss = term if ss is None else ss + term
    r = lax.rsqrt(ss * (1.0 / HEAD_DIM) + EPS)
    scale = None
    for part in _split2(r):
        term = _dot(part, seg_t)
        scale = term if scale is None else scale + term
    return y * scale * gain


def _seg_matrices(width):
    ind = (np.arange(width)[:, None] // HEAD_DIM == np.arange(LANES)[None, :]).astype(np.float32)
    return jnp.asarray(ind, BF16), jnp.asarray(ind.T, BF16)


def _dot_nt(a, b):
    return lax.dot_general(a, b, (((1,), (1,)), ((), ())), preferred_element_type=F32)


def _ffn_kernel(x_ref, g_ref, wgu_ref, wd_ref, *rest, dff, has_proj, n_tail_in, n_tail_out, tail):
    x = x_ref[...]
    if has_proj:
        a_ref, w_ref = rest[:2]
        rest = rest[2:]
        x = x + _dot(a_ref[...], w_ref[...])
    tail_in, o_ref, tail_out = rest[:n_tail_in], rest[n_tail_in], rest[n_tail_in + 1:]
    assert len(tail_out) >= n_tail_out
    xn = _rms(x, g_ref[...]).astype(BF16)
    ab = _dot(xn, wgu_ref[...])
    a = ab[:, 0:dff]
    h = (a * jax.nn.sigmoid(a) * ab[:, dff:2 * dff]).astype(BF16)
    out = x + 0.5 * _dot(h, wd_ref[...])
    o_ref[...] = out
    if tail is not None:
        tail(out, *tail_in, *tail_out)


def _resident(shape, lead=()):
    index = tuple(lead) + (0,) * len(shape)
    return pl.BlockSpec((None,) * len(lead) + tuple(shape), lambda *_: index, pipeline_mode=pl.Buffered(1))


FFN_TM = 512


def _ffn(h, g_all, w_gu_all, w_d_all, lead, proj=None, tail=None):
    tm = FFN_TM
    t, d = h.shape
    dff = w_d_all.shape[-2]
    in_specs = [
        pl.BlockSpec((tm, d), lambda i: (i, 0)),
        _resident((1, d), lead),
        _resident((d, 2 * dff), lead),
        _resident((dff, d), lead),
    ]
    args = [h, g_all, w_gu_all, w_d_all]
    if proj is not None:
        a, w = proj
        in_specs += [pl.BlockSpec((tm, a.shape[1]), lambda i: (i, 0)), _resident(w.shape)]
        args += [a, w]
    out_specs = [pl.BlockSpec((tm, d), lambda i: (i, 0))]
    out_shape = [jax.ShapeDtypeStruct((t, d), F32)]
    tail_fn, n_tail_in, n_tail_out, scratch = None, 0, 0, []
    if tail is not None:
        tail_fn, tail_args, tail_in_specs, tail_out_specs, tail_out_shapes, scratch = tail
        in_specs += tail_in_specs
        args += tail_args
        n_tail_in, n_tail_out = len(tail_args), len(tail_out_specs)
        out_specs += tail_out_specs
        out_shape += tail_out_shapes
    outs = pl.pallas_call(
        functools.partial(_ffn_kernel, dff=dff, has_proj=proj is not None, n_tail_in=n_tail_in,
                          n_tail_out=n_tail_out, tail=tail_fn),
        grid=(t // tm,),
        in_specs=in_specs,
        out_specs=out_specs,
        out_shape=out_shape,
        scratch_shapes=scratch,
        compiler_params=_cparams(("parallel",)),
        name="ffn",
    )(*args)
    return outs[0] if tail is None else outs


CARRY = 8


def _conv_kernel(x_ref, g_ref, win_ref, cw_ref, wout_ref, o_ref, vbuf_ref, *, tm, d):
    j = pl.program_id(1)

    @pl.when(j == 0)
    def _():
        vbuf_ref[0:CARRY, :] = jnp.zeros((CARRY, d), F32)

    x = x_ref[0]
    xn = _rms(x, g_ref[...]).astype(BF16)
    proj = _dot(xn, win_ref[...])
    bg = proj[:, 0:d]
    v = proj[:, d:2 * d] * proj[:, 2 * d:3 * d]
    vbuf_ref[CARRY:CARRY + tm, :] = v
    cw = cw_ref[...]
    conv = (cw[0:1, :] * vbuf_ref[CARRY - 2:CARRY - 2 + tm, :]
            + cw[1:2, :] * vbuf_ref[CARRY - 1:CARRY - 1 + tm, :]
            + cw[2:3, :] * v)
    y = (bg * conv).astype(BF16)
    o_ref[0] = x + _dot(y, wout_ref[...])
    vbuf_ref[0:CARRY, :] = vbuf_ref[tm:tm + CARRY, :]


def _conv_mixer(h, g, w_in, conv_w, w_out, *, tm=512):
    b, s, d = h.shape
    return pl.pallas_call(
        functools.partial(_conv_kernel, tm=tm, d=d),
        grid=(b, s // tm),
        in_specs=[
            pl.BlockSpec((1, tm, d), lambda i, j: (i, j, 0)),
            pl.BlockSpec((1, d), lambda i, j: (0, 0)),
            pl.BlockSpec((d, 3 * d), lambda i, j: (0, 0)),
            pl.BlockSpec((conv_w.shape[0], d), lambda i, j: (0, 0)),
            pl.BlockSpec((d, d), lambda i, j: (0, 0)),
        ],
        out_specs=pl.BlockSpec((1, tm, d), lambda i, j: (i, j, 0)),
        out_shape=jax.ShapeDtypeStruct((b, s, d), F32),
        scratch_shapes=[pltpu.VMEM((CARRY + tm, d), F32)],
        compiler_params=_cparams(("arbitrary", "arbitrary")),
        name="conv_mixer",
    )(h, g.reshape(1, d), w_in, conv_w, w_out)


def _kv_tail_kernel(x, g_ref, w_ref, kn_ref, seg_ref, segt_ref, blk_ref, ks_ref, vst_ref, kw_ref, vwt_ref,
                    raw_ref, *, nj):
    tm = x.shape[0]
    j = pl.program_id(0) % nj
    xn = _rms(x, g_ref[...]).astype(BF16)
    y = _dot(xn, w_ref[...])
    gw = N_KV_GROUPS * HEAD_DIM
    per = LANES // HEAD_DIM
    for slab in range(raw_ref.shape[0]):
        raw_ref[slab] = y[:, slab * LANES:(slab + 1) * LANES]
    for l in range(CMP_BLOCK):
        for slab in range(raw_ref.shape[0]):
            rows = raw_ref[slab, pl.ds(l, tm // CMP_BLOCK, stride=CMP_BLOCK), :]
            for k in range(per):
                sg = slab * per + k
                blk_ref[sg // N_KV_GROUPS, 0, sg % N_KV_GROUPS, :, l * HEAD_DIM:(l + 1) * HEAD_DIM] = (
                    rows[:, k * HEAD_DIM:(k + 1) * HEAD_DIM])
    pos = j * tm + lax.broadcasted_iota(jnp.int32, (tm, HEAD_DIM), 0)
    lane = lax.broadcasted_iota(jnp.int32, (tm, HEAD_DIM), 1)
    onehot = jnp.where(lane == pos // SEL_BLOCK, 1.0, 0.0).astype(BF16)
    k_sel = _seg_rms(y[:, 2 * gw:3 * gw], seg_ref[...], segt_ref[...], kn_ref[0:1, :]).astype(BF16)
    k_win = _seg_rms(y[:, 4 * gw:5 * gw], seg_ref[...], segt_ref[...], kn_ref[1:2, :]).astype(BF16)
    for grp in range(N_KV_GROUPS):
        ks_ref[0, grp, :, 0:HEAD_DIM] = k_sel[:, grp * HEAD_DIM:(grp + 1) * HEAD_DIM]
        ks_ref[0, grp, :, HEAD_DIM:2 * HEAD_DIM] = onehot
        kw_ref[0, grp] = k_win[:, grp * HEAD_DIM:(grp + 1) * HEAD_DIM]
    extra = jnp.where(lax.broadcasted_iota(jnp.int32, (V_ROWS - HEAD_DIM, CHUNK), 0) == 0, 1.0, 0.0).astype(BF16)
    for slot, ref in ((3, vst_ref), (5, vwt_ref)):
        vt = y[:, slot * gw:(slot + 1) * gw].T
        for grp in range(N_KV_GROUPS):
            for cc in range(tm // CHUNK):
                ref[0, grp, cc, 0:HEAD_DIM, :] = vt[grp * HEAD_DIM:(grp + 1) * HEAD_DIM,
                                                    cc * CHUNK:(cc + 1) * CHUNK].astype(BF16)
                ref[0, grp, cc, HEAD_DIM:V_ROWS, :] = extra


def _kv_tail(b, s, g, w, k_norm):
    tm = FFN_TM
    d, n = w.shape
    ng = N_KV_GROUPS
    gw = ng * HEAD_DIM
    cpt = tm // CHUNK
    nj = s // tm
    assert s // SEL_BLOCK <= HEAD_DIM and s % tm == 0
    seg, seg_t = _seg_matrices(gw)
    gains = jnp.tile(k_norm[1:3], (1, ng))
    return (
        functools.partial(_kv_tail_kernel, nj=nj),
        [g.reshape(1, d), w, gains, seg, seg_t],
        [_resident((1, d)), _resident((d, n)), _resident(gains.shape), _resident(seg.shape),
         _resident(seg_t.shape)],
        [
            pl.BlockSpec((2, 1, ng, tm // CMP_BLOCK, CMP_BLOCK * HEAD_DIM),
                         lambda i: (0, i // nj, 0, i % nj, 0)),
            pl.BlockSpec((1, ng, tm, 2 * HEAD_DIM), lambda i: (i // nj, 0, i % nj, 0)),
            pl.BlockSpec((1, ng, cpt, V_ROWS, CHUNK), lambda i: (i // nj, 0, i % nj, 0, 0)),
            pl.BlockSpec((1, ng, tm, HEAD_DIM), lambda i: (i // nj, 0, i % nj, 0)),
            pl.BlockSpec((1, ng, cpt, V_ROWS, CHUNK), lambda i: (i // nj, 0, i % nj, 0, 0)),
        ],
        [
            jax.ShapeDtypeStruct((2, b, ng, s // CMP_BLOCK, CMP_BLOCK * HEAD_DIM), F32),
            jax.ShapeDtypeStruct((b, ng, s, 2 * HEAD_DIM), BF16),
            jax.ShapeDtypeStruct((b, ng, s // CHUNK, V_ROWS, CHUNK), BF16),
            jax.ShapeDtypeStruct((b, ng, s, HEAD_DIM), BF16),
            jax.ShapeDtypeStruct((b, ng, s // CHUNK, V_ROWS, CHUNK), BF16),
        ],
        [pltpu.VMEM((2 * gw // LANES, tm, LANES), F32)],
    )


def _cmp_kernel(x_ref, pos_ref, w1_ref, b1_ref, w2_ref, kn_ref, o_ref):
    c = pl.program_id(0)
    nb, ne, width = x_ref.shape
    rows = nb * ne
    half = width // 2
    x = x_ref[...].reshape(rows, width)
    pos = pos_ref[...]
    w1 = w1_ref[...]
    h_even = _dot((x + pos).astype(BF16), w1)
    a = _dot((x[:, half:] + pos[:, :half]).astype(BF16), w1[0:half])
    b = _dot((x[:, :half] + pos[:, half:]).astype(BF16), w1[half:width])
    h_odd = a + pltpu.roll(b, rows - 1, 0)

    def tail(hid):
        y = _dot(jax.nn.gelu(hid + b1_ref[...]).astype(BF16), w2_ref[...])
        y = jnp.where(c == 0, _rms(y, kn_ref[...]), y)
        return y.astype(o_ref.dtype).reshape(nb, ne, y.shape[-1])

    o_ref[:, 0:ne, :] = tail(h_even)
    o_ref[:, ne:2 * ne, :] = tail(h_odd)


def _compress(xe, pos, w1, b1, w2, kn, *, nb=8):
    _, nseq, ne, width = xe.shape
    nb = min(nb, nseq)
    hid = w1.shape[-1]
    dk = w2.shape[-1]
    return pl.pallas_call(
        _cmp_kernel,
        grid=(2, nseq // nb),
        in_specs=[
            pl.BlockSpec((None, nb, ne, width), lambda c, i: (c, i, 0, 0)),
            pl.BlockSpec((None, 1, width), lambda c, i: (c, 0, 0)),
            pl.BlockSpec((None, width, hid), lambda c, i: (c, 0, 0)),
            pl.BlockSpec((None, 1, hid), lambda c, i: (c, 0, 0)),
            pl.BlockSpec((None, hid, dk), lambda c, i: (c, 0, 0)),
            pl.BlockSpec((1, dk), lambda c, i: (0, 0)),
        ],
        out_specs=pl.BlockSpec((None, nb, 2 * ne, dk), lambda c, i: (c, i, 0, 0)),
        out_shape=jax.ShapeDtypeStruct((2, nseq, 2 * ne, dk), BF16),
        compiler_params=_cparams(("parallel", "parallel")),
        name="compress",
    )(xe, pos.reshape(2, 1, width), w1, b1.reshape(2, 1, hid), w2, kn.reshape(1, dk))


def _qg_tail_kernel(x, g_ref, w_ref, q_ref, gate_ref):
    dq = q_ref.shape[-1]
    xn = _rms(x, g_ref[...]).astype(BF16)
    y = _dot(xn, w_ref[...])
    q_ref[...] = y[:, 0:dq]
    gate_ref[...] = jax.nn.sigmoid(y[:, dq:])


def _qg_tail(t, g, w):
    tm = FFN_TM
    d, n = w.shape
    dq = N_HEADS * HEAD_DIM
    return (
        _qg_tail_kernel,
        [g.reshape(1, d), w],
        [_resident((1, d)), _resident((d, n))],
        [pl.BlockSpec((tm, dq), lambda i: (i, 0)), pl.BlockSpec((tm, n - dq), lambda i: (i, 0))],
        [jax.ShapeDtypeStruct((t, dq), F32), jax.ShapeDtypeStruct((t, n - dq), F32)],
        [],
    )


def _split3(x):
    hi = x.astype(BF16)
    r = x - hi.astype(F32)
    mid = r.astype(BF16)
    lo = (r - mid.astype(F32)).astype(BF16)
    return hi, mid, lo


def _cmp_block_of_row(row, nrows):
    return jnp.where(row < nrows // 2, 2 * row, 2 * row - (nrows - 1))


def _nsa_kernel(*refs, n_cmp, n_blk):
    mask_ref, cmask_ref = refs[-2:]
    tq = CHUNK
    _, ncp, wide = cmask_ref.shape
    i = pl.program_id(1)
    kk = lax.broadcasted_iota(jnp.int32, (tq, wide), 0)
    tt = lax.broadcasted_iota(jnp.int32, (tq, wide), 1) & (tq - 1)
    mask_ref[0] = jnp.where(kk <= tt, 0.0, MASK_SCORE)
    mask_ref[1] = jnp.where(kk > tt + jnp.where(i >= 2, 0, tq), 0.0, MASK_SCORE)
    nidx = _cmp_block_of_row(lax.broadcasted_iota(jnp.int32, (ncp, wide), 0), ncp)
    tpos = i * tq + (lax.broadcasted_iota(jnp.int32, (ncp, wide), 1) & (tq - 1))
    valid = (nidx * CMP_STRIDE + (CMP_BLOCK - 1) <= tpos) & (nidx < n_cmp)
    cmask_ref[0] = jnp.where(valid, 0.0, MASK_SCORE)
    cmask_ref[1] = jnp.where(valid, 1.0, 0.0)

    n_in = 11
    scratch = refs[n_in:-2]
    groups = [_nsa_group(g, *refs[:n_in], *[r.at[g] for r in scratch], mask_ref, cmask_ref, n_blk=n_blk)
              for g in range(N_KV_GROUPS)]
    done = object()

    def advance():
        handed = [None] * len(groups)
        while any(r is None for r in handed):
            for gi, grp in enumerate(groups):
                if handed[gi] is None:
                    handed[gi] = next(grp, done)
        return handed

    bodies = advance()

    def pairs(p, carry):
        for body in bodies:
            body(p)
        return carry

    lax.fori_loop(0, i // 2, pairs, 0)
    assert all(r is done for r in advance())


def _nsa_group(g, q_ref, gate_ref, kc_ref, vct_ref, ks_ref, vst_ref, kw_ref, vwt_ref, ovt_ref, qgain_ref,
               o_ref, qa_ref, m_ref, acc_ref, out_ref, s_ref, cmax_ref, mask_ref, cmask_ref, *, n_blk):
    hpg = HEADS_PER_GROUP
    tq = CHUNK
    dk = HEAD_DIM
    wide = hpg * tq
    i = pl.program_id(1)
    gd = hpg * dk
    q_lanes = pl.ds(g * gd, gd)
    t0 = i * tq

    def heads(fn):
        return jnp.concatenate([fn(h) for h in range(hpg)], axis=1)

    q_t = q_ref[0, :, q_lanes].T
    q_t = heads(lambda h: q_t[h * dk:(h + 1) * dk, :])
    ms = jnp.sum(q_t * q_t, axis=0, keepdims=True) * (1.0 / dk)
    qa_ref[0:dk, :] = (q_t * lax.rsqrt(ms + EPS) * qgain_ref[...]).astype(BF16)
    gate_t = gate_ref[0, :, pl.ds(g * LANES, LANES)].T
    gates = [heads(lambda h: gate_t[h * N_BRANCH + br:h * N_BRANCH + br + 1, :]) for br in range(N_BRANCH)]

    causal, win_lo = 0, 1
    sel_state, win_state, void_state = 0, 1, 2

    m_ref[...] = jnp.full(m_ref.shape, MASK_SCORE, F32)
    acc_ref[...] = jnp.zeros(acc_ref.shape, F32)

    def key_rows(ref, c):
        return ref[0, g, pl.ds(pl.multiple_of(c * tq, tq), tq), :]

    def produce(b, k, qrows, mask):
        s = _dot(k, qa_ref[0:qrows, :])
        if mask is not None:
            s = s + mask_ref[mask]
        s_ref[b] = s
        cmax_ref[b] = jnp.max(s, axis=0, keepdims=True)

    def consume(b, vt, st):
        m_prev = m_ref[st]
        m_new = jnp.maximum(m_prev, cmax_ref[b])
        alpha = jnp.exp2(m_prev - m_new)
        p = jnp.exp2(s_ref[b] - m_new)
        acc_ref[st] = alpha * acc_ref[st] + _dot(vt, p.astype(BF16))
        m_ref[st] = m_new

    def flash_out(st, branch):
        out_ref[...] += gates[branch] * (acc_ref[st, 0:dk, :] * (1.0 / acc_ref[st, dk:dk + 1, :]))

    assert WINDOW == 2 * tq
    lo = jnp.maximum(i - 2, 0)
    mid = jnp.maximum(i - 1, 0)
    mid_state = jnp.where(i >= 1, win_state, void_state)
    yield
    produce(0, key_rows(kw_ref, lo), dk, win_lo)
    yield

    s = _dot(kc_ref[0, g], qa_ref[0:dk, :]) + cmask_ref[0]
    e = jnp.exp2(s - jnp.max(s, axis=0, keepdims=True))
    p = e * (1.0 / jnp.sum(e, axis=0, keepdims=True)) * cmask_ref[1]
    out_ref[...] = gates[0] * _dot(vct_ref[0, g], p.astype(BF16))
    psum = p[:, 0:tq]
    for h in range(1, hpg):
        psum = psum + p[:, h * tq:(h + 1) * tq]
    yield

    produce(1, key_rows(kw_ref, mid), dk, None)
    consume(0, vwt_ref[0, g, lo], win_state)
    yield

    ovt = ovt_ref[...]
    imp = None
    for part in _split3(psum):
        term = _dot(ovt, part)
        imp = term if imp is None else imp + term
    jblk = lax.broadcasted_iota(jnp.int32, (n_blk, tq), 0)
    cur = (t0 + lax.broadcasted_iota(jnp.int32, (n_blk, tq), 1)) // SEL_BLOCK
    forced = (jblk == 0) | (jblk == cur) | (jblk == cur - 1)
    imp = jnp.where(forced, FORCE_SCORE, imp)
    imp = jnp.where(jblk > cur, MASK_SCORE, imp)
    sub = 8
    parts = [imp[r:r + sub] for r in range(0, n_blk, sub)]
    ranks = [jnp.zeros((sub, tq), F32) for _ in parts]
    jsub = lax.broadcasted_iota(jnp.int32, (sub, tq), 0)
    for ii in range(n_blk):
        row = imp[ii:ii + 1, :]
        for gi, x in enumerate(parts):
            if ii < gi * sub:
                beats = row >= x
            elif ii >= (gi + 1) * sub:
                beats = row > x
            else:
                beats = (row > x) | ((row == x) & (jsub > ii - gi * sub))
            ranks[gi] = ranks[gi] + jnp.where(beats, 1.0, 0.0)
    rank = jnp.concatenate(ranks, axis=0)
    chosen = (rank < float(min(N_SELECT, n_blk))) & (jblk <= cur)
    bias = jnp.where(chosen, 0.0, MASK_SCORE)
    bias = jnp.concatenate([bias, jnp.zeros((dk - n_blk, tq), F32)], axis=0).astype(BF16)
    qa_ref[dk:2 * dk, :] = heads(lambda h: bias)
    yield

    produce(0, key_rows(kw_ref, i), dk, causal)
    consume(1, vwt_ref[0, g, mid], mid_state)
    yield

    npair = i // 2
    held = jnp.where(i >= 2, 2 * npair - 1, i)
    last = jnp.where(i % 2 == 0, i + 1, i - 1)
    produce(1, key_rows(ks_ref, i), 2 * dk, causal)
    consume(0, vwt_ref[0, g, i], 1)
    flash_out(1, 2)

    def pair_body(p):
        c = 2 * p
        produce(0, key_rows(ks_ref, c), 2 * dk, None)
        consume(1, vst_ref[0, g, jnp.where(p == 0, i, c - 1)], 0)
        produce(1, key_rows(ks_ref, c + 1), 2 * dk, None)
        consume(0, vst_ref[0, g, c], 0)

    yield pair_body
    produce(0, key_rows(ks_ref, last), 2 * dk, None)
    consume(1, vst_ref[0, g, held], 0)
    yield
    consume(0, vst_ref[0, g, last], 0)
    flash_out(0, 1)

    out = out_ref[...]
    out = jnp.concatenate([out[:, h * tq:(h + 1) * tq] for h in range(hpg)], axis=0)
    o_ref[0, :, q_lanes] = out.T.astype(o_ref.dtype)


def _nsa(q, gate, q_norm, kc, vct, ks, vst, kw, vwt):
    b, s, dq = q.shape
    g = N_KV_GROUPS
    hpg = HEADS_PER_GROUP
    dk = HEAD_DIM
    gd = hpg * dk
    tq = CHUNK
    n_blk = s // SEL_BLOCK
    n_cmp = (s - CMP_BLOCK) // CMP_STRIDE + 1
    ncp = kc.shape[2]
    assert WINDOW == 2 * tq and n_blk <= dk and (s // tq) % 2 == 0
    rows_np = np.arange(ncp)
    cmp_idx = np.where(rows_np < ncp // 2, 2 * rows_np, 2 * rows_np - (ncp - 1))
    cmp_start = cmp_idx * CMP_STRIDE
    blk_start = np.arange(n_blk) * SEL_BLOCK
    ovt = ((cmp_start[None, :] < blk_start[:, None] + SEL_BLOCK)
           & (cmp_start[None, :] + CMP_BLOCK > blk_start[:, None])
           & (cmp_idx[None, :] < n_cmp)).astype(np.float32)
    qgain = jnp.broadcast_to((q_norm * (dk ** -0.5 * LOG2E))[:, None], (dk, hpg * tq))
    idx = lambda bi, i: (bi, 0, 0, 0)
    idx5 = lambda bi, i: (bi, 0, 0, 0, 0)
    return pl.pallas_call(
        functools.partial(_nsa_kernel, n_cmp=n_cmp, n_blk=n_blk),
        grid=(b, s // tq),
        in_specs=[
            pl.BlockSpec((1, tq, dq), lambda bi, i: (bi, i, 0)),
            pl.BlockSpec((1, tq, g * LANES), lambda bi, i: (bi, i, 0)),
            pl.BlockSpec((1, g, ncp, dk), idx),
            pl.BlockSpec((1, g, dk, ncp), idx),
            pl.BlockSpec((1, g, s, 2 * dk), idx),
            pl.BlockSpec((1, g, s // tq, V_ROWS, tq), idx5),
            pl.BlockSpec((1, g, s, dk), idx),
            pl.BlockSpec((1, g, s // tq, V_ROWS, tq), idx5),
            _resident((n_blk, ncp)),
            _resident((dk, hpg * tq)),
        ],
        out_specs=pl.BlockSpec((1, tq, dq), lambda bi, i: (bi, i, 0)),
        out_shape=jax.ShapeDtypeStruct((b, s, dq), BF16),
        scratch_shapes=[
            pltpu.VMEM((g, 2 * dk, hpg * tq), BF16),
            pltpu.VMEM((g, 3, 1, hpg * tq), F32),
            pltpu.VMEM((g, 3, V_ROWS, hpg * tq), F32),
            pltpu.VMEM((g, dk, hpg * tq), F32),
            pltpu.VMEM((g, 2, tq, hpg * tq), F32),
            pltpu.VMEM((g, 2, 1, hpg * tq), F32),
            pltpu.VMEM((2, tq, hpg * tq), F32),
            pltpu.VMEM((2, ncp, hpg * tq), F32),
        ],
        compiler_params=_cparams(("parallel", "arbitrary")),
        name="nsa",
    )(q, gate, kc, vct, ks, vst, kw, vwt, jnp.asarray(ovt, BF16), qgain)


def _gate_columns(w_qg):
    dq = N_HEADS * HEAD_DIM
    per = HEADS_PER_GROUP * N_BRANCH
    wg = w_qg[:, dq:].reshape(-1, N_KV_GROUPS, per)
    wg = jnp.pad(wg, ((0, 0), (0, 0), (0, LANES - per)))
    return jnp.concatenate([w_qg[:, :dq], wg.reshape(-1, N_KV_GROUPS * LANES)], axis=1)


def kernel(x, ffn_norm, ffn_w_gate_up, ffn_w_down, mix_norm, conv_w_in, conv_w, conv_w_out, kv_norm, kv_w,
           cmp_pos, cmp_w1, cmp_b1, cmp_w2, k_norm, nsa_w_qg, q_norm, nsa_w_o):
    b, s, d = x.shape
    t = b * s
    depth = ffn_norm.shape[0]
    n_a = conv_w_in.shape[0]
    g = N_KV_GROUPS

    ffn_g = ffn_norm.reshape(depth, 2, 1, d)
    ffn_wgu = ffn_w_gate_up.astype(BF16)
    ffn_wd = ffn_w_down.astype(BF16)

    def ffn(h2, layer, j, proj=None, tail=None):
        return _ffn(h2, ffn_g, ffn_wgu, ffn_wd, (layer, j), proj, tail)

    h = x.reshape(t, d)
    shared = None
    for layer in range(depth):
        proj = None
        if layer < n_a:
            h = ffn(h, layer, 0)
            h = _conv_mixer(h.reshape(b, s, d), mix_norm[layer], conv_w_in[layer].astype(BF16),
                            conv_w[layer], conv_w_out[layer].astype(BF16)).reshape(t, d)
        else:
            li = layer - n_a
            h, q, gate = ffn(h, layer, 0, tail=_qg_tail(t, mix_norm[layer],
                                                        _gate_columns(nsa_w_qg[li]).astype(BF16)))
            o = _nsa(q.reshape(b, s, -1), gate.reshape(b, s, -1), q_norm[li], *shared)
            proj = (o.reshape(t, -1), nsa_w_o[li].astype(BF16))
        if layer == n_a - 1:
            h, blk, ks, vst, kw, vwt = ffn(h, layer, 1, proj,
                                           tail=_kv_tail(b, s, kv_norm, kv_w.astype(BF16), k_norm))
            blk = blk.reshape(2, b * g, s // CMP_BLOCK, CMP_BLOCK * HEAD_DIM)
            cmp = _compress(blk, cmp_pos.reshape(2, -1), cmp_w1.astype(BF16), cmp_b1,
                            cmp_w2.astype(BF16), k_norm[0])
            cmp = cmp.reshape(2, b, g, -1, HEAD_DIM)
            shared = (cmp[0], cmp[1].transpose(0, 1, 3, 2), ks, vst, kw, vwt)
        else:
            h = ffn(h, layer, 1, proj)
    return h.reshape(b, s, d)
```

```python
import functools

import numpy as np
import jax
import jax.numpy as jnp
from jax import lax
from jax.experimental import pallas as pl
from jax.experimental.pallas import tpu as pltpu

EPS = 1e-6
N_HEADS = 16
HEAD_DIM = 64
N_KV_GROUPS = 4
HEADS_PER_GROUP = N_HEADS // N_KV_GROUPS
N_KV_SLOTS = 6
N_BRANCH = 3
CMP_BLOCK = 32
CMP_STRIDE = 16
SEL_BLOCK = 64
N_SELECT = 16
WINDOW = 512
FORCE_SCORE = 1e9
MASK_SCORE = -1e30
LOG2E = 1.4426950408889634

LANES = 128
CHUNK = 256
V_ROWS = HEAD_DIM + 16
VMEM_LIMIT = 56 * 1024 * 1024

BF16 = jnp.bfloat16
F32 = jnp.float32


def _cparams(sem):
    return pltpu.CompilerParams(dimension_semantics=sem, vmem_limit_bytes=VMEM_LIMIT)


def _rms(x, g):
    ms = jnp.mean(x * x, axis=-1, keepdims=True)
    return x * lax.rsqrt(ms + EPS) * g


def _dot(a, b):
    return jnp.dot(a, b, preferred_element_type=F32)


def _split2(x):
    hi = x.astype(BF16)
    return hi, (x - hi.astype(F32)).astype(BF16)


def _seg_rms(y, seg, seg_t, gain):
    ss = None
    for part in _split2(y * y):
        term = _dot(part, seg)
        ss = term if ss is None else ss + term
    r = lax.rsqrt(ss * (1.0 / HEAD_DIM) + EPS)
    scale = None
    for part in _split2(r):
        term = _dot(part, seg_t)
        scale = term if scale is None else scale + term
    return y * scale * gain


def _seg_matrices(width):
    ind = (np.arange(width)[:, None] // HEAD_DIM == np.arange(LANES)[None, :]).astype(np.float32)
    return jnp.asarray(ind, BF16), jnp.asarray(ind.T, BF16)


def _dot_nt(a, b):
    return lax.dot_general(a, b, (((1,), (1,)), ((), ())), preferred_element_type=F32)


def _ffn_kernel(x_ref, g_ref, wgu_ref, wd_ref, *rest, dff, has_proj, n_tail_in, keep_h, tail):
    x = x_ref[...]
    if has_proj:
        a_ref, w_ref = rest[:2]
        rest = rest[2:]
        x = x + _dot(a_ref[...], w_ref[...])
    tail_in, rest = rest[:n_tail_in], rest[n_tail_in:]
    o_ref, tail_out = (rest[0], rest[1:]) if keep_h else (None, rest)
    xn = _rms(x, g_ref[...]).astype(BF16)
    ab = _dot(xn, wgu_ref[...])
    a = ab[:, 0:dff]
    h = (a * jax.nn.sigmoid(a) * ab[:, dff:2 * dff]).astype(BF16)
    out = x + 0.5 * _dot(h, wd_ref[...])
    if keep_h:
        o_ref[...] = out
    if tail is not None:
        tail(out, *tail_in, *tail_out)


def _resident(shape, lead=()):
    index = tuple(lead) + (0,) * len(shape)
    return pl.BlockSpec((None,) * len(lead) + tuple(shape), lambda *_: index, pipeline_mode=pl.Buffered(1))


FFN_TM = 512


def _ffn(h, g_all, w_gu_all, w_d_all, lead, proj=None, tail=None, keep_h=True):
    tm = FFN_TM
    t, d = h.shape
    dff = w_d_all.shape[-2]
    in_specs = [
        pl.BlockSpec((tm, d), lambda i: (i, 0)),
        _resident((1, d), lead),
        _resident((d, 2 * dff), lead),
        _resident((dff, d), lead),
    ]
    args = [h, g_all, w_gu_all, w_d_all]
    if proj is not None:
        a, w = proj
        in_specs += [pl.BlockSpec((tm, a.shape[1]), lambda i: (i, 0)), _resident(w.shape)]
        args += [a, w]
    out_specs = [pl.BlockSpec((tm, d), lambda i: (i, 0))] if keep_h else []
    out_shape = [jax.ShapeDtypeStruct((t, d), F32)] if keep_h else []
    tail_fn, n_tail_in, scratch = None, 0, []
    if tail is not None:
        tail_fn, tail_args, tail_in_specs, tail_out_specs, tail_out_shapes, scratch = tail
        in_specs += tail_in_specs
        args += tail_args
        n_tail_in = len(tail_args)
        out_specs += tail_out_specs
        out_shape += tail_out_shapes
    outs = pl.pallas_call(
        functools.partial(_ffn_kernel, dff=dff, has_proj=proj is not None, n_tail_in=n_tail_in,
                          keep_h=keep_h, tail=tail_fn),
        grid=(t // tm,),
        in_specs=in_specs,
        out_specs=out_specs,
        out_shape=out_shape,
        scratch_shapes=scratch,
        compiler_params=_cparams(("arbitrary",)),
        name="ffn",
    )(*args)
    return outs[0] if len(outs) == 1 else outs


CARRY = 8


def _conv_tail_kernel(x, g_ref, win_ref, cw_ref, wout_ref, o_ref, vbuf_ref, *, nj):
    tm, d = x.shape

    @pl.when(pl.program_id(0) % nj == 0)
    def _():
        vbuf_ref[0:CARRY, :] = jnp.zeros((CARRY, d), F32)

    xn = _rms(x, g_ref[...]).astype(BF16)
    proj = _dot(xn, win_ref[...])
    bg = proj[:, 0:d]
    v = proj[:, d:2 * d] * proj[:, 2 * d:3 * d]
    vbuf_ref[CARRY:CARRY + tm, :] = v
    cw = cw_ref[...]
    conv = (cw[0:1, :] * vbuf_ref[CARRY - 2:CARRY - 2 + tm, :]
            + cw[1:2, :] * vbuf_ref[CARRY - 1:CARRY - 1 + tm, :]
            + cw[2:3, :] * v)
    y = (bg * conv).astype(BF16)
    o_ref[...] = x + _dot(y, wout_ref[...])
    vbuf_ref[0:CARRY, :] = vbuf_ref[tm:tm + CARRY, :]


def _conv_tail(b, s, g, w_in, conv_w, w_out):
    tm = FFN_TM
    d = w_out.shape[0]
    assert s % tm == 0
    return (
        functools.partial(_conv_tail_kernel, nj=s // tm),
        [g.reshape(1, d), w_in, conv_w, w_out],
        [_resident((1, d)), _resident(w_in.shape), _resident(conv_w.shape), _resident(w_out.shape)],
        [pl.BlockSpec((tm, d), lambda i: (i, 0))],
        [jax.ShapeDtypeStruct((b * s, d), F32)],
        [pltpu.VMEM((CARRY + tm, d), F32)],
    )


def _kv_tail_kernel(x, g_ref, w_ref, kn_ref, seg_ref, segt_ref, blk_ref, ks_ref, vst_ref, kw_ref, vwt_ref,
                    raw_ref, *, nj):
    tm = x.shape[0]
    j = pl.program_id(0) % nj
    xn = _rms(x, g_ref[...]).astype(BF16)
    y = _dot(xn, w_ref[...])
    gw = N_KV_GROUPS * HEAD_DIM
    per = LANES // HEAD_DIM
    for slab in range(raw_ref.shape[0]):
        raw_ref[slab] = y[:, slab * LANES:(slab + 1) * LANES]
    for l in range(CMP_BLOCK):
        for slab in range(raw_ref.shape[0]):
            rows = raw_ref[slab, pl.ds(l, tm // CMP_BLOCK, stride=CMP_BLOCK), :]
            for k in range(per):
                sg = slab * per + k
                blk_ref[sg // N_KV_GROUPS, 0, sg % N_KV_GROUPS, :, l * HEAD_DIM:(l + 1) * HEAD_DIM] = (
                    rows[:, k * HEAD_DIM:(k + 1) * HEAD_DIM])
    pos = j * tm + lax.broadcasted_iota(jnp.int32, (tm, HEAD_DIM), 0)
    lane = lax.broadcasted_iota(jnp.int32, (tm, HEAD_DIM), 1)
    onehot = jnp.where(lane == pos // SEL_BLOCK, 1.0, 0.0).astype(BF16)
    k_sel = _seg_rms(y[:, 2 * gw:3 * gw], seg_ref[...], segt_ref[...], kn_ref[0:1, :]).astype(BF16)
    k_win = _seg_rms(y[:, 4 * gw:5 * gw], seg_ref[...], segt_ref[...], kn_ref[1:2, :]).astype(BF16)
    for grp in range(N_KV_GROUPS):
        ks_ref[0, grp, :, 0:HEAD_DIM] = k_sel[:, grp * HEAD_DIM:(grp + 1) * HEAD_DIM]
        ks_ref[0, grp, :, HEAD_DIM:2 * HEAD_DIM] = onehot
        kw_ref[0, grp] = k_win[:, grp * HEAD_DIM:(grp + 1) * HEAD_DIM]
    extra = jnp.where(lax.broadcasted_iota(jnp.int32, (V_ROWS - HEAD_DIM, CHUNK), 0) == 0, 1.0, 0.0).astype(BF16)
    for slot, ref in ((3, vst_ref), (5, vwt_ref)):
        vt = y[:, slot * gw:(slot + 1) * gw].T
        for grp in range(N_KV_GROUPS):
            for cc in range(tm // CHUNK):
                ref[0, grp, cc, 0:HEAD_DIM, :] = vt[grp * HEAD_DIM:(grp + 1) * HEAD_DIM,
                                                    cc * CHUNK:(cc + 1) * CHUNK].astype(BF16)
                ref[0, grp, cc, HEAD_DIM:V_ROWS, :] = extra


def _kv_tail(b, s, g, w, k_norm):
    tm = FFN_TM
    d, n = w.shape
    ng = N_KV_GROUPS
    gw = ng * HEAD_DIM
    cpt = tm // CHUNK
    nj = s // tm
    assert s // SEL_BLOCK <= HEAD_DIM and s % tm == 0
    seg, seg_t = _seg_matrices(gw)
    gains = jnp.tile(k_norm[1:3], (1, ng))
    return (
        functools.partial(_kv_tail_kernel, nj=nj),
        [g.reshape(1, d), w, gains, seg, seg_t],
        [_resident((1, d)), _resident((d, n)), _resident(gains.shape), _resident(seg.shape),
         _resident(seg_t.shape)],
        [
            pl.BlockSpec((2, 1, ng, tm // CMP_BLOCK, CMP_BLOCK * HEAD_DIM),
                         lambda i: (0, i // nj, 0, i % nj, 0)),
            pl.BlockSpec((1, ng, tm, 2 * HEAD_DIM), lambda i: (i // nj, 0, i % nj, 0)),
            pl.BlockSpec((1, ng, cpt, V_ROWS, CHUNK), lambda i: (i // nj, 0, i % nj, 0, 0)),
            pl.BlockSpec((1, ng, tm, HEAD_DIM), lambda i: (i // nj, 0, i % nj, 0)),
            pl.BlockSpec((1, ng, cpt, V_ROWS, CHUNK), lambda i: (i // nj, 0, i % nj, 0, 0)),
        ],
        [
            jax.ShapeDtypeStruct((2, b, ng, s // CMP_BLOCK, CMP_BLOCK * HEAD_DIM), F32),
            jax.ShapeDtypeStruct((b, ng, s, 2 * HEAD_DIM), BF16),
            jax.ShapeDtypeStruct((b, ng, s // CHUNK, V_ROWS, CHUNK), BF16),
            jax.ShapeDtypeStruct((b, ng, s, HEAD_DIM), BF16),
            jax.ShapeDtypeStruct((b, ng, s // CHUNK, V_ROWS, CHUNK), BF16),
        ],
        [pltpu.VMEM((2 * gw // LANES, tm, LANES), F32)],
    )


def _cmp_kernel(x_ref, pos_ref, w1_ref, b1_ref, w2_ref, kn_ref, o_ref):
    c = pl.program_id(0)
    nb, ne, width = x_ref.shape
    rows = nb * ne
    half = width // 2
    x = x_ref[...].reshape(rows, width)
    pos = pos_ref[...]
    w1 = w1_ref[...]
    h_even = _dot((x + pos).astype(BF16), w1)
    a = _dot((x[:, half:] + pos[:, :half]).astype(BF16), w1[0:half])
    b = _dot((x[:, :half] + pos[:, half:]).astype(BF16), w1[half:width])
    h_odd = a + pltpu.roll(b, rows - 1, 0)

    def tail(hid):
        y = _dot(jax.nn.gelu(hid + b1_ref[...]).astype(BF16), w2_ref[...])
        y = jnp.where(c == 0, _rms(y, kn_ref[...]), y)
        return y.astype(o_ref.dtype).reshape(nb, ne, y.shape[-1])

    o_ref[:, 0:ne, :] = tail(h_even)
    o_ref[:, ne:2 * ne, :] = tail(h_odd)


def _compress(xe, pos, w1, b1, w2, kn, *, nb=8):
    _, nseq, ne, width = xe.shape
    nb = min(nb, nseq)
    hid = w1.shape[-1]
    dk = w2.shape[-1]
    return pl.pallas_call(
        _cmp_kernel,
        grid=(2, nseq // nb),
        in_specs=[
            pl.BlockSpec((None, nb, ne, width), lambda c, i: (c, i, 0, 0)),
            pl.BlockSpec((None, 1, width), lambda c, i: (c, 0, 0)),
            pl.BlockSpec((None, width, hid), lambda c, i: (c, 0, 0)),
            pl.BlockSpec((None, 1, hid), lambda c, i: (c, 0, 0)),
            pl.BlockSpec((None, hid, dk), lambda c, i: (c, 0, 0)),
            pl.BlockSpec((1, dk), lambda c, i: (0, 0)),
        ],
        out_specs=pl.BlockSpec((None, nb, 2 * ne, dk), lambda c, i: (c, i, 0, 0)),
        out_shape=jax.ShapeDtypeStruct((2, nseq, 2 * ne, dk), BF16),
        compiler_params=_cparams(("parallel", "parallel")),
        name="compress",
    )(xe, pos.reshape(2, 1, width), w1, b1.reshape(2, 1, hid), w2, kn.reshape(1, dk))


def _qg_tail_kernel(x, g_ref, w_ref, q_ref, gate_ref):
    dq = q_ref.shape[-1]
    xn = _rms(x, g_ref[...]).astype(BF16)
    y = _dot(xn, w_ref[...])
    q_ref[...] = y[:, 0:dq]
    gate_ref[...] = jax.nn.sigmoid(y[:, dq:])


def _qg_tail(t, g, w):
    tm = FFN_TM
    d, n = w.shape
    dq = N_HEADS * HEAD_DIM
    return (
        _qg_tail_kernel,
        [g.reshape(1, d), w],
        [_resident((1, d)), _resident((d, n))],
        [pl.BlockSpec((tm, dq), lambda i: (i, 0)), pl.BlockSpec((tm, n - dq), lambda i: (i, 0))],
        [jax.ShapeDtypeStruct((t, dq), F32), jax.ShapeDtypeStruct((t, n - dq), F32)],
        [],
    )


def _split3(x):
    hi = x.astype(BF16)
    r = x - hi.astype(F32)
    mid = r.astype(BF16)
    lo = (r - mid.astype(F32)).astype(BF16)
    return hi, mid, lo


def _cmp_block_of_row(row, nrows):
    return jnp.where(row < nrows // 2, 2 * row, 2 * row - (nrows - 1))


def _nsa_kernel(*refs, n_cmp, n_blk):
    mask_ref, cmask_ref = refs[-2:]
    tq = CHUNK
    _, ncp, wide = cmask_ref.shape
    i = pl.program_id(1)
    kk = lax.broadcasted_iota(jnp.int32, (tq, wide), 0)
    tt = lax.broadcasted_iota(jnp.int32, (tq, wide), 1) & (tq - 1)
    mask_ref[0] = jnp.where(kk <= tt, 0.0, MASK_SCORE)
    mask_ref[1] = jnp.where(kk > tt + jnp.where(i >= 2, 0, tq), 0.0, MASK_SCORE)
    nidx = _cmp_block_of_row(lax.broadcasted_iota(jnp.int32, (ncp, wide), 0), ncp)
    tpos = i * tq + (lax.broadcasted_iota(jnp.int32, (ncp, wide), 1) & (tq - 1))
    valid = (nidx * CMP_STRIDE + (CMP_BLOCK - 1) <= tpos) & (nidx < n_cmp)
    cmask_ref[0] = jnp.where(valid, 0.0, MASK_SCORE)
    cmask_ref[1] = jnp.where(valid, 1.0, 0.0)

    n_in = 11
    scratch = refs[n_in:-2]
    groups = [_nsa_group(g, *refs[:n_in], *[r.at[g] for r in scratch], mask_ref, cmask_ref, n_blk=n_blk)
              for g in range(N_KV_GROUPS)]
    done = object()

    def advance():
        handed = [None] * len(groups)
        while any(r is None for r in handed):
            for gi, grp in enumerate(groups):
                if handed[gi] is None:
                    handed[gi] = next(grp, done)
        return handed

    bodies = advance()

    def pairs(p, carry):
        for body in bodies:
            body(p)
        return carry

    lax.fori_loop(0, i // 2, pairs, 0)
    assert all(r is done for r in advance())


def _nsa_group(g, q_ref, gate_ref, kc_ref, vct_ref, ks_ref, vst_ref, kw_ref, vwt_ref, ovt_ref, qgain_ref,
               o_ref, qa_ref, m_ref, acc_ref, out_ref, s_ref, cmax_ref, mask_ref, cmask_ref, *, n_blk):
    hpg = HEADS_PER_GROUP
    tq = CHUNK
    dk = HEAD_DIM
    wide = hpg * tq
    i = pl.program_id(1)
    gd = hpg * dk
    q_lanes = pl.ds(g * gd, gd)
    t0 = i * tq

    def heads(fn):
        return jnp.concatenate([fn(h) for h in range(hpg)], axis=1)

    q_t = q_ref[0, :, q_lanes].T
    q_t = heads(lambda h: q_t[h * dk:(h + 1) * dk, :])
    ms = jnp.sum(q_t * q_t, axis=0, keepdims=True) * (1.0 / dk)
    qa_ref[0:dk, :] = (q_t * lax.rsqrt(ms + EPS) * qgain_ref[...]).astype(BF16)
    gate_t = gate_ref[0, :, pl.ds(g * LANES, LANES)].T
    gates = [heads(lambda h: gate_t[h * N_BRANCH + br:h * N_BRANCH + br + 1, :]) for br in range(N_BRANCH)]

    causal, win_lo = 0, 1
    sel_state, win_state, void_state = 0, 1, 2

    m_ref[...] = jnp.full(m_ref.shape, MASK_SCORE, F32)
    acc_ref[...] = jnp.zeros(acc_ref.shape, F32)

    def key_rows(ref, c):
        return ref[0, g, pl.ds(pl.multiple_of(c * tq, tq), tq), :]

    def produce(b, k, qrows, mask):
        s = _dot(k, qa_ref[0:qrows, :])
        if mask is not None:
            s = s + mask_ref[mask]
        s_ref[b] = s
        cmax_ref[b] = jnp.max(s, axis=0, keepdims=True)

    def consume(b, vt, st):
        m_prev = m_ref[st]
        m_new = jnp.maximum(m_prev, cmax_ref[b])
        alpha = jnp.exp2(m_prev - m_new)
        p = jnp.exp2(s_ref[b] - m_new)
        acc_ref[st] = alpha * acc_ref[st] + _dot(vt, p.astype(BF16))
        m_ref[st] = m_new

    def flash_out(st, branch):
        out_ref[...] += gates[branch] * (acc_ref[st, 0:dk, :] * (1.0 / acc_ref[st, dk:dk + 1, :]))

    assert WINDOW == 2 * tq
    lo = jnp.maximum(i - 2, 0)
    mid = jnp.maximum(i - 1, 0)
    mid_state = jnp.where(i >= 1, win_state, void_state)
    yield
    produce(0, key_rows(kw_ref, lo), dk, win_lo)
    yield

    s = _dot(kc_ref[0, g], qa_ref[0:dk, :]) + cmask_ref[0]
    e = jnp.exp2(s - jnp.max(s, axis=0, keepdims=True))
    p = e * (1.0 / jnp.sum(e, axis=0, keepdims=True)) * cmask_ref[1]
    out_ref[...] = gates[0] * _dot(vct_ref[0, g], p.astype(BF16))
    psum = p[:, 0:tq]
    for h in range(1, hpg):
        psum = psum + p[:, h * tq:(h + 1) * tq]
    yield

    produce(1, key_rows(kw_ref, mid), dk, None)
    consume(0, vwt_ref[0, g, lo], win_state)
    yield

    ovt = ovt_ref[...]
    imp = None
    for part in _split3(psum):
        term = _dot(ovt, part)
        imp = term if imp is None else imp + term
    jblk = lax.broadcasted_iota(jnp.int32, (n_blk, tq), 0)
    cur = (t0 + lax.broadcasted_iota(jnp.int32, (n_blk, tq), 1)) // SEL_BLOCK
    forced = (jblk == 0) | (jblk == cur) | (jblk == cur - 1)
    imp = jnp.where(forced, FORCE_SCORE, imp)
    imp = jnp.where(jblk > cur, MASK_SCORE, imp)
    sub = 8
    parts = [imp[r:r + sub] for r in range(0, n_blk, sub)]
    ranks = [jnp.zeros((sub, tq), F32) for _ in parts]
    jsub = lax.broadcasted_iota(jnp.int32, (sub, tq), 0)
    for ii in range(n_blk):
        row = imp[ii:ii + 1, :]
        for gi, x in enumerate(parts):
            if ii < gi * sub:
                beats = row >= x
            elif ii >= (gi + 1) * sub:
                beats = row > x
            else:
                beats = (row > x) | ((row == x) & (jsub > ii - gi * sub))
            ranks[gi] = ranks[gi] + jnp.where(beats, 1.0, 0.0)
    rank = jnp.concatenate(ranks, axis=0)
    chosen = (rank < float(min(N_SELECT, n_blk))) & (jblk <= cur)
    bias = jnp.where(chosen, 0.0, MASK_SCORE)
    bias = jnp.concatenate([bias, jnp.zeros((dk - n_blk, tq), F32)], axis=0).astype(BF16)
    qa_ref[dk:2 * dk, :] = heads(lambda h: bias)
    yield

    produce(0, key_rows(kw_ref, i), dk, causal)
    consume(1, vwt_ref[0, g, mid], mid_state)
    yield

    npair = i // 2
    held = jnp.where(i >= 2, 2 * npair - 1, i)
    last = jnp.where(i % 2 == 0, i + 1, i - 1)
    produce(1, key_rows(ks_ref, i), 2 * dk, causal)
    consume(0, vwt_ref[0, g, i], 1)
    flash_out(1, 2)

    def pair_body(p):
        c = 2 * p
        produce(0, key_rows(ks_ref, c), 2 * dk, None)
        consume(1, vst_ref[0, g, jnp.where(p == 0, i, c - 1)], 0)
        produce(1, key_rows(ks_ref, c + 1), 2 * dk, None)
        consume(0, vst_ref[0, g, c], 0)

    yield pair_body
    produce(0, key_rows(ks_ref, last), 2 * dk, None)
    consume(1, vst_ref[0, g, held], 0)
    yield
    consume(0, vst_ref[0, g, last], 0)
    flash_out(0, 1)

    out = out_ref[...]
    out = jnp.concatenate([out[:, h * tq:(h + 1) * tq] for h in range(hpg)], axis=0)
    o_ref[0, :, q_lanes] = out.T.astype(o_ref.dtype)


def _nsa(q, gate, q_norm, kc, vct, ks, vst, kw, vwt):
    b, s, dq = q.shape
    g = N_KV_GROUPS
    hpg = HEADS_PER_GROUP
    dk = HEAD_DIM
    gd = hpg * dk
    tq = CHUNK
    n_blk = s // SEL_BLOCK
    n_cmp = (s - CMP_BLOCK) // CMP_STRIDE + 1
    ncp = kc.shape[2]
    assert WINDOW == 2 * tq and n_blk <= dk and (s // tq) % 2 == 0
    rows_np = np.arange(ncp)
    cmp_idx = np.where(rows_np < ncp // 2, 2 * rows_np, 2 * rows_np - (ncp - 1))
    cmp_start = cmp_idx * CMP_STRIDE
    blk_start = np.arange(n_blk) * SEL_BLOCK
    ovt = ((cmp_start[None, :] < blk_start[:, None] + SEL_BLOCK)
           & (cmp_start[None, :] + CMP_BLOCK > blk_start[:, None])
           & (cmp_idx[None, :] < n_cmp)).astype(np.float32)
    qgain = jnp.broadcast_to((q_norm * (dk ** -0.5 * LOG2E))[:, None], (dk, hpg * tq))
    idx = lambda bi, i: (bi, 0, 0, 0)
    idx5 = lambda bi, i: (bi, 0, 0, 0, 0)
    return pl.pallas_call(
        functools.partial(_nsa_kernel, n_cmp=n_cmp, n_blk=n_blk),
        grid=(b, s // tq),
        in_specs=[
            pl.BlockSpec((1, tq, dq), lambda bi, i: (bi, i, 0)),
            pl.BlockSpec((1, tq, g * LANES), lambda bi, i: (bi, i, 0)),
            pl.BlockSpec((1, g, ncp, dk), idx),
            pl.BlockSpec((1, g, dk, ncp), idx),
            pl.BlockSpec((1, g, s, 2 * dk), idx),
            pl.BlockSpec((1, g, s // tq, V_ROWS, tq), idx5),
            pl.BlockSpec((1, g, s, dk), idx),
            pl.BlockSpec((1, g, s // tq, V_ROWS, tq), idx5),
            _resident((n_blk, ncp)),
            _resident((dk, hpg * tq)),
        ],
        out_specs=pl.BlockSpec((1, tq, dq), lambda bi, i: (bi, i, 0)),
        out_shape=jax.ShapeDtypeStruct((b, s, dq), BF16),
        scratch_shapes=[
            pltpu.VMEM((g, 2 * dk, hpg * tq), BF16),
            pltpu.VMEM((g, 3, 1, hpg * tq), F32),
            pltpu.VMEM((g, 3, V_ROWS, hpg * tq), F32),
            pltpu.VMEM((g, dk, hpg * tq), F32),
            pltpu.VMEM((g, 2, tq, hpg * tq), F32),
            pltpu.VMEM((g, 2, 1, hpg * tq), F32),
            pltpu.VMEM((2, tq, hpg * tq), F32),
            pltpu.VMEM((2, ncp, hpg * tq), F32),
        ],
        compiler_params=_cparams(("parallel", "arbitrary")),
        name="nsa",
    )(q, gate, kc, vct, ks, vst, kw, vwt, jnp.asarray(ovt, BF16), qgain)


def _gate_columns(w_qg):
    dq = N_HEADS * HEAD_DIM
    per = HEADS_PER_GROUP * N_BRANCH
    wg = w_qg[:, dq:].reshape(-1, N_KV_GROUPS, per)
    wg = jnp.pad(wg, ((0, 0), (0, 0), (0, LANES - per)))
    return jnp.concatenate([w_qg[:, :dq], wg.reshape(-1, N_KV_GROUPS * LANES)], axis=1)


def kernel(x, ffn_norm, ffn_w_gate_up, ffn_w_down, mix_norm, conv_w_in, conv_w, conv_w_out, kv_norm, kv_w,
           cmp_pos, cmp_w1, cmp_b1, cmp_w2, k_norm, nsa_w_qg, q_norm, nsa_w_o):
    b, s, d = x.shape
    t = b * s
    depth = ffn_norm.shape[0]
    n_a = conv_w_in.shape[0]
    g = N_KV_GROUPS

    ffn_g = ffn_norm.reshape(depth, 2, 1, d)
    ffn_wgu = ffn_w_gate_up.astype(BF16)
    ffn_wd = ffn_w_down.astype(BF16)

    def ffn(h2, layer, j, proj=None, tail=None, keep_h=True):
        return _ffn(h2, ffn_g, ffn_wgu, ffn_wd, (layer, j), proj, tail, keep_h)

    h = x.reshape(t, d)
    shared = None
    for layer in range(depth):
        proj = None
        if layer < n_a:
            h = ffn(h, layer, 0, keep_h=False,
                    tail=_conv_tail(b, s, mix_norm[layer], conv_w_in[layer].astype(BF16), conv_w[layer],
                                    conv_w_out[layer].astype(BF16)))
        else:
            li = layer - n_a
            h, q, gate = ffn(h, layer, 0, tail=_qg_tail(t, mix_norm[layer],
                                                        _gate_columns(nsa_w_qg[li]).astype(BF16)))
            o = _nsa(q.reshape(b, s, -1), gate.reshape(b, s, -1), q_norm[li], *shared)
            proj = (o.reshape(t, -1), nsa_w_o[li].astype(BF16))
        if layer == n_a - 1:
            h, blk, ks, vst, kw, vwt = ffn(h, layer, 1, proj,
                                           tail=_kv_tail(b, s, kv_norm, kv_w.astype(BF16), k_norm))
            blk = blk.reshape(2, b * g, s // CMP_BLOCK, CMP_BLOCK * HEAD_DIM)
            cmp = _compress(blk, cmp_pos.reshape(2, -1), cmp_w1.astype(BF16), cmp_b1,
                            cmp_w2.astype(BF16), k_norm[0])
            cmp = cmp.reshape(2, b, g, -1, HEAD_DIM)
            shared = (cmp[0], cmp[1].transpose(0, 1, 3, 2), ks, vst, kw, vwt)
        else:
            h = ffn(h, layer, 1, proj)
    return h.reshape(b, s, d)
```

```python
import functools

import numpy as np
import jax
import jax.numpy as jnp
from jax import lax
from jax.experimental import pallas as pl
from jax.experimental.pallas import tpu as pltpu

EPS = 1e-6
N_HEADS = 16
HEAD_DIM = 64
N_KV_GROUPS = 4
HEADS_PER_GROUP = N_HEADS // N_KV_GROUPS
N_BRANCH = 3
CMP_BLOCK = 32
CMP_STRIDE = 16
SEL_BLOCK = 64
N_SELECT = 16
WINDOW = 512
FORCE_SCORE = 1e9
MASK_SCORE = -1e30
LOG2E = 1.4426950408889634

LANES = 128
CHUNK = 256
V_ROWS = HEAD_DIM + 16
VMEM_LIMIT = 56 * 1024 * 1024

BF16 = jnp.bfloat16
F32 = jnp.float32


def _cparams(sem):
    return pltpu.CompilerParams(dimension_semantics=sem, vmem_limit_bytes=VMEM_LIMIT)


def _rms(x, g):
    ms = jnp.mean(x * x, axis=-1, keepdims=True)
    return x * lax.rsqrt(ms + EPS) * g


def _dot(a, b):
    return jnp.dot(a, b, preferred_element_type=F32)


def _split2(x):
    hi = x.astype(BF16)
    return hi, (x - hi.astype(F32)).astype(BF16)


def _seg_rms(y, seg, seg_t, gain):
    ss = None
    for part in _split2(y * y):
        term = _dot(part, seg)
        ss = term if ss is None else ss + term
    r = lax.rsqrt(ss * (1.0 / HEAD_DIM) + EPS)
    scale = None
    for part in _split2(r):
        term = _dot(part, seg_t)
        scale = term if scale is None else scale + term
    return y * scale * gain


def _seg_matrices(width):
    ind = (np.arange(width)[:, None] // HEAD_DIM == np.arange(LANES)[None, :]).astype(np.float32)
    return jnp.asarray(ind, BF16), jnp.asarray(ind.T, BF16)


def _ffn_kernel(x_ref, g_ref, wgu32_ref, wd32_ref, *rest, dff, has_proj, n_tail_in, n_tail_out, keep_h, tail):
    step = pl.program_id(0)
    proj_refs, rest = (rest[:2], rest[2:]) if has_proj else ((), rest)
    tail_in, rest = rest[:n_tail_in], rest[n_tail_in:]
    o_ref, rest = (rest[0], rest[1:]) if keep_h else (None, rest)
    tail_out, (wgu_ref, wd_ref), tail_scratch = rest[:n_tail_out], rest[n_tail_out:n_tail_out + 2], rest[n_tail_out + 2:]

    @pl.when(step < FFN_INGEST)
    def _():
        for src32, dst in ((wgu32_ref, wgu_ref), (wd32_ref, wd_ref)):
            rows = src32.shape[0]
            dst[pl.ds(pl.multiple_of(step * rows, rows), rows), :] = src32[...].astype(BF16)

    @pl.when(step >= FFN_INGEST)
    def _():
        x = x_ref[...]
        if has_proj:
            a_ref, w_ref = proj_refs
            x = x + _dot(a_ref[...], w_ref[...])
        xn = _rms(x, g_ref[...]).astype(BF16)
        ab = _dot(xn, wgu_ref[...])
        a = ab[:, 0:dff]
        h = (a * jax.nn.sigmoid(a) * ab[:, dff:2 * dff]).astype(BF16)
        out = x + 0.5 * _dot(h, wd_ref[...])
        if keep_h:
            o_ref[...] = out
        if tail is not None:
            tail(out, step - FFN_INGEST, *tail_in, *tail_out, *tail_scratch)


def _resident(shape, lead=()):
    index = tuple(lead) + (0,) * len(shape)
    return pl.BlockSpec((None,) * len(lead) + tuple(shape), lambda *_: index, pipeline_mode=pl.Buffered(1))


FFN_TM = 512
FFN_INGEST = 8


def _ffn(h, g_all, w_gu_all, w_d_all, lead, proj=None, tail=None, keep_h=True):
    tm = FFN_TM
    t, d = h.shape
    dff = w_d_all.shape[-2]
    assert d % FFN_INGEST == 0 and dff % (16 * FFN_INGEST) == 0

    def tiled(shape, fn):
        return pl.BlockSpec(shape, lambda i: fn(jnp.maximum(i - FFN_INGEST, 0)))

    def chunked(rows, cols):
        return pl.BlockSpec((None, None, rows, cols), lambda i: (*lead, jnp.minimum(i, FFN_INGEST - 1), 0))

    in_specs = [
        tiled((tm, d), lambda tile: (tile, 0)),
        _resident((1, d), lead),
        chunked(d // FFN_INGEST, 2 * dff),
        chunked(dff // FFN_INGEST, d),
    ]
    args = [h, g_all, w_gu_all, w_d_all]
    if proj is not None:
        a, w = proj
        in_specs += [tiled((tm, a.shape[1]), lambda tile: (tile, 0)), _resident(w.shape)]
        args += [a, w]
    out_specs = [tiled((tm, d), lambda tile: (tile, 0))] if keep_h else []
    out_shape = [jax.ShapeDtypeStruct((t, d), F32)] if keep_h else []
    scratch = [pltpu.VMEM((d, 2 * dff), BF16), pltpu.VMEM((dff, d), BF16)]
    tail_fn, n_tail_in, n_tail_out = None, 0, 0
    if tail is not None:
        tail_fn, tail_args, tail_in_specs, tail_out_blocks, tail_out_shapes, tail_scratch = tail
        in_specs += tail_in_specs
        args += tail_args
        n_tail_in, n_tail_out = len(tail_args), len(tail_out_blocks)
        out_specs += [tiled(shape, fn) for shape, fn in tail_out_blocks]
        out_shape += tail_out_shapes
        scratch += tail_scratch
    outs = pl.pallas_call(
        functools.partial(_ffn_kernel, dff=dff, has_proj=proj is not None, n_tail_in=n_tail_in,
                          n_tail_out=n_tail_out, keep_h=keep_h, tail=tail_fn),
        grid=(FFN_INGEST + t // tm,),
        in_specs=in_specs,
        out_specs=out_specs,
        out_shape=out_shape,
        scratch_shapes=scratch,
        compiler_params=_cparams(("arbitrary",)),
        name="ffn",
    )(*args)
    return outs[0] if len(outs) == 1 else outs


CARRY = 8


def _conv_tail_kernel(x, tile, g_ref, win_ref, cw_ref, wout_ref, o_ref, vbuf_ref, *, nj):
    tm, d = x.shape

    @pl.when(tile % nj == 0)
    def _():
        vbuf_ref[0:CARRY, :] = jnp.zeros((CARRY, d), F32)

    xn = _rms(x, g_ref[...]).astype(BF16)
    proj = _dot(xn, win_ref[...])
    bg = proj[:, 0:d]
    v = proj[:, d:2 * d] * proj[:, 2 * d:3 * d]
    vbuf_ref[CARRY:CARRY + tm, :] = v
    cw = cw_ref[...]
    conv = (cw[0:1, :] * vbuf_ref[CARRY - 2:CARRY - 2 + tm, :]
            + cw[1:2, :] * vbuf_ref[CARRY - 1:CARRY - 1 + tm, :]
            + cw[2:3, :] * v)
    y = (bg * conv).astype(BF16)
    o_ref[...] = x + _dot(y, wout_ref[...])
    vbuf_ref[0:CARRY, :] = vbuf_ref[tm:tm + CARRY, :]


def _conv_tail(b, s, g, w_in, conv_w, w_out):
    tm = FFN_TM
    d = w_out.shape[0]
    assert s % tm == 0
    return (
        functools.partial(_conv_tail_kernel, nj=s // tm),
        [g.reshape(1, d), w_in, conv_w, w_out],
        [_resident((1, d)), _resident(w_in.shape), _resident(conv_w.shape), _resident(w_out.shape)],
        [((tm, d), lambda tile: (tile, 0))],
        [jax.ShapeDtypeStruct((b * s, d), F32)],
        [pltpu.VMEM((CARRY + tm, d), F32)],
    )


def _kv_tail_kernel(x, tile, g_ref, w_ref, kn_ref, seg_ref, segt_ref, blk_ref, ks_ref, vst_ref, kw_ref, vwt_ref,
                    raw_ref, *, nj):
    tm = x.shape[0]
    j = tile % nj
    xn = _rms(x, g_ref[...]).astype(BF16)
    y = _dot(xn, w_ref[...])
    gw = N_KV_GROUPS * HEAD_DIM
    per = LANES // HEAD_DIM
    for slab in range(raw_ref.shape[0]):
        raw_ref[slab] = y[:, slab * LANES:(slab + 1) * LANES]
    for l in range(CMP_BLOCK):
        for slab in range(raw_ref.shape[0]):
            rows = raw_ref[slab, pl.ds(l, tm // CMP_BLOCK, stride=CMP_BLOCK), :]
            for k in range(per):
                sg = slab * per + k
                blk_ref[sg // N_KV_GROUPS, 0, sg % N_KV_GROUPS, :, l * HEAD_DIM:(l + 1) * HEAD_DIM] = (
                    rows[:, k * HEAD_DIM:(k + 1) * HEAD_DIM])
    pos = j * tm + lax.broadcasted_iota(jnp.int32, (tm, HEAD_DIM), 0)
    lane = lax.broadcasted_iota(jnp.int32, (tm, HEAD_DIM), 1)
    onehot = jnp.where(lane == pos // SEL_BLOCK, 1.0, 0.0).astype(BF16)
    k_sel = _seg_rms(y[:, 2 * gw:3 * gw], seg_ref[...], segt_ref[...], kn_ref[0:1, :]).astype(BF16)
    k_win = _seg_rms(y[:, 4 * gw:5 * gw], seg_ref[...], segt_ref[...], kn_ref[1:2, :]).astype(BF16)
    for grp in range(N_KV_GROUPS):
        ks_ref[0, grp, :, 0:HEAD_DIM] = k_sel[:, grp * HEAD_DIM:(grp + 1) * HEAD_DIM]
        ks_ref[0, grp, :, HEAD_DIM:2 * HEAD_DIM] = onehot
        kw_ref[0, grp] = k_win[:, grp * HEAD_DIM:(grp + 1) * HEAD_DIM]
    extra = jnp.where(lax.broadcasted_iota(jnp.int32, (V_ROWS - HEAD_DIM, CHUNK), 0) == 0, 1.0, 0.0).astype(BF16)
    for slot, ref in ((3, vst_ref), (5, vwt_ref)):
        vt = y[:, slot * gw:(slot + 1) * gw].T
        for grp in range(N_KV_GROUPS):
            for cc in range(tm // CHUNK):
                ref[0, grp, cc, 0:HEAD_DIM, :] = vt[grp * HEAD_DIM:(grp + 1) * HEAD_DIM,
                                                    cc * CHUNK:(cc + 1) * CHUNK].astype(BF16)
                ref[0, grp, cc, HEAD_DIM:V_ROWS, :] = extra


def _kv_tail(b, s, g, w, k_norm):
    tm = FFN_TM
    d, n = w.shape
    ng = N_KV_GROUPS
    gw = ng * HEAD_DIM
    cpt = tm // CHUNK
    nj = s // tm
    assert s // SEL_BLOCK <= HEAD_DIM and s % tm == 0
    seg, seg_t = _seg_matrices(gw)
    gains = jnp.tile(k_norm[1:3], (1, ng))
    return (
        functools.partial(_kv_tail_kernel, nj=nj),
        [g.reshape(1, d), w, gains, seg, seg_t],
        [_resident((1, d)), _resident((d, n)), _resident(gains.shape), _resident(seg.shape),
         _resident(seg_t.shape)],
        [
            ((2, 1, ng, tm // CMP_BLOCK, CMP_BLOCK * HEAD_DIM), lambda i: (0, i // nj, 0, i % nj, 0)),
            ((1, ng, tm, 2 * HEAD_DIM), lambda i: (i // nj, 0, i % nj, 0)),
            ((1, ng, cpt, V_ROWS, CHUNK), lambda i: (i // nj, 0, i % nj, 0, 0)),
            ((1, ng, tm, HEAD_DIM), lambda i: (i // nj, 0, i % nj, 0)),
            ((1, ng, cpt, V_ROWS, CHUNK), lambda i: (i // nj, 0, i % nj, 0, 0)),
        ],
        [
            jax.ShapeDtypeStruct((2, b, ng, s // CMP_BLOCK, CMP_BLOCK * HEAD_DIM), F32),
            jax.ShapeDtypeStruct((b, ng, s, 2 * HEAD_DIM), BF16),
            jax.ShapeDtypeStruct((b, ng, s // CHUNK, V_ROWS, CHUNK), BF16),
            jax.ShapeDtypeStruct((b, ng, s, HEAD_DIM), BF16),
            jax.ShapeDtypeStruct((b, ng, s // CHUNK, V_ROWS, CHUNK), BF16),
        ],
        [pltpu.VMEM((2 * gw // LANES, tm, LANES), F32)],
    )


def _cmp_kernel(x_ref, pos_ref, w1_ref, b1_ref, w2_ref, kn_ref, o_ref):
    c = pl.program_id(0)
    nb, ne, width = x_ref.shape
    rows = nb * ne
    half = width // 2
    x = x_ref[...].reshape(rows, width)
    pos = pos_ref[...]
    w1 = w1_ref[...]
    h_even = _dot((x + pos).astype(BF16), w1)
    a = _dot((x[:, half:] + pos[:, :half]).astype(BF16), w1[0:half])
    b = _dot((x[:, :half] + pos[:, half:]).astype(BF16), w1[half:width])
    h_odd = a + pltpu.roll(b, rows - 1, 0)

    def tail(hid):
        y = _dot(jax.nn.gelu(hid + b1_ref[...]).astype(BF16), w2_ref[...])
        y = jnp.where(c == 0, _rms(y, kn_ref[...]), y)
        return y.astype(o_ref.dtype).reshape(nb, ne, y.shape[-1])

    o_ref[:, 0:ne, :] = tail(h_even)
    o_ref[:, ne:2 * ne, :] = tail(h_odd)


def _compress(xe, pos, w1, b1, w2, kn, *, nb=8):
    _, nseq, ne, width = xe.shape
    nb = min(nb, nseq)
    hid = w1.shape[-1]
    dk = w2.shape[-1]
    return pl.pallas_call(
        _cmp_kernel,
        grid=(2, nseq // nb),
        in_specs=[
            pl.BlockSpec((None, nb, ne, width), lambda c, i: (c, i, 0, 0)),
            pl.BlockSpec((None, 1, width), lambda c, i: (c, 0, 0)),
            pl.BlockSpec((None, width, hid), lambda c, i: (c, 0, 0)),
            pl.BlockSpec((None, 1, hid), lambda c, i: (c, 0, 0)),
            pl.BlockSpec((None, hid, dk), lambda c, i: (c, 0, 0)),
            pl.BlockSpec((1, dk), lambda c, i: (0, 0)),
        ],
        out_specs=pl.BlockSpec((None, nb, 2 * ne, dk), lambda c, i: (c, i, 0, 0)),
        out_shape=jax.ShapeDtypeStruct((2, nseq, 2 * ne, dk), BF16),
        compiler_params=_cparams(("parallel", "parallel")),
        name="compress",
    )(xe, pos.reshape(2, 1, width), w1, b1.reshape(2, 1, hid), w2, kn.reshape(1, dk))


def _qg_tail_kernel(x, tile, g_ref, w_ref, q_ref, gate_ref):
    dq = q_ref.shape[-1]
    xn = _rms(x, g_ref[...]).astype(BF16)
    y = _dot(xn, w_ref[...])
    q_ref[...] = y[:, 0:dq]
    gate_ref[...] = jax.nn.sigmoid(y[:, dq:])


def _qg_tail(t, g, w):
    tm = FFN_TM
    d, n = w.shape
    dq = N_HEADS * HEAD_DIM
    return (
        _qg_tail_kernel,
        [g.reshape(1, d), w],
        [_resident((1, d)), _resident((d, n))],
        [((tm, dq), lambda tile: (tile, 0)), ((tm, n - dq), lambda tile: (tile, 0))],
        [jax.ShapeDtypeStruct((t, dq), F32), jax.ShapeDtypeStruct((t, n - dq), F32)],
        [],
    )


def _split3(x):
    hi = x.astype(BF16)
    r = x - hi.astype(F32)
    mid = r.astype(BF16)
    lo = (r - mid.astype(F32)).astype(BF16)
    return hi, mid, lo


def _cmp_block_of_row(row, nrows):
    return jnp.where(row < nrows // 2, 2 * row, 2 * row - (nrows - 1))


def _nsa_kernel(*refs, n_cmp, n_blk):
    mask_ref, cmask_ref = refs[-2:]
    tq = CHUNK
    _, ncp, wide = cmask_ref.shape
    i = pl.program_id(1)
    kk = lax.broadcasted_iota(jnp.int32, (tq, wide), 0)
    tt = lax.broadcasted_iota(jnp.int32, (tq, wide), 1) & (tq - 1)
    mask_ref[0] = jnp.where(kk <= tt, 0.0, MASK_SCORE)
    mask_ref[1] = jnp.where(kk > tt + jnp.where(i >= 2, 0, tq), 0.0, MASK_SCORE)
    nidx = _cmp_block_of_row(lax.broadcasted_iota(jnp.int32, (ncp, wide), 0), ncp)
    tpos = i * tq + (lax.broadcasted_iota(jnp.int32, (ncp, wide), 1) & (tq - 1))
    valid = (nidx * CMP_STRIDE + (CMP_BLOCK - 1) <= tpos) & (nidx < n_cmp)
    cmask_ref[0] = jnp.where(valid, 0.0, MASK_SCORE)
    cmask_ref[1] = jnp.where(valid, 1.0, 0.0)

    n_in = 11
    scratch = refs[n_in:-2]
    groups = [_nsa_group(g, *refs[:n_in], *[r.at[g] for r in scratch], mask_ref, cmask_ref, n_blk=n_blk)
              for g in range(N_KV_GROUPS)]
    done = object()

    def advance():
        handed = [None] * len(groups)
        while any(r is None for r in handed):
            for gi, grp in enumerate(groups):
                if handed[gi] is None:
                    handed[gi] = next(grp, done)
        return handed

    bodies = advance()

    def pairs(p, carry):
        for body in bodies:
            body(p)
        return carry

    lax.fori_loop(0, i // 2, pairs, 0)
    assert all(r is done for r in advance())


def _nsa_group(g, q_ref, gate_ref, kc_ref, vct_ref, ks_ref, vst_ref, kw_ref, vwt_ref, ovt_ref, qgain_ref,
               o_ref, qa_ref, m_ref, acc_ref, out_ref, s_ref, cmax_ref, mask_ref, cmask_ref, *, n_blk):
    hpg = HEADS_PER_GROUP
    tq = CHUNK
    dk = HEAD_DIM
    i = pl.program_id(1)
    gd = hpg * dk
    q_lanes = pl.ds(g * gd, gd)
    t0 = i * tq

    def heads(fn):
        return jnp.concatenate([fn(h) for h in range(hpg)], axis=1)

    q_t = q_ref[0, :, q_lanes].T
    q_t = heads(lambda h: q_t[h * dk:(h + 1) * dk, :])
    ms = jnp.sum(q_t * q_t, axis=0, keepdims=True) * (1.0 / dk)
    qa_ref[0:dk, :] = (q_t * lax.rsqrt(ms + EPS) * qgain_ref[...]).astype(BF16)
    gate_t = gate_ref[0, :, pl.ds(g * LANES, LANES)].T
    gates = [heads(lambda h: gate_t[h * N_BRANCH + br:h * N_BRANCH + br + 1, :]) for br in range(N_BRANCH)]

    causal, win_lo = 0, 1
    sel_state, win_state, void_state = 0, 1, 2

    m_ref[...] = jnp.full(m_ref.shape, MASK_SCORE, F32)
    acc_ref[...] = jnp.zeros(acc_ref.shape, F32)

    def key_rows(ref, c):
        return ref[0, g, pl.ds(pl.multiple_of(c * tq, tq), tq), :]

    def produce(b, k, qrows, mask):
        s = _dot(k, qa_ref[0:qrows, :])
        if mask is not None:
            s = s + mask_ref[mask]
        s_ref[b] = s
        cmax_ref[b] = jnp.max(s, axis=0, keepdims=True)

    def consume(b, vt, st):
        m_prev = m_ref[st]
        m_new = jnp.maximum(m_prev, cmax_ref[b])
        alpha = jnp.exp2(m_prev - m_new)
        p = jnp.exp2(s_ref[b] - m_new)
        acc_ref[st] = alpha * acc_ref[st] + _dot(vt, p.astype(BF16))
        m_ref[st] = m_new

    def flash_out(st, branch):
        out_ref[...] += gates[branch] * (acc_ref[st, 0:dk, :] * (1.0 / acc_ref[st, dk:dk + 1, :]))

    assert WINDOW == 2 * tq
    lo = jnp.maximum(i - 2, 0)
    mid = jnp.maximum(i - 1, 0)
    mid_state = jnp.where(i >= 1, win_state, void_state)
    yield
    produce(0, key_rows(kw_ref, lo), dk, win_lo)
    yield

    s = _dot(kc_ref[0, g], qa_ref[0:dk, :]) + cmask_ref[0]
    e = jnp.exp2(s - jnp.max(s, axis=0, keepdims=True))
    p = e * (1.0 / jnp.sum(e, axis=0, keepdims=True)) * cmask_ref[1]
    out_ref[...] = gates[0] * _dot(vct_ref[0, g], p.astype(BF16))
    psum = p[:, 0:tq]
    for h in range(1, hpg):
        psum = psum + p[:, h * tq:(h + 1) * tq]
    yield

    produce(1, key_rows(kw_ref, mid), dk, None)
    consume(0, vwt_ref[0, g, lo], win_state)
    yield

    ovt = ovt_ref[...]
    imp = None
    for part in _split3(psum):
        term = _dot(ovt, part)
        imp = term if imp is None else imp + term
    jblk = lax.broadcasted_iota(jnp.int32, (n_blk, tq), 0)
    cur = (t0 + lax.broadcasted_iota(jnp.int32, (n_blk, tq), 1)) // SEL_BLOCK
    forced = (jblk == 0) | (jblk == cur) | (jblk == cur - 1)
    imp = jnp.where(forced, FORCE_SCORE, imp)
    imp = jnp.where(jblk > cur, MASK_SCORE, imp)
    sub = 8
    parts = [imp[r:r + sub] for r in range(0, n_blk, sub)]
    ranks = [jnp.zeros((sub, tq), F32) for _ in parts]
    jsub = lax.broadcasted_iota(jnp.int32, (sub, tq), 0)
    for ii in range(n_blk):
        row = imp[ii:ii + 1, :]
        for gi, x in enumerate(parts):
            if ii < gi * sub:
                beats = row >= x
            elif ii >= (gi + 1) * sub:
                beats = row > x
            else:
                beats = (row > x) | ((row == x) & (jsub > ii - gi * sub))
            ranks[gi] = ranks[gi] + jnp.where(beats, 1.0, 0.0)
    rank = jnp.concatenate(ranks, axis=0)
    chosen = (rank < float(min(N_SELECT, n_blk))) & (jblk <= cur)
    bias = jnp.where(chosen, 0.0, MASK_SCORE)
    bias = jnp.concatenate([bias, jnp.zeros((dk - n_blk, tq), F32)], axis=0).astype(BF16)
    qa_ref[dk:2 * dk, :] = heads(lambda h: bias)
    yield

    produce(0, key_rows(kw_ref, i), dk, causal)
    consume(1, vwt_ref[0, g, mid], mid_state)
    yield

    npair = i // 2
    held = jnp.where(i >= 2, 2 * npair - 1, i)
    last = jnp.where(i % 2 == 0, i + 1, i - 1)
    produce(1, key_rows(ks_ref, i), 2 * dk, causal)
    consume(0, vwt_ref[0, g, i], win_state)
    flash_out(win_state, 2)

    def pair_body(p):
        c = 2 * p
        produce(0, key_rows(ks_ref, c), 2 * dk, None)
        consume(1, vst_ref[0, g, jnp.where(p == 0, i, c - 1)], sel_state)
        produce(1, key_rows(ks_ref, c + 1), 2 * dk, None)
        consume(0, vst_ref[0, g, c], sel_state)

    yield pair_body
    produce(0, key_rows(ks_ref, last), 2 * dk, None)
    consume(1, vst_ref[0, g, held], sel_state)
    yield
    consume(0, vst_ref[0, g, last], sel_state)
    flash_out(sel_state, 1)

    out = out_ref[...]
    out = jnp.concatenate([out[:, h * tq:(h + 1) * tq] for h in range(hpg)], axis=0)
    o_ref[0, :, q_lanes] = out.T.astype(o_ref.dtype)


def _nsa(q, gate, q_norm, kc, vct, ks, vst, kw, vwt):
    b, s, dq = q.shape
    g = N_KV_GROUPS
    hpg = HEADS_PER_GROUP
    dk = HEAD_DIM
    tq = CHUNK
    n_blk = s // SEL_BLOCK
    n_cmp = (s - CMP_BLOCK) // CMP_STRIDE + 1
    ncp = kc.shape[2]
    assert WINDOW == 2 * tq and n_blk <= dk and (s // tq) % 2 == 0
    rows_np = np.arange(ncp)
    cmp_idx = np.where(rows_np < ncp // 2, 2 * rows_np, 2 * rows_np - (ncp - 1))
    cmp_start = cmp_idx * CMP_STRIDE
    blk_start = np.arange(n_blk) * SEL_BLOCK
    ovt = ((cmp_start[None, :] < blk_start[:, None] + SEL_BLOCK)
           & (cmp_start[None, :] + CMP_BLOCK > blk_start[:, None])
           & (cmp_idx[None, :] < n_cmp)).astype(np.float32)
    qgain = jnp.broadcast_to((q_norm * (dk ** -0.5 * LOG2E))[:, None], (dk, hpg * tq))
    idx = lambda bi, i: (bi, 0, 0, 0)
    idx5 = lambda bi, i: (bi, 0, 0, 0, 0)
    return pl.pallas_call(
        functools.partial(_nsa_kernel, n_cmp=n_cmp, n_blk=n_blk),
        grid=(b, s // tq),
        in_specs=[
            pl.BlockSpec((1, tq, dq), lambda bi, i: (bi, i, 0)),
            pl.BlockSpec((1, tq, g * LANES), lambda bi, i: (bi, i, 0)),
            pl.BlockSpec((1, g, ncp, dk), idx),
            pl.BlockSpec((1, g, dk, ncp), idx),
            pl.BlockSpec((1, g, s, 2 * dk), idx),
            pl.BlockSpec((1, g, s // tq, V_ROWS, tq), idx5),
            pl.BlockSpec((1, g, s, dk), idx),
            pl.BlockSpec((1, g, s // tq, V_ROWS, tq), idx5),
            _resident((n_blk, ncp)),
            _resident((dk, hpg * tq)),
        ],
        out_specs=pl.BlockSpec((1, tq, dq), lambda bi, i: (bi, i, 0)),
        out_shape=jax.ShapeDtypeStruct((b, s, dq), BF16),
        scratch_shapes=[
            pltpu.VMEM((g, 2 * dk, hpg * tq), BF16),
            pltpu.VMEM((g, 3, 1, hpg * tq), F32),
            pltpu.VMEM((g, 3, V_ROWS, hpg * tq), F32),
            pltpu.VMEM((g, dk, hpg * tq), F32),
            pltpu.VMEM((g, 2, tq, hpg * tq), F32),
            pltpu.VMEM((g, 2, 1, hpg * tq), F32),
            pltpu.VMEM((2, tq, hpg * tq), F32),
            pltpu.VMEM((2, ncp, hpg * tq), F32),
        ],
        compiler_params=_cparams(("parallel", "arbitrary")),
        name="nsa",
    )(q, gate, kc, vct, ks, vst, kw, vwt, jnp.asarray(ovt, BF16), qgain)


def _gate_columns(w_qg):
    dq = N_HEADS * HEAD_DIM
    per = HEADS_PER_GROUP * N_BRANCH
    wg = w_qg[:, dq:].reshape(-1, N_KV_GROUPS, per)
    wg = jnp.pad(wg, ((0, 0), (0, 0), (0, LANES - per)))
    return jnp.concatenate([w_qg[:, :dq], wg.reshape(-1, N_KV_GROUPS * LANES)], axis=1)


def kernel(x, ffn_norm, ffn_w_gate_up, ffn_w_down, mix_norm, conv_w_in, conv_w, conv_w_out, kv_norm, kv_w,
           cmp_pos, cmp_w1, cmp_b1, cmp_w2, k_norm, nsa_w_qg, q_norm, nsa_w_o):
    b, s, d = x.shape
    t = b * s
    depth = ffn_norm.shape[0]
    n_a = conv_w_in.shape[0]
    g = N_KV_GROUPS

    ffn_g = ffn_norm.reshape(depth, 2, 1, d)

    def ffn(h2, layer, j, proj=None, tail=None, keep_h=True):
        return _ffn(h2, ffn_g, ffn_w_gate_up, ffn_w_down, (layer, j), proj, tail, keep_h)

    h = x.reshape(t, d)
    shared = None
    for layer in range(depth):
        proj = None
        if layer < n_a:
            h = ffn(h, layer, 0, keep_h=False,
                    tail=_conv_tail(b, s, mix_norm[layer], conv_w_in[layer].astype(BF16), conv_w[layer],
                                    conv_w_out[layer].astype(BF16)))
        else:
            li = layer - n_a
            h, q, gate = ffn(h, layer, 0, tail=_qg_tail(t, mix_norm[layer],
                                                        _gate_columns(nsa_w_qg[li]).astype(BF16)))
            o = _nsa(q.reshape(b, s, -1), gate.reshape(b, s, -1), q_norm[li], *shared)
            proj = (o.reshape(t, -1), nsa_w_o[li].astype(BF16))
        if layer == n_a - 1:
            h, blk, ks, vst, kw, vwt = ffn(h, layer, 1, proj,
                                           tail=_kv_tail(b, s, kv_norm, kv_w.astype(BF16), k_norm))
            blk = blk.reshape(2, b * g, s // CMP_BLOCK, CMP_BLOCK * HEAD_DIM)
            cmp = _compress(blk, cmp_pos.reshape(2, -1), cmp_w1.astype(BF16), cmp_b1,
                            cmp_w2.astype(BF16), k_norm[0])
            cmp = cmp.reshape(2, b, g, -1, HEAD_DIM)
            shared = (cmp[0], cmp[1].transpose(0, 1, 3, 2), ks, vst, kw, vwt)
        else:
            h = ffn(h, layer, 1, proj)
    return h.reshape(b, s, d)
```

```python
import functools

import numpy as np
import jax
import jax.numpy as jnp
from jax import lax
from jax.experimental import pallas as pl
from jax.experimental.pallas import tpu as pltpu

EPS = 1e-6
N_HEADS = 16
HEAD_DIM = 64
N_KV_GROUPS = 4
HEADS_PER_GROUP = N_HEADS // N_KV_GROUPS
N_BRANCH = 3
CMP_BLOCK = 32
CMP_STRIDE = 16
SEL_BLOCK = 64
N_SELECT = 16
WINDOW = 512
FORCE_SCORE = 1e9
MASK_SCORE = -1e30
LOG2E = 1.4426950408889634

LANES = 128
SUBLANES = 8
BF16_ROWS = 16
CHUNK = 256
V_ROWS = HEAD_DIM + BF16_ROWS
VMEM_LIMIT = 56 * 1024 * 1024

BF16 = jnp.bfloat16
F32 = jnp.float32


def _cparams(sem):
    return pltpu.CompilerParams(dimension_semantics=sem, vmem_limit_bytes=VMEM_LIMIT)


def _rms(x, g):
    ms = jnp.mean(x * x, axis=-1, keepdims=True)
    return x * lax.rsqrt(ms + EPS) * g


def _dot(a, b):
    return jnp.dot(a, b, preferred_element_type=F32)


def _split2(x):
    hi = x.astype(BF16)
    return hi, (x - hi.astype(F32)).astype(BF16)


def _seg_rms(y, seg, seg_t, gain):
    ss = None
    for part in _split2(y * y):
        term = _dot(part, seg)
        ss = term if ss is None else ss + term
    r = lax.rsqrt(ss * (1.0 / HEAD_DIM) + EPS)
    scale = None
    for part in _split2(r):
        term = _dot(part, seg_t)
        scale = term if scale is None else scale + term
    return y * scale * gain


def _seg_matrices(width):
    ind = (np.arange(width)[:, None] // HEAD_DIM == np.arange(LANES)[None, :]).astype(np.float32)
    return jnp.asarray(ind, BF16), jnp.asarray(ind.T, BF16)


def _ffn_kernel(x_ref, g_ref, wgu32_ref, wd32_ref, *rest, dff, has_proj, n_tail_in, n_tail_out, keep_h, tail):
    step = pl.program_id(0)
    proj_refs, rest = (rest[:2], rest[2:]) if has_proj else ((), rest)
    tail_in, rest = rest[:n_tail_in], rest[n_tail_in:]
    o_ref, rest = (rest[0], rest[1:]) if keep_h else (None, rest)
    tail_out, (wgu_ref, wd_ref), tail_scratch = rest[:n_tail_out], rest[n_tail_out:n_tail_out + 2], rest[n_tail_out + 2:]

    @pl.when(step < FFN_INGEST)
    def _():
        for src32, dst in ((wgu32_ref, wgu_ref), (wd32_ref, wd_ref)):
            rows = src32.shape[0]
            dst[pl.ds(pl.multiple_of(step * rows, rows), rows), :] = src32[...].astype(BF16)

    @pl.when(step >= FFN_INGEST)
    def _():
        x = x_ref[...]
        if has_proj:
            a_ref, w_ref = proj_refs
            x = x + _dot(a_ref[...], w_ref[...])
        xn = _rms(x, g_ref[...]).astype(BF16)
        ab = _dot(xn, wgu_ref[...])
        a = ab[:, 0:dff]
        h = (a * jax.nn.sigmoid(a) * ab[:, dff:2 * dff]).astype(BF16)
        out = x + 0.5 * _dot(h, wd_ref[...])
        if keep_h:
            o_ref[...] = out
        if tail is not None:
            tail(out, step - FFN_INGEST, *tail_in, *tail_out, *tail_scratch)


def _resident(shape, lead=()):
    index = tuple(lead) + (0,) * len(shape)
    return pl.BlockSpec((None,) * len(lead) + tuple(shape), lambda *_: index, pipeline_mode=pl.Buffered(1))


FFN_TM = 512
FFN_INGEST = 8


def _ffn(h, g_all, w_gu_all, w_d_all, lead, proj=None, tail=None, keep_h=True):
    tm = FFN_TM
    t, d = h.shape
    dff = w_d_all.shape[-2]
    assert d % (BF16_ROWS * FFN_INGEST) == 0 and dff % (BF16_ROWS * FFN_INGEST) == 0

    def tiled(shape, fn):
        return pl.BlockSpec(shape, lambda i: fn(jnp.maximum(i - FFN_INGEST, 0)))

    def chunked(rows, cols):
        return pl.BlockSpec((None, None, rows, cols), lambda i: (*lead, jnp.minimum(i, FFN_INGEST - 1), 0))

    in_specs = [
        tiled((tm, d), lambda tile: (tile, 0)),
        _resident((1, d), lead),
        chunked(d // FFN_INGEST, 2 * dff),
        chunked(dff // FFN_INGEST, d),
    ]
    args = [h, g_all, w_gu_all, w_d_all]
    if proj is not None:
        a, w = proj
        in_specs += [tiled((tm, a.shape[1]), lambda tile: (tile, 0)), _resident(w.shape)]
        args += [a, w]
    out_specs = [tiled((tm, d), lambda tile: (tile, 0))] if keep_h else []
    out_shape = [jax.ShapeDtypeStruct((t, d), F32)] if keep_h else []
    scratch = [pltpu.VMEM((d, 2 * dff), BF16), pltpu.VMEM((dff, d), BF16)]
    tail_fn, n_tail_in, n_tail_out = None, 0, 0
    if tail is not None:
        tail_fn, tail_args, tail_in_specs, tail_out_blocks, tail_out_shapes, tail_scratch = tail
        in_specs += tail_in_specs
        args += tail_args
        n_tail_in, n_tail_out = len(tail_args), len(tail_out_blocks)
        out_specs += [tiled(shape, fn) for shape, fn in tail_out_blocks]
        out_shape += tail_out_shapes
        scratch += tail_scratch
    outs = pl.pallas_call(
        functools.partial(_ffn_kernel, dff=dff, has_proj=proj is not None, n_tail_in=n_tail_in,
                          n_tail_out=n_tail_out, keep_h=keep_h, tail=tail_fn),
        grid=(FFN_INGEST + t // tm,),
        in_specs=in_specs,
        out_specs=out_specs,
        out_shape=out_shape,
        scratch_shapes=scratch,
        compiler_params=_cparams(("arbitrary",)),
        name="ffn",
    )(*args)
    return outs[0] if len(outs) == 1 else outs


CARRY = SUBLANES


def _conv_tail_kernel(x, tile, g_ref, win_ref, cw_ref, wout_ref, o_ref, vbuf_ref, *, nj):
    tm, d = x.shape

    @pl.when(tile % nj == 0)
    def _():
        vbuf_ref[0:CARRY, :] = jnp.zeros((CARRY, d), F32)

    xn = _rms(x, g_ref[...]).astype(BF16)
    proj = _dot(xn, win_ref[...])
    bg = proj[:, 0:d]
    v = proj[:, d:2 * d] * proj[:, 2 * d:3 * d]
    vbuf_ref[CARRY:CARRY + tm, :] = v
    cw = cw_ref[...]
    conv = (cw[0:1, :] * vbuf_ref[CARRY - 2:CARRY - 2 + tm, :]
            + cw[1:2, :] * vbuf_ref[CARRY - 1:CARRY - 1 + tm, :]
            + cw[2:3, :] * v)
    y = (bg * conv).astype(BF16)
    o_ref[...] = x + _dot(y, wout_ref[...])
    vbuf_ref[0:CARRY, :] = vbuf_ref[tm:tm + CARRY, :]


def _conv_tail(b, s, g, w_in, conv_w, w_out):
    tm = FFN_TM
    d = w_out.shape[0]
    assert s % tm == 0
    return (
        functools.partial(_conv_tail_kernel, nj=s // tm),
        [g.reshape(1, d), w_in, conv_w, w_out],
        [_resident((1, d)), _resident(w_in.shape), _resident(conv_w.shape), _resident(w_out.shape)],
        [((tm, d), lambda tile: (tile, 0))],
        [jax.ShapeDtypeStruct((b * s, d), F32)],
        [pltpu.VMEM((CARRY + tm, d), F32)],
    )


def _kv_tail_kernel(x, tile, g_ref, w_ref, kn_ref, seg_ref, segt_ref, blk_ref, ks_ref, vst_ref, kw_ref, vwt_ref,
                    raw_ref, *, nj):
    tm = x.shape[0]
    j = tile % nj
    xn = _rms(x, g_ref[...]).astype(BF16)
    y = _dot(xn, w_ref[...])
    gw = N_KV_GROUPS * HEAD_DIM
    per = LANES // HEAD_DIM
    for slab in range(raw_ref.shape[0]):
        raw_ref[slab] = y[:, slab * LANES:(slab + 1) * LANES]
    for l in range(CMP_BLOCK):
        for slab in range(raw_ref.shape[0]):
            rows = raw_ref[slab, pl.ds(l, tm // CMP_BLOCK, stride=CMP_BLOCK), :]
            for k in range(per):
                sg = slab * per + k
                blk_ref[sg // N_KV_GROUPS, 0, sg % N_KV_GROUPS, :, l * HEAD_DIM:(l + 1) * HEAD_DIM] = (
                    rows[:, k * HEAD_DIM:(k + 1) * HEAD_DIM])
    pos = j * tm + lax.broadcasted_iota(jnp.int32, (tm, HEAD_DIM), 0)
    lane = lax.broadcasted_iota(jnp.int32, (tm, HEAD_DIM), 1)
    onehot = jnp.where(lane == pos // SEL_BLOCK, 1.0, 0.0).astype(BF16)
    k_sel = _seg_rms(y[:, 2 * gw:3 * gw], seg_ref[...], segt_ref[...], kn_ref[0:1, :]).astype(BF16)
    k_win = _seg_rms(y[:, 4 * gw:5 * gw], seg_ref[...], segt_ref[...], kn_ref[1:2, :]).astype(BF16)
    for grp in range(N_KV_GROUPS):
        ks_ref[0, grp, :, 0:HEAD_DIM] = k_sel[:, grp * HEAD_DIM:(grp + 1) * HEAD_DIM]
        ks_ref[0, grp, :, HEAD_DIM:2 * HEAD_DIM] = onehot
        kw_ref[0, grp] = k_win[:, grp * HEAD_DIM:(grp + 1) * HEAD_DIM]
    extra = jnp.where(lax.broadcasted_iota(jnp.int32, (V_ROWS - HEAD_DIM, CHUNK), 0) == 0, 1.0, 0.0).astype(BF16)
    for slot, ref in ((3, vst_ref), (5, vwt_ref)):
        vt = y[:, slot * gw:(slot + 1) * gw].T
        for grp in range(N_KV_GROUPS):
            for cc in range(tm // CHUNK):
                ref[0, grp, cc, 0:HEAD_DIM, :] = vt[grp * HEAD_DIM:(grp + 1) * HEAD_DIM,
                                                    cc * CHUNK:(cc + 1) * CHUNK].astype(BF16)
                ref[0, grp, cc, HEAD_DIM:V_ROWS, :] = extra


def _kv_tail(b, s, g, w, k_norm):
    tm = FFN_TM
    d, n = w.shape
    ng = N_KV_GROUPS
    gw = ng * HEAD_DIM
    cpt = tm // CHUNK
    nj = s // tm
    assert s // SEL_BLOCK <= HEAD_DIM and s % tm == 0
    seg, seg_t = _seg_matrices(gw)
    gains = jnp.tile(k_norm[1:3], (1, ng))
    return (
        functools.partial(_kv_tail_kernel, nj=nj),
        [g.reshape(1, d), w, gains, seg, seg_t],
        [_resident((1, d)), _resident((d, n)), _resident(gains.shape), _resident(seg.shape),
         _resident(seg_t.shape)],
        [
            ((2, 1, ng, tm // CMP_BLOCK, CMP_BLOCK * HEAD_DIM), lambda i: (0, i // nj, 0, i % nj, 0)),
            ((1, ng, tm, 2 * HEAD_DIM), lambda i: (i // nj, 0, i % nj, 0)),
            ((1, ng, cpt, V_ROWS, CHUNK), lambda i: (i // nj, 0, i % nj, 0, 0)),
            ((1, ng, tm, HEAD_DIM), lambda i: (i // nj, 0, i % nj, 0)),
            ((1, ng, cpt, V_ROWS, CHUNK), lambda i: (i // nj, 0, i % nj, 0, 0)),
        ],
        [
            jax.ShapeDtypeStruct((2, b, ng, s // CMP_BLOCK, CMP_BLOCK * HEAD_DIM), F32),
            jax.ShapeDtypeStruct((b, ng, s, 2 * HEAD_DIM), BF16),
            jax.ShapeDtypeStruct((b, ng, s // CHUNK, V_ROWS, CHUNK), BF16),
            jax.ShapeDtypeStruct((b, ng, s, HEAD_DIM), BF16),
            jax.ShapeDtypeStruct((b, ng, s // CHUNK, V_ROWS, CHUNK), BF16),
        ],
        [pltpu.VMEM((2 * gw // LANES, tm, LANES), F32)],
    )


def _cmp_kernel(x_ref, pos_ref, w1_ref, b1_ref, w2_ref, kn_ref, o_ref):
    c = pl.program_id(0)
    nb, ne, width = x_ref.shape
    rows = nb * ne
    half = width // 2
    x = x_ref[...].reshape(rows, width)
    pos = pos_ref[...]
    w1 = w1_ref[...]
    h_even = _dot((x + pos).astype(BF16), w1)
    a = _dot((x[:, half:] + pos[:, :half]).astype(BF16), w1[0:half])
    b = _dot((x[:, :half] + pos[:, half:]).astype(BF16), w1[half:width])
    h_odd = a + pltpu.roll(b, rows - 1, 0)

    def tail(hid):
        y = _dot(jax.nn.gelu(hid + b1_ref[...]).astype(BF16), w2_ref[...])
        y = jnp.where(c == 0, _rms(y, kn_ref[...]), y)
        return y.astype(o_ref.dtype).reshape(nb, ne, y.shape[-1])

    o_ref[:, 0:ne, :] = tail(h_even)
    o_ref[:, ne:2 * ne, :] = tail(h_odd)


def _compress(xe, pos, w1, b1, w2, kn, *, nb=8):
    _, nseq, ne, width = xe.shape
    nb = min(nb, nseq)
    hid = w1.shape[-1]
    dk = w2.shape[-1]
    return pl.pallas_call(
        _cmp_kernel,
        grid=(2, nseq // nb),
        in_specs=[
            pl.BlockSpec((None, nb, ne, width), lambda c, i: (c, i, 0, 0)),
            pl.BlockSpec((None, 1, width), lambda c, i: (c, 0, 0)),
            pl.BlockSpec((None, width, hid), lambda c, i: (c, 0, 0)),
            pl.BlockSpec((None, 1, hid), lambda c, i: (c, 0, 0)),
            pl.BlockSpec((None, hid, dk), lambda c, i: (c, 0, 0)),
            pl.BlockSpec((1, dk), lambda c, i: (0, 0)),
        ],
        out_specs=pl.BlockSpec((None, nb, 2 * ne, dk), lambda c, i: (c, i, 0, 0)),
        out_shape=jax.ShapeDtypeStruct((2, nseq, 2 * ne, dk), BF16),
        compiler_params=_cparams(("parallel", "parallel")),
        name="compress",
    )(xe, pos.reshape(2, 1, width), w1, b1.reshape(2, 1, hid), w2, kn.reshape(1, dk))


def _qg_tail_kernel(x, tile, g_ref, w_ref, q_ref, gate_ref):
    dq = q_ref.shape[-1]
    xn = _rms(x, g_ref[...]).astype(BF16)
    y = _dot(xn, w_ref[...])
    q_ref[...] = y[:, 0:dq]
    gate_ref[...] = jax.nn.sigmoid(y[:, dq:])


def _qg_tail(t, g, w):
    tm = FFN_TM
    d, n = w.shape
    dq = N_HEADS * HEAD_DIM
    return (
        _qg_tail_kernel,
        [g.reshape(1, d), w],
        [_resident((1, d)), _resident((d, n))],
        [((tm, dq), lambda tile: (tile, 0)), ((tm, n - dq), lambda tile: (tile, 0))],
        [jax.ShapeDtypeStruct((t, dq), F32), jax.ShapeDtypeStruct((t, n - dq), F32)],
        [],
    )


def _split3(x):
    hi = x.astype(BF16)
    r = x - hi.astype(F32)
    mid = r.astype(BF16)
    lo = (r - mid.astype(F32)).astype(BF16)
    return hi, mid, lo


def _cmp_block_of_row(row, nrows):
    return jnp.where(row < nrows // 2, 2 * row, 2 * row - (nrows - 1))


def _nsa_kernel(*refs, n_io, n_cmp, n_blk):
    mask_ref, cmask_ref = refs[-2:]
    tq = CHUNK
    _, ncp, wide = cmask_ref.shape
    i = pl.program_id(1)
    kk = lax.broadcasted_iota(jnp.int32, (tq, wide), 0)
    tt = lax.broadcasted_iota(jnp.int32, (tq, wide), 1) & (tq - 1)
    mask_ref[0] = jnp.where(kk <= tt, 0.0, MASK_SCORE)
    mask_ref[1] = jnp.where(kk > tt + jnp.where(i >= 2, 0, tq), 0.0, MASK_SCORE)
    nidx = _cmp_block_of_row(lax.broadcasted_iota(jnp.int32, (ncp, wide), 0), ncp)
    tpos = i * tq + (lax.broadcasted_iota(jnp.int32, (ncp, wide), 1) & (tq - 1))
    valid = (nidx * CMP_STRIDE + (CMP_BLOCK - 1) <= tpos) & (nidx < n_cmp)
    cmask_ref[0] = jnp.where(valid, 0.0, MASK_SCORE)
    cmask_ref[1] = jnp.where(valid, 1.0, 0.0)

    scratch = refs[n_io:-2]
    groups = [_nsa_group(g, *refs[:n_io], *[r.at[g] for r in scratch], mask_ref, cmask_ref, n_blk=n_blk)
              for g in range(N_KV_GROUPS)]
    done = object()

    def advance():
        handed = [None] * len(groups)
        while any(r is None for r in handed):
            for gi, grp in enumerate(groups):
                if handed[gi] is None:
                    handed[gi] = next(grp, done)
        return handed

    bodies = advance()

    def pairs(p, carry):
        for body in bodies:
            body(p)
        return carry

    lax.fori_loop(0, i // 2, pairs, 0)
    assert all(r is done for r in advance())


def _nsa_group(g, q_ref, gate_ref, kc_ref, vct_ref, ks_ref, vst_ref, kw_ref, vwt_ref, ovt_ref, qgain_ref,
               o_ref, qa_ref, m_ref, acc_ref, out_ref, s_ref, cmax_ref, mask_ref, cmask_ref, *, n_blk):
    hpg = HEADS_PER_GROUP
    tq = CHUNK
    dk = HEAD_DIM
    i = pl.program_id(1)
    gd = hpg * dk
    q_lanes = pl.ds(g * gd, gd)
    t0 = i * tq

    def heads(fn):
        return jnp.concatenate([fn(h) for h in range(hpg)], axis=1)

    q_t = q_ref[0, :, q_lanes].T
    q_t = heads(lambda h: q_t[h * dk:(h + 1) * dk, :])
    ms = jnp.sum(q_t * q_t, axis=0, keepdims=True) * (1.0 / dk)
    qa_ref[0:dk, :] = (q_t * lax.rsqrt(ms + EPS) * qgain_ref[...]).astype(BF16)
    gate_t = gate_ref[0, :, pl.ds(g * LANES, LANES)].T
    gates = [heads(lambda h: gate_t[h * N_BRANCH + br:h * N_BRANCH + br + 1, :]) for br in range(N_BRANCH)]

    causal, win_lo = 0, 1
    sel_state, win_state, void_state = 0, 1, 2

    m_ref[...] = jnp.full(m_ref.shape, MASK_SCORE, F32)
    acc_ref[...] = jnp.zeros(acc_ref.shape, F32)

    def key_rows(ref, c):
        return ref[0, g, pl.ds(pl.multiple_of(c * tq, tq), tq), :]

    def produce(b, k, qrows, mask):
        s = _dot(k, qa_ref[0:qrows, :])
        if mask is not None:
            s = s + mask_ref[mask]
        s_ref[b] = s
        cmax_ref[b] = jnp.max(s, axis=0, keepdims=True)

    def consume(b, vt, st):
        m_prev = m_ref[st]
        m_new = jnp.maximum(m_prev, cmax_ref[b])
        alpha = jnp.exp2(m_prev - m_new)
        p = jnp.exp2(s_ref[b] - m_new)
        acc_ref[st] = alpha * acc_ref[st] + _dot(vt, p.astype(BF16))
        m_ref[st] = m_new

    def flash_out(st, branch):
        out_ref[...] += gates[branch] * (acc_ref[st, 0:dk, :] * (1.0 / acc_ref[st, dk:dk + 1, :]))

    assert WINDOW == 2 * tq
    lo = jnp.maximum(i - 2, 0)
    mid = jnp.maximum(i - 1, 0)
    mid_state = jnp.where(i >= 1, win_state, void_state)
    yield
    produce(0, key_rows(kw_ref, lo), dk, win_lo)
    yield
    produce(1, key_rows(kw_ref, mid), dk, None)
    consume(0, vwt_ref[0, g, lo], win_state)
    yield
    produce(0, key_rows(kw_ref, i), dk, causal)
    consume(1, vwt_ref[0, g, mid], mid_state)
    yield

    s = _dot(kc_ref[0, g], qa_ref[0:dk, :]) + cmask_ref[0]
    e = jnp.exp2(s - jnp.max(s, axis=0, keepdims=True))
    p = e * (1.0 / jnp.sum(e, axis=0, keepdims=True)) * cmask_ref[1]
    out_ref[...] = gates[0] * _dot(vct_ref[0, g], p.astype(BF16))
    psum = p[:, 0:tq]
    for h in range(1, hpg):
        psum = psum + p[:, h * tq:(h + 1) * tq]
    yield

    ovt = ovt_ref[...]
    imp = None
    for part in _split3(psum):
        term = _dot(ovt, part)
        imp = term if imp is None else imp + term
    jblk = lax.broadcasted_iota(jnp.int32, (n_blk, tq), 0)
    cur = (t0 + lax.broadcasted_iota(jnp.int32, (n_blk, tq), 1)) // SEL_BLOCK
    forced = (jblk == 0) | (jblk == cur) | (jblk == cur - 1)
    imp = jnp.where(forced, FORCE_SCORE, imp)
    imp = jnp.where(jblk > cur, MASK_SCORE, imp)
    sub = SUBLANES
    parts = [imp[r:r + sub] for r in range(0, n_blk, sub)]
    ranks = [jnp.zeros((sub, tq), F32) for _ in parts]
    jsub = lax.broadcasted_iota(jnp.int32, (sub, tq), 0)
    for ii in range(n_blk):
        row = imp[ii:ii + 1, :]
        for gi, x in enumerate(parts):
            if ii < gi * sub:
                beats = row >= x
            elif ii >= (gi + 1) * sub:
                beats = row > x
            else:
                beats = (row > x) | ((row == x) & (jsub > ii - gi * sub))
            ranks[gi] = ranks[gi] + jnp.where(beats, 1.0, 0.0)
    rank = jnp.concatenate(ranks, axis=0)
    chosen = (rank < float(min(N_SELECT, n_blk))) & (jblk <= cur)
    bias = jnp.where(chosen, 0.0, MASK_SCORE)
    bias = jnp.concatenate([bias, jnp.zeros((dk - n_blk, tq), F32)], axis=0).astype(BF16)
    qa_ref[dk:2 * dk, :] = heads(lambda h: bias)
    yield

    npair = i // 2
    held = jnp.where(i >= 2, 2 * npair - 1, i)
    last = jnp.where(i % 2 == 0, i + 1, i - 1)
    produce(1, key_rows(ks_ref, i), 2 * dk, causal)
    consume(0, vwt_ref[0, g, i], win_state)
    flash_out(win_state, 2)

    def pair_body(p):
        c = 2 * p
        produce(0, key_rows(ks_ref, c), 2 * dk, None)
        consume(1, vst_ref[0, g, jnp.where(p == 0, i, c - 1)], sel_state)
        produce(1, key_rows(ks_ref, c + 1), 2 * dk, None)
        consume(0, vst_ref[0, g, c], sel_state)

    yield pair_body
    produce(0, key_rows(ks_ref, last), 2 * dk, None)
    consume(1, vst_ref[0, g, held], sel_state)
    yield
    consume(0, vst_ref[0, g, last], sel_state)
    flash_out(sel_state, 1)

    out = out_ref[...]
    out = jnp.concatenate([out[:, h * tq:(h + 1) * tq] for h in range(hpg)], axis=0)
    o_ref[0, :, q_lanes] = out.T.astype(o_ref.dtype)


def _nsa(q, gate, q_norm, kc, vct, ks, vst, kw, vwt):
    b, s, dq = q.shape
    g = N_KV_GROUPS
    hpg = HEADS_PER_GROUP
    dk = HEAD_DIM
    tq = CHUNK
    n_blk = s // SEL_BLOCK
    n_cmp = (s - CMP_BLOCK) // CMP_STRIDE + 1
    ncp = kc.shape[2]
    assert WINDOW == 2 * tq and n_blk <= dk and (s // tq) % 2 == 0
    rows_np = np.arange(ncp)
    cmp_idx = np.where(rows_np < ncp // 2, 2 * rows_np, 2 * rows_np - (ncp - 1))
    cmp_start = cmp_idx * CMP_STRIDE
    blk_start = np.arange(n_blk) * SEL_BLOCK
    ovt = ((cmp_start[None, :] < blk_start[:, None] + SEL_BLOCK)
           & (cmp_start[None, :] + CMP_BLOCK > blk_start[:, None])
           & (cmp_idx[None, :] < n_cmp)).astype(np.float32)
    qgain = jnp.broadcast_to((q_norm * (dk ** -0.5 * LOG2E))[:, None], (dk, hpg * tq))
    idx = lambda bi, i: (bi, 0, 0, 0)
    idx5 = lambda bi, i: (bi, 0, 0, 0, 0)
    args = (q, gate, kc, vct, ks, vst, kw, vwt, jnp.asarray(ovt, BF16), qgain)
    return pl.pallas_call(
        functools.partial(_nsa_kernel, n_io=len(args) + 1, n_cmp=n_cmp, n_blk=n_blk),
        grid=(b, s // tq),
        in_specs=[
            pl.BlockSpec((1, tq, dq), lambda bi, i: (bi, i, 0)),
            pl.BlockSpec((1, tq, g * LANES), lambda bi, i: (bi, i, 0)),
            pl.BlockSpec((1, g, ncp, dk), idx),
            pl.BlockSpec((1, g, dk, ncp), idx),
            pl.BlockSpec((1, g, s, 2 * dk), idx),
            pl.BlockSpec((1, g, s // tq, V_ROWS, tq), idx5),
            pl.BlockSpec((1, g, s, dk), idx),
            pl.BlockSpec((1, g, s // tq, V_ROWS, tq), idx5),
            _resident((n_blk, ncp)),
            _resident((dk, hpg * tq)),
        ],
        out_specs=pl.BlockSpec((1, tq, dq), lambda bi, i: (bi, i, 0)),
        out_shape=jax.ShapeDtypeStruct((b, s, dq), BF16),
        scratch_shapes=[
            pltpu.VMEM((g, 2 * dk, hpg * tq), BF16),
            pltpu.VMEM((g, 3, 1, hpg * tq), F32),
            pltpu.VMEM((g, 3, V_ROWS, hpg * tq), F32),
            pltpu.VMEM((g, dk, hpg * tq), F32),
            pltpu.VMEM((g, 2, tq, hpg * tq), F32),
            pltpu.VMEM((g, 2, 1, hpg * tq), F32),
            pltpu.VMEM((2, tq, hpg * tq), F32),
            pltpu.VMEM((2, ncp, hpg * tq), F32),
        ],
        compiler_params=_cparams(("parallel", "arbitrary")),
        name="nsa",
    )(*args)


def _gate_columns(w_qg):
    dq = N_HEADS * HEAD_DIM
    per = HEADS_PER_GROUP * N_BRANCH
    wg = w_qg[:, dq:].reshape(-1, N_KV_GROUPS, per)
    wg = jnp.pad(wg, ((0, 0), (0, 0), (0, LANES - per)))
    return jnp.concatenate([w_qg[:, :dq], wg.reshape(-1, N_KV_GROUPS * LANES)], axis=1)


def kernel(x, ffn_norm, ffn_w_gate_up, ffn_w_down, mix_norm, conv_w_in, conv_w, conv_w_out, kv_norm, kv_w,
           cmp_pos, cmp_w1, cmp_b1, cmp_w2, k_norm, nsa_w_qg, q_norm, nsa_w_o):
    b, s, d = x.shape
    t = b * s
    depth = ffn_norm.shape[0]
    n_a = conv_w_in.shape[0]
    g = N_KV_GROUPS

    ffn_g = ffn_norm.reshape(depth, 2, 1, d)

    def ffn(h2, layer, j, proj=None, tail=None, keep_h=True):
        return _ffn(h2, ffn_g, ffn_w_gate_up, ffn_w_down, (layer, j), proj, tail, keep_h)

    h = x.reshape(t, d)
    shared = None
    for layer in range(depth):
        proj = None
        if layer < n_a:
            h = ffn(h, layer, 0, keep_h=False,
                    tail=_conv_tail(b, s, mix_norm[layer], conv_w_in[layer].astype(BF16), conv_w[layer],
                                    conv_w_out[layer].astype(BF16)))
        else:
            li = layer - n_a
            h, q, gate = ffn(h, layer, 0, tail=_qg_tail(t, mix_norm[layer],
                                                        _gate_columns(nsa_w_qg[li]).astype(BF16)))
            o = _nsa(q.reshape(b, s, -1), gate.reshape(b, s, -1), q_norm[li], *shared)
            proj = (o.reshape(t, -1), nsa_w_o[li].astype(BF16))
        if layer == n_a - 1:
            h, blk, ks, vst, kw, vwt = ffn(h, layer, 1, proj,
                                           tail=_kv_tail(b, s, kv_norm, kv_w.astype(BF16), k_norm))
            blk = blk.reshape(2, b * g, s // CMP_BLOCK, CMP_BLOCK * HEAD_DIM)
            cmp = _compress(blk, cmp_pos.reshape(2, -1), cmp_w1.astype(BF16), cmp_b1,
                            cmp_w2.astype(BF16), k_norm[0])
            cmp = cmp.reshape(2, b, g, -1, HEAD_DIM)
            shared = (cmp[0], cmp[1].transpose(0, 1, 3, 2), ks, vst, kw, vwt)
        else:
            h = ffn(h, layer, 1, proj)
    return h.reshape(b, s, d)
```

```python
import functools

import numpy as np
import jax
import jax.numpy as jnp
from jax import lax
from jax.experimental import pallas as pl
from jax.experimental.pallas import tpu as pltpu

EPS = 1e-6
N_HEADS = 16
HEAD_DIM = 64
N_KV_GROUPS = 4
HEADS_PER_GROUP = N_HEADS // N_KV_GROUPS
N_BRANCH = 3
CMP_BLOCK = 32
CMP_STRIDE = 16
SEL_BLOCK = 64
N_SELECT = 16
WINDOW = 512
FORCE_SCORE = 1e9
MASK_SCORE = -1e30
LOG2E = 1.4426950408889634

LANES = 128
SUBLANES = 8
BF16_ROWS = 16
CHUNK = 256
V_ROWS = HEAD_DIM + BF16_ROWS
VMEM_LIMIT = 56 * 1024 * 1024

BF16 = jnp.bfloat16
F32 = jnp.float32


def _cparams(sem):
    return pltpu.CompilerParams(dimension_semantics=sem, vmem_limit_bytes=VMEM_LIMIT)


def _rms(x, g):
    ms = jnp.mean(x * x, axis=-1, keepdims=True)
    return x * lax.rsqrt(ms + EPS) * g


def _dot(a, b):
    return jnp.dot(a, b, preferred_element_type=F32)


def _split2(x):
    hi = x.astype(BF16)
    return hi, (x - hi.astype(F32)).astype(BF16)


def _seg_rms(y, seg, seg_t, gain):
    ss = None
    for part in _split2(y * y):
        term = _dot(part, seg)
        ss = term if ss is None else ss + term
    r = lax.rsqrt(ss * (1.0 / HEAD_DIM) + EPS)
    scale = None
    for part in _split2(r):
        term = _dot(part, seg_t)
        scale = term if scale is None else scale + term
    return y * scale * gain


def _seg_matrices(width):
    ind = (np.arange(width)[:, None] // HEAD_DIM == np.arange(LANES)[None, :]).astype(np.float32)
    return jnp.asarray(ind, BF16), jnp.asarray(ind.T, BF16)


def _ffn_kernel(x_ref, g_ref, wgu32_ref, wd32_ref, *rest, dff, has_proj, n_tail_in, n_tail_out, keep_h, tail):
    step = pl.program_id(0)
    proj_refs, rest = (rest[:2], rest[2:]) if has_proj else ((), rest)
    tail_in, rest = rest[:n_tail_in], rest[n_tail_in:]
    o_ref, rest = (rest[0], rest[1:]) if keep_h else (None, rest)
    tail_out, (wgu_ref, wd_ref), tail_scratch = rest[:n_tail_out], rest[n_tail_out:n_tail_out + 2], rest[n_tail_out + 2:]

    @pl.when(step < FFN_INGEST)
    def _():
        for src32, dst in ((wgu32_ref, wgu_ref), (wd32_ref, wd_ref)):
            rows = src32.shape[0]
            dst[pl.ds(pl.multiple_of(step * rows, rows), rows), :] = src32[...].astype(BF16)

    @pl.when(step >= FFN_INGEST)
    def _():
        x = x_ref[...]
        if has_proj:
            a_ref, w_ref = proj_refs
            x = x + _dot(a_ref[...], w_ref[...])
        xn = _rms(x, g_ref[...]).astype(BF16)
        ab = _dot(xn, wgu_ref[...])
        a = ab[:, 0:dff]
        h = (a * jax.nn.sigmoid(a) * ab[:, dff:2 * dff]).astype(BF16)
        out = x + 0.5 * _dot(h, wd_ref[...])
        if keep_h:
            o_ref[...] = out
        if tail is not None:
            tail(out, step - FFN_INGEST, *tail_in, *tail_out, *tail_scratch)


def _resident(shape, lead=()):
    index = tuple(lead) + (0,) * len(shape)
    return pl.BlockSpec((None,) * len(lead) + tuple(shape), lambda *_: index, pipeline_mode=pl.Buffered(1))


FFN_TM = 512
FFN_INGEST = 8


def _ffn(h, g_all, w_gu_all, w_d_all, lead, proj=None, tail=None, keep_h=True):
    tm = FFN_TM
    t, d = h.shape
    dff = w_d_all.shape[-2]
    assert d % (BF16_ROWS * FFN_INGEST) == 0 and dff % (BF16_ROWS * FFN_INGEST) == 0

    def tiled(shape, fn):
        return pl.BlockSpec(shape, lambda i: fn(jnp.maximum(i - FFN_INGEST, 0)))

    def chunked(rows, cols):
        return pl.BlockSpec((None, None, rows, cols), lambda i: (*lead, jnp.minimum(i, FFN_INGEST - 1), 0))

    in_specs = [
        tiled((tm, d), lambda tile: (tile, 0)),
        _resident((1, d), lead),
        chunked(d // FFN_INGEST, 2 * dff),
        chunked(dff // FFN_INGEST, d),
    ]
    args = [h, g_all, w_gu_all, w_d_all]
    if proj is not None:
        a, w = proj
        in_specs += [tiled((tm, a.shape[1]), lambda tile: (tile, 0)), _resident(w.shape)]
        args += [a, w]
    out_specs = [tiled((tm, d), lambda tile: (tile, 0))] if keep_h else []
    out_shape = [jax.ShapeDtypeStruct((t, d), F32)] if keep_h else []
    scratch = [pltpu.VMEM((d, 2 * dff), BF16), pltpu.VMEM((dff, d), BF16)]
    tail_fn, n_tail_in, n_tail_out = None, 0, 0
    if tail is not None:
        tail_fn, tail_args, tail_in_specs, tail_out_blocks, tail_out_shapes, tail_scratch = tail
        in_specs += tail_in_specs
        args += tail_args
        n_tail_in, n_tail_out = len(tail_args), len(tail_out_blocks)
        out_specs += [tiled(shape, fn) for shape, fn in tail_out_blocks]
        out_shape += tail_out_shapes
        scratch += tail_scratch
    outs = pl.pallas_call(
        functools.partial(_ffn_kernel, dff=dff, has_proj=proj is not None, n_tail_in=n_tail_in,
                          n_tail_out=n_tail_out, keep_h=keep_h, tail=tail_fn),
        grid=(FFN_INGEST + t // tm,),
        in_specs=in_specs,
        out_specs=out_specs,
        out_shape=out_shape,
        scratch_shapes=scratch,
        compiler_params=_cparams(("arbitrary",)),
        name="ffn",
    )(*args)
    return outs[0] if len(outs) == 1 else outs


CARRY = SUBLANES


def _conv_tail_kernel(x, tile, g_ref, win_ref, cw_ref, wout_ref, o_ref, vbuf_ref, *, nj):
    tm, d = x.shape

    @pl.when(tile % nj == 0)
    def _():
        vbuf_ref[0:CARRY, :] = jnp.zeros((CARRY, d), F32)

    xn = _rms(x, g_ref[...]).astype(BF16)
    proj = _dot(xn, win_ref[...])
    bg = proj[:, 0:d]
    v = proj[:, d:2 * d] * proj[:, 2 * d:3 * d]
    vbuf_ref[CARRY:CARRY + tm, :] = v
    cw = cw_ref[...]
    conv = (cw[0:1, :] * vbuf_ref[CARRY - 2:CARRY - 2 + tm, :]
            + cw[1:2, :] * vbuf_ref[CARRY - 1:CARRY - 1 + tm, :]
            + cw[2:3, :] * v)
    y = (bg * conv).astype(BF16)
    o_ref[...] = x + _dot(y, wout_ref[...])
    vbuf_ref[0:CARRY, :] = vbuf_ref[tm:tm + CARRY, :]


def _conv_tail(b, s, g, w_in, conv_w, w_out):
    tm = FFN_TM
    d = w_out.shape[0]
    assert s % tm == 0
    return (
        functools.partial(_conv_tail_kernel, nj=s // tm),
        [g.reshape(1, d), w_in, conv_w, w_out],
        [_resident((1, d)), _resident(w_in.shape), _resident(conv_w.shape), _resident(w_out.shape)],
        [((tm, d), lambda tile: (tile, 0))],
        [jax.ShapeDtypeStruct((b * s, d), F32)],
        [pltpu.VMEM((CARRY + tm, d), F32)],
    )


def _kv_tail_kernel(x, tile, g_ref, w_ref, kn_ref, seg_ref, segt_ref, blk_ref, ks_ref, vst_ref, kw_ref, vwt_ref,
                    raw_ref, *, nj):
    tm = x.shape[0]
    j = tile % nj
    xn = _rms(x, g_ref[...]).astype(BF16)
    y = _dot(xn, w_ref[...])
    gw = N_KV_GROUPS * HEAD_DIM
    per = LANES // HEAD_DIM
    for slab in range(raw_ref.shape[0]):
        raw_ref[slab] = y[:, slab * LANES:(slab + 1) * LANES]
    for l in range(CMP_BLOCK):
        for slab in range(raw_ref.shape[0]):
            rows = raw_ref[slab, pl.ds(l, tm // CMP_BLOCK, stride=CMP_BLOCK), :]
            for k in range(per):
                sg = slab * per + k
                blk_ref[sg // N_KV_GROUPS, 0, sg % N_KV_GROUPS, :, l * HEAD_DIM:(l + 1) * HEAD_DIM] = (
                    rows[:, k * HEAD_DIM:(k + 1) * HEAD_DIM])
    pos = j * tm + lax.broadcasted_iota(jnp.int32, (tm, HEAD_DIM), 0)
    lane = lax.broadcasted_iota(jnp.int32, (tm, HEAD_DIM), 1)
    onehot = jnp.where(lane == pos // SEL_BLOCK, 1.0, 0.0).astype(BF16)
    k_sel = _seg_rms(y[:, 2 * gw:3 * gw], seg_ref[...], segt_ref[...], kn_ref[0:1, :]).astype(BF16)
    k_win = _seg_rms(y[:, 4 * gw:5 * gw], seg_ref[...], segt_ref[...], kn_ref[1:2, :]).astype(BF16)
    for grp in range(N_KV_GROUPS):
        ks_ref[0, grp, :, 0:HEAD_DIM] = k_sel[:, grp * HEAD_DIM:(grp + 1) * HEAD_DIM]
        ks_ref[0, grp, :, HEAD_DIM:2 * HEAD_DIM] = onehot
        kw_ref[0, grp] = k_win[:, grp * HEAD_DIM:(grp + 1) * HEAD_DIM]
    extra = jnp.where(lax.broadcasted_iota(jnp.int32, (V_ROWS - HEAD_DIM, CHUNK), 0) == 0, 1.0, 0.0).astype(BF16)
    for slot, ref in ((3, vst_ref), (5, vwt_ref)):
        vt = y[:, slot * gw:(slot + 1) * gw].T
        for grp in range(N_KV_GROUPS):
            for cc in range(tm // CHUNK):
                ref[0, grp, cc, 0:HEAD_DIM, :] = vt[grp * HEAD_DIM:(grp + 1) * HEAD_DIM,
                                                    cc * CHUNK:(cc + 1) * CHUNK].astype(BF16)
                ref[0, grp, cc, HEAD_DIM:V_ROWS, :] = extra


def _kv_tail(b, s, g, w, k_norm):
    tm = FFN_TM
    d, n = w.shape
    ng = N_KV_GROUPS
    gw = ng * HEAD_DIM
    cpt = tm // CHUNK
    nj = s // tm
    assert s // SEL_BLOCK <= HEAD_DIM and s % tm == 0
    seg, seg_t = _seg_matrices(gw)
    gains = jnp.tile(k_norm[1:3], (1, ng))
    return (
        functools.partial(_kv_tail_kernel, nj=nj),
        [g.reshape(1, d), w, gains, seg, seg_t],
        [_resident((1, d)), _resident((d, n)), _resident(gains.shape), _resident(seg.shape),
         _resident(seg_t.shape)],
        [
            ((2, 1, ng, tm // CMP_BLOCK, CMP_BLOCK * HEAD_DIM), lambda i: (0, i // nj, 0, i % nj, 0)),
            ((1, ng, tm, 2 * HEAD_DIM), lambda i: (i // nj, 0, i % nj, 0)),
            ((1, ng, cpt, V_ROWS, CHUNK), lambda i: (i // nj, 0, i % nj, 0, 0)),
            ((1, ng, tm, HEAD_DIM), lambda i: (i // nj, 0, i % nj, 0)),
            ((1, ng, cpt, V_ROWS, CHUNK), lambda i: (i // nj, 0, i % nj, 0, 0)),
        ],
        [
            jax.ShapeDtypeStruct((2, b, ng, s // CMP_BLOCK, CMP_BLOCK * HEAD_DIM), F32),
            jax.ShapeDtypeStruct((b, ng, s, 2 * HEAD_DIM), BF16),
            jax.ShapeDtypeStruct((b, ng, s // CHUNK, V_ROWS, CHUNK), BF16),
            jax.ShapeDtypeStruct((b, ng, s, HEAD_DIM), BF16),
            jax.ShapeDtypeStruct((b, ng, s // CHUNK, V_ROWS, CHUNK), BF16),
        ],
        [pltpu.VMEM((2 * gw // LANES, tm, LANES), F32)],
    )


def _cmp_kernel(x_ref, pos_ref, w1_ref, b1_ref, w2_ref, kn_ref, o_ref):
    c = pl.program_id(0)
    nb, ne, width = x_ref.shape
    rows = nb * ne
    half = width // 2
    x = x_ref[...].reshape(rows, width)
    pos = pos_ref[...]
    w1 = w1_ref[...]
    h_even = _dot((x + pos).astype(BF16), w1)
    a = _dot((x[:, half:] + pos[:, :half]).astype(BF16), w1[0:half])
    b = _dot((x[:, :half] + pos[:, half:]).astype(BF16), w1[half:width])
    h_odd = a + pltpu.roll(b, rows - 1, 0)

    def tail(hid):
        y = _dot(jax.nn.gelu(hid + b1_ref[...]).astype(BF16), w2_ref[...])
        y = jnp.where(c == 0, _rms(y, kn_ref[...]), y)
        return y.astype(o_ref.dtype).reshape(nb, ne, y.shape[-1])

    o_ref[:, 0:ne, :] = tail(h_even)
    o_ref[:, ne:2 * ne, :] = tail(h_odd)


def _compress(xe, pos, w1, b1, w2, kn, *, nb=8):
    _, nseq, ne, width = xe.shape
    nb = min(nb, nseq)
    hid = w1.shape[-1]
    dk = w2.shape[-1]
    return pl.pallas_call(
        _cmp_kernel,
        grid=(2, nseq // nb),
        in_specs=[
            pl.BlockSpec((None, nb, ne, width), lambda c, i: (c, i, 0, 0)),
            pl.BlockSpec((None, 1, width), lambda c, i: (c, 0, 0)),
            pl.BlockSpec((None, width, hid), lambda c, i: (c, 0, 0)),
            pl.BlockSpec((None, 1, hid), lambda c, i: (c, 0, 0)),
            pl.BlockSpec((None, hid, dk), lambda c, i: (c, 0, 0)),
            pl.BlockSpec((1, dk), lambda c, i: (0, 0)),
        ],
        out_specs=pl.BlockSpec((None, nb, 2 * ne, dk), lambda c, i: (c, i, 0, 0)),
        out_shape=jax.ShapeDtypeStruct((2, nseq, 2 * ne, dk), BF16),
        compiler_params=_cparams(("parallel", "parallel")),
        name="compress",
    )(xe, pos.reshape(2, 1, width), w1, b1.reshape(2, 1, hid), w2, kn.reshape(1, dk))


def _qg_tail_kernel(x, tile, g_ref, w_ref, q_ref, gate_ref):
    dq = q_ref.shape[-1]
    xn = _rms(x, g_ref[...]).astype(BF16)
    y = _dot(xn, w_ref[...])
    q_ref[...] = y[:, 0:dq]
    gate_ref[...] = jax.nn.sigmoid(y[:, dq:])


def _qg_tail(t, g, w):
    tm = FFN_TM
    d, n = w.shape
    dq = N_HEADS * HEAD_DIM
    return (
        _qg_tail_kernel,
        [g.reshape(1, d), w],
        [_resident((1, d)), _resident((d, n))],
        [((tm, dq), lambda tile: (tile, 0)), ((tm, n - dq), lambda tile: (tile, 0))],
        [jax.ShapeDtypeStruct((t, dq), F32), jax.ShapeDtypeStruct((t, n - dq), F32)],
        [],
    )


def _split3(x):
    hi = x.astype(BF16)
    r = x - hi.astype(F32)
    mid = r.astype(BF16)
    lo = (r - mid.astype(F32)).astype(BF16)
    return hi, mid, lo


def _cmp_block_of_row(row, nrows):
    return jnp.where(row < nrows // 2, 2 * row, 2 * row - (nrows - 1))


def _nsa_kernel(*refs, n_io, n_cmp, n_blk):
    mask_ref, tri_ref, cmask_ref = refs[-3:]
    tq = CHUNK
    _, ncp, wide = cmask_ref.shape
    i = pl.program_id(1)
    half_w = wide // (tq // LANES)

    def query_of_lane(shape):
        lane = lax.broadcasted_iota(jnp.int32, shape, 1)
        return (lane // half_w) * LANES + (lane & (LANES - 1))

    kk = lax.broadcasted_iota(jnp.int32, (tq, wide), 0)
    mask_ref[...] = jnp.where(kk <= query_of_lane((tq, wide)), 0.0, MASK_SCORE)
    kb = lax.broadcasted_iota(jnp.int32, (LANES, half_w), 0)
    tb = lax.broadcasted_iota(jnp.int32, (LANES, half_w), 1) & (LANES - 1)
    tri_ref[0] = jnp.where(kb <= tb, 0.0, MASK_SCORE)
    tri_ref[1] = jnp.where(kb > tb, 0.0, MASK_SCORE)
    nidx = _cmp_block_of_row(lax.broadcasted_iota(jnp.int32, (ncp, wide), 0), ncp)
    tpos = i * tq + query_of_lane((ncp, wide))
    valid = (nidx * CMP_STRIDE + (CMP_BLOCK - 1) <= tpos) & (nidx < n_cmp)
    cmask_ref[0] = jnp.where(valid, 0.0, MASK_SCORE)
    cmask_ref[1] = jnp.where(valid, 1.0, 0.0)

    scratch = refs[n_io:-3]
    groups = [_nsa_group(g, *refs[:n_io], *[r.at[g] for r in scratch], mask_ref, tri_ref, cmask_ref, n_blk=n_blk)
              for g in range(N_KV_GROUPS)]
    done = object()

    def advance():
        handed = [None] * len(groups)
        while any(r is None for r in handed):
            for gi, grp in enumerate(groups):
                if handed[gi] is None:
                    handed[gi] = next(grp, done)
        return handed

    bodies = advance()

    def pairs(p, carry):
        for body in bodies:
            body(p)
        return carry

    lax.fori_loop(0, i // 2, pairs, 0)
    assert all(r is done for r in advance())


def _nsa_group(g, q_ref, gate_ref, kc_ref, vct_ref, ks_ref, vst_ref, kw_ref, vwt_ref, ovt_ref, qgain_ref,
               o_ref, qa_ref, m_ref, acc_ref, out_ref, s_ref, cmax_ref, mask_ref, tri_ref, cmask_ref, *, n_blk):
    hpg = HEADS_PER_GROUP
    tq = CHUNK
    dk = HEAD_DIM
    i = pl.program_id(1)
    gd = hpg * dk
    q_lanes = pl.ds(g * gd, gd)
    t0 = i * tq

    nqh = tq // LANES
    half_w = hpg * LANES

    def heads(fn):
        per_head = [fn(h) for h in range(hpg)]
        return jnp.concatenate([x[:, qh * LANES:(qh + 1) * LANES] for qh in range(nqh) for x in per_head], axis=1)

    def head_lanes(x, h):
        return jnp.concatenate([x[:, qh * half_w + h * LANES:qh * half_w + (h + 1) * LANES] for qh in range(nqh)],
                               axis=1)

    q_t = q_ref[0, :, q_lanes].T
    q_t = heads(lambda h: q_t[h * dk:(h + 1) * dk, :])
    ms = jnp.sum(q_t * q_t, axis=0, keepdims=True) * (1.0 / dk)
    qa_ref[0:dk, :] = (q_t * lax.rsqrt(ms + EPS) * qgain_ref[...]).astype(BF16)
    gate_t = gate_ref[0, :, pl.ds(g * LANES, LANES)].T
    gates = [heads(lambda h: gate_t[h * N_BRANCH + br:h * N_BRANCH + br + 1, :]) for br in range(N_BRANCH)]

    causal, win_lo = 0, 1
    sel_state, win_state, void_state = 0, 1, 2

    m_ref[...] = jnp.full(m_ref.shape, MASK_SCORE, F32)
    acc_ref[...] = jnp.zeros(acc_ref.shape, F32)

    def key_rows(ref, c):
        return ref[0, g, pl.ds(pl.multiple_of(c * tq, tq), tq), :]

    def produce(b, k, qrows, masked=False):
        s = _dot(k, qa_ref[0:qrows, :])
        if masked:
            s = s + mask_ref[...]
        s_ref[b] = s
        cmax_ref[b] = jnp.max(s, axis=0, keepdims=True)

    def consume(b, vt, st):
        m_prev = m_ref[st]
        m_new = jnp.maximum(m_prev, cmax_ref[b])
        alpha = jnp.exp2(m_prev - m_new)
        p = jnp.exp2(s_ref[b] - m_new)
        acc_ref[st] = alpha * acc_ref[st] + _dot(vt, p.astype(BF16))
        m_ref[st] = m_new

    def edge_layout(kind):
        lo_keys, hi_keys = slice(0, LANES), slice(LANES, tq)
        first, second = slice(0, half_w), slice(half_w, 2 * half_w)
        if kind == causal:
            return lo_keys, hi_keys, second, first
        return hi_keys, lo_keys, first, second

    def produce_edge(b, k, qrows, kind):
        full, part, sub, other = edge_layout(kind)
        tri = tri_ref[kind]
        s_full = _dot(k[full], qa_ref[0:qrows, :])
        s_part = _dot(k[part], qa_ref[0:qrows, sub]) + tri
        s_diag = s_full[:, other] + tri
        s_ref[b, full, other] = s_diag
        s_ref[b, full, sub] = s_full[:, sub]
        s_ref[b, part, sub] = s_part
        top_diag = jnp.max(s_diag, axis=0, keepdims=True)
        top_sub = jnp.maximum(jnp.max(s_full[:, sub], axis=0, keepdims=True), jnp.max(s_part, axis=0, keepdims=True))
        cmax_ref[b] = jnp.concatenate((top_diag, top_sub) if other.start == 0 else (top_sub, top_diag), axis=1)

    def consume_edge(b, vt, st, kind):
        full, part, sub, other = edge_layout(kind)
        m_prev = m_ref[st]
        m_new = jnp.maximum(m_prev, cmax_ref[b])
        alpha = jnp.exp2(m_prev - m_new)
        p_full = jnp.exp2(s_ref[b, full, :] - m_new)
        p_part = jnp.exp2(s_ref[b, part, sub] - m_new[:, sub])
        upd = _dot(vt[:, full], p_full.astype(BF16))
        upd_part = _dot(vt[:, part], p_part.astype(BF16))
        acc_ref[st, :, other] = alpha[:, other] * acc_ref[st, :, other] + upd[:, other]
        acc_ref[st, :, sub] = alpha[:, sub] * acc_ref[st, :, sub] + upd[:, sub] + upd_part
        m_ref[st] = m_new

    def flash_out(st, branch):
        out_ref[...] += gates[branch] * (acc_ref[st, 0:dk, :] * (1.0 / acc_ref[st, dk:dk + 1, :]))

    assert WINDOW == 2 * tq
    lo = jnp.maximum(i - 2, 0)
    mid = jnp.maximum(i - 1, 0)
    lo_state = jnp.where(i >= 2, win_state, void_state)
    mid_state = jnp.where(i >= 1, win_state, void_state)
    yield
    produce_edge(0, key_rows(kw_ref, lo), dk, win_lo)
    yield
    produce(1, key_rows(kw_ref, mid), dk)
    consume_edge(0, vwt_ref[0, g, lo], lo_state, win_lo)
    yield
    produce_edge(0, key_rows(kw_ref, i), dk, causal)
    consume(1, vwt_ref[0, g, mid], mid_state)
    yield

    s = _dot(kc_ref[0, g], qa_ref[0:dk, :]) + cmask_ref[0]
    e = jnp.exp2(s - jnp.max(s, axis=0, keepdims=True))
    p = e * (1.0 / jnp.sum(e, axis=0, keepdims=True)) * cmask_ref[1]
    out_ref[...] = gates[0] * _dot(vct_ref[0, g], p.astype(BF16))
    psum = head_lanes(p, 0)
    for h in range(1, hpg):
        psum = psum + head_lanes(p, h)
    yield

    ovt = ovt_ref[...]
    imp = None
    for part in _split3(psum):
        term = _dot(ovt, part)
        imp = term if imp is None else imp + term
    jblk = lax.broadcasted_iota(jnp.int32, (n_blk, tq), 0)
    cur = (t0 + lax.broadcasted_iota(jnp.int32, (n_blk, tq), 1)) // SEL_BLOCK
    forced = (jblk == 0) | (jblk == cur) | (jblk == cur - 1)
    imp = jnp.where(forced, FORCE_SCORE, imp)
    imp = jnp.where(jblk > cur, MASK_SCORE, imp)
    sub = SUBLANES
    parts = [imp[r:r + sub] for r in range(0, n_blk, sub)]
    ranks = [jnp.zeros((sub, tq), F32) for _ in parts]
    jsub = lax.broadcasted_iota(jnp.int32, (sub, tq), 0)
    for ii in range(n_blk):
        row = imp[ii:ii + 1, :]
        for gi, x in enumerate(parts):
            if ii < gi * sub:
                beats = row >= x
            elif ii >= (gi + 1) * sub:
                beats = row > x
            else:
                beats = (row > x) | ((row == x) & (jsub > ii - gi * sub))
            ranks[gi] = ranks[gi] + jnp.where(beats, 1.0, 0.0)
    rank = jnp.concatenate(ranks, axis=0)
    chosen = (rank < float(min(N_SELECT, n_blk))) & (jblk <= cur)
    bias = jnp.where(chosen, 0.0, MASK_SCORE)
    bias = jnp.concatenate([bias, jnp.zeros((dk - n_blk, tq), F32)], axis=0).astype(BF16)
    qa_ref[dk:2 * dk, :] = heads(lambda h: bias)
    yield

    npair = i // 2
    held = jnp.where(i >= 2, 2 * npair - 1, i)
    last = jnp.where(i % 2 == 0, i + 1, i - 1)
    produce(1, key_rows(ks_ref, i), 2 * dk, masked=True)
    consume_edge(0, vwt_ref[0, g, i], win_state, causal)
    flash_out(win_state, 2)

    def pair_body(p):
        c = 2 * p
        produce(0, key_rows(ks_ref, c), 2 * dk)
        consume(1, vst_ref[0, g, jnp.where(p == 0, i, c - 1)], sel_state)
        produce(1, key_rows(ks_ref, c + 1), 2 * dk)
        consume(0, vst_ref[0, g, c], sel_state)

    yield pair_body
    produce(0, key_rows(ks_ref, last), 2 * dk)
    consume(1, vst_ref[0, g, held], sel_state)
    yield
    consume(0, vst_ref[0, g, last], sel_state)
    flash_out(sel_state, 1)

    out = out_ref[...]
    out = jnp.concatenate([head_lanes(out, h) for h in range(hpg)], axis=0)
    o_ref[0, :, q_lanes] = out.T.astype(o_ref.dtype)


def _nsa(q, gate, q_norm, kc, vct, ks, vst, kw, vwt):
    b, s, dq = q.shape
    g = N_KV_GROUPS
    hpg = HEADS_PER_GROUP
    dk = HEAD_DIM
    tq = CHUNK
    n_blk = s // SEL_BLOCK
    n_cmp = (s - CMP_BLOCK) // CMP_STRIDE + 1
    ncp = kc.shape[2]
    assert WINDOW == 2 * tq and n_blk <= dk and (s // tq) % 2 == 0
    rows_np = np.arange(ncp)
    cmp_idx = np.where(rows_np < ncp // 2, 2 * rows_np, 2 * rows_np - (ncp - 1))
    cmp_start = cmp_idx * CMP_STRIDE
    blk_start = np.arange(n_blk) * SEL_BLOCK
    ovt = ((cmp_start[None, :] < blk_start[:, None] + SEL_BLOCK)
           & (cmp_start[None, :] + CMP_BLOCK > blk_start[:, None])
           & (cmp_idx[None, :] < n_cmp)).astype(np.float32)
    qgain = jnp.broadcast_to((q_norm * (dk ** -0.5 * LOG2E))[:, None], (dk, hpg * tq))
    idx = lambda bi, i: (bi, 0, 0, 0)
    idx5 = lambda bi, i: (bi, 0, 0, 0, 0)
    args = (q, gate, kc, vct, ks, vst, kw, vwt, jnp.asarray(ovt, BF16), qgain)
    return pl.pallas_call(
        functools.partial(_nsa_kernel, n_io=len(args) + 1, n_cmp=n_cmp, n_blk=n_blk),
        grid=(b, s // tq),
        in_specs=[
            pl.BlockSpec((1, tq, dq), lambda bi, i: (bi, i, 0)),
            pl.BlockSpec((1, tq, g * LANES), lambda bi, i: (bi, i, 0)),
            pl.BlockSpec((1, g, ncp, dk), idx),
            pl.BlockSpec((1, g, dk, ncp), idx),
            pl.BlockSpec((1, g, s, 2 * dk), idx),
            pl.BlockSpec((1, g, s // tq, V_ROWS, tq), idx5),
            pl.BlockSpec((1, g, s, dk), idx),
            pl.BlockSpec((1, g, s // tq, V_ROWS, tq), idx5),
            _resident((n_blk, ncp)),
            _resident((dk, hpg * tq)),
        ],
        out_specs=pl.BlockSpec((1, tq, dq), lambda bi, i: (bi, i, 0)),
        out_shape=jax.ShapeDtypeStruct((b, s, dq), BF16),
        scratch_shapes=[
            pltpu.VMEM((g, 2 * dk, hpg * tq), BF16),
            pltpu.VMEM((g, 3, 1, hpg * tq), F32),
            pltpu.VMEM((g, 3, V_ROWS, hpg * tq), F32),
            pltpu.VMEM((g, dk, hpg * tq), F32),
            pltpu.VMEM((g, 2, tq, hpg * tq), F32),
            pltpu.VMEM((g, 2, 1, hpg * tq), F32),
            pltpu.VMEM((tq, hpg * tq), F32),
            pltpu.VMEM((2, LANES, hpg * LANES), F32),
            pltpu.VMEM((2, ncp, hpg * tq), F32),
        ],
        compiler_params=_cparams(("parallel", "arbitrary")),
        name="nsa",
    )(*args)


def _gate_columns(w_qg):
    dq = N_HEADS * HEAD_DIM
    per = HEADS_PER_GROUP * N_BRANCH
    wg = w_qg[:, dq:].reshape(-1, N_KV_GROUPS, per)
    wg = jnp.pad(wg, ((0, 0), (0, 0), (0, LANES - per)))
    return jnp.concatenate([w_qg[:, :dq], wg.reshape(-1, N_KV_GROUPS * LANES)], axis=1)


def kernel(x, ffn_norm, ffn_w_gate_up, ffn_w_down, mix_norm, conv_w_in, conv_w, conv_w_out, kv_norm, kv_w,
           cmp_pos, cmp_w1, cmp_b1, cmp_w2, k_norm, nsa_w_qg, q_norm, nsa_w_o):
    b, s, d = x.shape
    t = b * s
    depth = ffn_norm.shape[0]
    n_a = conv_w_in.shape[0]
    g = N_KV_GROUPS

    ffn_g = ffn_norm.reshape(depth, 2, 1, d)

    def ffn(h2, layer, j, proj=None, tail=None, keep_h=True):
        return _ffn(h2, ffn_g, ffn_w_gate_up, ffn_w_down, (layer, j), proj, tail, keep_h)

    h = x.reshape(t, d)
    shared = None
    for layer in range(depth):
        proj = None
        if layer < n_a:
            h = ffn(h, layer, 0, keep_h=False,
                    tail=_conv_tail(b, s, mix_norm[layer], conv_w_in[layer].astype(BF16), conv_w[layer],
                                    conv_w_out[layer].astype(BF16)))
        else:
            li = layer - n_a
            h, q, gate = ffn(h, layer, 0, tail=_qg_tail(t, mix_norm[layer],
                                                        _gate_columns(nsa_w_qg[li]).astype(BF16)))
            o = _nsa(q.reshape(b, s, -1), gate.reshape(b, s, -1), q_norm[li], *shared)
            proj = (o.reshape(t, -1), nsa_w_o[li].astype(BF16))
        if layer == n_a - 1:
            h, blk, ks, vst, kw, vwt = ffn(h, layer, 1, proj,
                                           tail=_kv_tail(b, s, kv_norm, kv_w.astype(BF16), k_norm))
            blk = blk.reshape(2, b * g, s // CMP_BLOCK, CMP_BLOCK * HEAD_DIM)
            cmp = _compress(blk, cmp_pos.reshape(2, -1), cmp_w1.astype(BF16), cmp_b1,
                            cmp_w2.astype(BF16), k_norm[0])
            cmp = cmp.reshape(2, b, g, -1, HEAD_DIM)
            shared = (cmp[0], cmp[1].transpose(0, 1, 3, 2), ks, vst, kw, vwt)
        else:
            h = ffn(h, layer, 1, proj)
    return h.reshape(b, s, d)
```

```python
import functools

import numpy as np
import jax
import jax.numpy as jnp
from jax import lax
from jax.experimental import pallas as pl
from jax.experimental.pallas import tpu as pltpu

EPS = 1e-6
N_HEADS = 16
HEAD_DIM = 64
N_KV_GROUPS = 4
HEADS_PER_GROUP = N_HEADS // N_KV_GROUPS
N_BRANCH = 3
CMP_BLOCK = 32
CMP_STRIDE = 16
SEL_BLOCK = 64
N_SELECT = 16
WINDOW = 512
FORCE_SCORE = 1e9
MASK_SCORE = -(2.0 ** 100)
LOG2E = 1.4426950408889634

LANES = 128
SUBLANES = 8
BF16_ROWS = 16
CHUNK = 256
V_ROWS = HEAD_DIM + BF16_ROWS
VMEM_LIMIT = 56 * 1024 * 1024

BF16 = jnp.bfloat16
F32 = jnp.float32


def _cparams(sem):
    return pltpu.CompilerParams(dimension_semantics=sem, vmem_limit_bytes=VMEM_LIMIT)


def _rms(x, g):
    ms = jnp.mean(x * x, axis=-1, keepdims=True)
    return x * lax.rsqrt(ms + EPS) * g


def _dot(a, b):
    return jnp.dot(a, b, preferred_element_type=F32)


def _split2(x):
    hi = x.astype(BF16)
    return hi, (x - hi.astype(F32)).astype(BF16)


def _seg_rms(y, seg, seg_t, gain):
    ss = None
    for part in _split2(y * y):
        term = _dot(part, seg)
        ss = term if ss is None else ss + term
    r = lax.rsqrt(ss * (1.0 / HEAD_DIM) + EPS)
    scale = None
    for part in _split2(r):
        term = _dot(part, seg_t)
        scale = term if scale is None else scale + term
    return y * scale * gain


def _seg_matrices(width):
    ind = (np.arange(width)[:, None] // HEAD_DIM == np.arange(LANES)[None, :]).astype(np.float32)
    return jnp.asarray(ind, BF16), jnp.asarray(ind.T, BF16)


def _ffn_kernel(x_ref, g_ref, wgu32_ref, wd32_ref, *rest, dff, has_proj, n_tail_in, n_tail_out, keep_h, tail):
    step = pl.program_id(0)
    proj_refs, rest = (rest[:2], rest[2:]) if has_proj else ((), rest)
    tail_in, rest = rest[:n_tail_in], rest[n_tail_in:]
    o_ref, rest = (rest[0], rest[1:]) if keep_h else (None, rest)
    tail_out, (wgu_ref, wd_ref), tail_scratch = rest[:n_tail_out], rest[n_tail_out:n_tail_out + 2], rest[n_tail_out + 2:]

    @pl.when(step < FFN_INGEST)
    def _():
        for src32, dst in ((wgu32_ref, wgu_ref), (wd32_ref, wd_ref)):
            rows = src32.shape[0]
            dst[pl.ds(pl.multiple_of(step * rows, rows), rows), :] = src32[...].astype(BF16)

    @pl.when(step >= FFN_INGEST)
    def _():
        x = x_ref[...]
        if has_proj:
            a_ref, w_ref = proj_refs
            x = x + _dot(a_ref[...], w_ref[...])
        xn = _rms(x, g_ref[...]).astype(BF16)
        ab = _dot(xn, wgu_ref[...])
        a = ab[:, 0:dff]
        h = (a * jax.nn.sigmoid(a) * ab[:, dff:2 * dff]).astype(BF16)
        out = x + 0.5 * _dot(h, wd_ref[...])
        if keep_h:
            o_ref[...] = out
        if tail is not None:
            tail(out, step - FFN_INGEST, *tail_in, *tail_out, *tail_scratch)


def _resident(shape, lead=()):
    index = tuple(lead) + (0,) * len(shape)
    return pl.BlockSpec((None,) * len(lead) + tuple(shape), lambda *_: index, pipeline_mode=pl.Buffered(1))


FFN_TM = 512
FFN_INGEST = 8


def _ffn(h, g_all, w_gu_all, w_d_all, lead, proj=None, tail=None, keep_h=True):
    tm = FFN_TM
    t, d = h.shape
    dff = w_d_all.shape[-2]
    assert d % (BF16_ROWS * FFN_INGEST) == 0 and dff % (BF16_ROWS * FFN_INGEST) == 0

    def tiled(shape, fn):
        return pl.BlockSpec(shape, lambda i: fn(jnp.maximum(i - FFN_INGEST, 0)))

    def chunked(rows, cols):
        return pl.BlockSpec((None, None, rows, cols), lambda i: (*lead, jnp.minimum(i, FFN_INGEST - 1), 0))

    in_specs = [
        tiled((tm, d), lambda tile: (tile, 0)),
        _resident((1, d), lead),
        chunked(d // FFN_INGEST, 2 * dff),
        chunked(dff // FFN_INGEST, d),
    ]
    args = [h, g_all, w_gu_all, w_d_all]
    if proj is not None:
        a, w = proj
        in_specs += [tiled((tm, a.shape[1]), lambda tile: (tile, 0)), _resident(w.shape)]
        args += [a, w]
    out_specs = [tiled((tm, d), lambda tile: (tile, 0))] if keep_h else []
    out_shape = [jax.ShapeDtypeStruct((t, d), F32)] if keep_h else []
    scratch = [pltpu.VMEM((d, 2 * dff), BF16), pltpu.VMEM((dff, d), BF16)]
    tail_fn, n_tail_in, n_tail_out = None, 0, 0
    if tail is not None:
        tail_fn, tail_args, tail_in_specs, tail_out_blocks, tail_out_shapes, tail_scratch = tail
        in_specs += tail_in_specs
        args += tail_args
        n_tail_in, n_tail_out = len(tail_args), len(tail_out_blocks)
        out_specs += [tiled(shape, fn) for shape, fn in tail_out_blocks]
        out_shape += tail_out_shapes
        scratch += tail_scratch
    outs = pl.pallas_call(
        functools.partial(_ffn_kernel, dff=dff, has_proj=proj is not None, n_tail_in=n_tail_in,
                          n_tail_out=n_tail_out, keep_h=keep_h, tail=tail_fn),
        grid=(FFN_INGEST + t // tm,),
        in_specs=in_specs,
        out_specs=out_specs,
        out_shape=out_shape,
        scratch_shapes=scratch,
        compiler_params=_cparams(("arbitrary",)),
        name="ffn",
    )(*args)
    return outs[0] if len(outs) == 1 else outs


CARRY = SUBLANES


def _conv_tail_kernel(x, tile, g_ref, win_ref, cw_ref, wout_ref, o_ref, vbuf_ref, *, nj):
    tm, d = x.shape

    @pl.when(tile % nj == 0)
    def _():
        vbuf_ref[0:CARRY, :] = jnp.zeros((CARRY, d), F32)

    xn = _rms(x, g_ref[...]).astype(BF16)
    proj = _dot(xn, win_ref[...])
    bg = proj[:, 0:d]
    v = proj[:, d:2 * d] * proj[:, 2 * d:3 * d]
    vbuf_ref[CARRY:CARRY + tm, :] = v
    cw = cw_ref[...]
    conv = (cw[0:1, :] * vbuf_ref[CARRY - 2:CARRY - 2 + tm, :]
            + cw[1:2, :] * vbuf_ref[CARRY - 1:CARRY - 1 + tm, :]
            + cw[2:3, :] * v)
    y = (bg * conv).astype(BF16)
    o_ref[...] = x + _dot(y, wout_ref[...])
    vbuf_ref[0:CARRY, :] = vbuf_ref[tm:tm + CARRY, :]


def _conv_tail(b, s, g, w_in, conv_w, w_out):
    tm = FFN_TM
    d = w_out.shape[0]
    assert s % tm == 0
    return (
        functools.partial(_conv_tail_kernel, nj=s // tm),
        [g.reshape(1, d), w_in, conv_w, w_out],
        [_resident((1, d)), _resident(w_in.shape), _resident(conv_w.shape), _resident(w_out.shape)],
        [((tm, d), lambda tile: (tile, 0))],
        [jax.ShapeDtypeStruct((b * s, d), F32)],
        [pltpu.VMEM((CARRY + tm, d), F32)],
    )


def _kv_tail_kernel(x, tile, g_ref, w_ref, kn_ref, seg_ref, segt_ref, blk_ref, ks_ref, vst_ref, kw_ref, vwt_ref,
                    raw_ref, *, nj):
    tm = x.shape[0]
    j = tile % nj
    xn = _rms(x, g_ref[...]).astype(BF16)
    y = _dot(xn, w_ref[...])
    gw = N_KV_GROUPS * HEAD_DIM
    per = LANES // HEAD_DIM
    for slab in range(raw_ref.shape[0]):
        raw_ref[slab] = y[:, slab * LANES:(slab + 1) * LANES]
    for l in range(CMP_BLOCK):
        for slab in range(raw_ref.shape[0]):
            rows = raw_ref[slab, pl.ds(l, tm // CMP_BLOCK, stride=CMP_BLOCK), :]
            for k in range(per):
                sg = slab * per + k
                blk_ref[sg // N_KV_GROUPS, 0, sg % N_KV_GROUPS, :, l * HEAD_DIM:(l + 1) * HEAD_DIM] = (
                    rows[:, k * HEAD_DIM:(k + 1) * HEAD_DIM])
    pos = j * tm + lax.broadcasted_iota(jnp.int32, (tm, HEAD_DIM), 0)
    lane = lax.broadcasted_iota(jnp.int32, (tm, HEAD_DIM), 1)
    onehot = jnp.where(lane == pos // SEL_BLOCK, 1.0, 0.0).astype(BF16)
    k_sel = _seg_rms(y[:, 2 * gw:3 * gw], seg_ref[...], segt_ref[...], kn_ref[0:1, :]).astype(BF16)
    k_win = _seg_rms(y[:, 4 * gw:5 * gw], seg_ref[...], segt_ref[...], kn_ref[1:2, :]).astype(BF16)
    for grp in range(N_KV_GROUPS):
        ks_ref[0, grp, :, 0:HEAD_DIM] = k_sel[:, grp * HEAD_DIM:(grp + 1) * HEAD_DIM]
        ks_ref[0, grp, :, HEAD_DIM:2 * HEAD_DIM] = onehot
        kw_ref[0, grp] = k_win[:, grp * HEAD_DIM:(grp + 1) * HEAD_DIM]
    extra = jnp.where(lax.broadcasted_iota(jnp.int32, (V_ROWS - HEAD_DIM, CHUNK), 0) == 0, 1.0, 0.0).astype(BF16)
    for slot, ref in ((3, vst_ref), (5, vwt_ref)):
        vt = y[:, slot * gw:(slot + 1) * gw].T
        for grp in range(N_KV_GROUPS):
            for cc in range(tm // CHUNK):
                ref[0, grp, cc, 0:HEAD_DIM, :] = vt[grp * HEAD_DIM:(grp + 1) * HEAD_DIM,
                                                    cc * CHUNK:(cc + 1) * CHUNK].astype(BF16)
                ref[0, grp, cc, HEAD_DIM:V_ROWS, :] = extra


def _kv_tail(b, s, g, w, k_norm):
    tm = FFN_TM
    d, n = w.shape
    ng = N_KV_GROUPS
    gw = ng * HEAD_DIM
    cpt = tm // CHUNK
    nj = s // tm
    assert s // SEL_BLOCK <= HEAD_DIM and s % tm == 0
    seg, seg_t = _seg_matrices(gw)
    gains = jnp.tile(k_norm[1:3], (1, ng))
    return (
        functools.partial(_kv_tail_kernel, nj=nj),
        [g.reshape(1, d), w, gains, seg, seg_t],
        [_resident((1, d)), _resident((d, n)), _resident(gains.shape), _resident(seg.shape),
         _resident(seg_t.shape)],
        [
            ((2, 1, ng, tm // CMP_BLOCK, CMP_BLOCK * HEAD_DIM), lambda i: (0, i // nj, 0, i % nj, 0)),
            ((1, ng, tm, 2 * HEAD_DIM), lambda i: (i // nj, 0, i % nj, 0)),
            ((1, ng, cpt, V_ROWS, CHUNK), lambda i: (i // nj, 0, i % nj, 0, 0)),
            ((1, ng, tm, HEAD_DIM), lambda i: (i // nj, 0, i % nj, 0)),
            ((1, ng, cpt, V_ROWS, CHUNK), lambda i: (i // nj, 0, i % nj, 0, 0)),
        ],
        [
            jax.ShapeDtypeStruct((2, b, ng, s // CMP_BLOCK, CMP_BLOCK * HEAD_DIM), F32),
            jax.ShapeDtypeStruct((b, ng, s, 2 * HEAD_DIM), BF16),
            jax.ShapeDtypeStruct((b, ng, s // CHUNK, V_ROWS, CHUNK), BF16),
            jax.ShapeDtypeStruct((b, ng, s, HEAD_DIM), BF16),
            jax.ShapeDtypeStruct((b, ng, s // CHUNK, V_ROWS, CHUNK), BF16),
        ],
        [pltpu.VMEM((2 * gw // LANES, tm, LANES), F32)],
    )


def _cmp_kernel(x_ref, pos_ref, w1_ref, b1_ref, w2_ref, kn_ref, o_ref):
    c = pl.program_id(0)
    nb, ne, width = x_ref.shape
    rows = nb * ne
    half = width // 2
    x = x_ref[...].reshape(rows, width)
    pos = pos_ref[...]
    w1 = w1_ref[...]
    h_even = _dot((x + pos).astype(BF16), w1)
    a = _dot((x[:, half:] + pos[:, :half]).astype(BF16), w1[0:half])
    b = _dot((x[:, :half] + pos[:, half:]).astype(BF16), w1[half:width])
    h_odd = a + pltpu.roll(b, rows - 1, 0)

    def tail(hid):
        y = _dot(jax.nn.gelu(hid + b1_ref[...]).astype(BF16), w2_ref[...])
        y = jnp.where(c == 0, _rms(y, kn_ref[...]), y)
        return y.astype(o_ref.dtype).reshape(nb, ne, y.shape[-1])

    o_ref[:, 0:ne, :] = tail(h_even)
    o_ref[:, ne:2 * ne, :] = tail(h_odd)


def _compress(xe, pos, w1, b1, w2, kn, *, nb=8):
    _, nseq, ne, width = xe.shape
    nb = min(nb, nseq)
    hid = w1.shape[-1]
    dk = w2.shape[-1]
    return pl.pallas_call(
        _cmp_kernel,
        grid=(2, nseq // nb),
        in_specs=[
            pl.BlockSpec((None, nb, ne, width), lambda c, i: (c, i, 0, 0)),
            pl.BlockSpec((None, 1, width), lambda c, i: (c, 0, 0)),
            pl.BlockSpec((None, width, hid), lambda c, i: (c, 0, 0)),
            pl.BlockSpec((None, 1, hid), lambda c, i: (c, 0, 0)),
            pl.BlockSpec((None, hid, dk), lambda c, i: (c, 0, 0)),
            pl.BlockSpec((1, dk), lambda c, i: (0, 0)),
        ],
        out_specs=pl.BlockSpec((None, nb, 2 * ne, dk), lambda c, i: (c, i, 0, 0)),
        out_shape=jax.ShapeDtypeStruct((2, nseq, 2 * ne, dk), BF16),
        compiler_params=_cparams(("parallel", "parallel")),
        name="compress",
    )(xe, pos.reshape(2, 1, width), w1, b1.reshape(2, 1, hid), w2, kn.reshape(1, dk))


def _qg_tail_kernel(x, tile, g_ref, w_ref, q_ref, gate_ref):
    dq = q_ref.shape[-1]
    xn = _rms(x, g_ref[...]).astype(BF16)
    y = _dot(xn, w_ref[...])
    q_ref[...] = y[:, 0:dq]
    gate_ref[...] = jax.nn.sigmoid(y[:, dq:])


def _qg_tail(t, g, w):
    tm = FFN_TM
    d, n = w.shape
    dq = N_HEADS * HEAD_DIM
    return (
        _qg_tail_kernel,
        [g.reshape(1, d), w],
        [_resident((1, d)), _resident((d, n))],
        [((tm, dq), lambda tile: (tile, 0)), ((tm, n - dq), lambda tile: (tile, 0))],
        [jax.ShapeDtypeStruct((t, dq), F32), jax.ShapeDtypeStruct((t, n - dq), F32)],
        [],
    )


def _split3(x):
    hi = x.astype(BF16)
    r = x - hi.astype(F32)
    mid = r.astype(BF16)
    lo = (r - mid.astype(F32)).astype(BF16)
    return hi, mid, lo


def _cmp_block_of_row(row, nrows):
    return jnp.where(row < nrows // 2, 2 * row, 2 * row - (nrows - 1))


def _nsa_kernel(*refs, n_io, n_cmp, n_blk):
    mask_ref, tri_ref, cmask_ref = refs[-3:]
    tq = CHUNK
    _, ncp, wide = cmask_ref.shape
    i = pl.program_id(1)
    half_w = wide // (tq // LANES)

    def query_of_lane(shape):
        lane = lax.broadcasted_iota(jnp.int32, shape, 1)
        return (lane // half_w) * LANES + (lane & (LANES - 1))

    kk = lax.broadcasted_iota(jnp.int32, (tq, wide), 0)
    mask_ref[...] = jnp.where(kk <= query_of_lane((tq, wide)), 0.0, MASK_SCORE)
    kb = lax.broadcasted_iota(jnp.int32, (LANES, half_w), 0)
    tb = lax.broadcasted_iota(jnp.int32, (LANES, half_w), 1) & (LANES - 1)
    tri_ref[0] = jnp.where(kb <= tb, 0.0, MASK_SCORE)
    tri_ref[1] = jnp.where(kb > tb, 0.0, MASK_SCORE)
    nidx = _cmp_block_of_row(lax.broadcasted_iota(jnp.int32, (ncp, wide), 0), ncp)
    tpos = i * tq + query_of_lane((ncp, wide))
    valid = (nidx * CMP_STRIDE + (CMP_BLOCK - 1) <= tpos) & (nidx < n_cmp)
    cmask_ref[0] = jnp.where(valid, 0.0, MASK_SCORE)
    cmask_ref[1] = jnp.where(valid, 1.0, 0.0)

    scratch = refs[n_io:-3]
    groups = [_nsa_group(g, *refs[:n_io], *[r.at[g] for r in scratch], mask_ref, tri_ref, cmask_ref, n_blk=n_blk)
              for g in range(N_KV_GROUPS)]
    done = object()

    def advance():
        handed = [None] * len(groups)
        while any(r is None for r in handed):
            for gi, grp in enumerate(groups):
                if handed[gi] is None:
                    handed[gi] = next(grp, done)
        return handed

    bodies = advance()

    def pairs(p, carry):
        for body in bodies:
            body(p)
        return carry

    lax.fori_loop(0, i // 2, pairs, 0)
    tails = advance()

    @pl.when(i % 2 == 1)
    def _():
        for phase in range(2):
            for tail in tails:
                tail(phase)

    assert all(r is done for r in advance())


def _nsa_group(g, q_ref, gate_ref, kc_ref, vct_ref, ks_ref, vst_ref, kw_ref, vwt_ref, ovt_ref, qgain_ref,
               o_ref, qa_ref, m_ref, acc_ref, out_ref, s_ref, cmax_ref, mask_ref, tri_ref, cmask_ref, *, n_blk):
    hpg = HEADS_PER_GROUP
    tq = CHUNK
    dk = HEAD_DIM
    i = pl.program_id(1)
    gd = hpg * dk
    q_lanes = pl.ds(g * gd, gd)
    t0 = i * tq

    nqh = tq // LANES
    half_w = hpg * LANES

    def heads(fn):
        per_head = [fn(h) for h in range(hpg)]
        return jnp.concatenate([x[:, qh * LANES:(qh + 1) * LANES] for qh in range(nqh) for x in per_head], axis=1)

    def head_lanes(x, h):
        return jnp.concatenate([x[:, qh * half_w + h * LANES:qh * half_w + (h + 1) * LANES] for qh in range(nqh)],
                               axis=1)

    q_t = q_ref[0, :, q_lanes].T
    q_t = heads(lambda h: q_t[h * dk:(h + 1) * dk, :])
    ms = jnp.sum(q_t * q_t, axis=0, keepdims=True) * (1.0 / dk)
    qa_ref[0:dk, :] = (q_t * lax.rsqrt(ms + EPS) * qgain_ref[...]).astype(BF16)
    gate_t = gate_ref[0, :, pl.ds(g * LANES, LANES)].T
    gates = [heads(lambda h: gate_t[h * N_BRANCH + br:h * N_BRANCH + br + 1, :]) for br in range(N_BRANCH)]

    causal, win_lo = 0, 1
    sel_state, win_state, void_state = 0, 1, 2

    m_ref[...] = jnp.full(m_ref.shape, MASK_SCORE, F32)
    acc_ref[...] = jnp.zeros(acc_ref.shape, F32)

    def key_rows(ref, c):
        return ref[0, g, pl.ds(pl.multiple_of(c * tq, tq), tq), :]

    def produce(b, k, qrows, masked=False):
        s = _dot(k, qa_ref[0:qrows, :])
        if masked:
            s = s + mask_ref[...]
        s = s.astype(BF16)
        s_ref[b] = s
        cmax_ref[b] = jnp.max(s, axis=0, keepdims=True).astype(F32)

    def consume(b, vt, st):
        m_prev = m_ref[st]
        m_new = jnp.maximum(m_prev, cmax_ref[b])
        alpha = jnp.exp2(m_prev - m_new)
        p = jnp.exp2(s_ref[b] - m_new.astype(BF16))
        acc_ref[st] = alpha * acc_ref[st] + _dot(vt, p)
        m_ref[st] = m_new

    def edge_layout(kind):
        lo_keys, hi_keys = slice(0, LANES), slice(LANES, tq)
        first, second = slice(0, half_w), slice(half_w, 2 * half_w)
        if kind == causal:
            return lo_keys, hi_keys, second, first
        return hi_keys, lo_keys, first, second

    def produce_edge(b, k, qrows, kind):
        full, part, sub, other = edge_layout(kind)
        tri = tri_ref[kind]
        s_full = _dot(k[full], qa_ref[0:qrows, :])
        s_part = _dot(k[part], qa_ref[0:qrows, sub]) + tri
        s_diag = (s_full[:, other] + tri).astype(BF16)
        s_sub = s_full[:, sub].astype(BF16)
        s_part = s_part.astype(BF16)
        s_ref[b, full, other] = s_diag
        s_ref[b, full, sub] = s_sub
        s_ref[b, part, sub] = s_part
        top_diag = jnp.max(s_diag, axis=0, keepdims=True)
        top_sub = jnp.maximum(jnp.max(s_sub, axis=0, keepdims=True), jnp.max(s_part, axis=0, keepdims=True))
        top = jnp.concatenate((top_diag, top_sub) if other.start == 0 else (top_sub, top_diag), axis=1)
        cmax_ref[b] = top.astype(F32)

    def consume_edge(b, vt, st, kind):
        full, part, sub, other = edge_layout(kind)
        m_prev = m_ref[st]
        m_new = jnp.maximum(m_prev, cmax_ref[b])
        alpha = jnp.exp2(m_prev - m_new)
        m_low = m_new.astype(BF16)
        p_full = jnp.exp2(s_ref[b, full, :] - m_low)
        p_part = jnp.exp2(s_ref[b, part, sub] - m_low[:, sub])
        upd = _dot(vt[:, full], p_full)
        upd_part = _dot(vt[:, part], p_part)
        acc_ref[st, :, other] = alpha[:, other] * acc_ref[st, :, other] + upd[:, other]
        acc_ref[st, :, sub] = alpha[:, sub] * acc_ref[st, :, sub] + upd[:, sub] + upd_part
        m_ref[st] = m_new

    def flash_out(st, branch):
        out_ref[...] += gates[branch] * (acc_ref[st, 0:dk, :] * (1.0 / acc_ref[st, dk:dk + 1, :]))

    assert WINDOW == 2 * tq
    lo = jnp.maximum(i - 2, 0)
    mid = jnp.maximum(i - 1, 0)
    lo_state = jnp.where(i >= 2, win_state, void_state)
    mid_state = jnp.where(i >= 1, win_state, void_state)
    yield
    produce_edge(0, key_rows(kw_ref, lo), dk, win_lo)
    yield
    produce(1, key_rows(kw_ref, mid), dk)
    consume_edge(0, vwt_ref[0, g, lo], lo_state, win_lo)
    yield
    produce_edge(0, key_rows(kw_ref, i), dk, causal)
    consume(1, vwt_ref[0, g, mid], mid_state)
    yield

    s = _dot(kc_ref[0, g], qa_ref[0:dk, :]) + cmask_ref[0]
    e = jnp.exp2(s - jnp.max(s, axis=0, keepdims=True))
    p = e * (1.0 / jnp.sum(e, axis=0, keepdims=True)) * cmask_ref[1]
    out_ref[...] = gates[0] * _dot(vct_ref[0, g], p.astype(BF16))
    psum = head_lanes(p, 0)
    for h in range(1, hpg):
        psum = psum + head_lanes(p, h)
    yield

    ovt = ovt_ref[...]
    imp = None
    for part in _split3(psum):
        term = _dot(ovt, part)
        imp = term if imp is None else imp + term
    jblk = lax.broadcasted_iota(jnp.int32, (n_blk, tq), 0)
    cur = (t0 + lax.broadcasted_iota(jnp.int32, (n_blk, tq), 1)) // SEL_BLOCK
    forced = (jblk == 0) | (jblk == cur) | (jblk == cur - 1)
    imp = jnp.where(forced, FORCE_SCORE, imp)
    imp = jnp.where(jblk > cur, MASK_SCORE, imp)
    sub = SUBLANES
    parts = [imp[r:r + sub] for r in range(0, n_blk, sub)]
    ranks = [jnp.zeros((sub, tq), F32) for _ in parts]
    jsub = lax.broadcasted_iota(jnp.int32, (sub, tq), 0)
    for ii in range(n_blk):
        row = imp[ii:ii + 1, :]
        for gi, x in enumerate(parts):
            if ii < gi * sub:
                beats = row >= x
            elif ii >= (gi + 1) * sub:
                beats = row > x
            else:
                beats = (row > x) | ((row == x) & (jsub > ii - gi * sub))
            ranks[gi] = ranks[gi] + jnp.where(beats, 1.0, 0.0)
    rank = jnp.concatenate(ranks, axis=0)
    chosen = (rank < float(min(N_SELECT, n_blk))) & (jblk <= cur)
    bias = jnp.where(chosen, 0.0, MASK_SCORE)
    bias = jnp.concatenate([bias, jnp.zeros((dk - n_blk, tq), F32)], axis=0).astype(BF16)
    qa_ref[dk:2 * dk, :] = heads(lambda h: bias)
    yield

    npair = i // 2
    held = jnp.where(i >= 2, 2 * npair - 1, i)
    last = jnp.maximum(i - 1, 0)
    produce(1, key_rows(ks_ref, i), 2 * dk, masked=True)
    consume_edge(0, vwt_ref[0, g, i], win_state, causal)
    flash_out(win_state, 2)

    def pair_body(p):
        c = 2 * p
        produce(0, key_rows(ks_ref, c), 2 * dk)
        consume(1, vst_ref[0, g, jnp.where(p == 0, i, c - 1)], sel_state)
        produce(1, key_rows(ks_ref, c + 1), 2 * dk)
        consume(0, vst_ref[0, g, c], sel_state)

    yield pair_body
    consume(1, vst_ref[0, g, held], sel_state)

    def odd_tail(phase):
        if phase == 0:
            produce(0, key_rows(ks_ref, last), 2 * dk)
        else:
            consume(0, vst_ref[0, g, last], sel_state)

    yield odd_tail
    flash_out(sel_state, 1)

    out = out_ref[...]
    out = jnp.concatenate([head_lanes(out, h) for h in range(hpg)], axis=0)
    o_ref[0, :, q_lanes] = out.T.astype(o_ref.dtype)


def _nsa(q, gate, q_norm, kc, vct, ks, vst, kw, vwt):
    b, s, dq = q.shape
    g = N_KV_GROUPS
    hpg = HEADS_PER_GROUP
    dk = HEAD_DIM
    tq = CHUNK
    n_blk = s // SEL_BLOCK
    n_cmp = (s - CMP_BLOCK) // CMP_STRIDE + 1
    ncp = kc.shape[2]
    assert WINDOW == 2 * tq and n_blk <= dk and (s // tq) % 2 == 0
    rows_np = np.arange(ncp)
    cmp_idx = np.where(rows_np < ncp // 2, 2 * rows_np, 2 * rows_np - (ncp - 1))
    cmp_start = cmp_idx * CMP_STRIDE
    blk_start = np.arange(n_blk) * SEL_BLOCK
    ovt = ((cmp_start[None, :] < blk_start[:, None] + SEL_BLOCK)
           & (cmp_start[None, :] + CMP_BLOCK > blk_start[:, None])
           & (cmp_idx[None, :] < n_cmp)).astype(np.float32)
    qgain = jnp.broadcast_to((q_norm * (dk ** -0.5 * LOG2E))[:, None], (dk, hpg * tq))
    idx = lambda bi, i: (bi, 0, 0, 0)
    idx5 = lambda bi, i: (bi, 0, 0, 0, 0)
    args = (q, gate, kc, vct, ks, vst, kw, vwt, jnp.asarray(ovt, BF16), qgain)
    return pl.pallas_call(
        functools.partial(_nsa_kernel, n_io=len(args) + 1, n_cmp=n_cmp, n_blk=n_blk),
        grid=(b, s // tq),
        in_specs=[
            pl.BlockSpec((1, tq, dq), lambda bi, i: (bi, i, 0)),
            pl.BlockSpec((1, tq, g * LANES), lambda bi, i: (bi, i, 0)),
            pl.BlockSpec((1, g, ncp, dk), idx),
            pl.BlockSpec((1, g, dk, ncp), idx),
            pl.BlockSpec((1, g, s, 2 * dk), idx),
            pl.BlockSpec((1, g, s // tq, V_ROWS, tq), idx5),
            pl.BlockSpec((1, g, s, dk), idx),
            pl.BlockSpec((1, g, s // tq, V_ROWS, tq), idx5),
            _resident((n_blk, ncp)),
            _resident((dk, hpg * tq)),
        ],
        out_specs=pl.BlockSpec((1, tq, dq), lambda bi, i: (bi, i, 0)),
        out_shape=jax.ShapeDtypeStruct((b, s, dq), BF16),
        scratch_shapes=[
            pltpu.VMEM((g, 2 * dk, hpg * tq), BF16),
            pltpu.VMEM((g, 3, 1, hpg * tq), F32),
            pltpu.VMEM((g, 3, V_ROWS, hpg * tq), F32),
            pltpu.VMEM((g, dk, hpg * tq), F32),
            pltpu.VMEM((g, 2, tq, hpg * tq), BF16),
            pltpu.VMEM((g, 2, 1, hpg * tq), F32),
            pltpu.VMEM((tq, hpg * tq), F32),
            pltpu.VMEM((2, LANES, hpg * LANES), F32),
            pltpu.VMEM((2, ncp, hpg * tq), F32),
        ],
        compiler_params=_cparams(("parallel", "arbitrary")),
        name="nsa",
    )(*args)


def _gate_columns(w_qg):
    dq = N_HEADS * HEAD_DIM
    per = HEADS_PER_GROUP * N_BRANCH
    wg = w_qg[:, dq:].reshape(-1, N_KV_GROUPS, per)
    wg = jnp.pad(wg, ((0, 0), (0, 0), (0, LANES - per)))
    return jnp.concatenate([w_qg[:, :dq], wg.reshape(-1, N_KV_GROUPS * LANES)], axis=1)


def kernel(x, ffn_norm, ffn_w_gate_up, ffn_w_down, mix_norm, conv_w_in, conv_w, conv_w_out, kv_norm, kv_w,
           cmp_pos, cmp_w1, cmp_b1, cmp_w2, k_norm, nsa_w_qg, q_norm, nsa_w_o):
    b, s, d = x.shape
    t = b * s
    depth = ffn_norm.shape[0]
    n_a = conv_w_in.shape[0]
    g = N_KV_GROUPS

    ffn_g = ffn_norm.reshape(depth, 2, 1, d)

    def ffn(h2, layer, j, proj=None, tail=None, keep_h=True):
        return _ffn(h2, ffn_g, ffn_w_gate_up, ffn_w_down, (layer, j), proj, tail, keep_h)

    h = x.reshape(t, d)
    shared = None
    for layer in range(depth):
        proj = None
        if layer < n_a:
            h = ffn(h, layer, 0, keep_h=False,
                    tail=_conv_tail(b, s, mix_norm[layer], conv_w_in[layer].astype(BF16), conv_w[layer],
                                    conv_w_out[layer].astype(BF16)))
        else:
            li = layer - n_a
            h, q, gate = ffn(h, layer, 0, tail=_qg_tail(t, mix_norm[layer],
                                                        _gate_columns(nsa_w_qg[li]).astype(BF16)))
            o = _nsa(q.reshape(b, s, -1), gate.reshape(b, s, -1), q_norm[li], *shared)
            proj = (o.reshape(t, -1), nsa_w_o[li].astype(BF16))
        if layer == n_a - 1:
            h, blk, ks, vst, kw, vwt = ffn(h, layer, 1, proj,
                                           tail=_kv_tail(b, s, kv_norm, kv_w.astype(BF16), k_norm))
            blk = blk.reshape(2, b * g, s // CMP_BLOCK, CMP_BLOCK * HEAD_DIM)
            cmp = _compress(blk, cmp_pos.reshape(2, -1), cmp_w1.astype(BF16), cmp_b1,
                            cmp_w2.astype(BF16), k_norm[0])
            cmp = cmp.reshape(2, b, g, -1, HEAD_DIM)
            shared = (cmp[0], cmp[1].transpose(0, 1, 3, 2), ks, vst, kw, vwt)
        else:
            h = ffn(h, layer, 1, proj)
    return h.reshape(b, s, d)
```

```python
import functools

import numpy as np
import jax
import jax.numpy as jnp
from jax import lax
from jax.experimental import pallas as pl
from jax.experimental.pallas import tpu as pltpu

EPS = 1e-6
N_HEADS = 16
HEAD_DIM = 64
N_KV_GROUPS = 4
HEADS_PER_GROUP = N_HEADS // N_KV_GROUPS
N_BRANCH = 3
CMP_BLOCK = 32
CMP_STRIDE = 16
SEL_BLOCK = 64
N_SELECT = 16
WINDOW = 512
FORCE_SCORE = 1e9
MASK_SCORE = -(2.0 ** 100)
LOG2E = 1.4426950408889634

LANES = 128
SUBLANES = 8
BF16_ROWS = 16
CHUNK = 256
V_ROWS = HEAD_DIM + BF16_ROWS
VMEM_LIMIT = 56 * 1024 * 1024

BF16 = jnp.bfloat16
F32 = jnp.float32


def _cparams(sem):
    return pltpu.CompilerParams(dimension_semantics=sem, vmem_limit_bytes=VMEM_LIMIT)


def _rms(x, g):
    ms = jnp.mean(x * x, axis=-1, keepdims=True)
    return x * lax.rsqrt(ms + EPS) * g


def _dot(a, b):
    return jnp.dot(a, b, preferred_element_type=F32)


def _split2(x):
    hi = x.astype(BF16)
    return hi, (x - hi.astype(F32)).astype(BF16)


def _seg_rms(y, seg, seg_t, gain):
    ss = None
    for part in _split2(y * y):
        term = _dot(part, seg)
        ss = term if ss is None else ss + term
    r = lax.rsqrt(ss * (1.0 / HEAD_DIM) + EPS)
    scale = None
    for part in _split2(r):
        term = _dot(part, seg_t)
        scale = term if scale is None else scale + term
    return y * scale * gain


def _seg_matrices(width):
    ind = (np.arange(width)[:, None] // HEAD_DIM == np.arange(LANES)[None, :]).astype(np.float32)
    return jnp.asarray(ind, BF16), jnp.asarray(ind.T, BF16)


def _ffn_kernel(x_ref, g_ref, wgu32_ref, wd32_ref, *rest, dff, has_proj, n_tail_in, n_tail_out, keep_h, tail):
    step = pl.program_id(0)
    proj_refs, rest = (rest[:2], rest[2:]) if has_proj else ((), rest)
    tail_in, rest = rest[:n_tail_in], rest[n_tail_in:]
    o_ref, rest = (rest[0], rest[1:]) if keep_h else (None, rest)
    tail_out, (wgu_ref, wd_ref), tail_scratch = rest[:n_tail_out], rest[n_tail_out:n_tail_out + 2], rest[n_tail_out + 2:]

    @pl.when(step < FFN_INGEST)
    def _():
        for src32, dst in ((wgu32_ref, wgu_ref), (wd32_ref, wd_ref)):
            rows = src32.shape[0]
            dst[pl.ds(pl.multiple_of(step * rows, rows), rows), :] = src32[...].astype(BF16)

    @pl.when(step >= FFN_INGEST)
    def _():
        x = x_ref[...]
        if has_proj:
            a_ref, w_ref = proj_refs
            x = x + _dot(a_ref[...], w_ref[...])
        xn = _rms(x, g_ref[...]).astype(BF16)
        ab = _dot(xn, wgu_ref[...])
        a = ab[:, 0:dff]
        h = (a * jax.nn.sigmoid(a) * ab[:, dff:2 * dff]).astype(BF16)
        out = x + 0.5 * _dot(h, wd_ref[...])
        if keep_h:
            o_ref[...] = out
        if tail is not None:
            tail(out, step - FFN_INGEST, *tail_in, *tail_out, *tail_scratch)


def _resident(shape, lead=()):
    index = tuple(lead) + (0,) * len(shape)
    return pl.BlockSpec((None,) * len(lead) + tuple(shape), lambda *_: index, pipeline_mode=pl.Buffered(1))


FFN_TM = 512
FFN_INGEST = 8


def _ffn(h, g_all, w_gu_all, w_d_all, lead, proj=None, tail=None, keep_h=True):
    tm = FFN_TM
    t, d = h.shape
    dff = w_d_all.shape[-2]
    assert d % (BF16_ROWS * FFN_INGEST) == 0 and dff % (BF16_ROWS * FFN_INGEST) == 0

    def tiled(shape, fn):
        return pl.BlockSpec(shape, lambda i: fn(jnp.maximum(i - FFN_INGEST, 0)))

    def chunked(rows, cols):
        return pl.BlockSpec((None, None, rows, cols), lambda i: (*lead, jnp.minimum(i, FFN_INGEST - 1), 0))

    in_specs = [
        tiled((tm, d), lambda tile: (tile, 0)),
        _resident((1, d), lead),
        chunked(d // FFN_INGEST, 2 * dff),
        chunked(dff // FFN_INGEST, d),
    ]
    args = [h, g_all, w_gu_all, w_d_all]
    if proj is not None:
        a, w = proj
        in_specs += [tiled((tm, a.shape[1]), lambda tile: (tile, 0)), _resident(w.shape)]
        args += [a, w]
    out_specs = [tiled((tm, d), lambda tile: (tile, 0))] if keep_h else []
    out_shape = [jax.ShapeDtypeStruct((t, d), F32)] if keep_h else []
    scratch = [pltpu.VMEM((d, 2 * dff), BF16), pltpu.VMEM((dff, d), BF16)]
    tail_fn, n_tail_in, n_tail_out = None, 0, 0
    if tail is not None:
        tail_fn, tail_args, tail_in_specs, tail_out_blocks, tail_out_shapes, tail_scratch = tail
        in_specs += tail_in_specs
        args += tail_args
        n_tail_in, n_tail_out = len(tail_args), len(tail_out_blocks)
        out_specs += [tiled(shape, fn) for shape, fn in tail_out_blocks]
        out_shape += tail_out_shapes
        scratch += tail_scratch
    outs = pl.pallas_call(
        functools.partial(_ffn_kernel, dff=dff, has_proj=proj is not None, n_tail_in=n_tail_in,
                          n_tail_out=n_tail_out, keep_h=keep_h, tail=tail_fn),
        grid=(FFN_INGEST + t // tm,),
        in_specs=in_specs,
        out_specs=out_specs,
        out_shape=out_shape,
        scratch_shapes=scratch,
        compiler_params=_cparams(("arbitrary",)),
        name="ffn",
    )(*args)
    return outs[0] if len(outs) == 1 else outs


CARRY = SUBLANES


def _conv_tail_kernel(x, tile, g_ref, win_ref, cw_ref, wout_ref, o_ref, vbuf_ref, *, nj):
    tm, d = x.shape

    @pl.when(tile % nj == 0)
    def _():
        vbuf_ref[0:CARRY, :] = jnp.zeros((CARRY, d), F32)

    xn = _rms(x, g_ref[...]).astype(BF16)
    proj = _dot(xn, win_ref[...])
    bg = proj[:, 0:d]
    v = proj[:, d:2 * d] * proj[:, 2 * d:3 * d]
    vbuf_ref[CARRY:CARRY + tm, :] = v
    cw = cw_ref[...]
    conv = (cw[0:1, :] * vbuf_ref[CARRY - 2:CARRY - 2 + tm, :]
            + cw[1:2, :] * vbuf_ref[CARRY - 1:CARRY - 1 + tm, :]
            + cw[2:3, :] * v)
    y = (bg * conv).astype(BF16)
    o_ref[...] = x + _dot(y, wout_ref[...])
    vbuf_ref[0:CARRY, :] = vbuf_ref[tm:tm + CARRY, :]


def _conv_tail(b, s, g, w_in, conv_w, w_out):
    tm = FFN_TM
    d = w_out.shape[0]
    assert s % tm == 0
    return (
        functools.partial(_conv_tail_kernel, nj=s // tm),
        [g.reshape(1, d), w_in, conv_w, w_out],
        [_resident((1, d)), _resident(w_in.shape), _resident(conv_w.shape), _resident(w_out.shape)],
        [((tm, d), lambda tile: (tile, 0))],
        [jax.ShapeDtypeStruct((b * s, d), F32)],
        [pltpu.VMEM((CARRY + tm, d), F32)],
    )


def _kv_tail_kernel(x, tile, g_ref, w_ref, kn_ref, seg_ref, segt_ref, blk_ref, ks_ref, vst_ref, kw_ref, vwt_ref,
                    raw_ref, *, nj):
    tm = x.shape[0]
    j = tile % nj
    xn = _rms(x, g_ref[...]).astype(BF16)
    y = _dot(xn, w_ref[...])
    gw = N_KV_GROUPS * HEAD_DIM
    per = LANES // HEAD_DIM
    for slab in range(raw_ref.shape[0]):
        raw_ref[slab] = y[:, slab * LANES:(slab + 1) * LANES]
    for l in range(CMP_BLOCK):
        for slab in range(raw_ref.shape[0]):
            rows = raw_ref[slab, pl.ds(l, tm // CMP_BLOCK, stride=CMP_BLOCK), :]
            for k in range(per):
                sg = slab * per + k
                blk_ref[sg // N_KV_GROUPS, 0, sg % N_KV_GROUPS, :, l * HEAD_DIM:(l + 1) * HEAD_DIM] = (
                    rows[:, k * HEAD_DIM:(k + 1) * HEAD_DIM])
    pos = j * tm + lax.broadcasted_iota(jnp.int32, (tm, HEAD_DIM), 0)
    lane = lax.broadcasted_iota(jnp.int32, (tm, HEAD_DIM), 1)
    onehot = jnp.where(lane == pos // SEL_BLOCK, 1.0, 0.0).astype(BF16)
    k_sel = _seg_rms(y[:, 2 * gw:3 * gw], seg_ref[...], segt_ref[...], kn_ref[0:1, :]).astype(BF16)
    k_win = _seg_rms(y[:, 4 * gw:5 * gw], seg_ref[...], segt_ref[...], kn_ref[1:2, :]).astype(BF16)
    for grp in range(N_KV_GROUPS):
        ks_ref[0, grp, :, 0:HEAD_DIM] = k_sel[:, grp * HEAD_DIM:(grp + 1) * HEAD_DIM]
        ks_ref[0, grp, :, HEAD_DIM:2 * HEAD_DIM] = onehot
        kw_ref[0, grp] = k_win[:, grp * HEAD_DIM:(grp + 1) * HEAD_DIM]
    extra = jnp.where(lax.broadcasted_iota(jnp.int32, (V_ROWS - HEAD_DIM, CHUNK), 0) == 0, 1.0, 0.0).astype(BF16)
    for slot, ref in ((3, vst_ref), (5, vwt_ref)):
        vt = y[:, slot * gw:(slot + 1) * gw].T
        for grp in range(N_KV_GROUPS):
            for cc in range(tm // CHUNK):
                ref[0, grp, cc, 0:HEAD_DIM, :] = vt[grp * HEAD_DIM:(grp + 1) * HEAD_DIM,
                                                    cc * CHUNK:(cc + 1) * CHUNK].astype(BF16)
                ref[0, grp, cc, HEAD_DIM:V_ROWS, :] = extra


def _kv_tail(b, s, g, w, k_norm):
    tm = FFN_TM
    d, n = w.shape
    ng = N_KV_GROUPS
    gw = ng * HEAD_DIM
    cpt = tm // CHUNK
    nj = s // tm
    assert s // SEL_BLOCK <= HEAD_DIM and s % tm == 0
    seg, seg_t = _seg_matrices(gw)
    gains = jnp.tile(k_norm[1:3], (1, ng))
    return (
        functools.partial(_kv_tail_kernel, nj=nj),
        [g.reshape(1, d), w, gains, seg, seg_t],
        [_resident((1, d)), _resident((d, n)), _resident(gains.shape), _resident(seg.shape),
         _resident(seg_t.shape)],
        [
            ((2, 1, ng, tm // CMP_BLOCK, CMP_BLOCK * HEAD_DIM), lambda i: (0, i // nj, 0, i % nj, 0)),
            ((1, ng, tm, 2 * HEAD_DIM), lambda i: (i // nj, 0, i % nj, 0)),
            ((1, ng, cpt, V_ROWS, CHUNK), lambda i: (i // nj, 0, i % nj, 0, 0)),
            ((1, ng, tm, HEAD_DIM), lambda i: (i // nj, 0, i % nj, 0)),
            ((1, ng, cpt, V_ROWS, CHUNK), lambda i: (i // nj, 0, i % nj, 0, 0)),
        ],
        [
            jax.ShapeDtypeStruct((2, b, ng, s // CMP_BLOCK, CMP_BLOCK * HEAD_DIM), F32),
            jax.ShapeDtypeStruct((b, ng, s, 2 * HEAD_DIM), BF16),
            jax.ShapeDtypeStruct((b, ng, s // CHUNK, V_ROWS, CHUNK), BF16),
            jax.ShapeDtypeStruct((b, ng, s, HEAD_DIM), BF16),
            jax.ShapeDtypeStruct((b, ng, s // CHUNK, V_ROWS, CHUNK), BF16),
        ],
        [pltpu.VMEM((2 * gw // LANES, tm, LANES), F32)],
    )


def _cmp_kernel(x_ref, pos_ref, w1_ref, b1_ref, w2_ref, kn_ref, o_ref):
    c = pl.program_id(0)
    nb, ne, width = x_ref.shape
    rows = nb * ne
    half = width // 2
    x = x_ref[...].reshape(rows, width)
    pos = pos_ref[...]
    w1 = w1_ref[...]
    h_even = _dot((x + pos).astype(BF16), w1)
    a = _dot((x[:, half:] + pos[:, :half]).astype(BF16), w1[0:half])
    b = _dot((x[:, :half] + pos[:, half:]).astype(BF16), w1[half:width])
    h_odd = a + pltpu.roll(b, rows - 1, 0)

    def tail(hid):
        y = _dot(jax.nn.gelu(hid + b1_ref[...]).astype(BF16), w2_ref[...])
        y = jnp.where(c == 0, _rms(y, kn_ref[...]), y)
        return y.astype(o_ref.dtype).reshape(nb, ne, y.shape[-1])

    o_ref[:, 0:ne, :] = tail(h_even)
    o_ref[:, ne:2 * ne, :] = tail(h_odd)


def _compress(xe, pos, w1, b1, w2, kn, *, nb=8):
    _, nseq, ne, width = xe.shape
    nb = min(nb, nseq)
    hid = w1.shape[-1]
    dk = w2.shape[-1]
    return pl.pallas_call(
        _cmp_kernel,
        grid=(2, nseq // nb),
        in_specs=[
            pl.BlockSpec((None, nb, ne, width), lambda c, i: (c, i, 0, 0)),
            pl.BlockSpec((None, 1, width), lambda c, i: (c, 0, 0)),
            pl.BlockSpec((None, width, hid), lambda c, i: (c, 0, 0)),
            pl.BlockSpec((None, 1, hid), lambda c, i: (c, 0, 0)),
            pl.BlockSpec((None, hid, dk), lambda c, i: (c, 0, 0)),
            pl.BlockSpec((1, dk), lambda c, i: (0, 0)),
        ],
        out_specs=pl.BlockSpec((None, nb, 2 * ne, dk), lambda c, i: (c, i, 0, 0)),
        out_shape=jax.ShapeDtypeStruct((2, nseq, 2 * ne, dk), BF16),
        compiler_params=_cparams(("parallel", "parallel")),
        name="compress",
    )(xe, pos.reshape(2, 1, width), w1, b1.reshape(2, 1, hid), w2, kn.reshape(1, dk))


def _qg_tail_kernel(x, tile, g_ref, w_ref, q_ref, gate_ref):
    dq = q_ref.shape[-1]
    xn = _rms(x, g_ref[...]).astype(BF16)
    y = _dot(xn, w_ref[...])
    q_ref[...] = y[:, 0:dq]
    gate_ref[...] = jax.nn.sigmoid(y[:, dq:])


def _qg_tail(t, g, w):
    tm = FFN_TM
    d, n = w.shape
    dq = N_HEADS * HEAD_DIM
    return (
        _qg_tail_kernel,
        [g.reshape(1, d), w],
        [_resident((1, d)), _resident((d, n))],
        [((tm, dq), lambda tile: (tile, 0)), ((tm, n - dq), lambda tile: (tile, 0))],
        [jax.ShapeDtypeStruct((t, dq), F32), jax.ShapeDtypeStruct((t, n - dq), F32)],
        [],
    )


def _split3(x):
    hi = x.astype(BF16)
    r = x - hi.astype(F32)
    mid = r.astype(BF16)
    lo = (r - mid.astype(F32)).astype(BF16)
    return hi, mid, lo


def _cmp_block_of_row(row, nrows):
    return jnp.where(row < nrows // 2, 2 * row, 2 * row - (nrows - 1))


def _nsa_kernel(*refs, n_io, n_cmp, n_blk):
    mask_ref, tri_ref, cmask_ref = refs[-3:]
    tq = CHUNK
    _, ncp, wide = cmask_ref.shape
    i = pl.program_id(1)
    half_w = wide // (tq // LANES)

    def query_of_lane(shape):
        lane = lax.broadcasted_iota(jnp.int32, shape, 1)
        return (lane // half_w) * LANES + (lane & (LANES - 1))

    kk = lax.broadcasted_iota(jnp.int32, (tq, wide), 0)
    mask_ref[...] = jnp.where(kk <= query_of_lane((tq, wide)), 0.0, MASK_SCORE)
    kb = lax.broadcasted_iota(jnp.int32, (LANES, half_w), 0)
    tb = lax.broadcasted_iota(jnp.int32, (LANES, half_w), 1) & (LANES - 1)
    tri_ref[0] = jnp.where(kb <= tb, 0.0, MASK_SCORE)
    tri_ref[1] = jnp.where(kb > tb, 0.0, MASK_SCORE)
    nidx = _cmp_block_of_row(lax.broadcasted_iota(jnp.int32, (ncp, wide), 0), ncp)
    tpos = i * tq + query_of_lane((ncp, wide))
    valid = (nidx * CMP_STRIDE + (CMP_BLOCK - 1) <= tpos) & (nidx < n_cmp)
    cmask_ref[0] = jnp.where(valid, 0.0, MASK_SCORE)
    cmask_ref[1] = jnp.where(valid, 1.0, 0.0)

    scratch = refs[n_io:-3]
    groups = [_nsa_group(g, *refs[:n_io], *[r.at[g] for r in scratch], mask_ref, tri_ref, cmask_ref, n_blk=n_blk)
              for g in range(N_KV_GROUPS)]
    done = object()

    def advance():
        handed = [None] * len(groups)
        while any(r is None for r in handed):
            for gi, grp in enumerate(groups):
                if handed[gi] is None:
                    handed[gi] = next(grp, done)
        return handed

    bodies = advance()

    def pairs(p, carry):
        for phase in range(2):
            for body in bodies:
                body(p, phase)
        return carry

    lax.fori_loop(0, i // 2, pairs, 0)
    tails = advance()

    @pl.when(i % 2 == 1)
    def _():
        for phase in range(2):
            for tail in tails:
                tail(phase)

    assert all(r is done for r in advance())


def _nsa_group(g, q_ref, gate_ref, kc_ref, vct_ref, ks_ref, vst_ref, kw_ref, vwt_ref, ovt_ref, qgain_ref,
               o_ref, qa_ref, m_ref, acc_ref, out_ref, s_ref, cmax_ref, mask_ref, tri_ref, cmask_ref, *, n_blk):
    hpg = HEADS_PER_GROUP
    tq = CHUNK
    dk = HEAD_DIM
    i = pl.program_id(1)
    gd = hpg * dk
    q_lanes = pl.ds(g * gd, gd)
    t0 = i * tq

    nqh = tq // LANES
    half_w = hpg * LANES

    def heads(fn):
        per_head = [fn(h) for h in range(hpg)]
        return jnp.concatenate([x[:, qh * LANES:(qh + 1) * LANES] for qh in range(nqh) for x in per_head], axis=1)

    def head_lanes(x, h):
        return jnp.concatenate([x[:, qh * half_w + h * LANES:qh * half_w + (h + 1) * LANES] for qh in range(nqh)],
                               axis=1)

    q_t = q_ref[0, :, q_lanes].T
    q_t = heads(lambda h: q_t[h * dk:(h + 1) * dk, :])
    ms = jnp.sum(q_t * q_t, axis=0, keepdims=True) * (1.0 / dk)
    qa_ref[0:dk, :] = (q_t * lax.rsqrt(ms + EPS) * qgain_ref[...]).astype(BF16)
    gate_t = gate_ref[0, :, pl.ds(g * LANES, LANES)].T
    gates = [heads(lambda h: gate_t[h * N_BRANCH + br:h * N_BRANCH + br + 1, :]) for br in range(N_BRANCH)]

    causal, win_lo = 0, 1
    sel_state, win_state, void_state = 0, 1, 2

    m_ref[...] = jnp.full(m_ref.shape, MASK_SCORE, F32)
    acc_ref[...] = jnp.zeros(acc_ref.shape, F32)

    def key_rows(ref, c):
        return ref[0, g, pl.ds(pl.multiple_of(c * tq, tq), tq), :]

    def produce(b, k, qrows, masked=False):
        s = _dot(k, qa_ref[0:qrows, :])
        if masked:
            s = s + mask_ref[...]
        s = s.astype(BF16)
        s_ref[b] = s
        cmax_ref[b] = jnp.max(s, axis=0, keepdims=True).astype(F32)

    def consume(b, vt, st):
        m_prev = m_ref[st]
        m_new = jnp.maximum(m_prev, cmax_ref[b])
        alpha = jnp.exp2(m_prev - m_new)
        p = jnp.exp2(s_ref[b] - m_new.astype(BF16))
        acc_ref[st] = alpha * acc_ref[st] + _dot(vt, p)
        m_ref[st] = m_new

    def edge_layout(kind):
        lo_keys, hi_keys = slice(0, LANES), slice(LANES, tq)
        first, second = slice(0, half_w), slice(half_w, 2 * half_w)
        if kind == causal:
            return lo_keys, hi_keys, second, first
        return hi_keys, lo_keys, first, second

    def produce_edge(b, k, qrows, kind):
        full, part, sub, other = edge_layout(kind)
        tri = tri_ref[kind]
        s_full = _dot(k[full], qa_ref[0:qrows, :])
        s_part = _dot(k[part], qa_ref[0:qrows, sub]) + tri
        s_diag = (s_full[:, other] + tri).astype(BF16)
        s_sub = s_full[:, sub].astype(BF16)
        s_part = s_part.astype(BF16)
        s_ref[b, full, other] = s_diag
        s_ref[b, full, sub] = s_sub
        s_ref[b, part, sub] = s_part
        top_diag = jnp.max(s_diag, axis=0, keepdims=True)
        top_sub = jnp.maximum(jnp.max(s_sub, axis=0, keepdims=True), jnp.max(s_part, axis=0, keepdims=True))
        top = jnp.concatenate((top_diag, top_sub) if other.start == 0 else (top_sub, top_diag), axis=1)
        cmax_ref[b] = top.astype(F32)

    def consume_edge(b, vt, st, kind):
        full, part, sub, other = edge_layout(kind)
        m_prev = m_ref[st]
        m_new = jnp.maximum(m_prev, cmax_ref[b])
        alpha = jnp.exp2(m_prev - m_new)
        m_low = m_new.astype(BF16)
        p_full = jnp.exp2(s_ref[b, full, :] - m_low)
        p_part = jnp.exp2(s_ref[b, part, sub] - m_low[:, sub])
        upd = _dot(vt[:, full], p_full)
        upd_part = _dot(vt[:, part], p_part)
        acc_ref[st, :, other] = alpha[:, other] * acc_ref[st, :, other] + upd[:, other]
        acc_ref[st, :, sub] = alpha[:, sub] * acc_ref[st, :, sub] + upd[:, sub] + upd_part
        m_ref[st] = m_new

    def flash_out(st, branch):
        out_ref[...] += gates[branch] * (acc_ref[st, 0:dk, :] * (1.0 / acc_ref[st, dk:dk + 1, :]))

    assert WINDOW == 2 * tq
    lo = jnp.maximum(i - 2, 0)
    mid = jnp.maximum(i - 1, 0)
    lo_state = jnp.where(i >= 2, win_state, void_state)
    mid_state = jnp.where(i >= 1, win_state, void_state)
    yield
    produce_edge(0, key_rows(kw_ref, lo), dk, win_lo)
    yield
    produce(1, key_rows(kw_ref, mid), dk)
    consume_edge(0, vwt_ref[0, g, lo], lo_state, win_lo)
    yield
    produce_edge(0, key_rows(kw_ref, i), dk, causal)
    consume(1, vwt_ref[0, g, mid], mid_state)
    yield

    s = _dot(kc_ref[0, g], qa_ref[0:dk, :]) + cmask_ref[0]
    e = jnp.exp2(s - jnp.max(s, axis=0, keepdims=True))
    p = e * (1.0 / jnp.sum(e, axis=0, keepdims=True)) * cmask_ref[1]
    out_ref[...] = gates[0] * _dot(vct_ref[0, g], p.astype(BF16))
    psum = head_lanes(p, 0)
    for h in range(1, hpg):
        psum = psum + head_lanes(p, h)
    yield

    ovt = ovt_ref[...]
    imp = None
    for part in _split3(psum):
        term = _dot(ovt, part)
        imp = term if imp is None else imp + term
    jblk = lax.broadcasted_iota(jnp.int32, (n_blk, tq), 0)
    cur = (t0 + lax.broadcasted_iota(jnp.int32, (n_blk, tq), 1)) // SEL_BLOCK
    forced = (jblk == 0) | (jblk == cur) | (jblk == cur - 1)
    imp = jnp.where(forced, FORCE_SCORE, imp)
    imp = jnp.where(jblk > cur, MASK_SCORE, imp)
    sub = SUBLANES
    parts = [imp[r:r + sub] for r in range(0, n_blk, sub)]
    ranks = [jnp.zeros((sub, tq), F32) for _ in parts]
    jsub = lax.broadcasted_iota(jnp.int32, (sub, tq), 0)
    for ii in range(n_blk):
        row = imp[ii:ii + 1, :]
        for gi, x in enumerate(parts):
            if ii < gi * sub:
                beats = row >= x
            elif ii >= (gi + 1) * sub:
                beats = row > x
            else:
                beats = (row > x) | ((row == x) & (jsub > ii - gi * sub))
            ranks[gi] = jnp.where(beats, ranks[gi] + 1.0, ranks[gi])
    rank = jnp.concatenate(ranks, axis=0)
    chosen = (rank < float(min(N_SELECT, n_blk))) & (jblk <= cur)
    bias = jnp.where(chosen, 0.0, MASK_SCORE)
    bias = jnp.concatenate([bias, jnp.zeros((dk - n_blk, tq), F32)], axis=0).astype(BF16)
    qa_ref[dk:2 * dk, :] = heads(lambda h: bias)
    yield

    npair = i // 2
    held = jnp.where(i >= 2, 2 * npair - 1, i)
    last = jnp.maximum(i - 1, 0)
    produce(1, key_rows(ks_ref, i), 2 * dk, masked=True)
    consume_edge(0, vwt_ref[0, g, i], win_state, causal)
    flash_out(win_state, 2)

    def pair_body(p, phase):
        c = 2 * p
        if phase == 0:
            produce(0, key_rows(ks_ref, c), 2 * dk)
            consume(1, vst_ref[0, g, jnp.where(p == 0, i, c - 1)], sel_state)
        else:
            produce(1, key_rows(ks_ref, c + 1), 2 * dk)
            consume(0, vst_ref[0, g, c], sel_state)

    yield pair_body
    consume(1, vst_ref[0, g, held], sel_state)

    def odd_tail(phase):
        if phase == 0:
            produce(0, key_rows(ks_ref, last), 2 * dk)
        else:
            consume(0, vst_ref[0, g, last], sel_state)

    yield odd_tail
    flash_out(sel_state, 1)

    out = out_ref[...]
    out = jnp.concatenate([head_lanes(out, h) for h in range(hpg)], axis=0)
    o_ref[0, :, q_lanes] = out.T.astype(o_ref.dtype)


def _nsa(q, gate, q_norm, kc, vct, ks, vst, kw, vwt):
    b, s, dq = q.shape
    g = N_KV_GROUPS
    hpg = HEADS_PER_GROUP
    dk = HEAD_DIM
    tq = CHUNK
    n_blk = s // SEL_BLOCK
    n_cmp = (s - CMP_BLOCK) // CMP_STRIDE + 1
    ncp = kc.shape[2]
    assert WINDOW == 2 * tq and n_blk <= dk and (s // tq) % 2 == 0
    rows_np = np.arange(ncp)
    cmp_idx = np.where(rows_np < ncp // 2, 2 * rows_np, 2 * rows_np - (ncp - 1))
    cmp_start = cmp_idx * CMP_STRIDE
    blk_start = np.arange(n_blk) * SEL_BLOCK
    ovt = ((cmp_start[None, :] < blk_start[:, None] + SEL_BLOCK)
           & (cmp_start[None, :] + CMP_BLOCK > blk_start[:, None])
           & (cmp_idx[None, :] < n_cmp)).astype(np.float32)
    qgain = jnp.broadcast_to((q_norm * (dk ** -0.5 * LOG2E))[:, None], (dk, hpg * tq))
    idx = lambda bi, i: (bi, 0, 0, 0)
    idx5 = lambda bi, i: (bi, 0, 0, 0, 0)
    args = (q, gate, kc, vct, ks, vst, kw, vwt, jnp.asarray(ovt, BF16), qgain)
    return pl.pallas_call(
        functools.partial(_nsa_kernel, n_io=len(args) + 1, n_cmp=n_cmp, n_blk=n_blk),
        grid=(b, s // tq),
        in_specs=[
            pl.BlockSpec((1, tq, dq), lambda bi, i: (bi, i, 0)),
            pl.BlockSpec((1, tq, g * LANES), lambda bi, i: (bi, i, 0)),
            pl.BlockSpec((1, g, ncp, dk), idx),
            pl.BlockSpec((1, g, dk, ncp), idx),
            pl.BlockSpec((1, g, s, 2 * dk), idx),
            pl.BlockSpec((1, g, s // tq, V_ROWS, tq), idx5),
            pl.BlockSpec((1, g, s, dk), idx),
            pl.BlockSpec((1, g, s // tq, V_ROWS, tq), idx5),
            _resident((n_blk, ncp)),
            _resident((dk, hpg * tq)),
        ],
        out_specs=pl.BlockSpec((1, tq, dq), lambda bi, i: (bi, i, 0)),
        out_shape=jax.ShapeDtypeStruct((b, s, dq), BF16),
        scratch_shapes=[
            pltpu.VMEM((g, 2 * dk, hpg * tq), BF16),
            pltpu.VMEM((g, 3, 1, hpg * tq), F32),
            pltpu.VMEM((g, 3, V_ROWS, hpg * tq), F32),
            pltpu.VMEM((g, dk, hpg * tq), F32),
            pltpu.VMEM((g, 2, tq, hpg * tq), BF16),
            pltpu.VMEM((g, 2, 1, hpg * tq), F32),
            pltpu.VMEM((tq, hpg * tq), F32),
            pltpu.VMEM((2, LANES, hpg * LANES), F32),
            pltpu.VMEM((2, ncp, hpg * tq), F32),
        ],
        compiler_params=_cparams(("parallel", "arbitrary")),
        name="nsa",
    )(*args)


def _gate_columns(w_qg):
    dq = N_HEADS * HEAD_DIM
    per = HEADS_PER_GROUP * N_BRANCH
    wg = w_qg[:, dq:].reshape(-1, N_KV_GROUPS, per)
    wg = jnp.pad(wg, ((0, 0), (0, 0), (0, LANES - per)))
    return jnp.concatenate([w_qg[:, :dq], wg.reshape(-1, N_KV_GROUPS * LANES)], axis=1)


def kernel(x, ffn_norm, ffn_w_gate_up, ffn_w_down, mix_norm, conv_w_in, conv_w, conv_w_out, kv_norm, kv_w,
           cmp_pos, cmp_w1, cmp_b1, cmp_w2, k_norm, nsa_w_qg, q_norm, nsa_w_o):
    b, s, d = x.shape
    t = b * s
    depth = ffn_norm.shape[0]
    n_a = conv_w_in.shape[0]
    g = N_KV_GROUPS

    ffn_g = ffn_norm.reshape(depth, 2, 1, d)

    def ffn(h2, layer, j, proj=None, tail=None, keep_h=True):
        return _ffn(h2, ffn_g, ffn_w_gate_up, ffn_w_down, (layer, j), proj, tail, keep_h)

    h = x.reshape(t, d)
    shared = None
    for layer in range(depth):
        proj = None
        if layer < n_a:
            h = ffn(h, layer, 0, keep_h=False,
                    tail=_conv_tail(b, s, mix_norm[layer], conv_w_in[layer].astype(BF16), conv_w[layer],
                                    conv_w_out[layer].astype(BF16)))
        else:
            li = layer - n_a
            h, q, gate = ffn(h, layer, 0, tail=_qg_tail(t, mix_norm[layer],
                                                        _gate_columns(nsa_w_qg[li]).astype(BF16)))
            o = _nsa(q.reshape(b, s, -1), gate.reshape(b, s, -1), q_norm[li], *shared)
            proj = (o.reshape(t, -1), nsa_w_o[li].astype(BF16))
        if layer == n_a - 1:
            h, blk, ks, vst, kw, vwt = ffn(h, layer, 1, proj,
                                           tail=_kv_tail(b, s, kv_norm, kv_w.astype(BF16), k_norm))
            blk = blk.reshape(2, b * g, s // CMP_BLOCK, CMP_BLOCK * HEAD_DIM)
            cmp = _compress(blk, cmp_pos.reshape(2, -1), cmp_w1.astype(BF16), cmp_b1,
                            cmp_w2.astype(BF16), k_norm[0])
            cmp = cmp.reshape(2, b, g, -1, HEAD_DIM)
            shared = (cmp[0], cmp[1].transpose(0, 1, 3, 2), ks, vst, kw, vwt)
        else:
            h = ffn(h, layer, 1, proj)
    return h.reshape(b, s, d)
```

```python
import functools

import numpy as np
import jax
import jax.numpy as jnp
from jax import lax
from jax.experimental import pallas as pl
from jax.experimental.pallas import tpu as pltpu

EPS = 1e-6
N_HEADS = 16
HEAD_DIM = 64
N_KV_GROUPS = 4
HEADS_PER_GROUP = N_HEADS // N_KV_GROUPS
N_BRANCH = 3
CMP_BLOCK = 32
CMP_STRIDE = 16
SEL_BLOCK = 64
N_SELECT = 16
WINDOW = 512
FORCE_SCORE = 1e9
MASK_SCORE = -(2.0 ** 100)
LOG2E = 1.4426950408889634

LANES = 128
SUBLANES = 8
BF16_ROWS = 16
CHUNK = 256
V_ROWS = HEAD_DIM + BF16_ROWS
VMEM_LIMIT = 56 * 1024 * 1024

BF16 = jnp.bfloat16
F32 = jnp.float32


def _cparams(sem):
    return pltpu.CompilerParams(dimension_semantics=sem, vmem_limit_bytes=VMEM_LIMIT)


def _rms(x, g):
    ms = jnp.mean(x * x, axis=-1, keepdims=True)
    return x * lax.rsqrt(ms + EPS) * g


def _dot(a, b):
    return jnp.dot(a, b, preferred_element_type=F32)


def _split2(x):
    hi = x.astype(BF16)
    return hi, (x - hi.astype(F32)).astype(BF16)


def _seg_rms(y, seg, seg_t, gain):
    ss = None
    for part in _split2(y * y):
        term = _dot(part, seg)
        ss = term if ss is None else ss + term
    r = lax.rsqrt(ss * (1.0 / HEAD_DIM) + EPS)
    scale = None
    for part in _split2(r):
        term = _dot(part, seg_t)
        scale = term if scale is None else scale + term
    return y * scale * gain


def _seg_matrices(width):
    ind = (np.arange(width)[:, None] // HEAD_DIM == np.arange(LANES)[None, :]).astype(np.float32)
    return jnp.asarray(ind, BF16), jnp.asarray(ind.T, BF16)


def _ffn_kernel(x_ref, g_ref, wgu32_ref, wd32_ref, *rest, dff, has_proj, n_tail_in, n_tail_out, keep_h, tail):
    step = pl.program_id(0)
    proj_refs, rest = (rest[:2], rest[2:]) if has_proj else ((), rest)
    tail_in, rest = rest[:n_tail_in], rest[n_tail_in:]
    o_ref, rest = (rest[0], rest[1:]) if keep_h else (None, rest)
    tail_out, (wgu_ref, wd_ref), tail_scratch = rest[:n_tail_out], rest[n_tail_out:n_tail_out + 2], rest[n_tail_out + 2:]

    @pl.when(step < FFN_INGEST)
    def _():
        for src32, dst in ((wgu32_ref, wgu_ref), (wd32_ref, wd_ref)):
            rows = src32.shape[0]
            dst[pl.ds(pl.multiple_of(step * rows, rows), rows), :] = src32[...].astype(BF16)

    @pl.when(step >= FFN_INGEST)
    def _():
        x = x_ref[...]
        if has_proj:
            a_ref, w_ref = proj_refs
            x = x + _dot(a_ref[...], w_ref[...])
        xn = _rms(x, g_ref[...]).astype(BF16)
        ab = _dot(xn, wgu_ref[...])
        a = ab[:, 0:dff]
        h = (a * jax.nn.sigmoid(a) * ab[:, dff:2 * dff]).astype(BF16)
        out = x + 0.5 * _dot(h, wd_ref[...])
        if keep_h:
            o_ref[...] = out
        if tail is not None:
            tail(out, step - FFN_INGEST, *tail_in, *tail_out, *tail_scratch)


def _resident(shape, lead=()):
    index = tuple(lead) + (0,) * len(shape)
    return pl.BlockSpec((None,) * len(lead) + tuple(shape), lambda *_: index, pipeline_mode=pl.Buffered(1))


FFN_TM = 512
FFN_INGEST = 8


def _ffn(h, g_all, w_gu_all, w_d_all, lead, proj=None, tail=None, keep_h=True):
    tm = FFN_TM
    t, d = h.shape
    dff = w_d_all.shape[-2]
    assert d % (BF16_ROWS * FFN_INGEST) == 0 and dff % (BF16_ROWS * FFN_INGEST) == 0

    def tiled(shape, fn):
        return pl.BlockSpec(shape, lambda i: fn(jnp.maximum(i - FFN_INGEST, 0)))

    def chunked(rows, cols):
        return pl.BlockSpec((None, None, rows, cols), lambda i: (*lead, jnp.minimum(i, FFN_INGEST - 1), 0))

    in_specs = [
        tiled((tm, d), lambda tile: (tile, 0)),
        _resident((1, d), lead),
        chunked(d // FFN_INGEST, 2 * dff),
        chunked(dff // FFN_INGEST, d),
    ]
    args = [h, g_all, w_gu_all, w_d_all]
    if proj is not None:
        a, w = proj
        in_specs += [tiled((tm, a.shape[1]), lambda tile: (tile, 0)), _resident(w.shape)]
        args += [a, w]
    out_specs = [tiled((tm, d), lambda tile: (tile, 0))] if keep_h else []
    out_shape = [jax.ShapeDtypeStruct((t, d), F32)] if keep_h else []
    scratch = [pltpu.VMEM((d, 2 * dff), BF16), pltpu.VMEM((dff, d), BF16)]
    tail_fn, n_tail_in, n_tail_out = None, 0, 0
    if tail is not None:
        tail_fn, tail_args, tail_in_specs, tail_out_blocks, tail_out_shapes, tail_scratch = tail
        in_specs += tail_in_specs
        args += tail_args
        n_tail_in, n_tail_out = len(tail_args), len(tail_out_blocks)
        out_specs += [tiled(shape, fn) for shape, fn in tail_out_blocks]
        out_shape += tail_out_shapes
        scratch += tail_scratch
    outs = pl.pallas_call(
        functools.partial(_ffn_kernel, dff=dff, has_proj=proj is not None, n_tail_in=n_tail_in,
                          n_tail_out=n_tail_out, keep_h=keep_h, tail=tail_fn),
        grid=(FFN_INGEST + t // tm,),
        in_specs=in_specs,
        out_specs=out_specs,
        out_shape=out_shape,
        scratch_shapes=scratch,
        compiler_params=_cparams(("arbitrary",)),
        name="ffn",
    )(*args)
    return outs[0] if len(outs) == 1 else outs


CARRY = SUBLANES


def _conv_tail_kernel(x, tile, g_ref, win_ref, cw_ref, wout_ref, o_ref, vbuf_ref, *, nj):
    tm, d = x.shape

    @pl.when(tile % nj == 0)
    def _():
        vbuf_ref[0:CARRY, :] = jnp.zeros((CARRY, d), F32)

    xn = _rms(x, g_ref[...]).astype(BF16)
    proj = _dot(xn, win_ref[...])
    bg = proj[:, 0:d]
    v = proj[:, d:2 * d] * proj[:, 2 * d:3 * d]
    vbuf_ref[CARRY:CARRY + tm, :] = v
    cw = cw_ref[...]
    conv = (cw[0:1, :] * vbuf_ref[CARRY - 2:CARRY - 2 + tm, :]
            + cw[1:2, :] * vbuf_ref[CARRY - 1:CARRY - 1 + tm, :]
            + cw[2:3, :] * v)
    y = (bg * conv).astype(BF16)
    o_ref[...] = x + _dot(y, wout_ref[...])
    vbuf_ref[0:CARRY, :] = vbuf_ref[tm:tm + CARRY, :]


def _conv_tail(b, s, g, w_in, conv_w, w_out):
    tm = FFN_TM
    d = w_out.shape[0]
    assert s % tm == 0
    return (
        functools.partial(_conv_tail_kernel, nj=s // tm),
        [g.reshape(1, d), w_in, conv_w, w_out],
        [_resident((1, d)), _resident(w_in.shape), _resident(conv_w.shape), _resident(w_out.shape)],
        [((tm, d), lambda tile: (tile, 0))],
        [jax.ShapeDtypeStruct((b * s, d), F32)],
        [pltpu.VMEM((CARRY + tm, d), F32)],
    )


def _kv_tail_kernel(x, tile, g_ref, w_ref, kn_ref, seg_ref, segt_ref, blk_ref, ks_ref, vst_ref, kw_ref, vwt_ref,
                    raw_ref, *, nj):
    tm = x.shape[0]
    j = tile % nj
    xn = _rms(x, g_ref[...]).astype(BF16)
    y = _dot(xn, w_ref[...])
    gw = N_KV_GROUPS * HEAD_DIM
    per = LANES // HEAD_DIM
    for slab in range(raw_ref.shape[0]):
        raw_ref[slab] = y[:, slab * LANES:(slab + 1) * LANES]
    for l in range(CMP_BLOCK):
        for slab in range(raw_ref.shape[0]):
            rows = raw_ref[slab, pl.ds(l, tm // CMP_BLOCK, stride=CMP_BLOCK), :]
            for k in range(per):
                sg = slab * per + k
                blk_ref[sg // N_KV_GROUPS, 0, sg % N_KV_GROUPS, :, l * HEAD_DIM:(l + 1) * HEAD_DIM] = (
                    rows[:, k * HEAD_DIM:(k + 1) * HEAD_DIM])
    pos = j * tm + lax.broadcasted_iota(jnp.int32, (tm, HEAD_DIM), 0)
    lane = lax.broadcasted_iota(jnp.int32, (tm, HEAD_DIM), 1)
    onehot = jnp.where(lane == pos // SEL_BLOCK, 1.0, 0.0).astype(BF16)
    k_sel = _seg_rms(y[:, 2 * gw:3 * gw], seg_ref[...], segt_ref[...], kn_ref[0:1, :]).astype(BF16)
    k_win = _seg_rms(y[:, 4 * gw:5 * gw], seg_ref[...], segt_ref[...], kn_ref[1:2, :]).astype(BF16)
    for grp in range(N_KV_GROUPS):
        ks_ref[0, grp, :, 0:HEAD_DIM] = k_sel[:, grp * HEAD_DIM:(grp + 1) * HEAD_DIM]
        ks_ref[0, grp, :, HEAD_DIM:2 * HEAD_DIM] = onehot
        kw_ref[0, grp] = k_win[:, grp * HEAD_DIM:(grp + 1) * HEAD_DIM]
    extra = jnp.where(lax.broadcasted_iota(jnp.int32, (V_ROWS - HEAD_DIM, CHUNK), 0) == 0, 1.0, 0.0).astype(BF16)
    for slot, ref in ((3, vst_ref), (5, vwt_ref)):
        vt = y[:, slot * gw:(slot + 1) * gw].T
        for grp in range(N_KV_GROUPS):
            for cc in range(tm // CHUNK):
                ref[0, grp, cc, 0:HEAD_DIM, :] = vt[grp * HEAD_DIM:(grp + 1) * HEAD_DIM,
                                                    cc * CHUNK:(cc + 1) * CHUNK].astype(BF16)
                ref[0, grp, cc, HEAD_DIM:V_ROWS, :] = extra


def _kv_tail(b, s, g, w, k_norm):
    tm = FFN_TM
    d, n = w.shape
    ng = N_KV_GROUPS
    gw = ng * HEAD_DIM
    cpt = tm // CHUNK
    nj = s // tm
    assert s // SEL_BLOCK <= HEAD_DIM and s % tm == 0
    seg, seg_t = _seg_matrices(gw)
    gains = jnp.tile(k_norm[1:3], (1, ng))
    return (
        functools.partial(_kv_tail_kernel, nj=nj),
        [g.reshape(1, d), w, gains, seg, seg_t],
        [_resident((1, d)), _resident((d, n)), _resident(gains.shape), _resident(seg.shape),
         _resident(seg_t.shape)],
        [
            ((2, 1, ng, tm // CMP_BLOCK, CMP_BLOCK * HEAD_DIM), lambda i: (0, i // nj, 0, i % nj, 0)),
            ((1, ng, tm, 2 * HEAD_DIM), lambda i: (i // nj, 0, i % nj, 0)),
            ((1, ng, cpt, V_ROWS, CHUNK), lambda i: (i // nj, 0, i % nj, 0, 0)),
            ((1, ng, tm, HEAD_DIM), lambda i: (i // nj, 0, i % nj, 0)),
            ((1, ng, cpt, V_ROWS, CHUNK), lambda i: (i // nj, 0, i % nj, 0, 0)),
        ],
        [
            jax.ShapeDtypeStruct((2, b, ng, s // CMP_BLOCK, CMP_BLOCK * HEAD_DIM), F32),
            jax.ShapeDtypeStruct((b, ng, s, 2 * HEAD_DIM), BF16),
            jax.ShapeDtypeStruct((b, ng, s // CHUNK, V_ROWS, CHUNK), BF16),
            jax.ShapeDtypeStruct((b, ng, s, HEAD_DIM), BF16),
            jax.ShapeDtypeStruct((b, ng, s // CHUNK, V_ROWS, CHUNK), BF16),
        ],
        [pltpu.VMEM((2 * gw // LANES, tm, LANES), F32)],
    )


def _cmp_kernel(x_ref, pos_ref, w1_ref, b1_ref, w2_ref, kn_ref, o_ref):
    c = pl.program_id(0)
    nb, ne, width = x_ref.shape
    rows = nb * ne
    half = width // 2
    x = x_ref[...].reshape(rows, width)
    pos = pos_ref[...]
    w1 = w1_ref[...]
    h_even = _dot((x + pos).astype(BF16), w1)
    a = _dot((x[:, half:] + pos[:, :half]).astype(BF16), w1[0:half])
    b = _dot((x[:, :half] + pos[:, half:]).astype(BF16), w1[half:width])
    h_odd = a + pltpu.roll(b, rows - 1, 0)

    def tail(hid):
        y = _dot(jax.nn.gelu(hid + b1_ref[...]).astype(BF16), w2_ref[...])
        y = jnp.where(c == 0, _rms(y, kn_ref[...]), y)
        return y.astype(o_ref.dtype).reshape(nb, ne, y.shape[-1])

    o_ref[:, 0:ne, :] = tail(h_even)
    o_ref[:, ne:2 * ne, :] = tail(h_odd)


def _compress(xe, pos, w1, b1, w2, kn, *, nb=8):
    _, nseq, ne, width = xe.shape
    nb = min(nb, nseq)
    hid = w1.shape[-1]
    dk = w2.shape[-1]
    return pl.pallas_call(
        _cmp_kernel,
        grid=(2, nseq // nb),
        in_specs=[
            pl.BlockSpec((None, nb, ne, width), lambda c, i: (c, i, 0, 0)),
            pl.BlockSpec((None, 1, width), lambda c, i: (c, 0, 0)),
            pl.BlockSpec((None, width, hid), lambda c, i: (c, 0, 0)),
            pl.BlockSpec((None, 1, hid), lambda c, i: (c, 0, 0)),
            pl.BlockSpec((None, hid, dk), lambda c, i: (c, 0, 0)),
            pl.BlockSpec((1, dk), lambda c, i: (0, 0)),
        ],
        out_specs=pl.BlockSpec((None, nb, 2 * ne, dk), lambda c, i: (c, i, 0, 0)),
        out_shape=jax.ShapeDtypeStruct((2, nseq, 2 * ne, dk), BF16),
        compiler_params=_cparams(("parallel", "parallel")),
        name="compress",
    )(xe, pos.reshape(2, 1, width), w1, b1.reshape(2, 1, hid), w2, kn.reshape(1, dk))


def _qg_tail_kernel(x, tile, g_ref, w_ref, q_ref, gate_ref):
    dq = q_ref.shape[-1]
    xn = _rms(x, g_ref[...]).astype(BF16)
    y = _dot(xn, w_ref[...])
    q_ref[...] = y[:, 0:dq]
    gate_ref[...] = jax.nn.sigmoid(y[:, dq:])


def _qg_tail(t, g, w):
    tm = FFN_TM
    d, n = w.shape
    dq = N_HEADS * HEAD_DIM
    return (
        _qg_tail_kernel,
        [g.reshape(1, d), w],
        [_resident((1, d)), _resident((d, n))],
        [((tm, dq), lambda tile: (tile, 0)), ((tm, n - dq), lambda tile: (tile, 0))],
        [jax.ShapeDtypeStruct((t, dq), F32), jax.ShapeDtypeStruct((t, n - dq), F32)],
        [],
    )


def _split3(x):
    hi = x.astype(BF16)
    r = x - hi.astype(F32)
    mid = r.astype(BF16)
    lo = (r - mid.astype(F32)).astype(BF16)
    return hi, mid, lo


def _cmp_block_of_row(row, nrows):
    return jnp.where(row < nrows // 2, 2 * row, 2 * row - (nrows - 1))


def _nsa_kernel(*refs, n_io, n_cmp, n_blk):
    mask_ref, tri_ref, cmask_ref = refs[-3:]
    tq = CHUNK
    _, ncp, wide = cmask_ref.shape
    i = pl.program_id(1)
    half_w = wide // (tq // LANES)

    def query_of_lane(shape):
        lane = lax.broadcasted_iota(jnp.int32, shape, 1)
        return (lane // half_w) * LANES + (lane & (LANES - 1))

    kk = lax.broadcasted_iota(jnp.int32, (tq, wide), 0)
    mask_ref[...] = jnp.where(kk <= query_of_lane((tq, wide)), 0.0, MASK_SCORE)
    kb = lax.broadcasted_iota(jnp.int32, (LANES, half_w), 0)
    tb = lax.broadcasted_iota(jnp.int32, (LANES, half_w), 1) & (LANES - 1)
    tri_ref[0] = jnp.where(kb <= tb, 0.0, MASK_SCORE)
    tri_ref[1] = jnp.where(kb > tb, 0.0, MASK_SCORE)
    nidx = _cmp_block_of_row(lax.broadcasted_iota(jnp.int32, (ncp, wide), 0), ncp)
    tpos = i * tq + query_of_lane((ncp, wide))
    valid = (nidx * CMP_STRIDE + (CMP_BLOCK - 1) <= tpos) & (nidx < n_cmp)
    cmask_ref[0] = jnp.where(valid, 0.0, MASK_SCORE)
    cmask_ref[1] = jnp.where(valid, 1.0, 0.0)

    scratch = refs[n_io:-3]
    groups = [_nsa_group(g, *refs[:n_io], *[r.at[g] for r in scratch], mask_ref, tri_ref, cmask_ref, n_blk=n_blk)
              for g in range(N_KV_GROUPS)]
    done = object()

    def advance():
        handed = [None] * len(groups)
        while any(r is None for r in handed):
            for gi, grp in enumerate(groups):
                if handed[gi] is None:
                    handed[gi] = next(grp, done)
        return handed

    bodies = advance()

    def pairs(p, carry):
        for phase in range(2):
            for body in bodies:
                body(p, phase)
        return carry

    lax.fori_loop(0, i // 2, pairs, 0)
    finishers = advance()
    for odd in (False, True):
        @pl.when(i % 2 == int(odd))
        def _():
            for phase in range(2):
                for finish in finishers:
                    finish(odd, phase)

    assert all(r is done for r in advance())


def _nsa_group(g, q_ref, gate_ref, kc_ref, vct_ref, ks_ref, vst_ref, kw_ref, vwt_ref, ovt_ref, qgain_ref,
               o_ref, qa_ref, m_ref, acc_ref, out_ref, s_ref, cmax_ref, mask_ref, tri_ref, cmask_ref, *, n_blk):
    hpg = HEADS_PER_GROUP
    tq = CHUNK
    dk = HEAD_DIM
    i = pl.program_id(1)
    gd = hpg * dk
    q_lanes = pl.ds(g * gd, gd)
    t0 = i * tq

    nqh = tq // LANES
    half_w = hpg * LANES

    def heads(fn):
        per_head = [fn(h) for h in range(hpg)]
        return jnp.concatenate([x[:, qh * LANES:(qh + 1) * LANES] for qh in range(nqh) for x in per_head], axis=1)

    def head_lanes(x, h):
        return jnp.concatenate([x[:, qh * half_w + h * LANES:qh * half_w + (h + 1) * LANES] for qh in range(nqh)],
                               axis=1)

    q_t = q_ref[0, :, q_lanes].T
    q_t = heads(lambda h: q_t[h * dk:(h + 1) * dk, :])
    ms = jnp.sum(q_t * q_t, axis=0, keepdims=True) * (1.0 / dk)
    qa_ref[0:dk, :] = (q_t * lax.rsqrt(ms + EPS) * qgain_ref[...]).astype(BF16)
    gate_t = gate_ref[0].T
    gate_row = lambda h, br: (g * hpg + h) * N_BRANCH + br
    gates = [heads(lambda h: gate_t[gate_row(h, br):gate_row(h, br) + 1, :]) for br in range(N_BRANCH)]

    causal, win_lo = 0, 1
    sel_state, win_state, void_state = 0, 1, 2

    m_ref[...] = jnp.full(m_ref.shape, MASK_SCORE, F32)
    acc_ref[...] = jnp.zeros(acc_ref.shape, F32)

    def key_rows(ref, c):
        return ref[0, g, pl.ds(pl.multiple_of(c * tq, tq), tq), :]

    def produce(b, k, qrows, masked=False):
        s = _dot(k, qa_ref[0:qrows, :])
        if masked:
            s = s + mask_ref[...]
        s = s.astype(BF16)
        s_ref[b] = s
        cmax_ref[b] = jnp.max(s, axis=0, keepdims=True).astype(F32)

    def consume(b, vt, st):
        m_prev = m_ref[st]
        m_new = jnp.maximum(m_prev, cmax_ref[b])
        alpha = jnp.exp2(m_prev - m_new)
        p = jnp.exp2(s_ref[b] - m_new.astype(BF16))
        acc_ref[st] = alpha * acc_ref[st] + _dot(vt, p)
        m_ref[st] = m_new

    def edge_layout(kind):
        lo_keys, hi_keys = slice(0, LANES), slice(LANES, tq)
        first, second = slice(0, half_w), slice(half_w, 2 * half_w)
        if kind == causal:
            return lo_keys, hi_keys, second, first
        return hi_keys, lo_keys, first, second

    def produce_edge(b, k, qrows, kind):
        full, part, sub, other = edge_layout(kind)
        tri = tri_ref[kind]
        s_full = _dot(k[full], qa_ref[0:qrows, :])
        s_part = _dot(k[part], qa_ref[0:qrows, sub]) + tri
        s_diag = (s_full[:, other] + tri).astype(BF16)
        s_sub = s_full[:, sub].astype(BF16)
        s_part = s_part.astype(BF16)
        s_ref[b, full, other] = s_diag
        s_ref[b, full, sub] = s_sub
        s_ref[b, part, sub] = s_part
        top_diag = jnp.max(s_diag, axis=0, keepdims=True)
        top_sub = jnp.maximum(jnp.max(s_sub, axis=0, keepdims=True), jnp.max(s_part, axis=0, keepdims=True))
        top = jnp.concatenate((top_diag, top_sub) if other.start == 0 else (top_sub, top_diag), axis=1)
        cmax_ref[b] = top.astype(F32)

    def consume_edge(b, vt, st, kind):
        full, part, sub, other = edge_layout(kind)
        m_prev = m_ref[st]
        m_new = jnp.maximum(m_prev, cmax_ref[b])
        alpha = jnp.exp2(m_prev - m_new)
        m_low = m_new.astype(BF16)
        p_full = jnp.exp2(s_ref[b, full, :] - m_low)
        p_part = jnp.exp2(s_ref[b, part, sub] - m_low[:, sub])
        upd = _dot(vt[:, full], p_full)
        upd_part = _dot(vt[:, part], p_part)
        acc_ref[st, :, other] = alpha[:, other] * acc_ref[st, :, other] + upd[:, other]
        acc_ref[st, :, sub] = alpha[:, sub] * acc_ref[st, :, sub] + upd[:, sub] + upd_part
        m_ref[st] = m_new

    def flash_out(st, branch):
        out_ref[...] += gates[branch] * (acc_ref[st, 0:dk, :] * (1.0 / acc_ref[st, dk:dk + 1, :]))

    assert WINDOW == 2 * tq
    lo = jnp.maximum(i - 2, 0)
    mid = jnp.maximum(i - 1, 0)
    lo_state = jnp.where(i >= 2, win_state, void_state)
    mid_state = jnp.where(i >= 1, win_state, void_state)
    yield
    produce_edge(0, key_rows(kw_ref, lo), dk, win_lo)
    yield
    produce(1, key_rows(kw_ref, mid), dk)
    consume_edge(0, vwt_ref[0, g, lo], lo_state, win_lo)
    yield
    produce_edge(0, key_rows(kw_ref, i), dk, causal)
    consume(1, vwt_ref[0, g, mid], mid_state)
    yield

    s = _dot(kc_ref[0, g], qa_ref[0:dk, :]) + cmask_ref[0]
    e = jnp.exp2(s - jnp.max(s, axis=0, keepdims=True))
    p = e * (1.0 / jnp.sum(e, axis=0, keepdims=True)) * cmask_ref[1]
    out_ref[...] = gates[0] * _dot(vct_ref[0, g], p.astype(BF16))
    psum = head_lanes(p, 0)
    for h in range(1, hpg):
        psum = psum + head_lanes(p, h)
    yield

    ovt = ovt_ref[...]
    imp = None
    for part in _split3(psum):
        term = _dot(ovt, part)
        imp = term if imp is None else imp + term
    jblk = lax.broadcasted_iota(jnp.int32, (n_blk, tq), 0)
    cur = (t0 + lax.broadcasted_iota(jnp.int32, (n_blk, tq), 1)) // SEL_BLOCK
    forced = (jblk == 0) | (jblk == cur) | (jblk == cur - 1)
    imp = jnp.where(forced, FORCE_SCORE, imp)
    imp = jnp.where(jblk > cur, MASK_SCORE, imp)
    sub = SUBLANES
    parts = [imp[r:r + sub] for r in range(0, n_blk, sub)]
    ranks = [jnp.zeros((sub, tq), F32) for _ in parts]
    jsub = lax.broadcasted_iota(jnp.int32, (sub, tq), 0)
    for ii in range(n_blk):
        row = imp[ii:ii + 1, :]
        for gi, x in enumerate(parts):
            if ii < gi * sub:
                beats = row >= x
            elif ii >= (gi + 1) * sub:
                beats = row > x
            else:
                beats = (row > x) | ((row == x) & (jsub > ii - gi * sub))
            ranks[gi] = jnp.where(beats, ranks[gi] + 1.0, ranks[gi])
    rank = jnp.concatenate(ranks, axis=0)
    chosen = (rank < float(min(N_SELECT, n_blk))) & (jblk <= cur)
    bias = jnp.where(chosen, 0.0, MASK_SCORE)
    bias = jnp.concatenate([bias, jnp.zeros((dk - n_blk, tq), F32)], axis=0).astype(BF16)
    qa_ref[dk:2 * dk, :] = heads(lambda h: bias)
    yield

    npair = i // 2
    held = jnp.where(i >= 2, 2 * npair - 1, i)
    last = jnp.maximum(i - 1, 0)
    produce(1, key_rows(ks_ref, i), 2 * dk, masked=True)
    consume_edge(0, vwt_ref[0, g, i], win_state, causal)
    flash_out(win_state, 2)

    def pair_body(p, phase):
        c = 2 * p
        if phase == 0:
            produce(0, key_rows(ks_ref, c), 2 * dk)
            consume(1, vst_ref[0, g, jnp.where(p == 0, i, c - 1)], sel_state)
        else:
            produce(1, key_rows(ks_ref, c + 1), 2 * dk)
            consume(0, vst_ref[0, g, c], sel_state)

    yield pair_body

    def finish(odd, phase):
        if phase == 0:
            if odd:
                produce(0, key_rows(ks_ref, last), 2 * dk)
            consume(1, vst_ref[0, g, held], sel_state)
            return
        if odd:
            consume(0, vst_ref[0, g, last], sel_state)
        flash_out(sel_state, 1)
        out = out_ref[...]
        out = jnp.concatenate([head_lanes(out, h) for h in range(hpg)], axis=0)
        o_ref[0, :, q_lanes] = out.T.astype(o_ref.dtype)

    yield finish


def _nsa(q, gate, q_norm, kc, vct, ks, vst, kw, vwt):
    b, s, dq = q.shape
    g = N_KV_GROUPS
    hpg = HEADS_PER_GROUP
    dk = HEAD_DIM
    tq = CHUNK
    n_blk = s // SEL_BLOCK
    n_cmp = (s - CMP_BLOCK) // CMP_STRIDE + 1
    ncp = kc.shape[2]
    assert WINDOW == 2 * tq and n_blk <= dk and (s // tq) % 2 == 0
    rows_np = np.arange(ncp)
    cmp_idx = np.where(rows_np < ncp // 2, 2 * rows_np, 2 * rows_np - (ncp - 1))
    cmp_start = cmp_idx * CMP_STRIDE
    blk_start = np.arange(n_blk) * SEL_BLOCK
    ovt = ((cmp_start[None, :] < blk_start[:, None] + SEL_BLOCK)
           & (cmp_start[None, :] + CMP_BLOCK > blk_start[:, None])
           & (cmp_idx[None, :] < n_cmp)).astype(np.float32)
    qgain = jnp.broadcast_to((q_norm * (dk ** -0.5 * LOG2E))[:, None], (dk, hpg * tq))
    idx = lambda bi, i: (bi, 0, 0, 0)
    idx5 = lambda bi, i: (bi, 0, 0, 0, 0)
    args = (q, gate, kc, vct, ks, vst, kw, vwt, jnp.asarray(ovt, BF16), qgain)
    return pl.pallas_call(
        functools.partial(_nsa_kernel, n_io=len(args) + 1, n_cmp=n_cmp, n_blk=n_blk),
        grid=(b, s // tq),
        in_specs=[
            pl.BlockSpec((1, tq, dq), lambda bi, i: (bi, i, 0)),
            pl.BlockSpec((1, tq, LANES), lambda bi, i: (bi, i, 0)),
            pl.BlockSpec((1, g, ncp, dk), idx),
            pl.BlockSpec((1, g, dk, ncp), idx),
            pl.BlockSpec((1, g, s, 2 * dk), idx),
            pl.BlockSpec((1, g, s // tq, V_ROWS, tq), idx5),
            pl.BlockSpec((1, g, s, dk), idx),
            pl.BlockSpec((1, g, s // tq, V_ROWS, tq), idx5),
            _resident((n_blk, ncp)),
            _resident((dk, hpg * tq)),
        ],
        out_specs=pl.BlockSpec((1, tq, dq), lambda bi, i: (bi, i, 0)),
        out_shape=jax.ShapeDtypeStruct((b, s, dq), BF16),
        scratch_shapes=[
            pltpu.VMEM((g, 2 * dk, hpg * tq), BF16),
            pltpu.VMEM((g, 3, 1, hpg * tq), F32),
            pltpu.VMEM((g, 3, V_ROWS, hpg * tq), F32),
            pltpu.VMEM((g, dk, hpg * tq), F32),
            pltpu.VMEM((g, 2, tq, hpg * tq), BF16),
            pltpu.VMEM((g, 2, 1, hpg * tq), F32),
            pltpu.VMEM((tq, hpg * tq), F32),
            pltpu.VMEM((2, LANES, hpg * LANES), F32),
            pltpu.VMEM((2, ncp, hpg * tq), F32),
        ],
        compiler_params=_cparams(("parallel", "arbitrary")),
        name="nsa",
    )(*args)


def _gate_columns(w_qg):
    n_gate = N_HEADS * N_BRANCH
    assert n_gate <= LANES
    return jnp.pad(w_qg, ((0, 0), (0, LANES - n_gate)))


def kernel(x, ffn_norm, ffn_w_gate_up, ffn_w_down, mix_norm, conv_w_in, conv_w, conv_w_out, kv_norm, kv_w,
           cmp_pos, cmp_w1, cmp_b1, cmp_w2, k_norm, nsa_w_qg, q_norm, nsa_w_o):
    b, s, d = x.shape
    t = b * s
    depth = ffn_norm.shape[0]
    n_a = conv_w_in.shape[0]
    g = N_KV_GROUPS

    ffn_g = ffn_norm.reshape(depth, 2, 1, d)

    def ffn(h2, layer, j, proj=None, tail=None, keep_h=True):
        return _ffn(h2, ffn_g, ffn_w_gate_up, ffn_w_down, (layer, j), proj, tail, keep_h)

    h = x.reshape(t, d)
    shared = None
    for layer in range(depth):
        proj = None
        if layer < n_a:
            h = ffn(h, layer, 0, keep_h=False,
                    tail=_conv_tail(b, s, mix_norm[layer], conv_w_in[layer].astype(BF16), conv_w[layer],
                                    conv_w_out[layer].astype(BF16)))
        else:
            li = layer - n_a
            h, q, gate = ffn(h, layer, 0, tail=_qg_tail(t, mix_norm[layer],
                                                        _gate_columns(nsa_w_qg[li]).astype(BF16)))
            o = _nsa(q.reshape(b, s, -1), gate.reshape(b, s, -1), q_norm[li], *shared)
            proj = (o.reshape(t, -1), nsa_w_o[li].astype(BF16))
        if layer == n_a - 1:
            h, blk, ks, vst, kw, vwt = ffn(h, layer, 1, proj,
                                           tail=_kv_tail(b, s, kv_norm, kv_w.astype(BF16), k_norm))
            blk = blk.reshape(2, b * g, s // CMP_BLOCK, CMP_BLOCK * HEAD_DIM)
            cmp = _compress(blk, cmp_pos.reshape(2, -1), cmp_w1.astype(BF16), cmp_b1,
                            cmp_w2.astype(BF16), k_norm[0])
            cmp = cmp.reshape(2, b, g, -1, HEAD_DIM)
            shared = (cmp[0], cmp[1].transpose(0, 1, 3, 2), ks, vst, kw, vwt)
        else:
            h = ffn(h, layer, 1, proj)
    return h.reshape(b, s, d)
```

```python
import functools

import numpy as np
import jax
import jax.numpy as jnp
from jax import lax
from jax.experimental import pallas as pl
from jax.experimental.pallas import tpu as pltpu

EPS = 1e-6
N_HEADS = 16
HEAD_DIM = 64
N_KV_GROUPS = 4
HEADS_PER_GROUP = N_HEADS // N_KV_GROUPS
N_BRANCH = 3
CMP_BLOCK = 32
CMP_STRIDE = 16
SEL_BLOCK = 64
N_SELECT = 16
WINDOW = 512
FORCE_SCORE = 1e9
MASK_SCORE = -(2.0 ** 100)
LOG2E = 1.4426950408889634

LANES = 128
SUBLANES = 8
BF16_ROWS = 16
CHUNK = 256
V_ROWS = HEAD_DIM + BF16_ROWS
VMEM_LIMIT = 56 * 1024 * 1024

BF16 = jnp.bfloat16
F32 = jnp.float32


def _cparams(sem):
    return pltpu.CompilerParams(dimension_semantics=sem, vmem_limit_bytes=VMEM_LIMIT)


def _rms(x, g):
    ms = jnp.mean(x * x, axis=-1, keepdims=True)
    return x * lax.rsqrt(ms + EPS) * g


def _dot(a, b):
    return jnp.dot(a, b, preferred_element_type=F32)


def _split2(x):
    hi = x.astype(BF16)
    return hi, (x - hi.astype(F32)).astype(BF16)


def _seg_rms(y, seg, seg_t, gain):
    ss = None
    for part in _split2(y * y):
        term = _dot(part, seg)
        ss = term if ss is None else ss + term
    r = lax.rsqrt(ss * (1.0 / HEAD_DIM) + EPS)
    scale = None
    for part in _split2(r):
        term = _dot(part, seg_t)
        scale = term if scale is None else scale + term
    return y * scale * gain


def _seg_matrices(width):
    ind = (np.arange(width)[:, None] // HEAD_DIM == np.arange(LANES)[None, :]).astype(np.float32)
    return jnp.asarray(ind, BF16), jnp.asarray(ind.T, BF16)


def _ffn_kernel(x_ref, g_ref, wgu32_ref, wd32_ref, *rest, dff, has_proj, n_tail_in, n_tail_out, keep_h, tail):
    step = pl.program_id(0)
    proj_refs, rest = (rest[:2], rest[2:]) if has_proj else ((), rest)
    tail_in, rest = rest[:n_tail_in], rest[n_tail_in:]
    o_ref, rest = (rest[0], rest[1:]) if keep_h else (None, rest)
    tail_out, (wgu_ref, wd_ref), tail_scratch = rest[:n_tail_out], rest[n_tail_out:n_tail_out + 2], rest[n_tail_out + 2:]

    @pl.when(step < FFN_INGEST)
    def _():
        for src32, dst in ((wgu32_ref, wgu_ref), (wd32_ref, wd_ref)):
            rows = src32.shape[0]
            dst[pl.ds(pl.multiple_of(step * rows, rows), rows), :] = src32[...].astype(BF16)

    @pl.when(step >= FFN_INGEST)
    def _():
        x = x_ref[...]
        if has_proj:
            a_ref, w_ref = proj_refs
            x = x + _dot(a_ref[...], w_ref[...])
        xn = _rms(x, g_ref[...]).astype(BF16)
        ab = _dot(xn, wgu_ref[...])
        a = ab[:, 0:dff]
        h = (a * jax.nn.sigmoid(a) * ab[:, dff:2 * dff]).astype(BF16)
        out = x + 0.5 * _dot(h, wd_ref[...])
        if keep_h:
            o_ref[...] = out
        if tail is not None:
            tail(out, step - FFN_INGEST, *tail_in, *tail_out, *tail_scratch)


def _resident(shape, lead=()):
    index = tuple(lead) + (0,) * len(shape)
    return pl.BlockSpec((None,) * len(lead) + tuple(shape), lambda *_: index, pipeline_mode=pl.Buffered(1))


FFN_TM = 512
FFN_INGEST = 8


def _ffn(h, g_all, w_gu_all, w_d_all, lead, proj=None, tail=None, keep_h=True):
    tm = FFN_TM
    t, d = h.shape
    dff = w_d_all.shape[-2]
    assert d % (BF16_ROWS * FFN_INGEST) == 0 and dff % (BF16_ROWS * FFN_INGEST) == 0

    def tiled(shape, fn):
        return pl.BlockSpec(shape, lambda i: fn(jnp.maximum(i - FFN_INGEST, 0)))

    def chunked(rows, cols):
        return pl.BlockSpec((None, None, rows, cols), lambda i: (*lead, jnp.minimum(i, FFN_INGEST - 1), 0))

    in_specs = [
        tiled((tm, d), lambda tile: (tile, 0)),
        _resident((1, d), lead),
        chunked(d // FFN_INGEST, 2 * dff),
        chunked(dff // FFN_INGEST, d),
    ]
    args = [h, g_all, w_gu_all, w_d_all]
    if proj is not None:
        a, w = proj
        in_specs += [tiled((tm, a.shape[1]), lambda tile: (tile, 0)), _resident(w.shape)]
        args += [a, w]
    out_specs = [tiled((tm, d), lambda tile: (tile, 0))] if keep_h else []
    out_shape = [jax.ShapeDtypeStruct((t, d), F32)] if keep_h else []
    scratch = [pltpu.VMEM((d, 2 * dff), BF16), pltpu.VMEM((dff, d), BF16)]
    tail_fn, n_tail_in, n_tail_out = None, 0, 0
    if tail is not None:
        tail_fn, tail_args, tail_in_specs, tail_out_blocks, tail_out_shapes, tail_scratch = tail
        in_specs += tail_in_specs
        args += tail_args
        n_tail_in, n_tail_out = len(tail_args), len(tail_out_blocks)
        out_specs += [tiled(shape, fn) for shape, fn in tail_out_blocks]
        out_shape += tail_out_shapes
        scratch += tail_scratch
    outs = pl.pallas_call(
        functools.partial(_ffn_kernel, dff=dff, has_proj=proj is not None, n_tail_in=n_tail_in,
                          n_tail_out=n_tail_out, keep_h=keep_h, tail=tail_fn),
        grid=(FFN_INGEST + t // tm,),
        in_specs=in_specs,
        out_specs=out_specs,
        out_shape=out_shape,
        scratch_shapes=scratch,
        compiler_params=_cparams(("arbitrary",)),
        name="ffn",
    )(*args)
    return outs[0] if len(outs) == 1 else outs


CARRY = SUBLANES


def _conv_tail_kernel(x, tile, g_ref, win_ref, cw_ref, wout_ref, o_ref, vbuf_ref, *, nj):
    tm, d = x.shape

    @pl.when(tile % nj == 0)
    def _():
        vbuf_ref[0:CARRY, :] = jnp.zeros((CARRY, d), F32)

    xn = _rms(x, g_ref[...]).astype(BF16)
    proj = _dot(xn, win_ref[...])
    bg = proj[:, 0:d]
    v = proj[:, d:2 * d] * proj[:, 2 * d:3 * d]
    vbuf_ref[CARRY:CARRY + tm, :] = v
    cw = cw_ref[...]
    conv = (cw[0:1, :] * vbuf_ref[CARRY - 2:CARRY - 2 + tm, :]
            + cw[1:2, :] * vbuf_ref[CARRY - 1:CARRY - 1 + tm, :]
            + cw[2:3, :] * v)
    y = (bg * conv).astype(BF16)
    o_ref[...] = x + _dot(y, wout_ref[...])
    vbuf_ref[0:CARRY, :] = vbuf_ref[tm:tm + CARRY, :]


def _conv_tail(b, s, g, w_in, conv_w, w_out):
    tm = FFN_TM
    d = w_out.shape[0]
    assert s % tm == 0
    return (
        functools.partial(_conv_tail_kernel, nj=s // tm),
        [g.reshape(1, d), w_in, conv_w, w_out],
        [_resident((1, d)), _resident(w_in.shape), _resident(conv_w.shape), _resident(w_out.shape)],
        [((tm, d), lambda tile: (tile, 0))],
        [jax.ShapeDtypeStruct((b * s, d), F32)],
        [pltpu.VMEM((CARRY + tm, d), F32)],
    )


def _kv_tail_kernel(x, tile, g_ref, w_ref, kn_ref, seg_ref, segt_ref, blk_ref, ks_ref, vst_ref, kw_ref, vwt_ref,
                    raw_ref, *, nj):
    tm = x.shape[0]
    j = tile % nj
    xn = _rms(x, g_ref[...]).astype(BF16)
    y = _dot(xn, w_ref[...])
    gw = N_KV_GROUPS * HEAD_DIM
    per = LANES // HEAD_DIM
    for slab in range(raw_ref.shape[0]):
        raw_ref[slab] = y[:, slab * LANES:(slab + 1) * LANES]
    for l in range(CMP_BLOCK):
        for slab in range(raw_ref.shape[0]):
            rows = raw_ref[slab, pl.ds(l, tm // CMP_BLOCK, stride=CMP_BLOCK), :]
            for k in range(per):
                sg = slab * per + k
                blk_ref[sg // N_KV_GROUPS, 0, sg % N_KV_GROUPS, :, l * HEAD_DIM:(l + 1) * HEAD_DIM] = (
                    rows[:, k * HEAD_DIM:(k + 1) * HEAD_DIM])
    pos = j * tm + lax.broadcasted_iota(jnp.int32, (tm, HEAD_DIM), 0)
    lane = lax.broadcasted_iota(jnp.int32, (tm, HEAD_DIM), 1)
    onehot = jnp.where(lane == pos // SEL_BLOCK, 1.0, 0.0).astype(BF16)
    k_sel = _seg_rms(y[:, 2 * gw:3 * gw], seg_ref[...], segt_ref[...], kn_ref[0:1, :]).astype(BF16)
    k_win = _seg_rms(y[:, 4 * gw:5 * gw], seg_ref[...], segt_ref[...], kn_ref[1:2, :]).astype(BF16)
    for grp in range(N_KV_GROUPS):
        ks_ref[0, grp, :, 0:HEAD_DIM] = k_sel[:, grp * HEAD_DIM:(grp + 1) * HEAD_DIM]
        ks_ref[0, grp, :, HEAD_DIM:2 * HEAD_DIM] = onehot
        kw_ref[0, grp] = k_win[:, grp * HEAD_DIM:(grp + 1) * HEAD_DIM]
    extra = jnp.where(lax.broadcasted_iota(jnp.int32, (V_ROWS - HEAD_DIM, CHUNK), 0) == 0, 1.0, 0.0).astype(BF16)
    for slot, ref in ((3, vst_ref), (5, vwt_ref)):
        vt = y[:, slot * gw:(slot + 1) * gw].T
        for grp in range(N_KV_GROUPS):
            for cc in range(tm // CHUNK):
                ref[0, grp, cc, 0:HEAD_DIM, :] = vt[grp * HEAD_DIM:(grp + 1) * HEAD_DIM,
                                                    cc * CHUNK:(cc + 1) * CHUNK].astype(BF16)
                ref[0, grp, cc, HEAD_DIM:V_ROWS, :] = extra


def _kv_tail(b, s, g, w, k_norm):
    tm = FFN_TM
    d, n = w.shape
    ng = N_KV_GROUPS
    gw = ng * HEAD_DIM
    cpt = tm // CHUNK
    nj = s // tm
    assert s // SEL_BLOCK <= HEAD_DIM and s % tm == 0
    seg, seg_t = _seg_matrices(gw)
    gains = jnp.tile(k_norm[1:3], (1, ng))
    return (
        functools.partial(_kv_tail_kernel, nj=nj),
        [g.reshape(1, d), w, gains, seg, seg_t],
        [_resident((1, d)), _resident((d, n)), _resident(gains.shape), _resident(seg.shape),
         _resident(seg_t.shape)],
        [
            ((2, 1, ng, tm // CMP_BLOCK, CMP_BLOCK * HEAD_DIM), lambda i: (0, i // nj, 0, i % nj, 0)),
            ((1, ng, tm, 2 * HEAD_DIM), lambda i: (i // nj, 0, i % nj, 0)),
            ((1, ng, cpt, V_ROWS, CHUNK), lambda i: (i // nj, 0, i % nj, 0, 0)),
            ((1, ng, tm, HEAD_DIM), lambda i: (i // nj, 0, i % nj, 0)),
            ((1, ng, cpt, V_ROWS, CHUNK), lambda i: (i // nj, 0, i % nj, 0, 0)),
        ],
        [
            jax.ShapeDtypeStruct((2, b, ng, s // CMP_BLOCK, CMP_BLOCK * HEAD_DIM), F32),
            jax.ShapeDtypeStruct((b, ng, s, 2 * HEAD_DIM), BF16),
            jax.ShapeDtypeStruct((b, ng, s // CHUNK, V_ROWS, CHUNK), BF16),
            jax.ShapeDtypeStruct((b, ng, s, HEAD_DIM), BF16),
            jax.ShapeDtypeStruct((b, ng, s // CHUNK, V_ROWS, CHUNK), BF16),
        ],
        [pltpu.VMEM((2 * gw // LANES, tm, LANES), F32)],
    )


def _cmp_kernel(x_ref, pos_ref, w1_ref, b1_ref, w2_ref, kn_ref, o_ref):
    c = pl.program_id(0)
    nb, ne, width = x_ref.shape
    rows = nb * ne
    half = width // 2
    x = x_ref[...].reshape(rows, width)
    pos = pos_ref[...]
    w1 = w1_ref[...]
    h_even = _dot((x + pos).astype(BF16), w1)
    a = _dot((x[:, half:] + pos[:, :half]).astype(BF16), w1[0:half])
    b = _dot((x[:, :half] + pos[:, half:]).astype(BF16), w1[half:width])
    h_odd = a + pltpu.roll(b, rows - 1, 0)

    def tail(hid):
        y = _dot(jax.nn.gelu(hid + b1_ref[...]).astype(BF16), w2_ref[...])
        y = jnp.where(c == 0, _rms(y, kn_ref[...]), y)
        return y.astype(o_ref.dtype).reshape(nb, ne, y.shape[-1])

    o_ref[:, 0:ne, :] = tail(h_even)
    o_ref[:, ne:2 * ne, :] = tail(h_odd)


def _compress(xe, pos, w1, b1, w2, kn, *, nb=8):
    _, nseq, ne, width = xe.shape
    nb = min(nb, nseq)
    hid = w1.shape[-1]
    dk = w2.shape[-1]
    return pl.pallas_call(
        _cmp_kernel,
        grid=(2, nseq // nb),
        in_specs=[
            pl.BlockSpec((None, nb, ne, width), lambda c, i: (c, i, 0, 0)),
            pl.BlockSpec((None, 1, width), lambda c, i: (c, 0, 0)),
            pl.BlockSpec((None, width, hid), lambda c, i: (c, 0, 0)),
            pl.BlockSpec((None, 1, hid), lambda c, i: (c, 0, 0)),
            pl.BlockSpec((None, hid, dk), lambda c, i: (c, 0, 0)),
            pl.BlockSpec((1, dk), lambda c, i: (0, 0)),
        ],
        out_specs=pl.BlockSpec((None, nb, 2 * ne, dk), lambda c, i: (c, i, 0, 0)),
        out_shape=jax.ShapeDtypeStruct((2, nseq, 2 * ne, dk), BF16),
        compiler_params=_cparams(("parallel", "parallel")),
        name="compress",
    )(xe, pos.reshape(2, 1, width), w1, b1.reshape(2, 1, hid), w2, kn.reshape(1, dk))


def _qg_tail_kernel(x, tile, g_ref, w_ref, q_ref, gate_ref):
    dq = q_ref.shape[-1]
    xn = _rms(x, g_ref[...]).astype(BF16)
    y = _dot(xn, w_ref[...])
    q_ref[...] = y[:, 0:dq]
    gate_ref[...] = jax.nn.sigmoid(y[:, dq:])


def _qg_tail(t, g, w):
    tm = FFN_TM
    d, n = w.shape
    dq = N_HEADS * HEAD_DIM
    return (
        _qg_tail_kernel,
        [g.reshape(1, d), w],
        [_resident((1, d)), _resident((d, n))],
        [((tm, dq), lambda tile: (tile, 0)), ((tm, n - dq), lambda tile: (tile, 0))],
        [jax.ShapeDtypeStruct((t, dq), F32), jax.ShapeDtypeStruct((t, n - dq), F32)],
        [],
    )


def _split3(x):
    hi = x.astype(BF16)
    r = x - hi.astype(F32)
    mid = r.astype(BF16)
    lo = (r - mid.astype(F32)).astype(BF16)
    return hi, mid, lo


def _cmp_block_of_row(row, nrows):
    return jnp.where(row < nrows // 2, 2 * row, 2 * row - (nrows - 1))


def _nsa_kernel(*refs, n_io, n_cmp, n_blk):
    mask_ref, tri_ref, cmask_ref = refs[-3:]
    tq = CHUNK
    _, ncp, wide = cmask_ref.shape
    i = pl.program_id(1)
    half_w = wide // (tq // LANES)

    def query_of_lane(shape):
        lane = lax.broadcasted_iota(jnp.int32, shape, 1)
        return (lane // half_w) * LANES + (lane & (LANES - 1))

    kk = lax.broadcasted_iota(jnp.int32, (tq, wide), 0)
    mask_ref[...] = jnp.where(kk <= query_of_lane((tq, wide)), 0.0, MASK_SCORE)
    kb = lax.broadcasted_iota(jnp.int32, (LANES, half_w), 0)
    tb = lax.broadcasted_iota(jnp.int32, (LANES, half_w), 1) & (LANES - 1)
    tri_ref[0] = jnp.where(kb <= tb, 0.0, MASK_SCORE)
    tri_ref[1] = jnp.where(kb > tb, 0.0, MASK_SCORE)
    nidx = _cmp_block_of_row(lax.broadcasted_iota(jnp.int32, (ncp, wide), 0), ncp)
    tpos = i * tq + query_of_lane((ncp, wide))
    valid = (nidx * CMP_STRIDE + (CMP_BLOCK - 1) <= tpos) & (nidx < n_cmp)
    cmask_ref[0] = jnp.where(valid, 0.0, MASK_SCORE)
    cmask_ref[1] = jnp.where(valid, 1.0, 0.0)

    scratch = refs[n_io:-3]
    groups = [_nsa_group(g, *refs[:n_io], *[r.at[g] for r in scratch], mask_ref, tri_ref, cmask_ref, n_blk=n_blk)
              for g in range(N_KV_GROUPS)]
    done = object()

    def advance():
        handed = [None] * len(groups)
        while any(r is None for r in handed):
            for gi, grp in enumerate(groups):
                if handed[gi] is None:
                    handed[gi] = next(grp, done)
        return handed

    bodies = advance()

    def pairs(p, carry):
        for phase in range(2):
            for body in bodies:
                body(p, phase)
        return carry

    npair = i // 2
    lax.fori_loop(0, npair - 1, pairs, 0)
    finishers = advance()
    for peeled in (False, True):
        for odd in (False, True):
            @pl.when(((npair >= 1) == peeled) & (i % 2 == int(odd)))
            def _():
                if peeled:
                    pairs(npair - 1, 0)
                for phase in range(2):
                    for finish in finishers:
                        finish(odd, phase)

    assert all(r is done for r in advance())


def _nsa_group(g, q_ref, gate_ref, kc_ref, vct_ref, ks_ref, vst_ref, kw_ref, vwt_ref, ovt_ref, qgain_ref,
               o_ref, qa_ref, m_ref, acc_ref, out_ref, s_ref, cmax_ref, mask_ref, tri_ref, cmask_ref, *, n_blk):
    hpg = HEADS_PER_GROUP
    tq = CHUNK
    dk = HEAD_DIM
    i = pl.program_id(1)
    gd = hpg * dk
    q_lanes = pl.ds(g * gd, gd)
    t0 = i * tq

    nqh = tq // LANES
    half_w = hpg * LANES

    def heads(fn):
        per_head = [fn(h) for h in range(hpg)]
        return jnp.concatenate([x[:, qh * LANES:(qh + 1) * LANES] for qh in range(nqh) for x in per_head], axis=1)

    def head_lanes(x, h):
        return jnp.concatenate([x[:, qh * half_w + h * LANES:qh * half_w + (h + 1) * LANES] for qh in range(nqh)],
                               axis=1)

    q_t = q_ref[0, :, q_lanes].T
    q_t = heads(lambda h: q_t[h * dk:(h + 1) * dk, :])
    ms = jnp.sum(q_t * q_t, axis=0, keepdims=True) * (1.0 / dk)
    qa_ref[0:dk, :] = (q_t * lax.rsqrt(ms + EPS) * qgain_ref[...]).astype(BF16)
    gate_t = gate_ref[0, :, pl.ds(g * LANES, LANES)].T
    gates = [heads(lambda h: gate_t[h * N_BRANCH + br:h * N_BRANCH + br + 1, :]) for br in range(N_BRANCH)]

    causal, win_lo = 0, 1
    sel_state, win_state, void_state = 0, 1, 2

    m_ref[...] = jnp.full(m_ref.shape, MASK_SCORE, F32)
    acc_ref[...] = jnp.zeros(acc_ref.shape, F32)

    def key_rows(ref, c):
        return ref[0, g, pl.ds(pl.multiple_of(c * tq, tq), tq), :]

    def produce(b, k, qrows, masked=False):
        s = _dot(k, qa_ref[0:qrows, :])
        if masked:
            s = s + mask_ref[...]
        s = s.astype(BF16)
        s_ref[b] = s
        cmax_ref[b] = jnp.max(s, axis=0, keepdims=True).astype(F32)

    def consume(b, vt, st):
        m_prev = m_ref[st]
        m_new = jnp.maximum(m_prev, cmax_ref[b])
        alpha = jnp.exp2(m_prev - m_new)
        p = jnp.exp2(s_ref[b] - m_new.astype(BF16))
        acc_ref[st] = alpha * acc_ref[st] + _dot(vt, p)
        m_ref[st] = m_new

    def edge_layout(kind):
        lo_keys, hi_keys = slice(0, LANES), slice(LANES, tq)
        first, second = slice(0, half_w), slice(half_w, 2 * half_w)
        if kind == causal:
            return lo_keys, hi_keys, second, first
        return hi_keys, lo_keys, first, second

    def produce_edge(b, k, qrows, kind):
        full, part, sub, other = edge_layout(kind)
        tri = tri_ref[kind]
        s_full = _dot(k[full], qa_ref[0:qrows, :])
        s_part = _dot(k[part], qa_ref[0:qrows, sub]) + tri
        s_diag = (s_full[:, other] + tri).astype(BF16)
        s_sub = s_full[:, sub].astype(BF16)
        s_part = s_part.astype(BF16)
        s_ref[b, full, other] = s_diag
        s_ref[b, full, sub] = s_sub
        s_ref[b, part, sub] = s_part
        top_diag = jnp.max(s_diag, axis=0, keepdims=True)
        top_sub = jnp.maximum(jnp.max(s_sub, axis=0, keepdims=True), jnp.max(s_part, axis=0, keepdims=True))
        top = jnp.concatenate((top_diag, top_sub) if other.start == 0 else (top_sub, top_diag), axis=1)
        cmax_ref[b] = top.astype(F32)

    def consume_edge(b, vt, st, kind):
        full, part, sub, other = edge_layout(kind)
        m_prev = m_ref[st]
        m_new = jnp.maximum(m_prev, cmax_ref[b])
        alpha = jnp.exp2(m_prev - m_new)
        m_low = m_new.astype(BF16)
        p_full = jnp.exp2(s_ref[b, full, :] - m_low)
        p_part = jnp.exp2(s_ref[b, part, sub] - m_low[:, sub])
        upd = _dot(vt[:, full], p_full)
        upd_part = _dot(vt[:, part], p_part)
        acc_ref[st, :, other] = alpha[:, other] * acc_ref[st, :, other] + upd[:, other]
        acc_ref[st, :, sub] = alpha[:, sub] * acc_ref[st, :, sub] + upd[:, sub] + upd_part
        m_ref[st] = m_new

    def flash_out(st, branch):
        out_ref[...] += gates[branch] * (acc_ref[st, 0:dk, :] * (1.0 / acc_ref[st, dk:dk + 1, :]))

    assert WINDOW == 2 * tq
    lo = jnp.maximum(i - 2, 0)
    mid = jnp.maximum(i - 1, 0)
    lo_state = jnp.where(i >= 2, win_state, void_state)
    mid_state = jnp.where(i >= 1, win_state, void_state)
    yield
    produce_edge(0, key_rows(kw_ref, lo), dk, win_lo)
    yield
    produce(1, key_rows(kw_ref, mid), dk)
    consume_edge(0, vwt_ref[0, g, lo], lo_state, win_lo)
    yield
    produce_edge(0, key_rows(kw_ref, i), dk, causal)
    consume(1, vwt_ref[0, g, mid], mid_state)
    yield

    s = _dot(kc_ref[0, g], qa_ref[0:dk, :]) + cmask_ref[0]
    e = jnp.exp2(s - jnp.max(s, axis=0, keepdims=True))
    p = e * (1.0 / jnp.sum(e, axis=0, keepdims=True)) * cmask_ref[1]
    out_ref[...] = gates[0] * _dot(vct_ref[0, g], p.astype(BF16))
    psum = head_lanes(p, 0)
    for h in range(1, hpg):
        psum = psum + head_lanes(p, h)
    yield

    ovt = ovt_ref[...]
    imp = None
    for part in _split3(psum):
        term = _dot(ovt, part)
        imp = term if imp is None else imp + term
    jblk = lax.broadcasted_iota(jnp.int32, (n_blk, tq), 0)
    cur = (t0 + lax.broadcasted_iota(jnp.int32, (n_blk, tq), 1)) // SEL_BLOCK
    forced = (jblk == 0) | (jblk == cur) | (jblk == cur - 1)
    imp = jnp.where(forced, FORCE_SCORE, imp)
    imp = jnp.where(jblk > cur, MASK_SCORE, imp)
    sub = SUBLANES
    parts = [imp[r:r + sub] for r in range(0, n_blk, sub)]
    ranks = [jnp.zeros((sub, tq), F32) for _ in parts]
    jsub = lax.broadcasted_iota(jnp.int32, (sub, tq), 0)
    for ii in range(n_blk):
        row = imp[ii:ii + 1, :]
        for gi, x in enumerate(parts):
            if ii < gi * sub:
                beats = row >= x
            elif ii >= (gi + 1) * sub:
                beats = row > x
            else:
                beats = (row > x) | ((row == x) & (jsub > ii - gi * sub))
            ranks[gi] = jnp.where(beats, ranks[gi] + 1.0, ranks[gi])
    rank = jnp.concatenate(ranks, axis=0)
    chosen = (rank < float(min(N_SELECT, n_blk))) & (jblk <= cur)
    bias = jnp.where(chosen, 0.0, MASK_SCORE)
    bias = jnp.concatenate([bias, jnp.zeros((dk - n_blk, tq), F32)], axis=0).astype(BF16)
    qa_ref[dk:2 * dk, :] = heads(lambda h: bias)
    yield

    npair = i // 2
    held = jnp.where(i >= 2, 2 * npair - 1, i)
    last = jnp.maximum(i - 1, 0)
    produce(1, key_rows(ks_ref, i), 2 * dk, masked=True)
    consume_edge(0, vwt_ref[0, g, i], win_state, causal)
    flash_out(win_state, 2)

    def pair_body(p, phase):
        c = 2 * p
        if phase == 0:
            produce(0, key_rows(ks_ref, c), 2 * dk)
            consume(1, vst_ref[0, g, jnp.where(p == 0, i, c - 1)], sel_state)
        else:
            produce(1, key_rows(ks_ref, c + 1), 2 * dk)
            consume(0, vst_ref[0, g, c], sel_state)

    yield pair_body

    def finish(odd, phase):
        if phase == 0:
            if odd:
                produce(0, key_rows(ks_ref, last), 2 * dk)
            consume(1, vst_ref[0, g, held], sel_state)
            return
        if odd:
            consume(0, vst_ref[0, g, last], sel_state)
        flash_out(sel_state, 1)
        out = out_ref[...]
        out = jnp.concatenate([head_lanes(out, h) for h in range(hpg)], axis=0)
        o_ref[0, :, q_lanes] = out.T.astype(o_ref.dtype)

    yield finish


def _nsa(q, gate, q_norm, kc, vct, ks, vst, kw, vwt):
    b, s, dq = q.shape
    g = N_KV_GROUPS
    hpg = HEADS_PER_GROUP
    dk = HEAD_DIM
    tq = CHUNK
    n_blk = s // SEL_BLOCK
    n_cmp = (s - CMP_BLOCK) // CMP_STRIDE + 1
    ncp = kc.shape[2]
    assert WINDOW == 2 * tq and n_blk <= dk and (s // tq) % 2 == 0
    rows_np = np.arange(ncp)
    cmp_idx = np.where(rows_np < ncp // 2, 2 * rows_np, 2 * rows_np - (ncp - 1))
    cmp_start = cmp_idx * CMP_STRIDE
    blk_start = np.arange(n_blk) * SEL_BLOCK
    ovt = ((cmp_start[None, :] < blk_start[:, None] + SEL_BLOCK)
           & (cmp_start[None, :] + CMP_BLOCK > blk_start[:, None])
           & (cmp_idx[None, :] < n_cmp)).astype(np.float32)
    qgain = jnp.broadcast_to((q_norm * (dk ** -0.5 * LOG2E))[:, None], (dk, hpg * tq))
    idx = lambda bi, i: (bi, 0, 0, 0)
    idx5 = lambda bi, i: (bi, 0, 0, 0, 0)
    args = (q, gate, kc, vct, ks, vst, kw, vwt, jnp.asarray(ovt, BF16), qgain)
    return pl.pallas_call(
        functools.partial(_nsa_kernel, n_io=len(args) + 1, n_cmp=n_cmp, n_blk=n_blk),
        grid=(b, s // tq),
        in_specs=[
            pl.BlockSpec((1, tq, dq), lambda bi, i: (bi, i, 0)),
            pl.BlockSpec((1, tq, g * LANES), lambda bi, i: (bi, i, 0)),
            pl.BlockSpec((1, g, ncp, dk), idx),
            pl.BlockSpec((1, g, dk, ncp), idx),
            pl.BlockSpec((1, g, s, 2 * dk), idx),
            pl.BlockSpec((1, g, s // tq, V_ROWS, tq), idx5),
            pl.BlockSpec((1, g, s, dk), idx),
            pl.BlockSpec((1, g, s // tq, V_ROWS, tq), idx5),
            _resident((n_blk, ncp)),
            _resident((dk, hpg * tq)),
        ],
        out_specs=pl.BlockSpec((1, tq, dq), lambda bi, i: (bi, i, 0)),
        out_shape=jax.ShapeDtypeStruct((b, s, dq), BF16),
        scratch_shapes=[
            pltpu.VMEM((g, 2 * dk, hpg * tq), BF16),
            pltpu.VMEM((g, 3, 1, hpg * tq), F32),
            pltpu.VMEM((g, 3, V_ROWS, hpg * tq), F32),
            pltpu.VMEM((g, dk, hpg * tq), F32),
            pltpu.VMEM((g, 2, tq, hpg * tq), BF16),
            pltpu.VMEM((g, 2, 1, hpg * tq), F32),
            pltpu.VMEM((tq, hpg * tq), F32),
            pltpu.VMEM((2, LANES, hpg * LANES), F32),
            pltpu.VMEM((2, ncp, hpg * tq), F32),
        ],
        compiler_params=_cparams(("parallel", "arbitrary")),
        name="nsa",
    )(*args)


def _gate_columns(w_qg):
    dq = N_HEADS * HEAD_DIM
    per = HEADS_PER_GROUP * N_BRANCH
    wg = w_qg[:, dq:].reshape(-1, N_KV_GROUPS, per)
    wg = jnp.pad(wg, ((0, 0), (0, 0), (0, LANES - per)))
    return jnp.concatenate([w_qg[:, :dq], wg.reshape(-1, N_KV_GROUPS * LANES)], axis=1)


def kernel(x, ffn_norm, ffn_w_gate_up, ffn_w_down, mix_norm, conv_w_in, conv_w, conv_w_out, kv_norm, kv_w,
           cmp_pos, cmp_w1, cmp_b1, cmp_w2, k_norm, nsa_w_qg, q_norm, nsa_w_o):
    b, s, d = x.shape
    t = b * s
    depth = ffn_norm.shape[0]
    n_a = conv_w_in.shape[0]
    g = N_KV_GROUPS

    ffn_g = ffn_norm.reshape(depth, 2, 1, d)

    def ffn(h2, layer, j, proj=None, tail=None, keep_h=True):
        return _ffn(h2, ffn_g, ffn_w_gate_up, ffn_w_down, (layer, j), proj, tail, keep_h)

    h = x.reshape(t, d)
    shared = None
    for layer in range(depth):
        proj = None
        if layer < n_a:
            h = ffn(h, layer, 0, keep_h=False,
                    tail=_conv_tail(b, s, mix_norm[layer], conv_w_in[layer].astype(BF16), conv_w[layer],
                                    conv_w_out[layer].astype(BF16)))
        else:
            li = layer - n_a
            h, q, gate = ffn(h, layer, 0, tail=_qg_tail(t, mix_norm[layer],
                                                        _gate_columns(nsa_w_qg[li]).astype(BF16)))
            o = _nsa(q.reshape(b, s, -1), gate.reshape(b, s, -1), q_norm[li], *shared)
            proj = (o.reshape(t, -1), nsa_w_o[li].astype(BF16))
        if layer == n_a - 1:
            h, blk, ks, vst, kw, vwt = ffn(h, layer, 1, proj,
                                           tail=_kv_tail(b, s, kv_norm, kv_w.astype(BF16), k_norm))
            blk = blk.reshape(2, b * g, s // CMP_BLOCK, CMP_BLOCK * HEAD_DIM)
            cmp = _compress(blk, cmp_pos.reshape(2, -1), cmp_w1.astype(BF16), cmp_b1,
                            cmp_w2.astype(BF16), k_norm[0])
            cmp = cmp.reshape(2, b, g, -1, HEAD_DIM)
            shared = (cmp[0], cmp[1].transpose(0, 1, 3, 2), ks, vst, kw, vwt)
        else:
            h = ffn(h, layer, 1, proj)
    return h.reshape(b, s, d)
```

```python
import functools

import numpy as np
import jax
import jax.numpy as jnp
from jax import lax
from jax.experimental import pallas as pl
from jax.experimental.pallas import tpu as pltpu

EPS = 1e-6
N_HEADS = 16
HEAD_DIM = 64
N_KV_GROUPS = 4
HEADS_PER_GROUP = N_HEADS // N_KV_GROUPS
N_BRANCH = 3
CMP_BLOCK = 32
CMP_STRIDE = 16
SEL_BLOCK = 64
N_SELECT = 16
WINDOW = 512
FORCE_SCORE = 1e9
MASK_SCORE = -(2.0 ** 100)
LOG2E = 1.4426950408889634

LANES = 128
SUBLANES = 8
BF16_ROWS = 16
CHUNK = 256
V_ROWS = HEAD_DIM + BF16_ROWS
VMEM_LIMIT = 56 * 1024 * 1024

BF16 = jnp.bfloat16
F32 = jnp.float32


def _cparams(sem):
    return pltpu.CompilerParams(dimension_semantics=sem, vmem_limit_bytes=VMEM_LIMIT)


def _rms(x, g):
    ms = jnp.mean(x * x, axis=-1, keepdims=True)
    return x * lax.rsqrt(ms + EPS) * g


def _dot(a, b):
    return jnp.dot(a, b, preferred_element_type=F32)


def _split2(x):
    hi = x.astype(BF16)
    return hi, (x - hi.astype(F32)).astype(BF16)


def _seg_rms(y, seg, seg_t, gain):
    ss = None
    for part in _split2(y * y):
        term = _dot(part, seg)
        ss = term if ss is None else ss + term
    r = lax.rsqrt(ss * (1.0 / HEAD_DIM) + EPS)
    scale = None
    for part in _split2(r):
        term = _dot(part, seg_t)
        scale = term if scale is None else scale + term
    return y * scale * gain


def _seg_matrices(width):
    ind = (np.arange(width)[:, None] // HEAD_DIM == np.arange(LANES)[None, :]).astype(np.float32)
    return jnp.asarray(ind, BF16), jnp.asarray(ind.T, BF16)


def _ffn_kernel(x_ref, g_ref, wgu32_ref, wd32_ref, *rest, dff, has_proj, n_tail_in, n_tail_out, keep_h, tail):
    step = pl.program_id(0)
    proj_refs, rest = (rest[:2], rest[2:]) if has_proj else ((), rest)
    tail_in, rest = rest[:n_tail_in], rest[n_tail_in:]
    o_ref, rest = (rest[0], rest[1:]) if keep_h else (None, rest)
    tail_out, (wgu_ref, wd_ref), tail_scratch = rest[:n_tail_out], rest[n_tail_out:n_tail_out + 2], rest[n_tail_out + 2:]

    @pl.when(step < FFN_INGEST)
    def _():
        for src32, dst in ((wgu32_ref, wgu_ref), (wd32_ref, wd_ref)):
            rows = src32.shape[0]
            dst[pl.ds(pl.multiple_of(step * rows, rows), rows), :] = src32[...].astype(BF16)

    @pl.when(step >= FFN_INGEST)
    def _():
        x = x_ref[...]
        if has_proj:
            a_ref, w_ref = proj_refs
            x = x + _dot(a_ref[...], w_ref[...])
        xn = _rms(x, g_ref[...]).astype(BF16)
        ab = _dot(xn, wgu_ref[...])
        a = ab[:, 0:dff]
        h = (a * jax.nn.sigmoid(a) * ab[:, dff:2 * dff]).astype(BF16)
        out = x + 0.5 * _dot(h, wd_ref[...])
        if keep_h:
            o_ref[...] = out
        if tail is not None:
            tail(out, step - FFN_INGEST, *tail_in, *tail_out, *tail_scratch)


def _resident(shape, lead=()):
    index = tuple(lead) + (0,) * len(shape)
    return pl.BlockSpec((None,) * len(lead) + tuple(shape), lambda *_: index, pipeline_mode=pl.Buffered(1))


FFN_TM = 512
FFN_INGEST = 8


def _ffn(h, g_all, w_gu_all, w_d_all, lead, proj=None, tail=None, keep_h=True):
    tm = FFN_TM
    t, d = h.shape
    dff = w_d_all.shape[-2]
    assert d % (BF16_ROWS * FFN_INGEST) == 0 and dff % (BF16_ROWS * FFN_INGEST) == 0

    def tiled(shape, fn):
        return pl.BlockSpec(shape, lambda i: fn(jnp.maximum(i - FFN_INGEST, 0)))

    def chunked(rows, cols):
        return pl.BlockSpec((None, None, rows, cols), lambda i: (*lead, jnp.minimum(i, FFN_INGEST - 1), 0))

    in_specs = [
        tiled((tm, d), lambda tile: (tile, 0)),
        _resident((1, d), lead),
        chunked(d // FFN_INGEST, 2 * dff),
        chunked(dff // FFN_INGEST, d),
    ]
    args = [h, g_all, w_gu_all, w_d_all]
    if proj is not None:
        a, w = proj
        in_specs += [tiled((tm, a.shape[1]), lambda tile: (tile, 0)), _resident(w.shape)]
        args += [a, w]
    out_specs = [tiled((tm, d), lambda tile: (tile, 0))] if keep_h else []
    out_shape = [jax.ShapeDtypeStruct((t, d), F32)] if keep_h else []
    scratch = [pltpu.VMEM((d, 2 * dff), BF16), pltpu.VMEM((dff, d), BF16)]
    tail_fn, n_tail_in, n_tail_out = None, 0, 0
    if tail is not None:
        tail_fn, tail_args, tail_in_specs, tail_out_blocks, tail_out_shapes, tail_scratch = tail
        in_specs += tail_in_specs
        args += tail_args
        n_tail_in, n_tail_out = len(tail_args), len(tail_out_blocks)
        out_specs += [tiled(shape, fn) for shape, fn in tail_out_blocks]
        out_shape += tail_out_shapes
        scratch += tail_scratch
    outs = pl.pallas_call(
        functools.partial(_ffn_kernel, dff=dff, has_proj=proj is not None, n_tail_in=n_tail_in,
                          n_tail_out=n_tail_out, keep_h=keep_h, tail=tail_fn),
        grid=(FFN_INGEST + t // tm,),
        in_specs=in_specs,
        out_specs=out_specs,
        out_shape=out_shape,
        scratch_shapes=scratch,
        compiler_params=_cparams(("arbitrary",)),
        name="ffn",
    )(*args)
    return outs[0] if len(outs) == 1 else outs


CARRY = SUBLANES


def _conv_tail_kernel(x, tile, g_ref, win_ref, cw_ref, wout_ref, o_ref, vbuf_ref, *, nj):
    tm, d = x.shape

    @pl.when(tile % nj == 0)
    def _():
        vbuf_ref[0:CARRY, :] = jnp.zeros((CARRY, d), F32)

    xn = _rms(x, g_ref[...]).astype(BF16)
    proj = _dot(xn, win_ref[...])
    bg = proj[:, 0:d]
    v = proj[:, d:2 * d] * proj[:, 2 * d:3 * d]
    vbuf_ref[CARRY:CARRY + tm, :] = v
    cw = cw_ref[...]
    conv = (cw[0:1, :] * vbuf_ref[CARRY - 2:CARRY - 2 + tm, :]
            + cw[1:2, :] * vbuf_ref[CARRY - 1:CARRY - 1 + tm, :]
            + cw[2:3, :] * v)
    y = (bg * conv).astype(BF16)
    o_ref[...] = x + _dot(y, wout_ref[...])
    vbuf_ref[0:CARRY, :] = vbuf_ref[tm:tm + CARRY, :]


def _conv_tail(b, s, g, w_in, conv_w, w_out):
    tm = FFN_TM
    d = w_out.shape[0]
    assert s % tm == 0
    return (
        functools.partial(_conv_tail_kernel, nj=s // tm),
        [g.reshape(1, d), w_in, conv_w, w_out],
        [_resident((1, d)), _resident(w_in.shape), _resident(conv_w.shape), _resident(w_out.shape)],
        [((tm, d), lambda tile: (tile, 0))],
        [jax.ShapeDtypeStruct((b * s, d), F32)],
        [pltpu.VMEM((CARRY + tm, d), F32)],
    )


def _kv_tail_kernel(x, tile, g_ref, w_ref, kn_ref, seg_ref, segt_ref, blk_ref, ks_ref, vst_ref, kw_ref, vwt_ref,
                    raw_ref, *, nj):
    tm = x.shape[0]
    j = tile % nj
    xn = _rms(x, g_ref[...]).astype(BF16)
    y = _dot(xn, w_ref[...])
    gw = N_KV_GROUPS * HEAD_DIM
    per = LANES // HEAD_DIM
    for slab in range(raw_ref.shape[0]):
        raw_ref[slab] = y[:, slab * LANES:(slab + 1) * LANES]
    for l in range(CMP_BLOCK):
        for slab in range(raw_ref.shape[0]):
            rows = raw_ref[slab, pl.ds(l, tm // CMP_BLOCK, stride=CMP_BLOCK), :]
            for k in range(per):
                sg = slab * per + k
                blk_ref[sg // N_KV_GROUPS, 0, sg % N_KV_GROUPS, :, l * HEAD_DIM:(l + 1) * HEAD_DIM] = (
                    rows[:, k * HEAD_DIM:(k + 1) * HEAD_DIM])
    pos = j * tm + lax.broadcasted_iota(jnp.int32, (tm, HEAD_DIM), 0)
    lane = lax.broadcasted_iota(jnp.int32, (tm, HEAD_DIM), 1)
    onehot = jnp.where(lane == pos // SEL_BLOCK, 1.0, 0.0).astype(BF16)
    k_sel = _seg_rms(y[:, 2 * gw:3 * gw], seg_ref[...], segt_ref[...], kn_ref[0:1, :]).astype(BF16)
    k_win = _seg_rms(y[:, 4 * gw:5 * gw], seg_ref[...], segt_ref[...], kn_ref[1:2, :]).astype(BF16)
    for grp in range(N_KV_GROUPS):
        ks_ref[0, grp, :, 0:HEAD_DIM] = k_sel[:, grp * HEAD_DIM:(grp + 1) * HEAD_DIM]
        ks_ref[0, grp, :, HEAD_DIM:2 * HEAD_DIM] = onehot
        kw_ref[0, grp] = k_win[:, grp * HEAD_DIM:(grp + 1) * HEAD_DIM]
    extra = jnp.where(lax.broadcasted_iota(jnp.int32, (V_ROWS - HEAD_DIM, CHUNK), 0) == 0, 1.0, 0.0).astype(BF16)
    for slot, ref in ((3, vst_ref), (5, vwt_ref)):
        vt = y[:, slot * gw:(slot + 1) * gw].T
        for grp in range(N_KV_GROUPS):
            for cc in range(tm // CHUNK):
                ref[0, grp, cc, 0:HEAD_DIM, :] = vt[grp * HEAD_DIM:(grp + 1) * HEAD_DIM,
                                                    cc * CHUNK:(cc + 1) * CHUNK].astype(BF16)
                ref[0, grp, cc, HEAD_DIM:V_ROWS, :] = extra


def _kv_tail(b, s, g, w, k_norm):
    tm = FFN_TM
    d, n = w.shape
    ng = N_KV_GROUPS
    gw = ng * HEAD_DIM
    cpt = tm // CHUNK
    nj = s // tm
    assert s // SEL_BLOCK <= HEAD_DIM and s % tm == 0
    seg, seg_t = _seg_matrices(gw)
    gains = jnp.tile(k_norm[1:3], (1, ng))
    return (
        functools.partial(_kv_tail_kernel, nj=nj),
        [g.reshape(1, d), w, gains, seg, seg_t],
        [_resident((1, d)), _resident((d, n)), _resident(gains.shape), _resident(seg.shape),
         _resident(seg_t.shape)],
        [
            ((2, 1, ng, tm // CMP_BLOCK, CMP_BLOCK * HEAD_DIM), lambda i: (0, i // nj, 0, i % nj, 0)),
            ((1, ng, tm, 2 * HEAD_DIM), lambda i: (i // nj, 0, i % nj, 0)),
            ((1, ng, cpt, V_ROWS, CHUNK), lambda i: (i // nj, 0, i % nj, 0, 0)),
            ((1, ng, tm, HEAD_DIM), lambda i: (i // nj, 0, i % nj, 0)),
            ((1, ng, cpt, V_ROWS, CHUNK), lambda i: (i // nj, 0, i % nj, 0, 0)),
        ],
        [
            jax.ShapeDtypeStruct((2, b, ng, s // CMP_BLOCK, CMP_BLOCK * HEAD_DIM), F32),
            jax.ShapeDtypeStruct((b, ng, s, 2 * HEAD_DIM), BF16),
            jax.ShapeDtypeStruct((b, ng, s // CHUNK, V_ROWS, CHUNK), BF16),
            jax.ShapeDtypeStruct((b, ng, s, HEAD_DIM), BF16),
            jax.ShapeDtypeStruct((b, ng, s // CHUNK, V_ROWS, CHUNK), BF16),
        ],
        [pltpu.VMEM((2 * gw // LANES, tm, LANES), F32)],
    )


def _cmp_kernel(x_ref, pos_ref, w1_ref, b1_ref, w2_ref, kn_ref, o_ref):
    c = pl.program_id(0)
    nb, ne, width = x_ref.shape
    rows = nb * ne
    half = width // 2
    x = x_ref[...].reshape(rows, width)
    pos = pos_ref[...]
    w1 = w1_ref[...]
    h_even = _dot((x + pos).astype(BF16), w1)
    a = _dot((x[:, half:] + pos[:, :half]).astype(BF16), w1[0:half])
    b = _dot((x[:, :half] + pos[:, half:]).astype(BF16), w1[half:width])
    h_odd = a + pltpu.roll(b, rows - 1, 0)

    def tail(hid):
        y = _dot(jax.nn.gelu(hid + b1_ref[...]).astype(BF16), w2_ref[...])
        y = jnp.where(c == 0, _rms(y, kn_ref[...]), y)
        return y.astype(o_ref.dtype).reshape(nb, ne, y.shape[-1])

    o_ref[:, 0:ne, :] = tail(h_even)
    o_ref[:, ne:2 * ne, :] = tail(h_odd)


def _compress(xe, pos, w1, b1, w2, kn, *, nb=8):
    _, nseq, ne, width = xe.shape
    nb = min(nb, nseq)
    hid = w1.shape[-1]
    dk = w2.shape[-1]
    return pl.pallas_call(
        _cmp_kernel,
        grid=(2, nseq // nb),
        in_specs=[
            pl.BlockSpec((None, nb, ne, width), lambda c, i: (c, i, 0, 0)),
            pl.BlockSpec((None, 1, width), lambda c, i: (c, 0, 0)),
            pl.BlockSpec((None, width, hid), lambda c, i: (c, 0, 0)),
            pl.BlockSpec((None, 1, hid), lambda c, i: (c, 0, 0)),
            pl.BlockSpec((None, hid, dk), lambda c, i: (c, 0, 0)),
            pl.BlockSpec((1, dk), lambda c, i: (0, 0)),
        ],
        out_specs=pl.BlockSpec((None, nb, 2 * ne, dk), lambda c, i: (c, i, 0, 0)),
        out_shape=jax.ShapeDtypeStruct((2, nseq, 2 * ne, dk), BF16),
        compiler_params=_cparams(("parallel", "parallel")),
        name="compress",
    )(xe, pos.reshape(2, 1, width), w1, b1.reshape(2, 1, hid), w2, kn.reshape(1, dk))


def _qg_tail_kernel(x, tile, g_ref, w_ref, q_ref, gate_ref):
    dq = q_ref.shape[-1]
    xn = _rms(x, g_ref[...]).astype(BF16)
    y = _dot(xn, w_ref[...])
    q_ref[...] = y[:, 0:dq]
    gate_ref[...] = jax.nn.sigmoid(y[:, dq:])


def _qg_tail(t, g, w):
    tm = FFN_TM
    d, n = w.shape
    dq = N_HEADS * HEAD_DIM
    return (
        _qg_tail_kernel,
        [g.reshape(1, d), w],
        [_resident((1, d)), _resident((d, n))],
        [((tm, dq), lambda tile: (tile, 0)), ((tm, n - dq), lambda tile: (tile, 0))],
        [jax.ShapeDtypeStruct((t, dq), F32), jax.ShapeDtypeStruct((t, n - dq), F32)],
        [],
    )


def _split3(x):
    hi = x.astype(BF16)
    r = x - hi.astype(F32)
    mid = r.astype(BF16)
    lo = (r - mid.astype(F32)).astype(BF16)
    return hi, mid, lo


def _cmp_block_of_row(row, nrows):
    return jnp.where(row < nrows // 2, 2 * row, 2 * row - (nrows - 1))


PEELED = 3


def _nsa_kernel(*refs, n_io, n_cmp, n_blk):
    mask_ref, tri_ref, cmask_ref = refs[-3:]
    tq = CHUNK
    _, ncp, wide = cmask_ref.shape
    i = pl.program_id(1)
    half_w = wide // (tq // LANES)

    def query_of_lane(shape):
        lane = lax.broadcasted_iota(jnp.int32, shape, 1)
        return (lane // half_w) * LANES + (lane & (LANES - 1))

    kk = lax.broadcasted_iota(jnp.int32, (tq, wide), 0)
    mask_ref[...] = jnp.where(kk <= query_of_lane((tq, wide)), 0.0, MASK_SCORE)
    kb = lax.broadcasted_iota(jnp.int32, (LANES, half_w), 0)
    tb = lax.broadcasted_iota(jnp.int32, (LANES, half_w), 1) & (LANES - 1)
    tri_ref[0] = jnp.where(kb <= tb, 0.0, MASK_SCORE)
    tri_ref[1] = jnp.where(kb > tb, 0.0, MASK_SCORE)
    nidx = _cmp_block_of_row(lax.broadcasted_iota(jnp.int32, (ncp, wide), 0), ncp)
    tpos = i * tq + query_of_lane((ncp, wide))
    valid = (nidx * CMP_STRIDE + (CMP_BLOCK - 1) <= tpos) & (nidx < n_cmp)
    cmask_ref[0] = jnp.where(valid, 0.0, MASK_SCORE)
    cmask_ref[1] = jnp.where(valid, 1.0, 0.0)

    scratch = refs[n_io:-3]
    groups = [_nsa_group(g, *refs[:n_io], *[r.at[g] for r in scratch], mask_ref, tri_ref, cmask_ref, n_blk=n_blk)
              for g in range(N_KV_GROUPS)]
    done = object()

    def advance():
        handed = [None] * len(groups)
        while any(r is None for r in handed):
            for gi, grp in enumerate(groups):
                if handed[gi] is None:
                    handed[gi] = next(grp, done)
        return handed

    bodies = advance()

    def pairs(p, carry):
        for phase in range(2):
            for body in bodies:
                body(p, phase)
        return carry

    npair = i // 2
    lax.fori_loop(0, npair - PEELED, pairs, 0)
    finishers = advance()
    for peeled in range(PEELED + 1):
        for odd in (False, True):
            @pl.when((jnp.minimum(npair, PEELED) == peeled) & (i % 2 == int(odd)))
            def _():
                for k in range(peeled):
                    pairs(npair - peeled + k, 0)
                for phase in range(2):
                    for finish in finishers:
                        finish(odd, phase)

    assert all(r is done for r in advance())


def _nsa_group(g, q_ref, gate_ref, kc_ref, vct_ref, ks_ref, vst_ref, kw_ref, vwt_ref, ovt_ref, qgain_ref,
               o_ref, qa_ref, m_ref, acc_ref, out_ref, s_ref, cmax_ref, mask_ref, tri_ref, cmask_ref, *, n_blk):
    hpg = HEADS_PER_GROUP
    tq = CHUNK
    dk = HEAD_DIM
    i = pl.program_id(1)
    gd = hpg * dk
    q_lanes = pl.ds(g * gd, gd)
    t0 = i * tq

    nqh = tq // LANES
    half_w = hpg * LANES

    def heads(fn):
        per_head = [fn(h) for h in range(hpg)]
        return jnp.concatenate([x[:, qh * LANES:(qh + 1) * LANES] for qh in range(nqh) for x in per_head], axis=1)

    def head_lanes(x, h):
        return jnp.concatenate([x[:, qh * half_w + h * LANES:qh * half_w + (h + 1) * LANES] for qh in range(nqh)],
                               axis=1)

    q_t = q_ref[0, :, q_lanes].T
    q_t = heads(lambda h: q_t[h * dk:(h + 1) * dk, :])
    ms = jnp.sum(q_t * q_t, axis=0, keepdims=True) * (1.0 / dk)
    qa_ref[0:dk, :] = (q_t * lax.rsqrt(ms + EPS) * qgain_ref[...]).astype(BF16)
    gate_t = gate_ref[0, :, pl.ds(g * LANES, LANES)].T
    gates = [heads(lambda h: gate_t[h * N_BRANCH + br:h * N_BRANCH + br + 1, :]) for br in range(N_BRANCH)]

    causal, win_lo = 0, 1
    sel_state, win_state, void_state = 0, 1, 2

    m_ref[...] = jnp.full(m_ref.shape, MASK_SCORE, F32)
    acc_ref[...] = jnp.zeros(acc_ref.shape, F32)

    def key_rows(ref, c):
        return ref[0, g, pl.ds(pl.multiple_of(c * tq, tq), tq), :]

    def produce(b, k, qrows, masked=False):
        s = _dot(k, qa_ref[0:qrows, :])
        if masked:
            s = s + mask_ref[...]
        s = s.astype(BF16)
        s_ref[b] = s
        cmax_ref[b] = jnp.max(s, axis=0, keepdims=True).astype(F32)

    def consume(b, vt, st):
        m_prev = m_ref[st]
        m_new = jnp.maximum(m_prev, cmax_ref[b])
        alpha = jnp.exp2(m_prev - m_new)
        p = jnp.exp2(s_ref[b] - m_new.astype(BF16))
        acc_ref[st] = alpha * acc_ref[st] + _dot(vt, p)
        m_ref[st] = m_new

    def edge_layout(kind):
        lo_keys, hi_keys = slice(0, LANES), slice(LANES, tq)
        first, second = slice(0, half_w), slice(half_w, 2 * half_w)
        if kind == causal:
            return lo_keys, hi_keys, second, first
        return hi_keys, lo_keys, first, second

    def produce_edge(b, k, qrows, kind):
        full, part, sub, other = edge_layout(kind)
        tri = tri_ref[kind]
        s_full = _dot(k[full], qa_ref[0:qrows, :])
        s_part = _dot(k[part], qa_ref[0:qrows, sub]) + tri
        s_diag = (s_full[:, other] + tri).astype(BF16)
        s_sub = s_full[:, sub].astype(BF16)
        s_part = s_part.astype(BF16)
        s_ref[b, full, other] = s_diag
        s_ref[b, full, sub] = s_sub
        s_ref[b, part, sub] = s_part
        top_diag = jnp.max(s_diag, axis=0, keepdims=True)
        top_sub = jnp.maximum(jnp.max(s_sub, axis=0, keepdims=True), jnp.max(s_part, axis=0, keepdims=True))
        top = jnp.concatenate((top_diag, top_sub) if other.start == 0 else (top_sub, top_diag), axis=1)
        cmax_ref[b] = top.astype(F32)

    def consume_edge(b, vt, st, kind):
        full, part, sub, other = edge_layout(kind)
        m_prev = m_ref[st]
        m_new = jnp.maximum(m_prev, cmax_ref[b])
        alpha = jnp.exp2(m_prev - m_new)
        m_low = m_new.astype(BF16)
        p_full = jnp.exp2(s_ref[b, full, :] - m_low)
        p_part = jnp.exp2(s_ref[b, part, sub] - m_low[:, sub])
        upd = _dot(vt[:, full], p_full)
        upd_part = _dot(vt[:, part], p_part)
        acc_ref[st, :, other] = alpha[:, other] * acc_ref[st, :, other] + upd[:, other]
        acc_ref[st, :, sub] = alpha[:, sub] * acc_ref[st, :, sub] + upd[:, sub] + upd_part
        m_ref[st] = m_new

    def flash_out(st, branch):
        out_ref[...] += gates[branch] * (acc_ref[st, 0:dk, :] * (1.0 / acc_ref[st, dk:dk + 1, :]))

    assert WINDOW == 2 * tq
    lo = jnp.maximum(i - 2, 0)
    mid = jnp.maximum(i - 1, 0)
    lo_state = jnp.where(i >= 2, win_state, void_state)
    mid_state = jnp.where(i >= 1, win_state, void_state)
    yield
    produce_edge(0, key_rows(kw_ref, lo), dk, win_lo)
    yield
    produce(1, key_rows(kw_ref, mid), dk)
    consume_edge(0, vwt_ref[0, g, lo], lo_state, win_lo)
    yield
    produce_edge(0, key_rows(kw_ref, i), dk, causal)
    consume(1, vwt_ref[0, g, mid], mid_state)
    yield

    s = _dot(kc_ref[0, g], qa_ref[0:dk, :]) + cmask_ref[0]
    e = jnp.exp2(s - jnp.max(s, axis=0, keepdims=True))
    p = e * (1.0 / jnp.sum(e, axis=0, keepdims=True)) * cmask_ref[1]
    out_ref[...] = gates[0] * _dot(vct_ref[0, g], p.astype(BF16))
    psum = head_lanes(p, 0)
    for h in range(1, hpg):
        psum = psum + head_lanes(p, h)
    yield

    ovt = ovt_ref[...]
    imp = None
    for part in _split3(psum):
        term = _dot(ovt, part)
        imp = term if imp is None else imp + term
    jblk = lax.broadcasted_iota(jnp.int32, (n_blk, tq), 0)
    cur = (t0 + lax.broadcasted_iota(jnp.int32, (n_blk, tq), 1)) // SEL_BLOCK
    forced = (jblk == 0) | (jblk == cur) | (jblk == cur - 1)
    imp = jnp.where(forced, FORCE_SCORE, imp)
    imp = jnp.where(jblk > cur, MASK_SCORE, imp)
    sub = SUBLANES
    parts = [imp[r:r + sub] for r in range(0, n_blk, sub)]
    ranks = [jnp.zeros((sub, tq), F32) for _ in parts]
    jsub = lax.broadcasted_iota(jnp.int32, (sub, tq), 0)
    for ii in range(n_blk):
        row = imp[ii:ii + 1, :]
        for gi, x in enumerate(parts):
            if ii < gi * sub:
                beats = row >= x
            elif ii >= (gi + 1) * sub:
                beats = row > x
            else:
                beats = (row > x) | ((row == x) & (jsub > ii - gi * sub))
            ranks[gi] = jnp.where(beats, ranks[gi] + 1.0, ranks[gi])
    rank = jnp.concatenate(ranks, axis=0)
    chosen = (rank < float(min(N_SELECT, n_blk))) & (jblk <= cur)
    bias = jnp.where(chosen, 0.0, MASK_SCORE)
    bias = jnp.concatenate([bias, jnp.zeros((dk - n_blk, tq), F32)], axis=0).astype(BF16)
    qa_ref[dk:2 * dk, :] = heads(lambda h: bias)
    yield

    npair = i // 2
    held = jnp.where(i >= 2, 2 * npair - 1, i)
    last = jnp.maximum(i - 1, 0)
    produce(1, key_rows(ks_ref, i), 2 * dk, masked=True)
    consume_edge(0, vwt_ref[0, g, i], win_state, causal)
    flash_out(win_state, 2)

    def pair_body(p, phase):
        c = 2 * p
        if phase == 0:
            produce(0, key_rows(ks_ref, c), 2 * dk)
            consume(1, vst_ref[0, g, jnp.where(p == 0, i, c - 1)], sel_state)
        else:
            produce(1, key_rows(ks_ref, c + 1), 2 * dk)
            consume(0, vst_ref[0, g, c], sel_state)

    yield pair_body

    def finish(odd, phase):
        if phase == 0:
            if odd:
                produce(0, key_rows(ks_ref, last), 2 * dk)
            consume(1, vst_ref[0, g, held], sel_state)
            return
        if odd:
            consume(0, vst_ref[0, g, last], sel_state)
        flash_out(sel_state, 1)
        out = out_ref[...]
        out = jnp.concatenate([head_lanes(out, h) for h in range(hpg)], axis=0)
        o_ref[0, :, q_lanes] = out.T.astype(o_ref.dtype)

    yield finish


def _nsa(q, gate, q_norm, kc, vct, ks, vst, kw, vwt):
    b, s, dq = q.shape
    g = N_KV_GROUPS
    hpg = HEADS_PER_GROUP
    dk = HEAD_DIM
    tq = CHUNK
    n_blk = s // SEL_BLOCK
    n_cmp = (s - CMP_BLOCK) // CMP_STRIDE + 1
    ncp = kc.shape[2]
    assert WINDOW == 2 * tq and n_blk <= dk and (s // tq) % 2 == 0
    rows_np = np.arange(ncp)
    cmp_idx = np.where(rows_np < ncp // 2, 2 * rows_np, 2 * rows_np - (ncp - 1))
    cmp_start = cmp_idx * CMP_STRIDE
    blk_start = np.arange(n_blk) * SEL_BLOCK
    ovt = ((cmp_start[None, :] < blk_start[:, None] + SEL_BLOCK)
           & (cmp_start[None, :] + CMP_BLOCK > blk_start[:, None])
           & (cmp_idx[None, :] < n_cmp)).astype(np.float32)
    qgain = jnp.broadcast_to((q_norm * (dk ** -0.5 * LOG2E))[:, None], (dk, hpg * tq))
    idx = lambda bi, i: (bi, 0, 0, 0)
    idx5 = lambda bi, i: (bi, 0, 0, 0, 0)
    args = (q, gate, kc, vct, ks, vst, kw, vwt, jnp.asarray(ovt, BF16), qgain)
    return pl.pallas_call(
        functools.partial(_nsa_kernel, n_io=len(args) + 1, n_cmp=n_cmp, n_blk=n_blk),
        grid=(b, s // tq),
        in_specs=[
            pl.BlockSpec((1, tq, dq), lambda bi, i: (bi, i, 0)),
            pl.BlockSpec((1, tq, g * LANES), lambda bi, i: (bi, i, 0)),
            pl.BlockSpec((1, g, ncp, dk), idx),
            pl.BlockSpec((1, g, dk, ncp), idx),
            pl.BlockSpec((1, g, s, 2 * dk), idx),
            pl.BlockSpec((1, g, s // tq, V_ROWS, tq), idx5),
            pl.BlockSpec((1, g, s, dk), idx),
            pl.BlockSpec((1, g, s // tq, V_ROWS, tq), idx5),
            _resident((n_blk, ncp)),
            _resident((dk, hpg * tq)),
        ],
        out_specs=pl.BlockSpec((1, tq, dq), lambda bi, i: (bi, i, 0)),
        out_shape=jax.ShapeDtypeStruct((b, s, dq), BF16),
        scratch_shapes=[
            pltpu.VMEM((g, 2 * dk, hpg * tq), BF16),
            pltpu.VMEM((g, 3, 1, hpg * tq), F32),
            pltpu.VMEM((g, 3, V_ROWS, hpg * tq), F32),
            pltpu.VMEM((g, dk, hpg * tq), F32),
            pltpu.VMEM((g, 2, tq, hpg * tq), BF16),
            pltpu.VMEM((g, 2, 1, hpg * tq), F32),
            pltpu.VMEM((tq, hpg * tq), F32),
            pltpu.VMEM((2, LANES, hpg * LANES), F32),
            pltpu.VMEM((2, ncp, hpg * tq), F32),
        ],
        compiler_params=_cparams(("parallel", "arbitrary")),
        name="nsa",
    )(*args)


def _gate_columns(w_qg):
    dq = N_HEADS * HEAD_DIM
    per = HEADS_PER_GROUP * N_BRANCH
    wg = w_qg[:, dq:].reshape(-1, N_KV_GROUPS, per)
    wg = jnp.pad(wg, ((0, 0), (0, 0), (0, LANES - per)))
    return jnp.concatenate([w_qg[:, :dq], wg.reshape(-1, N_KV_GROUPS * LANES)], axis=1)


def kernel(x, ffn_norm, ffn_w_gate_up, ffn_w_down, mix_norm, conv_w_in, conv_w, conv_w_out, kv_norm, kv_w,
           cmp_pos, cmp_w1, cmp_b1, cmp_w2, k_norm, nsa_w_qg, q_norm, nsa_w_o):
    b, s, d = x.shape
    t = b * s
    depth = ffn_norm.shape[0]
    n_a = conv_w_in.shape[0]
    g = N_KV_GROUPS

    ffn_g = ffn_norm.reshape(depth, 2, 1, d)

    def ffn(h2, layer, j, proj=None, tail=None, keep_h=True):
        return _ffn(h2, ffn_g, ffn_w_gate_up, ffn_w_down, (layer, j), proj, tail, keep_h)

    h = x.reshape(t, d)
    shared = None
    for layer in range(depth):
        proj = None
        if layer < n_a:
            h = ffn(h, layer, 0, keep_h=False,
                    tail=_conv_tail(b, s, mix_norm[layer], conv_w_in[layer].astype(BF16), conv_w[layer],
                                    conv_w_out[layer].astype(BF16)))
        else:
            li = layer - n_a
            h, q, gate = ffn(h, layer, 0, tail=_qg_tail(t, mix_norm[layer],
                                                        _gate_columns(nsa_w_qg[li]).astype(BF16)))
            o = _nsa(q.reshape(b, s, -1), gate.reshape(b, s, -1), q_norm[li], *shared)
            proj = (o.reshape(t, -1), nsa_w_o[li].astype(BF16))
        if layer == n_a - 1:
            h, blk, ks, vst, kw, vwt = ffn(h, layer, 1, proj,
                                           tail=_kv_tail(b, s, kv_norm, kv_w.astype(BF16), k_norm))
            blk = blk.reshape(2, b * g, s // CMP_BLOCK, CMP_BLOCK * HEAD_DIM)
            cmp = _compress(blk, cmp_pos.reshape(2, -1), cmp_w1.astype(BF16), cmp_b1,
                            cmp_w2.astype(BF16), k_norm[0])
            cmp = cmp.reshape(2, b, g, -1, HEAD_DIM)
            shared = (cmp[0], cmp[1].transpose(0, 1, 3, 2), ks, vst, kw, vwt)
        else:
            h = ffn(h, layer, 1, proj)
    return h.reshape(b, s, d)
```

```python
import functools

import numpy as np
import jax
import jax.numpy as jnp
from jax import lax
from jax.experimental import pallas as pl
from jax.experimental.pallas import tpu as pltpu

EPS = 1e-6
N_HEADS = 16
HEAD_DIM = 64
N_KV_GROUPS = 4
HEADS_PER_GROUP = N_HEADS // N_KV_GROUPS
N_BRANCH = 3
CMP_BLOCK = 32
CMP_STRIDE = 16
SEL_BLOCK = 64
N_SELECT = 16
WINDOW = 512
FORCE_SCORE = 1e9
MASK_SCORE = -(2.0 ** 100)
LOG2E = 1.4426950408889634

LANES = 128
SUBLANES = 8
BF16_ROWS = 16
CHUNK = 256
V_ROWS = HEAD_DIM + BF16_ROWS
VMEM_LIMIT = 56 * 1024 * 1024

BF16 = jnp.bfloat16
F32 = jnp.float32


def _cparams(sem):
    return pltpu.CompilerParams(dimension_semantics=sem, vmem_limit_bytes=VMEM_LIMIT)


def _rms(x, g):
    ms = jnp.mean(x * x, axis=-1, keepdims=True)
    return x * lax.rsqrt(ms + EPS) * g


def _dot(a, b):
    return jnp.dot(a, b, preferred_element_type=F32)


def _split2(x):
    hi = x.astype(BF16)
    return hi, (x - hi.astype(F32)).astype(BF16)


def _seg_rms(y, seg, seg_t, gain):
    ss = None
    for part in _split2(y * y):
        term = _dot(part, seg)
        ss = term if ss is None else ss + term
    r = lax.rsqrt(ss * (1.0 / HEAD_DIM) + EPS)
    scale = None
    for part in _split2(r):
        term = _dot(part, seg_t)
        scale = term if scale is None else scale + term
    return y * scale * gain


def _seg_matrices(width):
    ind = (np.arange(width)[:, None] // HEAD_DIM == np.arange(LANES)[None, :]).astype(np.float32)
    return jnp.asarray(ind, BF16), jnp.asarray(ind.T, BF16)


def _ffn_kernel(x_ref, g_ref, wgu32_ref, wd32_ref, *rest, dff, has_proj, n_tail_in, n_tail_out, keep_h, tail):
    step = pl.program_id(0)
    proj_refs, rest = (rest[:2], rest[2:]) if has_proj else ((), rest)
    tail_in, rest = rest[:n_tail_in], rest[n_tail_in:]
    o_ref, rest = (rest[0], rest[1:]) if keep_h else (None, rest)
    tail_out, (wgu_ref, wd_ref), tail_scratch = rest[:n_tail_out], rest[n_tail_out:n_tail_out + 2], rest[n_tail_out + 2:]

    @pl.when(step < FFN_INGEST)
    def _():
        for src32, dst in ((wgu32_ref, wgu_ref), (wd32_ref, wd_ref)):
            rows = src32.shape[0]
            dst[pl.ds(pl.multiple_of(step * rows, rows), rows), :] = src32[...].astype(BF16)

    @pl.when(step >= FFN_INGEST)
    def _():
        x = x_ref[...]
        if has_proj:
            a_ref, w_ref = proj_refs
            x = x + _dot(a_ref[...], w_ref[...])
        xn = _rms(x, g_ref[...]).astype(BF16)
        ab = _dot(xn, wgu_ref[...])
        a = ab[:, 0:dff]
        h = (a * jax.nn.sigmoid(a) * ab[:, dff:2 * dff]).astype(BF16)
        out = x + 0.5 * _dot(h, wd_ref[...])
        if keep_h:
            o_ref[...] = out
        if tail is not None:
            tail(out, step - FFN_INGEST, *tail_in, *tail_out, *tail_scratch)


def _resident(shape, lead=()):
    index = tuple(lead) + (0,) * len(shape)
    return pl.BlockSpec((None,) * len(lead) + tuple(shape), lambda *_: index, pipeline_mode=pl.Buffered(1))


FFN_TM = 512
FFN_INGEST = 16


def _ffn(h, g_all, w_gu_all, w_d_all, lead, proj=None, tail=None, keep_h=True):
    tm = FFN_TM
    t, d = h.shape
    dff = w_d_all.shape[-2]
    assert d % (BF16_ROWS * FFN_INGEST) == 0 and dff % (BF16_ROWS * FFN_INGEST) == 0

    def tiled(shape, fn):
        return pl.BlockSpec(shape, lambda i: fn(jnp.maximum(i - FFN_INGEST, 0)))

    def chunked(rows, cols):
        return pl.BlockSpec((None, None, rows, cols), lambda i: (*lead, jnp.minimum(i, FFN_INGEST - 1), 0))

    in_specs = [
        tiled((tm, d), lambda tile: (tile, 0)),
        _resident((1, d), lead),
        chunked(d // FFN_INGEST, 2 * dff),
        chunked(dff // FFN_INGEST, d),
    ]
    args = [h, g_all, w_gu_all, w_d_all]
    if proj is not None:
        a, w = proj
        in_specs += [tiled((tm, a.shape[1]), lambda tile: (tile, 0)), _resident(w.shape)]
        args += [a, w]
    out_specs = [tiled((tm, d), lambda tile: (tile, 0))] if keep_h else []
    out_shape = [jax.ShapeDtypeStruct((t, d), F32)] if keep_h else []
    scratch = [pltpu.VMEM((d, 2 * dff), BF16), pltpu.VMEM((dff, d), BF16)]
    tail_fn, n_tail_in, n_tail_out = None, 0, 0
    if tail is not None:
        tail_fn, tail_args, tail_in_specs, tail_out_blocks, tail_out_shapes, tail_scratch = tail
        in_specs += tail_in_specs
        args += tail_args
        n_tail_in, n_tail_out = len(tail_args), len(tail_out_blocks)
        out_specs += [tiled(shape, fn) for shape, fn in tail_out_blocks]
        out_shape += tail_out_shapes
        scratch += tail_scratch
    outs = pl.pallas_call(
        functools.partial(_ffn_kernel, dff=dff, has_proj=proj is not None, n_tail_in=n_tail_in,
                          n_tail_out=n_tail_out, keep_h=keep_h, tail=tail_fn),
        grid=(FFN_INGEST + t // tm,),
        in_specs=in_specs,
        out_specs=out_specs,
        out_shape=out_shape,
        scratch_shapes=scratch,
        compiler_params=_cparams(("arbitrary",)),
        name="ffn",
    )(*args)
    return outs[0] if len(outs) == 1 else outs


CARRY = SUBLANES


def _conv_tail_kernel(x, tile, g_ref, win_ref, cw_ref, wout_ref, o_ref, vbuf_ref, *, nj):
    tm, d = x.shape

    @pl.when(tile % nj == 0)
    def _():
        vbuf_ref[0:CARRY, :] = jnp.zeros((CARRY, d), F32)

    xn = _rms(x, g_ref[...]).astype(BF16)
    proj = _dot(xn, win_ref[...])
    bg = proj[:, 0:d]
    v = proj[:, d:2 * d] * proj[:, 2 * d:3 * d]
    vbuf_ref[CARRY:CARRY + tm, :] = v
    cw = cw_ref[...]
    conv = (cw[0:1, :] * vbuf_ref[CARRY - 2:CARRY - 2 + tm, :]
            + cw[1:2, :] * vbuf_ref[CARRY - 1:CARRY - 1 + tm, :]
            + cw[2:3, :] * v)
    y = (bg * conv).astype(BF16)
    o_ref[...] = x + _dot(y, wout_ref[...])
    vbuf_ref[0:CARRY, :] = vbuf_ref[tm:tm + CARRY, :]


def _conv_tail(b, s, g, w_in, conv_w, w_out):
    tm = FFN_TM
    d = w_out.shape[0]
    assert s % tm == 0
    return (
        functools.partial(_conv_tail_kernel, nj=s // tm),
        [g.reshape(1, d), w_in, conv_w, w_out],
        [_resident((1, d)), _resident(w_in.shape), _resident(conv_w.shape), _resident(w_out.shape)],
        [((tm, d), lambda tile: (tile, 0))],
        [jax.ShapeDtypeStruct((b * s, d), F32)],
        [pltpu.VMEM((CARRY + tm, d), F32)],
    )


def _kv_tail_kernel(x, tile, g_ref, w_ref, kn_ref, seg_ref, segt_ref, blk_ref, ks_ref, vst_ref, kw_ref, vwt_ref,
                    raw_ref, *, nj):
    tm = x.shape[0]
    j = tile % nj
    xn = _rms(x, g_ref[...]).astype(BF16)
    y = _dot(xn, w_ref[...])
    gw = N_KV_GROUPS * HEAD_DIM
    per = LANES // HEAD_DIM
    for slab in range(raw_ref.shape[0]):
        raw_ref[slab] = y[:, slab * LANES:(slab + 1) * LANES]
    for l in range(CMP_BLOCK):
        for slab in range(raw_ref.shape[0]):
            rows = raw_ref[slab, pl.ds(l, tm // CMP_BLOCK, stride=CMP_BLOCK), :]
            for k in range(per):
                sg = slab * per + k
                blk_ref[sg // N_KV_GROUPS, 0, sg % N_KV_GROUPS, :, l * HEAD_DIM:(l + 1) * HEAD_DIM] = (
                    rows[:, k * HEAD_DIM:(k + 1) * HEAD_DIM])
    pos = j * tm + lax.broadcasted_iota(jnp.int32, (tm, HEAD_DIM), 0)
    lane = lax.broadcasted_iota(jnp.int32, (tm, HEAD_DIM), 1)
    onehot = jnp.where(lane == pos // SEL_BLOCK, 1.0, 0.0).astype(BF16)
    k_sel = _seg_rms(y[:, 2 * gw:3 * gw], seg_ref[...], segt_ref[...], kn_ref[0:1, :]).astype(BF16)
    k_win = _seg_rms(y[:, 4 * gw:5 * gw], seg_ref[...], segt_ref[...], kn_ref[1:2, :]).astype(BF16)
    for grp in range(N_KV_GROUPS):
        ks_ref[0, grp, :, 0:HEAD_DIM] = k_sel[:, grp * HEAD_DIM:(grp + 1) * HEAD_DIM]
        ks_ref[0, grp, :, HEAD_DIM:2 * HEAD_DIM] = onehot
        kw_ref[0, grp] = k_win[:, grp * HEAD_DIM:(grp + 1) * HEAD_DIM]
    extra = jnp.where(lax.broadcasted_iota(jnp.int32, (V_ROWS - HEAD_DIM, CHUNK), 0) == 0, 1.0, 0.0).astype(BF16)
    for slot, ref in ((3, vst_ref), (5, vwt_ref)):
        vt = y[:, slot * gw:(slot + 1) * gw].T
        for grp in range(N_KV_GROUPS):
            for cc in range(tm // CHUNK):
                ref[0, grp, cc, 0:HEAD_DIM, :] = vt[grp * HEAD_DIM:(grp + 1) * HEAD_DIM,
                                                    cc * CHUNK:(cc + 1) * CHUNK].astype(BF16)
                ref[0, grp, cc, HEAD_DIM:V_ROWS, :] = extra


def _kv_tail(b, s, g, w, k_norm):
    tm = FFN_TM
    d, n = w.shape
    ng = N_KV_GROUPS
    gw = ng * HEAD_DIM
    cpt = tm // CHUNK
    nj = s // tm
    assert s // SEL_BLOCK <= HEAD_DIM and s % tm == 0
    seg, seg_t = _seg_matrices(gw)
    gains = jnp.tile(k_norm[1:3], (1, ng))
    return (
        functools.partial(_kv_tail_kernel, nj=nj),
        [g.reshape(1, d), w, gains, seg, seg_t],
        [_resident((1, d)), _resident((d, n)), _resident(gains.shape), _resident(seg.shape),
         _resident(seg_t.shape)],
        [
            ((2, 1, ng, tm // CMP_BLOCK, CMP_BLOCK * HEAD_DIM), lambda i: (0, i // nj, 0, i % nj, 0)),
            ((1, ng, tm, 2 * HEAD_DIM), lambda i: (i // nj, 0, i % nj, 0)),
            ((1, ng, cpt, V_ROWS, CHUNK), lambda i: (i // nj, 0, i % nj, 0, 0)),
            ((1, ng, tm, HEAD_DIM), lambda i: (i // nj, 0, i % nj, 0)),
            ((1, ng, cpt, V_ROWS, CHUNK), lambda i: (i // nj, 0, i % nj, 0, 0)),
        ],
        [
            jax.ShapeDtypeStruct((2, b, ng, s // CMP_BLOCK, CMP_BLOCK * HEAD_DIM), F32),
            jax.ShapeDtypeStruct((b, ng, s, 2 * HEAD_DIM), BF16),
            jax.ShapeDtypeStruct((b, ng, s // CHUNK, V_ROWS, CHUNK), BF16),
            jax.ShapeDtypeStruct((b, ng, s, HEAD_DIM), BF16),
            jax.ShapeDtypeStruct((b, ng, s // CHUNK, V_ROWS, CHUNK), BF16),
        ],
        [pltpu.VMEM((2 * gw // LANES, tm, LANES), F32)],
    )


def _cmp_kernel(x_ref, pos_ref, w1_ref, b1_ref, w2_ref, kn_ref, o_ref):
    c = pl.program_id(0)
    nb, ne, width = x_ref.shape
    rows = nb * ne
    half = width // 2
    x = x_ref[...].reshape(rows, width)
    pos = pos_ref[...]
    w1 = w1_ref[...]
    h_even = _dot((x + pos).astype(BF16), w1)
    a = _dot((x[:, half:] + pos[:, :half]).astype(BF16), w1[0:half])
    b = _dot((x[:, :half] + pos[:, half:]).astype(BF16), w1[half:width])
    h_odd = a + pltpu.roll(b, rows - 1, 0)

    def tail(hid):
        y = _dot(jax.nn.gelu(hid + b1_ref[...]).astype(BF16), w2_ref[...])
        y = jnp.where(c == 0, _rms(y, kn_ref[...]), y)
        return y.astype(o_ref.dtype).reshape(nb, ne, y.shape[-1])

    o_ref[:, 0:ne, :] = tail(h_even)
    o_ref[:, ne:2 * ne, :] = tail(h_odd)


def _compress(xe, pos, w1, b1, w2, kn, *, nb=8):
    _, nseq, ne, width = xe.shape
    nb = min(nb, nseq)
    hid = w1.shape[-1]
    dk = w2.shape[-1]
    return pl.pallas_call(
        _cmp_kernel,
        grid=(2, nseq // nb),
        in_specs=[
            pl.BlockSpec((None, nb, ne, width), lambda c, i: (c, i, 0, 0)),
            pl.BlockSpec((None, 1, width), lambda c, i: (c, 0, 0)),
            pl.BlockSpec((None, width, hid), lambda c, i: (c, 0, 0)),
            pl.BlockSpec((None, 1, hid), lambda c, i: (c, 0, 0)),
            pl.BlockSpec((None, hid, dk), lambda c, i: (c, 0, 0)),
            pl.BlockSpec((1, dk), lambda c, i: (0, 0)),
        ],
        out_specs=pl.BlockSpec((None, nb, 2 * ne, dk), lambda c, i: (c, i, 0, 0)),
        out_shape=jax.ShapeDtypeStruct((2, nseq, 2 * ne, dk), BF16),
        compiler_params=_cparams(("parallel", "parallel")),
        name="compress",
    )(xe, pos.reshape(2, 1, width), w1, b1.reshape(2, 1, hid), w2, kn.reshape(1, dk))


def _qg_tail_kernel(x, tile, g_ref, w_ref, q_ref, gate_ref):
    dq = q_ref.shape[-1]
    xn = _rms(x, g_ref[...]).astype(BF16)
    y = _dot(xn, w_ref[...])
    q_ref[...] = y[:, 0:dq]
    gate_ref[...] = jax.nn.sigmoid(y[:, dq:])


def _qg_tail(t, g, w):
    tm = FFN_TM
    d, n = w.shape
    dq = N_HEADS * HEAD_DIM
    return (
        _qg_tail_kernel,
        [g.reshape(1, d), w],
        [_resident((1, d)), _resident((d, n))],
        [((tm, dq), lambda tile: (tile, 0)), ((tm, n - dq), lambda tile: (tile, 0))],
        [jax.ShapeDtypeStruct((t, dq), F32), jax.ShapeDtypeStruct((t, n - dq), F32)],
        [],
    )


def _split3(x):
    hi = x.astype(BF16)
    r = x - hi.astype(F32)
    mid = r.astype(BF16)
    lo = (r - mid.astype(F32)).astype(BF16)
    return hi, mid, lo


def _cmp_block_of_row(row, nrows):
    return jnp.where(row < nrows // 2, 2 * row, 2 * row - (nrows - 1))


def _nsa_kernel(*refs, n_io, n_cmp, n_blk):
    mask_ref, tri_ref, cmask_ref = refs[-3:]
    tq = CHUNK
    _, ncp, wide = cmask_ref.shape
    i = pl.program_id(1)
    half_w = wide // (tq // LANES)

    def query_of_lane(shape):
        lane = lax.broadcasted_iota(jnp.int32, shape, 1)
        return (lane // half_w) * LANES + (lane & (LANES - 1))

    kk = lax.broadcasted_iota(jnp.int32, (tq, wide), 0)
    mask_ref[...] = jnp.where(kk <= query_of_lane((tq, wide)), 0.0, MASK_SCORE)
    kb = lax.broadcasted_iota(jnp.int32, (LANES, half_w), 0)
    tb = lax.broadcasted_iota(jnp.int32, (LANES, half_w), 1) & (LANES - 1)
    tri_ref[0] = jnp.where(kb <= tb, 0.0, MASK_SCORE)
    tri_ref[1] = jnp.where(kb > tb, 0.0, MASK_SCORE)
    nidx = _cmp_block_of_row(lax.broadcasted_iota(jnp.int32, (ncp, wide), 0), ncp)
    tpos = i * tq + query_of_lane((ncp, wide))
    valid = (nidx * CMP_STRIDE + (CMP_BLOCK - 1) <= tpos) & (nidx < n_cmp)
    cmask_ref[0] = jnp.where(valid, 0.0, MASK_SCORE)
    cmask_ref[1] = jnp.where(valid, 1.0, 0.0)

    scratch = refs[n_io:-3]
    groups = [_nsa_group(g, *refs[:n_io], *[r.at[g] for r in scratch], mask_ref, tri_ref, cmask_ref, n_blk=n_blk)
              for g in range(N_KV_GROUPS)]
    done = object()

    def advance():
        handed = [None] * len(groups)
        while any(r is None for r in handed):
            for gi, grp in enumerate(groups):
                if handed[gi] is None:
                    handed[gi] = next(grp, done)
        return handed

    bodies = advance()

    def pairs(p, carry):
        for phase in range(2):
            for body in bodies:
                body(p, phase)
        return carry

    npair = i // 2
    lax.fori_loop(0, npair - 1, pairs, 0)
    finishers = advance()
    for peeled in (False, True):
        for odd in (False, True):
            @pl.when(((npair >= 1) == peeled) & (i % 2 == int(odd)))
            def _():
                if peeled:
                    pairs(npair - 1, 0)
                for phase in range(2):
                    for finish in finishers:
                        finish(odd, phase)

    assert all(r is done for r in advance())


def _nsa_group(g, q_ref, gate_ref, kc_ref, vct_ref, ks_ref, vst_ref, kw_ref, vwt_ref, ovt_ref, qgain_ref,
               o_ref, qa_ref, m_ref, acc_ref, out_ref, s_ref, cmax_ref, mask_ref, tri_ref, cmask_ref, *, n_blk):
    hpg = HEADS_PER_GROUP
    tq = CHUNK
    dk = HEAD_DIM
    i = pl.program_id(1)
    gd = hpg * dk
    q_lanes = pl.ds(g * gd, gd)
    t0 = i * tq

    nqh = tq // LANES
    half_w = hpg * LANES

    def heads(fn):
        per_head = [fn(h) for h in range(hpg)]
        return jnp.concatenate([x[:, qh * LANES:(qh + 1) * LANES] for qh in range(nqh) for x in per_head], axis=1)

    def head_lanes(x, h):
        return jnp.concatenate([x[:, qh * half_w + h * LANES:qh * half_w + (h + 1) * LANES] for qh in range(nqh)],
                               axis=1)

    q_t = q_ref[0, :, q_lanes].T
    q_t = heads(lambda h: q_t[h * dk:(h + 1) * dk, :])
    ms = jnp.sum(q_t * q_t, axis=0, keepdims=True) * (1.0 / dk)
    qa_ref[0:dk, :] = (q_t * lax.rsqrt(ms + EPS) * qgain_ref[...]).astype(BF16)
    gate_t = gate_ref[0, :, pl.ds(g * LANES, LANES)].T
    gates = [heads(lambda h: gate_t[h * N_BRANCH + br:h * N_BRANCH + br + 1, :]) for br in range(N_BRANCH)]

    causal, win_lo = 0, 1
    sel_state, win_state, void_state = 0, 1, 2

    m_ref[...] = jnp.full(m_ref.shape, MASK_SCORE, F32)
    acc_ref[...] = jnp.zeros(acc_ref.shape, F32)

    def key_rows(ref, c):
        return ref[0, g, pl.ds(pl.multiple_of(c * tq, tq), tq), :]

    def produce(b, k, qrows, masked=False):
        s = _dot(k, qa_ref[0:qrows, :])
        if masked:
            s = s + mask_ref[...]
        s = s.astype(BF16)
        s_ref[b] = s
        cmax_ref[b] = jnp.max(s, axis=0, keepdims=True).astype(F32)

    def consume(b, vt, st):
        m_prev = m_ref[st]
        m_new = jnp.maximum(m_prev, cmax_ref[b])
        alpha = jnp.exp2(m_prev - m_new)
        p = jnp.exp2(s_ref[b] - m_new.astype(BF16))
        acc_ref[st] = alpha * acc_ref[st] + _dot(vt, p)
        m_ref[st] = m_new

    def edge_layout(kind):
        lo_keys, hi_keys = slice(0, LANES), slice(LANES, tq)
        first, second = slice(0, half_w), slice(half_w, 2 * half_w)
        if kind == causal:
            return lo_keys, hi_keys, second, first
        return hi_keys, lo_keys, first, second

    def produce_edge(b, k, qrows, kind):
        full, part, sub, other = edge_layout(kind)
        tri = tri_ref[kind]
        s_full = _dot(k[full], qa_ref[0:qrows, :])
        s_part = _dot(k[part], qa_ref[0:qrows, sub]) + tri
        s_diag = (s_full[:, other] + tri).astype(BF16)
        s_sub = s_full[:, sub].astype(BF16)
        s_part = s_part.astype(BF16)
        s_ref[b, full, other] = s_diag
        s_ref[b, full, sub] = s_sub
        s_ref[b, part, sub] = s_part
        top_diag = jnp.max(s_diag, axis=0, keepdims=True)
        top_sub = jnp.maximum(jnp.max(s_sub, axis=0, keepdims=True), jnp.max(s_part, axis=0, keepdims=True))
        top = jnp.concatenate((top_diag, top_sub) if other.start == 0 else (top_sub, top_diag), axis=1)
        cmax_ref[b] = top.astype(F32)

    def consume_edge(b, vt, st, kind):
        full, part, sub, other = edge_layout(kind)
        m_prev = m_ref[st]
        m_new = jnp.maximum(m_prev, cmax_ref[b])
        alpha = jnp.exp2(m_prev - m_new)
        m_low = m_new.astype(BF16)
        p_full = jnp.exp2(s_ref[b, full, :] - m_low)
        p_part = jnp.exp2(s_ref[b, part, sub] - m_low[:, sub])
        upd = _dot(vt[:, full], p_full)
        upd_part = _dot(vt[:, part], p_part)
        acc_ref[st, :, other] = alpha[:, other] * acc_ref[st, :, other] + upd[:, other]
        acc_ref[st, :, sub] = alpha[:, sub] * acc_ref[st, :, sub] + upd[:, sub] + upd_part
        m_ref[st] = m_new

    def flash_out(st, branch):
        out_ref[...] += gates[branch] * (acc_ref[st, 0:dk, :] * (1.0 / acc_ref[st, dk:dk + 1, :]))

    assert WINDOW == 2 * tq
    lo = jnp.maximum(i - 2, 0)
    mid = jnp.maximum(i - 1, 0)
    lo_state = jnp.where(i >= 2, win_state, void_state)
    mid_state = jnp.where(i >= 1, win_state, void_state)
    yield
    produce_edge(0, key_rows(kw_ref, lo), dk, win_lo)
    yield
    produce(1, key_rows(kw_ref, mid), dk)
    consume_edge(0, vwt_ref[0, g, lo], lo_state, win_lo)
    yield
    produce_edge(0, key_rows(kw_ref, i), dk, causal)
    consume(1, vwt_ref[0, g, mid], mid_state)
    yield

    s = _dot(kc_ref[0, g], qa_ref[0:dk, :]) + cmask_ref[0]
    e = jnp.exp2(s - jnp.max(s, axis=0, keepdims=True))
    p = e * (1.0 / jnp.sum(e, axis=0, keepdims=True)) * cmask_ref[1]
    out_ref[...] = gates[0] * _dot(vct_ref[0, g], p.astype(BF16))
    psum = head_lanes(p, 0)
    for h in range(1, hpg):
        psum = psum + head_lanes(p, h)
    yield

    ovt = ovt_ref[...]
    imp = None
    for part in _split3(psum):
        term = _dot(ovt, part)
        imp = term if imp is None else imp + term
    jblk = lax.broadcasted_iota(jnp.int32, (n_blk, tq), 0)
    cur = (t0 + lax.broadcasted_iota(jnp.int32, (n_blk, tq), 1)) // SEL_BLOCK
    forced = (jblk == 0) | (jblk == cur) | (jblk == cur - 1)
    imp = jnp.where(forced, FORCE_SCORE, imp)
    imp = jnp.where(jblk > cur, MASK_SCORE, imp)
    sub = SUBLANES
    parts = [imp[r:r + sub] for r in range(0, n_blk, sub)]
    ranks = [jnp.zeros((sub, tq), F32) for _ in parts]
    jsub = lax.broadcasted_iota(jnp.int32, (sub, tq), 0)
    for ii in range(n_blk):
        row = imp[ii:ii + 1, :]
        for gi, x in enumerate(parts):
            if ii < gi * sub:
                beats = row >= x
            elif ii >= (gi + 1) * sub:
                beats = row > x
            else:
                beats = (row > x) | ((row == x) & (jsub > ii - gi * sub))
            ranks[gi] = jnp.where(beats, ranks[gi] + 1.0, ranks[gi])
    rank = jnp.concatenate(ranks, axis=0)
    chosen = (rank < float(min(N_SELECT, n_blk))) & (jblk <= cur)
    bias = jnp.where(chosen, 0.0, MASK_SCORE)
    bias = jnp.concatenate([bias, jnp.zeros((dk - n_blk, tq), F32)], axis=0).astype(BF16)
    qa_ref[dk:2 * dk, :] = heads(lambda h: bias)
    yield

    npair = i // 2
    held = jnp.where(i >= 2, 2 * npair - 1, i)
    last = jnp.maximum(i - 1, 0)
    produce(1, key_rows(ks_ref, i), 2 * dk, masked=True)
    consume_edge(0, vwt_ref[0, g, i], win_state, causal)
    flash_out(win_state, 2)

    def pair_body(p, phase):
        c = 2 * p
        if phase == 0:
            produce(0, key_rows(ks_ref, c), 2 * dk)
            consume(1, vst_ref[0, g, jnp.where(p == 0, i, c - 1)], sel_state)
        else:
            produce(1, key_rows(ks_ref, c + 1), 2 * dk)
            consume(0, vst_ref[0, g, c], sel_state)

    yield pair_body

    def finish(odd, phase):
        if phase == 0:
            if odd:
                produce(0, key_rows(ks_ref, last), 2 * dk)
            consume(1, vst_ref[0, g, held], sel_state)
            return
        if odd:
            consume(0, vst_ref[0, g, last], sel_state)
        flash_out(sel_state, 1)
        out = out_ref[...]
        out = jnp.concatenate([head_lanes(out, h) for h in range(hpg)], axis=0)
        o_ref[0, :, q_lanes] = out.T.astype(o_ref.dtype)

    yield finish


def _nsa(q, gate, q_norm, kc, vct, ks, vst, kw, vwt):
    b, s, dq = q.shape
    g = N_KV_GROUPS
    hpg = HEADS_PER_GROUP
    dk = HEAD_DIM
    tq = CHUNK
    n_blk = s // SEL_BLOCK
    n_cmp = (s - CMP_BLOCK) // CMP_STRIDE + 1
    ncp = kc.shape[2]
    assert WINDOW == 2 * tq and n_blk <= dk and (s // tq) % 2 == 0
    rows_np = np.arange(ncp)
    cmp_idx = np.where(rows_np < ncp // 2, 2 * rows_np, 2 * rows_np - (ncp - 1))
    cmp_start = cmp_idx * CMP_STRIDE
    blk_start = np.arange(n_blk) * SEL_BLOCK
    ovt = ((cmp_start[None, :] < blk_start[:, None] + SEL_BLOCK)
           & (cmp_start[None, :] + CMP_BLOCK > blk_start[:, None])
           & (cmp_idx[None, :] < n_cmp)).astype(np.float32)
    qgain = jnp.broadcast_to((q_norm * (dk ** -0.5 * LOG2E))[:, None], (dk, hpg * tq))
    idx = lambda bi, i: (bi, 0, 0, 0)
    idx5 = lambda bi, i: (bi, 0, 0, 0, 0)
    args = (q, gate, kc, vct, ks, vst, kw, vwt, jnp.asarray(ovt, BF16), qgain)
    return pl.pallas_call(
        functools.partial(_nsa_kernel, n_io=len(args) + 1, n_cmp=n_cmp, n_blk=n_blk),
        grid=(b, s // tq),
        in_specs=[
            pl.BlockSpec((1, tq, dq), lambda bi, i: (bi, i, 0)),
            pl.BlockSpec((1, tq, g * LANES), lambda bi, i: (bi, i, 0)),
            pl.BlockSpec((1, g, ncp, dk), idx),
            pl.BlockSpec((1, g, dk, ncp), idx),
            pl.BlockSpec((1, g, s, 2 * dk), idx),
            pl.BlockSpec((1, g, s // tq, V_ROWS, tq), idx5),
            pl.BlockSpec((1, g, s, dk), idx),
            pl.BlockSpec((1, g, s // tq, V_ROWS, tq), idx5),
            _resident((n_blk, ncp)),
            _resident((dk, hpg * tq)),
        ],
        out_specs=pl.BlockSpec((1, tq, dq), lambda bi, i: (bi, i, 0)),
        out_shape=jax.ShapeDtypeStruct((b, s, dq), BF16),
        scratch_shapes=[
            pltpu.VMEM((g, 2 * dk, hpg * tq), BF16),
            pltpu.VMEM((g, 3, 1, hpg * tq), F32),
            pltpu.VMEM((g, 3, V_ROWS, hpg * tq), F32),
            pltpu.VMEM((g, dk, hpg * tq), F32),
            pltpu.VMEM((g, 2, tq, hpg * tq), BF16),
            pltpu.VMEM((g, 2, 1, hpg * tq), F32),
            pltpu.VMEM((tq, hpg * tq), F32),
            pltpu.VMEM((2, LANES, hpg * LANES), F32),
            pltpu.VMEM((2, ncp, hpg * tq), F32),
        ],
        compiler_params=_cparams(("parallel", "arbitrary")),
        name="nsa",
    )(*args)


def _gate_columns(w_qg):
    dq = N_HEADS * HEAD_DIM
    per = HEADS_PER_GROUP * N_BRANCH
    wg = w_qg[:, dq:].reshape(-1, N_KV_GROUPS, per)
    wg = jnp.pad(wg, ((0, 0), (0, 0), (0, LANES - per)))
    return jnp.concatenate([w_qg[:, :dq], wg.reshape(-1, N_KV_GROUPS * LANES)], axis=1)


def kernel(x, ffn_norm, ffn_w_gate_up, ffn_w_down, mix_norm, conv_w_in, conv_w, conv_w_out, kv_norm, kv_w,
           cmp_pos, cmp_w1, cmp_b1, cmp_w2, k_norm, nsa_w_qg, q_norm, nsa_w_o):
    b, s, d = x.shape
    t = b * s
    depth = ffn_norm.shape[0]
    n_a = conv_w_in.shape[0]
    g = N_KV_GROUPS

    ffn_g = ffn_norm.reshape(depth, 2, 1, d)

    def ffn(h2, layer, j, proj=None, tail=None, keep_h=True):
        return _ffn(h2, ffn_g, ffn_w_gate_up, ffn_w_down, (layer, j), proj, tail, keep_h)

    h = x.reshape(t, d)
    shared = None
    for layer in range(depth):
        proj = None
        if layer < n_a:
            h = ffn(h, layer, 0, keep_h=False,
                    tail=_conv_tail(b, s, mix_norm[layer], conv_w_in[layer].astype(BF16), conv_w[layer],
                                    conv_w_out[layer].astype(BF16)))
        else:
            li = layer - n_a
            h, q, gate = ffn(h, layer, 0, tail=_qg_tail(t, mix_norm[layer],
                                                        _gate_columns(nsa_w_qg[li]).astype(BF16)))
            o = _nsa(q.reshape(b, s, -1), gate.reshape(b, s, -1), q_norm[li], *shared)
            proj = (o.reshape(t, -1), nsa_w_o[li].astype(BF16))
        if layer == n_a - 1:
            h, blk, ks, vst, kw, vwt = ffn(h, layer, 1, proj,
                                           tail=_kv_tail(b, s, kv_norm, kv_w.astype(BF16), k_norm))
            blk = blk.reshape(2, b * g, s // CMP_BLOCK, CMP_BLOCK * HEAD_DIM)
            cmp = _compress(blk, cmp_pos.reshape(2, -1), cmp_w1.astype(BF16), cmp_b1,
                            cmp_w2.astype(BF16), k_norm[0])
            cmp = cmp.reshape(2, b, g, -1, HEAD_DIM)
            shared = (cmp[0], cmp[1].transpose(0, 1, 3, 2), ks, vst, kw, vwt)
        else:
            h = ffn(h, layer, 1, proj)
    return h.reshape(b, s, d)
```

```python
import functools

import numpy as np
import jax
import jax.numpy as jnp
from jax import lax
from jax.experimental import pallas as pl
from jax.experimental.pallas import tpu as pltpu

EPS = 1e-6
N_HEADS = 16
HEAD_DIM = 64
N_KV_GROUPS = 4
HEADS_PER_GROUP = N_HEADS // N_KV_GROUPS
N_BRANCH = 3
CMP_BLOCK = 32
CMP_STRIDE = 16
SEL_BLOCK = 64
N_SELECT = 16
WINDOW = 512
FORCE_SCORE = 1e9
MASK_SCORE = -(2.0 ** 100)
LOG2E = 1.4426950408889634

LANES = 128
SUBLANES = 8
BF16_ROWS = 16
CHUNK = 256
V_ROWS = HEAD_DIM + BF16_ROWS
VMEM_LIMIT = 56 * 1024 * 1024

BF16 = jnp.bfloat16
F32 = jnp.float32


def _cparams(sem):
    return pltpu.CompilerParams(dimension_semantics=sem, vmem_limit_bytes=VMEM_LIMIT)


def _rms(x, g):
    ms = jnp.mean(x * x, axis=-1, keepdims=True)
    return x * lax.rsqrt(ms + EPS) * g


def _dot(a, b):
    return jnp.dot(a, b, preferred_element_type=F32)


def _split2(x):
    hi = x.astype(BF16)
    return hi, (x - hi.astype(F32)).astype(BF16)


def _seg_rms(y, seg, seg_t, gain):
    ss = None
    for part in _split2(y * y):
        term = _dot(part, seg)
        ss = term if ss is None else ss + term
    r = lax.rsqrt(ss * (1.0 / HEAD_DIM) + EPS)
    scale = None
    for part in _split2(r):
        term = _dot(part, seg_t)
        scale = term if scale is None else scale + term
    return y * scale * gain


def _seg_matrices(width):
    ind = (np.arange(width)[:, None] // HEAD_DIM == np.arange(LANES)[None, :]).astype(np.float32)
    return jnp.asarray(ind, BF16), jnp.asarray(ind.T, BF16)


def _ffn_kernel(x_ref, g_ref, wgu32_ref, wd32_ref, *rest, dff, has_proj, n_tail_in, n_tail_out, keep_h, tail):
    step = pl.program_id(0)
    proj_refs, rest = (rest[:2], rest[2:]) if has_proj else ((), rest)
    tail_in, rest = rest[:n_tail_in], rest[n_tail_in:]
    o_ref, rest = (rest[0], rest[1:]) if keep_h else (None, rest)
    tail_out, (wgu_ref, wd_ref), tail_scratch = rest[:n_tail_out], rest[n_tail_out:n_tail_out + 2], rest[n_tail_out + 2:]

    @pl.when(step < FFN_INGEST)
    def _():
        for src32, dst in ((wgu32_ref, wgu_ref), (wd32_ref, wd_ref)):
            rows = src32.shape[0]
            dst[pl.ds(pl.multiple_of(step * rows, rows), rows), :] = src32[...].astype(BF16)

    @pl.when(step >= FFN_INGEST)
    def _():
        x = x_ref[...]
        if has_proj:
            a_ref, w_ref = proj_refs
            x = x + _dot(a_ref[...], w_ref[...])
        xn = _rms(x, g_ref[...]).astype(BF16)
        ab = _dot(xn, wgu_ref[...])
        a = ab[:, 0:dff]
        h = (a * jax.nn.sigmoid(a) * ab[:, dff:2 * dff]).astype(BF16)
        out = x + 0.5 * _dot(h, wd_ref[...])
        if keep_h:
            o_ref[...] = out
        if tail is not None:
            tail(out, step - FFN_INGEST, *tail_in, *tail_out, *tail_scratch)


def _resident(shape, lead=()):
    index = tuple(lead) + (0,) * len(shape)
    return pl.BlockSpec((None,) * len(lead) + tuple(shape), lambda *_: index, pipeline_mode=pl.Buffered(1))


FFN_TM = 512
FFN_INGEST = 8


def _ffn(h, g_all, w_gu_all, w_d_all, lead, proj=None, tail=None, keep_h=True):
    tm = FFN_TM
    t, d = h.shape
    dff = w_d_all.shape[-2]
    assert d % (BF16_ROWS * FFN_INGEST) == 0 and dff % (BF16_ROWS * FFN_INGEST) == 0

    def tiled(shape, fn):
        return pl.BlockSpec(shape, lambda i: fn(jnp.maximum(i - FFN_INGEST, 0)))

    def chunked(rows, cols):
        return pl.BlockSpec((None, None, rows, cols), lambda i: (*lead, jnp.minimum(i, FFN_INGEST - 1), 0))

    in_specs = [
        tiled((tm, d), lambda tile: (tile, 0)),
        _resident((1, d), lead),
        chunked(d // FFN_INGEST, 2 * dff),
        chunked(dff // FFN_INGEST, d),
    ]
    args = [h, g_all, w_gu_all, w_d_all]
    if proj is not None:
        a, w = proj
        in_specs += [tiled((tm, a.shape[1]), lambda tile: (tile, 0)), _resident(w.shape)]
        args += [a, w]
    out_specs = [tiled((tm, d), lambda tile: (tile, 0))] if keep_h else []
    out_shape = [jax.ShapeDtypeStruct((t, d), F32)] if keep_h else []
    scratch = [pltpu.VMEM((d, 2 * dff), BF16), pltpu.VMEM((dff, d), BF16)]
    tail_fn, n_tail_in, n_tail_out = None, 0, 0
    if tail is not None:
        tail_fn, tail_args, tail_in_specs, tail_out_blocks, tail_out_shapes, tail_scratch = tail
        in_specs += tail_in_specs
        args += tail_args
        n_tail_in, n_tail_out = len(tail_args), len(tail_out_blocks)
        out_specs += [tiled(shape, fn) for shape, fn in tail_out_blocks]
        out_shape += tail_out_shapes
        scratch += tail_scratch
    outs = pl.pallas_call(
        functools.partial(_ffn_kernel, dff=dff, has_proj=proj is not None, n_tail_in=n_tail_in,
                          n_tail_out=n_tail_out, keep_h=keep_h, tail=tail_fn),
        grid=(FFN_INGEST + t // tm,),
        in_specs=in_specs,
        out_specs=out_specs,
        out_shape=out_shape,
        scratch_shapes=scratch,
        compiler_params=_cparams(("arbitrary",)),
        name="ffn",
    )(*args)
    return outs[0] if len(outs) == 1 else outs


CARRY = SUBLANES


def _conv_tail_kernel(x, tile, g_ref, win_ref, cw_ref, wout_ref, o_ref, vbuf_ref, *, nj):
    tm, d = x.shape

    @pl.when(tile % nj == 0)
    def _():
        vbuf_ref[0:CARRY, :] = jnp.zeros((CARRY, d), F32)

    xn = _rms(x, g_ref[...]).astype(BF16)
    proj = _dot(xn, win_ref[...])
    bg = proj[:, 0:d]
    v = proj[:, d:2 * d] * proj[:, 2 * d:3 * d]
    vbuf_ref[CARRY:CARRY + tm, :] = v
    cw = cw_ref[...]
    conv = (cw[0:1, :] * vbuf_ref[CARRY - 2:CARRY - 2 + tm, :]
            + cw[1:2, :] * vbuf_ref[CARRY - 1:CARRY - 1 + tm, :]
            + cw[2:3, :] * v)
    y = (bg * conv).astype(BF16)
    o_ref[...] = x + _dot(y, wout_ref[...])
    vbuf_ref[0:CARRY, :] = vbuf_ref[tm:tm + CARRY, :]


def _conv_tail(b, s, g, w_in, conv_w, w_out):
    tm = FFN_TM
    d = w_out.shape[0]
    assert s % tm == 0
    return (
        functools.partial(_conv_tail_kernel, nj=s // tm),
        [g.reshape(1, d), w_in, conv_w, w_out],
        [_resident((1, d)), _resident(w_in.shape), _resident(conv_w.shape), _resident(w_out.shape)],
        [((tm, d), lambda tile: (tile, 0))],
        [jax.ShapeDtypeStruct((b * s, d), F32)],
        [pltpu.VMEM((CARRY + tm, d), F32)],
    )


def _kv_tail_kernel(x, tile, g_ref, w_ref, kn_ref, seg_ref, segt_ref, blk_ref, ks_ref, vst_ref, kw_ref, vwt_ref,
                    raw_ref, *, nj):
    tm = x.shape[0]
    j = tile % nj
    xn = _rms(x, g_ref[...]).astype(BF16)
    y = _dot(xn, w_ref[...])
    gw = N_KV_GROUPS * HEAD_DIM
    per = LANES // HEAD_DIM
    for slab in range(raw_ref.shape[0]):
        raw_ref[slab] = y[:, slab * LANES:(slab + 1) * LANES]
    for l in range(CMP_BLOCK):
        for slab in range(raw_ref.shape[0]):
            rows = raw_ref[slab, pl.ds(l, tm // CMP_BLOCK, stride=CMP_BLOCK), :]
            for k in range(per):
                sg = slab * per + k
                blk_ref[sg // N_KV_GROUPS, 0, sg % N_KV_GROUPS, :, l * HEAD_DIM:(l + 1) * HEAD_DIM] = (
                    rows[:, k * HEAD_DIM:(k + 1) * HEAD_DIM])
    pos = j * tm + lax.broadcasted_iota(jnp.int32, (tm, HEAD_DIM), 0)
    lane = lax.broadcasted_iota(jnp.int32, (tm, HEAD_DIM), 1)
    onehot = jnp.where(lane == pos // SEL_BLOCK, 1.0, 0.0).astype(BF16)
    k_sel = _seg_rms(y[:, 2 * gw:3 * gw], seg_ref[...], segt_ref[...], kn_ref[0:1, :]).astype(BF16)
    k_win = _seg_rms(y[:, 4 * gw:5 * gw], seg_ref[...], segt_ref[...], kn_ref[1:2, :]).astype(BF16)
    for grp in range(N_KV_GROUPS):
        ks_ref[0, grp, :, 0:HEAD_DIM] = k_sel[:, grp * HEAD_DIM:(grp + 1) * HEAD_DIM]
        ks_ref[0, grp, :, HEAD_DIM:2 * HEAD_DIM] = onehot
        kw_ref[0, grp] = k_win[:, grp * HEAD_DIM:(grp + 1) * HEAD_DIM]
    extra = jnp.where(lax.broadcasted_iota(jnp.int32, (V_ROWS - HEAD_DIM, CHUNK), 0) == 0, 1.0, 0.0).astype(BF16)
    for slot, ref in ((3, vst_ref), (5, vwt_ref)):
        vt = y[:, slot * gw:(slot + 1) * gw].T
        for grp in range(N_KV_GROUPS):
            for cc in range(tm // CHUNK):
                ref[0, grp, cc, 0:HEAD_DIM, :] = vt[grp * HEAD_DIM:(grp + 1) * HEAD_DIM,
                                                    cc * CHUNK:(cc + 1) * CHUNK].astype(BF16)
                ref[0, grp, cc, HEAD_DIM:V_ROWS, :] = extra


def _kv_tail(b, s, g, w, k_norm):
    tm = FFN_TM
    d, n = w.shape
    ng = N_KV_GROUPS
    gw = ng * HEAD_DIM
    cpt = tm // CHUNK
    nj = s // tm
    assert s // SEL_BLOCK <= HEAD_DIM and s % tm == 0
    seg, seg_t = _seg_matrices(gw)
    gains = jnp.tile(k_norm[1:3], (1, ng))
    return (
        functools.partial(_kv_tail_kernel, nj=nj),
        [g.reshape(1, d), w, gains, seg, seg_t],
        [_resident((1, d)), _resident((d, n)), _resident(gains.shape), _resident(seg.shape),
         _resident(seg_t.shape)],
        [
            ((2, 1, ng, tm // CMP_BLOCK, CMP_BLOCK * HEAD_DIM), lambda i: (0, i // nj, 0, i % nj, 0)),
            ((1, ng, tm, 2 * HEAD_DIM), lambda i: (i // nj, 0, i % nj, 0)),
            ((1, ng, cpt, V_ROWS, CHUNK), lambda i: (i // nj, 0, i % nj, 0, 0)),
            ((1, ng, tm, HEAD_DIM), lambda i: (i // nj, 0, i % nj, 0)),
            ((1, ng, cpt, V_ROWS, CHUNK), lambda i: (i // nj, 0, i % nj, 0, 0)),
        ],
        [
            jax.ShapeDtypeStruct((2, b, ng, s // CMP_BLOCK, CMP_BLOCK * HEAD_DIM), F32),
            jax.ShapeDtypeStruct((b, ng, s, 2 * HEAD_DIM), BF16),
            jax.ShapeDtypeStruct((b, ng, s // CHUNK, V_ROWS, CHUNK), BF16),
            jax.ShapeDtypeStruct((b, ng, s, HEAD_DIM), BF16),
            jax.ShapeDtypeStruct((b, ng, s // CHUNK, V_ROWS, CHUNK), BF16),
        ],
        [pltpu.VMEM((2 * gw // LANES, tm, LANES), F32)],
    )


def _cmp_kernel(x_ref, pos_ref, w1_ref, b1_ref, w2_ref, kn_ref, o_ref):
    c = pl.program_id(0)
    nb, ne, width = x_ref.shape
    rows = nb * ne
    half = width // 2
    x = x_ref[...].reshape(rows, width)
    pos = pos_ref[...]
    w1 = w1_ref[...]
    h_even = _dot((x + pos).astype(BF16), w1)
    a = _dot((x[:, half:] + pos[:, :half]).astype(BF16), w1[0:half])
    b = _dot((x[:, :half] + pos[:, half:]).astype(BF16), w1[half:width])
    h_odd = a + pltpu.roll(b, rows - 1, 0)

    def tail(hid):
        y = _dot(jax.nn.gelu(hid + b1_ref[...]).astype(BF16), w2_ref[...])
        y = jnp.where(c == 0, _rms(y, kn_ref[...]), y)
        return y.astype(o_ref.dtype).reshape(nb, ne, y.shape[-1])

    o_ref[:, 0:ne, :] = tail(h_even)
    o_ref[:, ne:2 * ne, :] = tail(h_odd)


def _compress(xe, pos, w1, b1, w2, kn, *, nb=8):
    _, nseq, ne, width = xe.shape
    nb = min(nb, nseq)
    hid = w1.shape[-1]
    dk = w2.shape[-1]
    return pl.pallas_call(
        _cmp_kernel,
        grid=(2, nseq // nb),
        in_specs=[
            pl.BlockSpec((None, nb, ne, width), lambda c, i: (c, i, 0, 0)),
            pl.BlockSpec((None, 1, width), lambda c, i: (c, 0, 0)),
            pl.BlockSpec((None, width, hid), lambda c, i: (c, 0, 0)),
            pl.BlockSpec((None, 1, hid), lambda c, i: (c, 0, 0)),
            pl.BlockSpec((None, hid, dk), lambda c, i: (c, 0, 0)),
            pl.BlockSpec((1, dk), lambda c, i: (0, 0)),
        ],
        out_specs=pl.BlockSpec((None, nb, 2 * ne, dk), lambda c, i: (c, i, 0, 0)),
        out_shape=jax.ShapeDtypeStruct((2, nseq, 2 * ne, dk), BF16),
        compiler_params=_cparams(("parallel", "parallel")),
        name="compress",
    )(xe, pos.reshape(2, 1, width), w1, b1.reshape(2, 1, hid), w2, kn.reshape(1, dk))


def _qg_tail_kernel(x, tile, g_ref, w_ref, q_ref, gate_ref):
    dq = q_ref.shape[-1]
    xn = _rms(x, g_ref[...]).astype(BF16)
    y = _dot(xn, w_ref[...])
    q_ref[...] = y[:, 0:dq]
    gate_ref[...] = jax.nn.sigmoid(y[:, dq:])


def _qg_tail(t, g, w):
    tm = FFN_TM
    d, n = w.shape
    dq = N_HEADS * HEAD_DIM
    return (
        _qg_tail_kernel,
        [g.reshape(1, d), w],
        [_resident((1, d)), _resident((d, n))],
        [((tm, dq), lambda tile: (tile, 0)), ((tm, n - dq), lambda tile: (tile, 0))],
        [jax.ShapeDtypeStruct((t, dq), F32), jax.ShapeDtypeStruct((t, n - dq), F32)],
        [],
    )


def _split3(x):
    hi = x.astype(BF16)
    r = x - hi.astype(F32)
    mid = r.astype(BF16)
    lo = (r - mid.astype(F32)).astype(BF16)
    return hi, mid, lo


def _cmp_block_of_row(row, nrows):
    return jnp.where(row < nrows // 2, 2 * row, 2 * row - (nrows - 1))


def _nsa_kernel(*refs, n_io, n_cmp, n_blk):
    mask_ref, tri_ref, cmask_ref = refs[-3:]
    tq = CHUNK
    _, ncp, wide = cmask_ref.shape
    i = pl.program_id(1)
    half_w = wide // (tq // LANES)

    def query_of_lane(shape):
        lane = lax.broadcasted_iota(jnp.int32, shape, 1)
        return (lane // half_w) * LANES + (lane & (LANES - 1))

    kk = lax.broadcasted_iota(jnp.int32, (tq, wide), 0)
    mask_ref[...] = jnp.where(kk <= query_of_lane((tq, wide)), 0.0, MASK_SCORE)
    kb = lax.broadcasted_iota(jnp.int32, (LANES, half_w), 0)
    tb = lax.broadcasted_iota(jnp.int32, (LANES, half_w), 1) & (LANES - 1)
    tri_ref[0] = jnp.where(kb <= tb, 0.0, MASK_SCORE)
    tri_ref[1] = jnp.where(kb > tb, 0.0, MASK_SCORE)
    nidx = _cmp_block_of_row(lax.broadcasted_iota(jnp.int32, (ncp, wide), 0), ncp)
    tpos = i * tq + query_of_lane((ncp, wide))
    valid = (nidx * CMP_STRIDE + (CMP_BLOCK - 1) <= tpos) & (nidx < n_cmp)
    cmask_ref[0] = jnp.where(valid, 0.0, MASK_SCORE)
    cmask_ref[1] = jnp.where(valid, 1.0, 0.0)

    scratch = refs[n_io:-3]
    groups = [_nsa_group(g, *refs[:n_io], *[r.at[g] for r in scratch], mask_ref, tri_ref, cmask_ref, n_blk=n_blk)
              for g in range(N_KV_GROUPS)]
    done = object()

    def advance():
        handed = [None] * len(groups)
        while any(r is None for r in handed):
            for gi, grp in enumerate(groups):
                if handed[gi] is None:
                    handed[gi] = next(grp, done)
        return handed

    bodies = advance()

    def pairs(p, carry):
        for phase in range(2):
            for body in bodies:
                body(p, phase)
        return carry

    npair = i // 2
    lax.fori_loop(0, npair - 1, pairs, 0)
    finishers = advance()
    for peeled in (False, True):
        for odd in (False, True):
            @pl.when(((npair >= 1) == peeled) & (i % 2 == int(odd)))
            def _():
                if peeled:
                    pairs(npair - 1, 0)
                for phase in range(2):
                    for finish in finishers:
                        finish(odd, phase)

    assert all(r is done for r in advance())


def _nsa_group(g, q_ref, gate_ref, kc_ref, vct_ref, ks_ref, vst_ref, kw_ref, vwt_ref, ovt_ref, qgain_ref,
               o_ref, qa_ref, m_ref, acc_ref, out_ref, s_ref, cmax_ref, gates_ref, mask_ref, tri_ref, cmask_ref,
               *, n_blk):
    hpg = HEADS_PER_GROUP
    tq = CHUNK
    dk = HEAD_DIM
    i = pl.program_id(1)
    gd = hpg * dk
    q_lanes = pl.ds(g * gd, gd)
    t0 = i * tq

    nqh = tq // LANES
    half_w = hpg * LANES

    def heads(fn):
        per_head = [fn(h) for h in range(hpg)]
        return jnp.concatenate([x[:, qh * LANES:(qh + 1) * LANES] for qh in range(nqh) for x in per_head], axis=1)

    def head_lanes(x, h):
        return jnp.concatenate([x[:, qh * half_w + h * LANES:qh * half_w + (h + 1) * LANES] for qh in range(nqh)],
                               axis=1)

    q_t = q_ref[0, :, q_lanes].T
    q_t = heads(lambda h: q_t[h * dk:(h + 1) * dk, :])
    ms = jnp.sum(q_t * q_t, axis=0, keepdims=True) * (1.0 / dk)
    qa_ref[0:dk, :] = (q_t * lax.rsqrt(ms + EPS) * qgain_ref[...]).astype(BF16)
    gate_t = gate_ref[0, :, pl.ds(g * LANES, LANES)].T
    for br in range(N_BRANCH):
        gates_ref[br] = heads(lambda h: gate_t[h * N_BRANCH + br:h * N_BRANCH + br + 1, :])

    causal, win_lo = 0, 1
    sel_state, win_state, void_state = 0, 1, 2

    m_ref[...] = jnp.full(m_ref.shape, MASK_SCORE, F32)
    acc_ref[...] = jnp.zeros(acc_ref.shape, F32)

    def key_rows(ref, c):
        return ref[0, g, pl.ds(pl.multiple_of(c * tq, tq), tq), :]

    def produce(b, k, qrows, masked=False):
        s = _dot(k, qa_ref[0:qrows, :])
        if masked:
            s = s + mask_ref[...]
        s = s.astype(BF16)
        s_ref[b] = s
        cmax_ref[b] = jnp.max(s, axis=0, keepdims=True).astype(F32)

    def consume(b, vt, st):
        m_prev = m_ref[st]
        m_new = jnp.maximum(m_prev, cmax_ref[b])
        alpha = jnp.exp2(m_prev - m_new)
        p = jnp.exp2(s_ref[b] - m_new.astype(BF16))
        acc_ref[st] = alpha * acc_ref[st] + _dot(vt, p)
        m_ref[st] = m_new

    def edge_layout(kind):
        lo_keys, hi_keys = slice(0, LANES), slice(LANES, tq)
        first, second = slice(0, half_w), slice(half_w, 2 * half_w)
        if kind == causal:
            return lo_keys, hi_keys, second, first
        return hi_keys, lo_keys, first, second

    def produce_edge(b, k, qrows, kind):
        full, part, sub, other = edge_layout(kind)
        tri = tri_ref[kind]
        s_full = _dot(k[full], qa_ref[0:qrows, :])
        s_part = _dot(k[part], qa_ref[0:qrows, sub]) + tri
        s_diag = (s_full[:, other] + tri).astype(BF16)
        s_sub = s_full[:, sub].astype(BF16)
        s_part = s_part.astype(BF16)
        s_ref[b, full, other] = s_diag
        s_ref[b, full, sub] = s_sub
        s_ref[b, part, sub] = s_part
        top_diag = jnp.max(s_diag, axis=0, keepdims=True)
        top_sub = jnp.maximum(jnp.max(s_sub, axis=0, keepdims=True), jnp.max(s_part, axis=0, keepdims=True))
        top = jnp.concatenate((top_diag, top_sub) if other.start == 0 else (top_sub, top_diag), axis=1)
        cmax_ref[b] = top.astype(F32)

    def consume_edge(b, vt, st, kind):
        full, part, sub, other = edge_layout(kind)
        m_prev = m_ref[st]
        m_new = jnp.maximum(m_prev, cmax_ref[b])
        alpha = jnp.exp2(m_prev - m_new)
        m_low = m_new.astype(BF16)
        p_full = jnp.exp2(s_ref[b, full, :] - m_low)
        p_part = jnp.exp2(s_ref[b, part, sub] - m_low[:, sub])
        upd = _dot(vt[:, full], p_full)
        upd_part = _dot(vt[:, part], p_part)
        acc_ref[st, :, other] = alpha[:, other] * acc_ref[st, :, other] + upd[:, other]
        acc_ref[st, :, sub] = alpha[:, sub] * acc_ref[st, :, sub] + upd[:, sub] + upd_part
        m_ref[st] = m_new

    def flash_out(st, branch):
        out_ref[...] += gates_ref[branch] * (acc_ref[st, 0:dk, :] * (1.0 / acc_ref[st, dk:dk + 1, :]))

    assert WINDOW == 2 * tq
    lo = jnp.maximum(i - 2, 0)
    mid = jnp.maximum(i - 1, 0)
    lo_state = jnp.where(i >= 2, win_state, void_state)
    mid_state = jnp.where(i >= 1, win_state, void_state)
    yield
    produce_edge(0, key_rows(kw_ref, lo), dk, win_lo)
    yield
    produce(1, key_rows(kw_ref, mid), dk)
    consume_edge(0, vwt_ref[0, g, lo], lo_state, win_lo)
    yield
    produce_edge(0, key_rows(kw_ref, i), dk, causal)
    consume(1, vwt_ref[0, g, mid], mid_state)
    yield

    s = _dot(kc_ref[0, g], qa_ref[0:dk, :]) + cmask_ref[0]
    e = jnp.exp2(s - jnp.max(s, axis=0, keepdims=True))
    p = e * (1.0 / jnp.sum(e, axis=0, keepdims=True)) * cmask_ref[1]
    out_ref[...] = gates_ref[0] * _dot(vct_ref[0, g], p.astype(BF16))
    psum = head_lanes(p, 0)
    for h in range(1, hpg):
        psum = psum + head_lanes(p, h)
    yield

    ovt = ovt_ref[...]
    imp = None
    for part in _split3(psum):
        term = _dot(ovt, part)
        imp = term if imp is None else imp + term
    jblk = lax.broadcasted_iota(jnp.int32, (n_blk, tq), 0)
    cur = (t0 + lax.broadcasted_iota(jnp.int32, (n_blk, tq), 1)) // SEL_BLOCK
    forced = (jblk == 0) | (jblk == cur) | (jblk == cur - 1)
    imp = jnp.where(forced, FORCE_SCORE, imp)
    imp = jnp.where(jblk > cur, MASK_SCORE, imp)
    sub = SUBLANES
    parts = [imp[r:r + sub] for r in range(0, n_blk, sub)]
    ranks = [jnp.zeros((sub, tq), F32) for _ in parts]
    jsub = lax.broadcasted_iota(jnp.int32, (sub, tq), 0)
    for ii in range(n_blk):
        row = imp[ii:ii + 1, :]
        for gi, x in enumerate(parts):
            if ii < gi * sub:
                beats = row >= x
            elif ii >= (gi + 1) * sub:
                beats = row > x
            else:
                beats = (row > x) | ((row == x) & (jsub > ii - gi * sub))
            ranks[gi] = jnp.where(beats, ranks[gi] + 1.0, ranks[gi])
    rank = jnp.concatenate(ranks, axis=0)
    chosen = (rank < float(min(N_SELECT, n_blk))) & (jblk <= cur)
    bias = jnp.where(chosen, 0.0, MASK_SCORE)
    bias = jnp.concatenate([bias, jnp.zeros((dk - n_blk, tq), F32)], axis=0).astype(BF16)
    qa_ref[dk:2 * dk, :] = heads(lambda h: bias)
    yield

    npair = i // 2
    held = jnp.where(i >= 2, 2 * npair - 1, i)
    last = jnp.maximum(i - 1, 0)
    produce(1, key_rows(ks_ref, i), 2 * dk, masked=True)
    consume_edge(0, vwt_ref[0, g, i], win_state, causal)
    flash_out(win_state, 2)

    def pair_body(p, phase):
        c = 2 * p
        if phase == 0:
            produce(0, key_rows(ks_ref, c), 2 * dk)
            consume(1, vst_ref[0, g, jnp.where(p == 0, i, c - 1)], sel_state)
        else:
            produce(1, key_rows(ks_ref, c + 1), 2 * dk)
            consume(0, vst_ref[0, g, c], sel_state)

    yield pair_body

    def finish(odd, phase):
        if phase == 0:
            if odd:
                produce(0, key_rows(ks_ref, last), 2 * dk)
            consume(1, vst_ref[0, g, held], sel_state)
            return
        if odd:
            consume(0, vst_ref[0, g, last], sel_state)
        flash_out(sel_state, 1)
        out = out_ref[...]
        out = jnp.concatenate([head_lanes(out, h) for h in range(hpg)], axis=0)
        o_ref[0, :, q_lanes] = out.T.astype(o_ref.dtype)

    yield finish


def _nsa(q, gate, q_norm, kc, vct, ks, vst, kw, vwt):
    b, s, dq = q.shape
    g = N_KV_GROUPS
    hpg = HEADS_PER_GROUP
    dk = HEAD_DIM
    tq = CHUNK
    n_blk = s // SEL_BLOCK
    n_cmp = (s - CMP_BLOCK) // CMP_STRIDE + 1
    ncp = kc.shape[2]
    assert WINDOW == 2 * tq and n_blk <= dk and (s // tq) % 2 == 0
    rows_np = np.arange(ncp)
    cmp_idx = np.where(rows_np < ncp // 2, 2 * rows_np, 2 * rows_np - (ncp - 1))
    cmp_start = cmp_idx * CMP_STRIDE
    blk_start = np.arange(n_blk) * SEL_BLOCK
    ovt = ((cmp_start[None, :] < blk_start[:, None] + SEL_BLOCK)
           & (cmp_start[None, :] + CMP_BLOCK > blk_start[:, None])
           & (cmp_idx[None, :] < n_cmp)).astype(np.float32)
    qgain = jnp.broadcast_to((q_norm * (dk ** -0.5 * LOG2E))[:, None], (dk, hpg * tq))
    idx = lambda bi, i: (bi, 0, 0, 0)
    idx5 = lambda bi, i: (bi, 0, 0, 0, 0)
    args = (q, gate, kc, vct, ks, vst, kw, vwt, jnp.asarray(ovt, BF16), qgain)
    return pl.pallas_call(
        functools.partial(_nsa_kernel, n_io=len(args) + 1, n_cmp=n_cmp, n_blk=n_blk),
        grid=(b, s // tq),
        in_specs=[
            pl.BlockSpec((1, tq, dq), lambda bi, i: (bi, i, 0)),
            pl.BlockSpec((1, tq, g * LANES), lambda bi, i: (bi, i, 0)),
            pl.BlockSpec((1, g, ncp, dk), idx),
            pl.BlockSpec((1, g, dk, ncp), idx),
            pl.BlockSpec((1, g, s, 2 * dk), idx),
            pl.BlockSpec((1, g, s // tq, V_ROWS, tq), idx5),
            pl.BlockSpec((1, g, s, dk), idx),
            pl.BlockSpec((1, g, s // tq, V_ROWS, tq), idx5),
            _resident((n_blk, ncp)),
            _resident((dk, hpg * tq)),
        ],
        out_specs=pl.BlockSpec((1, tq, dq), lambda bi, i: (bi, i, 0)),
        out_shape=jax.ShapeDtypeStruct((b, s, dq), BF16),
        scratch_shapes=[
            pltpu.VMEM((g, 2 * dk, hpg * tq), BF16),
            pltpu.VMEM((g, 3, 1, hpg * tq), F32),
            pltpu.VMEM((g, 3, V_ROWS, hpg * tq), F32),
            pltpu.VMEM((g, dk, hpg * tq), F32),
            pltpu.VMEM((g, 2, tq, hpg * tq), BF16),
            pltpu.VMEM((g, 2, 1, hpg * tq), F32),
            pltpu.VMEM((g, N_BRANCH, 1, hpg * tq), F32),
            pltpu.VMEM((tq, hpg * tq), F32),
            pltpu.VMEM((2, LANES, hpg * LANES), F32),
            pltpu.VMEM((2, ncp, hpg * tq), F32),
        ],
        compiler_params=_cparams(("parallel", "arbitrary")),
        name="nsa",
    )(*args)


def _gate_columns(w_qg):
    dq = N_HEADS * HEAD_DIM
    per = HEADS_PER_GROUP * N_BRANCH
    wg = w_qg[:, dq:].reshape(-1, N_KV_GROUPS, per)
    wg = jnp.pad(wg, ((0, 0), (0, 0), (0, LANES - per)))
    return jnp.concatenate([w_qg[:, :dq], wg.reshape(-1, N_KV_GROUPS * LANES)], axis=1)


def kernel(x, ffn_norm, ffn_w_gate_up, ffn_w_down, mix_norm, conv_w_in, conv_w, conv_w_out, kv_norm, kv_w,
           cmp_pos, cmp_w1, cmp_b1, cmp_w2, k_norm, nsa_w_qg, q_norm, nsa_w_o):
    b, s, d = x.shape
    t = b * s
    depth = ffn_norm.shape[0]
    n_a = conv_w_in.shape[0]
    g = N_KV_GROUPS

    ffn_g = ffn_norm.reshape(depth, 2, 1, d)

    def ffn(h2, layer, j, proj=None, tail=None, keep_h=True):
        return _ffn(h2, ffn_g, ffn_w_gate_up, ffn_w_down, (layer, j), proj, tail, keep_h)

    h = x.reshape(t, d)
    shared = None
    for layer in range(depth):
        proj = None
        if layer < n_a:
            h = ffn(h, layer, 0, keep_h=False,
                    tail=_conv_tail(b, s, mix_norm[layer], conv_w_in[layer].astype(BF16), conv_w[layer],
                                    conv_w_out[layer].astype(BF16)))
        else:
            li = layer - n_a
            h, q, gate = ffn(h, layer, 0, tail=_qg_tail(t, mix_norm[layer],
                                                        _gate_columns(nsa_w_qg[li]).astype(BF16)))
            o = _nsa(q.reshape(b, s, -1), gate.reshape(b, s, -1), q_norm[li], *shared)
            proj = (o.reshape(t, -1), nsa_w_o[li].astype(BF16))
        if layer == n_a - 1:
            h, blk, ks, vst, kw, vwt = ffn(h, layer, 1, proj,
                                           tail=_kv_tail(b, s, kv_norm, kv_w.astype(BF16), k_norm))
            blk = blk.reshape(2, b * g, s // CMP_BLOCK, CMP_BLOCK * HEAD_DIM)
            cmp = _compress(blk, cmp_pos.reshape(2, -1), cmp_w1.astype(BF16), cmp_b1,
                            cmp_w2.astype(BF16), k_norm[0])
            cmp = cmp.reshape(2, b, g, -1, HEAD_DIM)
            shared = (cmp[0], cmp[1].transpose(0, 1, 3, 2), ks, vst, kw, vwt)
        else:
            h = ffn(h, layer, 1, proj)
    return h.reshape(b, s, d)
```
